```python
import jax
import jax.numpy as jnp
from jax import lax
import numpy as np

D_MODEL = 1024
BATCH = 2
SEQ = 8192
DEPTH = 2

GRID_W = 64
CTX_LEN = 256
N_MIXERS = 2
EXPAND = 2
D_INNER = EXPAND * D_MODEL
MLSTM_HEADS = 4
MLSTM_HEAD_DIM = D_INNER // MLSTM_HEADS
QKV_BLOCK = 4
CONV_W = 3
CHUNK = 64
POOL_WINDOWS = (2, 4, 8, 16)
POOL_GROUPS = len(POOL_WINDOWS)
POOL_GROUP_DIM = D_INNER // POOL_GROUPS
N_LAYERS_A = (DEPTH + 1) // 2
N_LAYERS_B = DEPTH // 2
EPS = 1e-6

kernel_name = "hybrid_mlstm_pool_diffusion_block"


def rmsnorm(x, g):
    xf = x.astype(jnp.float32)
    y = xf * lax.rsqrt(jnp.mean(xf * xf, axis=-1, keepdims=True) + EPS)
    return (y * g.astype(jnp.float32)).astype(x.dtype)


def ada(cond, w, b):
    mod = jax.nn.silu(cond) @ w + b
    return jnp.split(mod, 3, axis=-1)


def modulate(h, shift, scale):
    return h * (1.0 + scale) + shift


def short_conv(x, w, b):
    pad = CONV_W // 2
    t = x.shape[1]
    xp = jnp.pad(x, ((0, 0), (pad, CONV_W - 1 - pad), (0, 0)))
    y = xp[:, 0:t] * w[0]
    for j in range(1, CONV_W):
        y = y + xp[:, j:j + t] * w[j]
    return y + b


def headwise(x, w):
    bsz, t, e = x.shape
    xb = x.reshape(bsz, t, e // QKV_BLOCK, QKV_BLOCK)
    return jnp.einsum("btgi,gio->btgo", xb, w).reshape(bsz, t, e)


def to_heads(x):
    bsz, t, _ = x.shape
    return x.reshape(bsz, t, MLSTM_HEADS, MLSTM_HEAD_DIM).transpose(0, 2, 1, 3).astype(jnp.float32)


def to_chunks(a):
    bsz, h, t = a.shape[:3]
    a = a.reshape(bsz, h, t // CHUNK, CHUNK, *a.shape[3:])
    return jnp.moveaxis(a, 2, 0)


def mlstm_scan(q, k, v, log_i, log_f, state):
    tril = jnp.tril(jnp.ones((CHUNK, CHUNK), dtype=bool))

    def step(carry, chunk):
        c_mem, n_vec, m_prev = carry
        qc, kc, vc, ic, fc = chunk
        b = jnp.cumsum(fc, axis=-1)
        d = jnp.where(tril, b[..., :, None] - b[..., None, :] + ic[..., None, :], -jnp.inf)
        m_inter = b + m_prev[..., None]
        m_t = jnp.maximum(m_inter, jnp.max(d, axis=-1))
        s = jnp.einsum("bhld,bhsd->bhls", qc, kc) * jnp.exp(d - m_t[..., None])
        w_inter = jnp.exp(m_inter - m_t)
        num = jnp.einsum("bhls,bhse->bhle", s, vc) + w_inter[..., None] * jnp.einsum("bhld,bhde->bhle", qc, c_mem)
        den = jnp.sum(s, axis=-1) + w_inter * jnp.einsum("bhld,bhd->bhl", qc, n_vec)
        h = num / jnp.maximum(jnp.abs(den), jnp.exp(-m_t))[..., None]
        b_last = b[..., -1]
        g = b_last[..., None] - b + ic
        m_new = jnp.maximum(b_last + m_prev, jnp.max(g, axis=-1))
        decay = jnp.exp(b_last + m_prev - m_new)
        wk = kc * jnp.exp(g - m_new[..., None])[..., None]
        c_new = decay[..., None, None] * c_mem + jnp.einsum("bhsd,bhse->bhde", wk, vc)
        n_new = decay[..., None] * n_vec + jnp.sum(wk, axis=2)
        return (c_new, n_new, m_new), h

    xs = (to_chunks(q), to_chunks(k), to_chunks(v), to_chunks(log_i), to_chunks(log_f))
    state, h = lax.scan(step, state, xs)
    h = jnp.moveaxis(h, 0, 2)
    bsz, nh, nc, l, dh = h.shape
    return h.reshape(bsz, nh, nc * l, dh), state


def mlstm_features(h, w_in, conv_w, conv_b, w_q, w_k, w_v):
    xm, z, og = jnp.split(h @ w_in, 3, axis=-1)
    xc = jax.nn.silu(short_conv(xm, conv_w, conv_b))
    q = headwise(xc, w_q)
    k = headwise(xc, w_k) * (MLSTM_HEAD_DIM ** -0.5)
    v = headwise(xm, w_v)
    return q, k, v, xc, z, og


def gate_logs(qkv, w_g, b_g):
    pre = (qkv @ w_g + b_g).astype(jnp.float32)
    log_i = pre[..., :MLSTM_HEADS]
    log_f = jax.nn.log_sigmoid(pre[..., MLSTM_HEADS:])
    return log_i.transpose(0, 2, 1), log_f.transpose(0, 2, 1)


def bidir_mlstm(q, k, v, w_gf, b_gf, w_gb, b_gb, init_f, init_b):
    qkv = jnp.concatenate([q, k, v], axis=-1)
    li_f, lf_f = gate_logs(qkv, w_gf, b_gf)
    li_b, lf_b = gate_logs(qkv, w_gb, b_gb)
    qh, kh, vh = to_heads(q), to_heads(k), to_heads(v)
    h_f, st_f = mlstm_scan(qh, kh, vh, li_f, lf_f, init_f)
    flip = lambda a: jnp.flip(a, axis=2)
    h_b, st_b = mlstm_scan(flip(qh), flip(kh), flip(vh), jnp.flip(li_b, -1), jnp.flip(lf_b, -1), init_b)
    return h_f + flip(h_b), st_f, st_b


def mlstm_output(hsum, xc, z, og, head_norm, skip, w_out):
    hs = hsum.transpose(0, 2, 1, 3)
    mu = jnp.mean(hs, axis=-1, keepdims=True)
    var = jnp.mean(jnp.square(hs - mu), axis=-1, keepdims=True)
    hn = ((hs - mu) * lax.rsqrt(var + EPS)).reshape(hs.shape[0], hs.shape[1], D_INNER).astype(xc.dtype) * head_norm
    y = jax.nn.sigmoid(og) * hn + skip * xc
    return (y * jax.nn.silu(z)) @ w_out


def mlstm_mixer(h_lat, h_ctx, ctx_out, w_in, conv_w, conv_b, w_q, w_k, w_v,
                w_gf, b_gf, w_gb, b_gb, head_norm, skip, w_out):
    bsz = h_lat.shape[0]
    zero = (jnp.zeros((bsz, MLSTM_HEADS, MLSTM_HEAD_DIM, MLSTM_HEAD_DIM), jnp.float32),
            jnp.zeros((bsz, MLSTM_HEADS, MLSTM_HEAD_DIM), jnp.float32),
            jnp.zeros((bsz, MLSTM_HEADS), jnp.float32))
    qc, kc, vc, xcc, zc, ogc = mlstm_features(h_ctx, w_in, conv_w, conv_b, w_q, w_k, w_v)
    h_c, st_f, st_b = bidir_mlstm(qc, kc, vc, w_gf, b_gf, w_gb, b_gb, zero, zero)
    ql, kl, vl, xcl, zl, ogl = mlstm_features(h_lat, w_in, conv_w, conv_b, w_q, w_k, w_v)
    h_l, _, _ = bidir_mlstm(ql, kl, vl, w_gf, b_gf, w_gb, b_gb, st_f, st_b)
    y_lat = mlstm_output(h_l, xcl, zl, ogl, head_norm, skip, w_out)
    y_ctx = mlstm_output(h_c, xcc, zc, ogc, head_norm, skip, w_out) if ctx_out else None
    return y_lat, y_ctx


def box_mean(x, w, axis):
    n = x.shape[axis]
    pad = [(0, 0)] * x.ndim
    pad[axis] = (1, 0)
    cs = jnp.pad(jnp.cumsum(x, axis=axis), pad)
    idx = np.arange(n)
    lo = np.clip(idx - w // 2, 0, n)
    hi = np.clip(idx + w - w // 2, 0, n)
    s = jnp.take(cs, hi, axis=axis) - jnp.take(cs, lo, axis=axis)
    shape = [1] * x.ndim
    shape[axis] = n
    cnt = (hi - lo).astype(np.float32).reshape(shape)
    return s / cnt


def pool_mixer(h, grid, w_in, w_pool, pool_scale, w_out):
    u, z = jnp.split(h @ w_in, 2, axis=-1)
    bsz, t, _ = u.shape
    ug = u.reshape(bsz, t, POOL_GROUPS, POOL_GROUP_DIM).astype(jnp.float32)
    diffs = []
    for g, w in enumerate(POOL_WINDOWS):
        xg = ug[:, :, g, :]
        if grid:
            rows = t // GRID_W
            x2 = xg.reshape(bsz, rows, GRID_W, POOL_GROUP_DIM)
            m = box_mean(box_mean(x2, w, 2), w, 1).reshape(bsz, t, POOL_GROUP_DIM)
        else:
            m = box_mean(xg, w, 1)
        diffs.append(m - xg)
    d = jnp.stack(diffs, axis=2).astype(h.dtype)
    y = jnp.einsum("btgi,gio->btgo", d, w_pool).reshape(bsz, t, D_INNER) * pool_scale
    return (y * jax.nn.silu(z)) @ w_out


def setup_inputs(seed: int = 0) -> dict:
    key = jax.random.key(seed)
    ks = jax.random.split(key, 32)
    nrm = lambda k, shape, s: jax.random.normal(k, shape, jnp.float32) * s
    d, e, h = D_MODEL, D_INNER, MLSTM_HEADS
    na, nb = N_LAYERS_A, N_LAYERS_B
    b_gate = lambda k: jnp.concatenate([
        nrm(k, (na, h), 0.1),
        jnp.broadcast_to(jnp.linspace(3.0, 6.0, h, dtype=jnp.float32), (na, h)) + nrm(jax.random.fold_in(k, 1), (na, h), 0.1)], axis=-1)
    return {
        "x": nrm(ks[0], (BATCH, SEQ, d), 1.0),
        "c": nrm(ks[1], (BATCH, d), 1.0),
        "ctx": nrm(ks[2], (BATCH, CTX_LEN, d), 1.0),
        "c_ctx": nrm(ks[3], (d,), 1.0),
        "w_ada": nrm(ks[4], (DEPTH, d, 3 * d), d ** -0.5),
        "b_ada": nrm(ks[5], (DEPTH, 3 * d), 0.02),
        "a_norm_pre": 1.0 + nrm(ks[6], (na, d), 0.02),
        "a_norm_post": 1.0 + nrm(ks[7], (na, d), 0.02),
        "a_w_in": nrm(ks[8], (na, d, 3 * e), d ** -0.5),
        "a_conv_w": nrm(ks[9], (na, CONV_W, e), CONV_W ** -0.5),
        "a_conv_b": nrm(ks[10], (na, e), 0.02),
        "a_w_q": nrm(ks[11], (na, e // QKV_BLOCK, QKV_BLOCK, QKV_BLOCK), QKV_BLOCK ** -0.5),
        "a_w_k": nrm(ks[12], (na, e // QKV_BLOCK, QKV_BLOCK, QKV_BLOCK), QKV_BLOCK ** -0.5),
        "a_w_v": nrm(ks[13], (na, e // QKV_BLOCK, QKV_BLOCK, QKV_BLOCK), QKV_BLOCK ** -0.5),
        "a_w_gate_f": nrm(ks[14], (na, 3 * e, 2 * h), 0.5 * (3 * e) ** -0.5),
        "a_b_gate_f": b_gate(ks[15]),
        "a_w_gate_b": nrm(ks[16], (na, 3 * e, 2 * h), 0.5 * (3 * e) ** -0.5),
        "a_b_gate_b": b_gate(ks[17]),
        "a_head_norm": 1.0 + nrm(ks[18], (na, e), 0.02),
        "a_skip": 1.0 + nrm(ks[19], (na, e), 0.02),
        "a_w_out": nrm(ks[20], (na, e, d), e ** -0.5),
        "b_norm_pre": 1.0 + nrm(ks[21], (nb, d), 0.02),
        "b_norm_post": 1.0 + nrm(ks[22], (nb, d), 0.02),
        "b_w_in": nrm(ks[23], (nb, d, 2 * e), d ** -0.5),
        "b_w_pool": nrm(ks[24], (nb, POOL_GROUPS, POOL_GROUP_DIM, POOL_GROUP_DIM), POOL_GROUP_DIM ** -0.5),
        "b_pool_scale": 1.0 + nrm(ks[25], (nb, e), 0.1),
        "b_w_out": nrm(ks[26], (nb, e, d), e ** -0.5),
    }


def reference(x, c, ctx, c_ctx, w_ada, b_ada,
              a_norm_pre, a_norm_post, a_w_in, a_conv_w, a_conv_b, a_w_q, a_w_k, a_w_v,
              a_w_gate_f, a_b_gate_f, a_w_gate_b, a_b_gate_b, a_head_norm, a_skip, a_w_out,
              b_norm_pre, b_norm_post, b_w_in, b_w_pool, b_pool_scale, b_w_out):
    for i in range(DEPTH):
        kind = i % N_MIXERS
        j = i // N_MIXERS
        ctx_out = any(l % N_MIXERS == 0 for l in range(i + 1, DEPTH))
        ctx_in = (kind == 0) or ctx_out
        shift, scale, gate = ada(c, w_ada[i], b_ada[i])
        pre_g = a_norm_pre[j] if kind == 0 else b_norm_pre[j]
        post_g = a_norm_post[j] if kind == 0 else b_norm_post[j]
        h_lat = modulate(rmsnorm(x, pre_g), shift[:, None, :], scale[:, None, :])
        if ctx_in:
            c_shift, c_scale, c_gate = ada(c_ctx, w_ada[i], b_ada[i])
            h_ctx = modulate(rmsnorm(ctx, pre_g), c_shift, c_scale)
        if kind == 0:
            y_lat, y_ctx = mlstm_mixer(h_lat, h_ctx, ctx_out, a_w_in[j], a_conv_w[j], a_conv_b[j],
                                       a_w_q[j], a_w_k[j], a_w_v[j], a_w_gate_f[j], a_b_gate_f[j],
                                       a_w_gate_b[j], a_b_gate_b[j], a_head_norm[j], a_skip[j], a_w_out[j])
        else:
            y_lat = pool_mixer(h_lat, True, b_w_in[j], b_w_pool[j], b_pool_scale[j], b_w_out[j])
            y_ctx = pool_mixer(h_ctx, False, b_w_in[j], b_w_pool[j], b_pool_scale[j], b_w_out[j]) if ctx_out else None
        x = x + gate[:, None, :] * rmsnorm(y_lat, post_g)
        if ctx_out:
            ctx = ctx + c_gate * rmsnorm(y_ctx, post_g)
    return x
```

```python
import functools

import numpy as np
import jax
import jax.numpy as jnp
from jax import lax
from jax.experimental import pallas as pl
from jax.experimental.pallas import tpu as pltpu

EPS = 1e-6
N_HEADS = 4
QKV_BLOCK = 4
CONV_W = 3
GRID_W = 64
POOL_WINDOWS = (2, 4, 8, 16)

LANES = 128
MXU_DIM = 256
HALO = 16
VMEM_LIMIT_BYTES = 56 * 1024 * 1024

SCAN_CHUNK = 256

BF16 = jnp.bfloat16
F32 = jnp.float32


def _dot(a, b):
    return jnp.dot(a, b, preferred_element_type=F32)


def _silu(v):
    return v * jax.nn.sigmoid(v)


def _cparams(sem):
    return pltpu.CompilerParams(dimension_semantics=sem, vmem_limit_bytes=VMEM_LIMIT_BYTES)


def _const_spec(shape):
    nd = len(shape)
    return pl.BlockSpec(shape, lambda *_: (0,) * nd, pipeline_mode=pl.Buffered(1))


def _ada_kernel(cond_t_ref, w_ref, b_ref, o_ref):
    s = _silu(cond_t_ref[...])
    w = w_ref[0]
    n_rows = o_ref.shape[1]
    for r in range(n_rows):
        o_ref[0, r:r + 1, :] = jnp.sum(w * s[:, r:r + 1], axis=0, keepdims=True) + b_ref[0]


def _ada(cond, w_ada, b_ada):
    depth, d, n3 = w_ada.shape
    rows = cond.shape[0]
    tn = max(w for w in range(LANES, 1024 + 1, LANES) if n3 % w == 0)
    return pl.pallas_call(
        _ada_kernel,
        grid=(depth, n3 // tn),
        in_specs=[
            pl.BlockSpec((d, rows), lambda l, j: (0, 0)),
            pl.BlockSpec((1, d, tn), lambda l, j: (l, 0, j)),
            pl.BlockSpec((1, 1, tn), lambda l, j: (l, 0, j)),
        ],
        out_specs=pl.BlockSpec((1, rows, tn), lambda l, j: (l, 0, j)),
        out_shape=jax.ShapeDtypeStruct((depth, rows, n3), F32),
        compiler_params=_cparams(("parallel", "parallel")),
        name="ada_mod",
    )(cond.T, w_ada, b_ada.reshape(depth, 1, n3))


def _norm_mod(xv, g, shift, scale):
    y = xv * lax.rsqrt(jnp.mean(xv * xv, axis=-1, keepdims=True) + EPS) * g
    return y * (1.0 + scale) + shift


def _mlstm_pre_kernel(x_ref, halo_ref, mod_ref, g_ref, win_ref, convw_ref, convb_ref,
                      bdq_ref, bdk_ref, bdv_ref, wg_ref, bg_ref, hnorm_ref, skip_ref,
                      q_ref, k_ref, v_ref, g1_ref, g2_ref, pre_ref,
                      hn_scr, xm_scr, *, k_scale):
    i = pl.program_id(1)
    nt = pl.num_programs(1)
    tm = x_ref.shape[1]
    e = q_ref.shape[2]
    shift = mod_ref[0, 0:1, :]
    scale = mod_ref[0, 1:2, :]
    g = g_ref[...]

    hn_scr[0:HALO, :] = _norm_mod(halo_ref[0, 0, 0:HALO, :], g, shift, scale).astype(BF16)
    hn_scr[HALO:HALO + tm, :] = _norm_mod(x_ref[0], g, shift, scale).astype(BF16)
    hn_scr[HALO + tm:, :] = _norm_mod(halo_ref[0, 0, HALO:, :], g, shift, scale).astype(BF16)

    pre = jnp.zeros((tm, LANES), F32) + bg_ref[...]
    for c in range(e // MXU_DIM):
        cs = slice(c * MXU_DIM, (c + 1) * MXU_DIM)
        xm_ext = _dot(hn_scr[...], win_ref[:, cs])
        xm_scr[0:HALO, :] = jnp.where(i > 0, xm_ext[0:HALO], 0.0)
        xm_scr[HALO:HALO + tm, :] = xm_ext[HALO:HALO + tm]
        xm_scr[HALO + tm:, :] = jnp.where(i < nt - 1, xm_ext[HALO + tm:], 0.0)
        hc = hn_scr[HALO:HALO + tm, :]
        z = _dot(hc, win_ref[:, e + c * MXU_DIM:e + (c + 1) * MXU_DIM])
        og = _dot(hc, win_ref[:, 2 * e + c * MXU_DIM:2 * e + (c + 1) * MXU_DIM])

        xm = xm_scr[HALO:HALO + tm, :]
        conv = (xm_scr[HALO - 1:HALO - 1 + tm, :] * convw_ref[0:1, cs]
                + xm * convw_ref[1:2, cs]
                + xm_scr[HALO + 1:HALO + 1 + tm, :] * convw_ref[2:3, cs]
                + convb_ref[:, cs])
        xc = _silu(conv)
        xc_b = xc.astype(BF16)
        q = _dot(xc_b, bdq_ref[c]).astype(BF16)
        k = (_dot(xc_b, bdk_ref[c]) * k_scale).astype(BF16)
        v = _dot(xm.astype(BF16), bdv_ref[c]).astype(BF16)
        pre = (pre + _dot(q, wg_ref[cs, :])
               + _dot(k, wg_ref[e + c * MXU_DIM:e + (c + 1) * MXU_DIM, :])
               + _dot(v, wg_ref[2 * e + c * MXU_DIM:2 * e + (c + 1) * MXU_DIM, :]))
        sz = _silu(z)
        q_ref[0, :, cs] = q
        k_ref[0, :, cs] = k
        v_ref[0, :, cs] = v
        g1_ref[0, :, cs] = (hnorm_ref[:, cs] * jax.nn.sigmoid(og) * sz).astype(BF16)
        g2_ref[0, :, cs] = (skip_ref[:, cs] * xc * sz).astype(BF16)
    pre_ref[0] = pre


def _mlstm_pre(x, mod, pre_g, w_in_b, conv_w, conv_b, bdq, bdk, bdv, wg, bg, head_norm, skip, tm):
    b, t, d = x.shape
    e = w_in_b.shape[1] // 3
    nt = t // tm
    xr = x.reshape(b, nt, tm, d)
    zeros = jnp.zeros((b, 1, HALO, d), x.dtype)
    prev = jnp.concatenate([zeros, xr[:, :-1, tm - HALO:, :]], axis=1)
    nxt = jnp.concatenate([xr[:, 1:, :HALO, :], zeros], axis=1)
    halo = jnp.concatenate([prev, nxt], axis=2)

    row = lambda bi, i: (bi, i, 0)
    big = pl.BlockSpec((1, tm, e), row)
    kernel = functools.partial(_mlstm_pre_kernel, k_scale=float((e // N_HEADS) ** -0.5))
    out_shapes = [jax.ShapeDtypeStruct((b, t, e), BF16)] * 5 + [jax.ShapeDtypeStruct((b, t, LANES), F32)]
    return pl.pallas_call(
        kernel,
        grid=(b, nt),
        in_specs=[
            pl.BlockSpec((1, tm, d), row),
            pl.BlockSpec((1, 1, 2 * HALO, d), lambda bi, i: (bi, i, 0, 0)),
            pl.BlockSpec((1, 3, d), lambda bi, i: (bi, 0, 0)),
            _const_spec((1, d)),
            _const_spec(w_in_b.shape),
            _const_spec(conv_w.shape),
            _const_spec((1, e)),
            _const_spec(bdq.shape), _const_spec(bdk.shape), _const_spec(bdv.shape),
            _const_spec(wg.shape), _const_spec((1, LANES)),
            _const_spec((1, e)), _const_spec((1, e)),
        ],
        out_specs=[big] * 5 + [pl.BlockSpec((1, tm, LANES), row)],
        out_shape=out_shapes,
        scratch_shapes=[pltpu.VMEM((tm + 2 * HALO, d), BF16),
                        pltpu.VMEM((tm + 2 * HALO, MXU_DIM), F32)],
        compiler_params=_cparams(("parallel", "parallel")),
        name="mlstm_pre",
    )(x, halo, mod, pre_g.reshape(1, d), w_in_b, conv_w, conv_b.reshape(1, e),
      bdq, bdk, bdv, wg, bg, head_norm.reshape(1, e), skip.reshape(1, e))


def _gate_prep_kernel(pre_ref, cols_ref, rows_ref):
    h = N_HEADS
    l = pre_ref.shape[1]
    pre = pre_ref[0]
    lf = jnp.minimum(pre, 0.0) - jnp.log1p(jnp.exp(-jnp.abs(pre)))
    ti = lax.broadcasted_iota(jnp.int32, (l, l), 0)
    si = lax.broadcasted_iota(jnp.int32, (l, l), 1)
    tril = ti >= si
    hp = lax.Precision.HIGHEST
    b_fwd = jnp.dot(tril.astype(F32), lf, precision=hp, preferred_element_type=F32)
    b_bwd = jnp.dot((si >= ti).astype(F32), lf, precision=hp, preferred_element_type=F32)
    lane = lax.broadcasted_iota(jnp.int32, (l, LANES), 1)
    b_all = jnp.where(lane < 2 * h, b_fwd, b_bwd)
    a_all = pre - pltpu.roll(b_all, LANES - h, axis=1)
    a_t = a_all.T
    out = jnp.where((lane % (2 * h)) < h, a_all, b_all)
    neg = -jnp.inf
    for j in range(h):
        cf = jnp.max(jnp.where(tril, a_t[j:j + 1, :], neg), axis=1, keepdims=True)
        cb = jnp.max(jnp.where(si >= ti, a_t[2 * h + j:2 * h + j + 1, :], neg), axis=1, keepdims=True)
        out = jnp.where(lane == 4 * h + j, cf, out)
        out = jnp.where(lane == 5 * h + j, cb, out)
    cols_ref[0] = out
    rows_ref[0, 0] = a_t[0:4 * h, :]


def _gate_prep(pre, l):
    b, t, _ = pre.shape
    nc = t // l
    return pl.pallas_call(
        _gate_prep_kernel,
        grid=(b, nc),
        in_specs=[pl.BlockSpec((1, l, LANES), lambda bi, c: (bi, c, 0))],
        out_specs=[pl.BlockSpec((1, l, LANES), lambda bi, c: (bi, c, 0)),
                   pl.BlockSpec((1, 1, 4 * N_HEADS, l), lambda bi, c: (bi, c, 0, 0))],
        out_shape=[jax.ShapeDtypeStruct((b, t, LANES), F32),
                   jax.ShapeDtypeStruct((b, nc, 4 * N_HEADS, l), F32)],
        compiler_params=_cparams(("parallel", "parallel")),
        name="gate_prep",
    )(pre)


def _scan_step(q, k, v, cols, rows, head, backward, c_ref, n_ref, m_ref, want_h):
    h = N_HEADS
    l = k.shape[0]
    off = 2 * h if backward else 0
    a_col = cols[:, off + head:off + head + 1]
    b_col = cols[:, off + h + head:off + h + head + 1]
    cm_col = cols[:, (5 if backward else 4) * h + head:(5 if backward else 4) * h + head + 1]
    a_row = rows[off + head:off + head + 1, :]
    last = 0 if backward else l - 1
    total = b_col[last:last + 1, :]
    amax = cm_col[last:last + 1, :]

    m_prev = m_ref[...]
    c_mem = c_ref[...]
    n_vec = n_ref[...]
    k32 = k.astype(F32)

    h_out = None
    if want_h:
        ti = lax.broadcasted_iota(jnp.int32, (l, l), 0)
        si = lax.broadcasted_iota(jnp.int32, (l, l), 1)
        mask = (si >= ti) if backward else (ti >= si)
        u = jnp.maximum(m_prev, cm_col)
        p = jnp.exp(jnp.where(mask, a_row - u, -jnp.inf))
        s = lax.dot_general(q, k, (((1,), (1,)), ((), ())), preferred_element_type=F32) * p
        w_inter = jnp.exp(m_prev - u)
        num = _dot(s.astype(BF16), v) + w_inter * _dot(q, c_mem.astype(BF16))
        qn = jnp.sum(q.astype(F32) * n_vec, axis=-1, keepdims=True)
        den = jnp.sum(s, axis=-1, keepdims=True) + w_inter * qn
        h_out = num / jnp.maximum(jnp.abs(den), jnp.exp(-(b_col + u)))

    u_last = jnp.maximum(m_prev, amax)
    decay = jnp.exp(m_prev - u_last)
    wk = k32 * jnp.exp(a_col - u_last)
    c_ref[...] = decay * c_mem + lax.dot_general(
        wk.astype(BF16), v, (((0,), (0,)), ((), ())), preferred_element_type=F32)
    n_ref[...] = decay * n_vec + jnp.sum(wk, axis=0, keepdims=True)
    m_ref[...] = total + u_last
    return h_out


def _ctx_state_kernel(k_ref, v_ref, cols_ref, rows_ref, c_out, n_out, m_out):
    dh = c_out.shape[-1]
    nc = k_ref.shape[1] // SCAN_CHUNK
    c_out[...] = jnp.zeros(c_out.shape, F32)
    n_out[...] = jnp.zeros(n_out.shape, F32)
    m_out[...] = jnp.zeros(m_out.shape, F32)
    for direction in range(2):
        for step in range(nc):
            c = nc - 1 - step if direction else step
            rs = slice(c * SCAN_CHUNK, (c + 1) * SCAN_CHUNK)
            cols = cols_ref[0, rs, :]
            rows = rows_ref[0, c]
            for head in range(N_HEADS):
                hs = slice(head * dh, (head + 1) * dh)
                _scan_step(None, k_ref[0, rs, hs], v_ref[0, rs, hs], cols, rows, head, bool(direction),
                           c_out.at[0, direction, head], n_out.at[0, direction, head],
                           m_out.at[0, direction, head], False)


def _ctx_states(k, v, cols, rows):
    b, t, e = k.shape
    dh = e // N_HEADS
    nc = t // SCAN_CHUNK
    per_b = lambda nd: (lambda bi: (bi,) + (0,) * (nd - 1))
    return pl.pallas_call(
        _ctx_state_kernel,
        grid=(b,),
        in_specs=[pl.BlockSpec((1, t, e), per_b(3)), pl.BlockSpec((1, t, e), per_b(3)),
                  pl.BlockSpec((1, t, LANES), per_b(3)),
                  pl.BlockSpec((1, nc, 4 * N_HEADS, SCAN_CHUNK), per_b(4))],
        out_specs=[pl.BlockSpec((1, 2, N_HEADS, dh, dh), per_b(5)),
                   pl.BlockSpec((1, 2, N_HEADS, 1, dh), per_b(5)),
                   pl.BlockSpec((1, 2, N_HEADS, 1, 1), per_b(5))],
        out_shape=[jax.ShapeDtypeStruct((b, 2, N_HEADS, dh, dh), F32),
                   jax.ShapeDtypeStruct((b, 2, N_HEADS, 1, dh), F32),
                   jax.ShapeDtypeStruct((b, 2, N_HEADS, 1, 1), F32)],
        compiler_params=_cparams(("parallel",)),
        name="ctx_states",
    )(k, v, cols, rows)


def _load_state(c0_ref, n0_ref, m0_ref, c_scr, n_scr, m_scr):
    @pl.when(pl.program_id(1) == 0)
    def _():
        c_scr[...] = c0_ref[0, 0]
        n_scr[...] = n0_ref[0, 0]
        m_scr[...] = m0_ref[0, 0]


def _scan_bwd_kernel(q_ref, k_ref, v_ref, cols_ref, rows_ref, c0_ref, n0_ref, m0_ref,
                     hb_ref, c_scr, n_scr, m_scr):
    _load_state(c0_ref, n0_ref, m0_ref, c_scr, n_scr, m_scr)
    dh = c_scr.shape[-1]
    cols = cols_ref[0]
    rows = rows_ref[0, 0]
    for head in range(N_HEADS):
        hs = slice(head * dh, (head + 1) * dh)
        hb = _scan_step(q_ref[0, :, hs], k_ref[0, :, hs], v_ref[0, :, hs], cols, rows, head, True,
                        c_scr.at[head], n_scr.at[head], m_scr.at[head], True)
        hb_ref[0, :, hs] = hb.astype(hb_ref.dtype)


def _scan_fwd_out_kernel(q_ref, k_ref, v_ref, cols_ref, rows_ref, c0_ref, n0_ref, m0_ref,
                         hb_ref, g1_ref, g2_ref, wout_ref, x_ref, mod_ref, postg_ref,
                         o_ref, c_scr, n_scr, m_scr):
    _load_state(c0_ref, n0_ref, m0_ref, c_scr, n_scr, m_scr)
    dh = c_scr.shape[-1]
    cols = cols_ref[0]
    rows = rows_ref[0, 0]
    y = jnp.zeros(o_ref.shape[1:], F32)
    for head in range(N_HEADS):
        hs = slice(head * dh, (head + 1) * dh)
        hf = _scan_step(q_ref[0, :, hs], k_ref[0, :, hs], v_ref[0, :, hs], cols, rows, head, False,
                        c_scr.at[head], n_scr.at[head], m_scr.at[head], True)
        hsum = hf + hb_ref[0, :, hs].astype(F32)
        mu = jnp.mean(hsum, axis=-1, keepdims=True)
        cen = hsum - mu
        var = jnp.mean(cen * cen, axis=-1, keepdims=True)
        hn = cen * lax.rsqrt(var + EPS)
        gated = hn * g1_ref[0, :, hs].astype(F32) + g2_ref[0, :, hs].astype(F32)
        y = y + _dot(gated.astype(BF16), wout_ref[hs, :])
    yn = y * lax.rsqrt(jnp.mean(y * y, axis=-1, keepdims=True) + EPS) * postg_ref[...]
    o_ref[0] = x_ref[0] + mod_ref[0, 2:3, :] * yn


def _scan_specs(b, t, e, nc, l, reverse):
    dh = e // N_HEADS
    cidx = (lambda c: nc - 1 - c) if reverse else (lambda c: c)
    tok = lambda w: pl.BlockSpec((1, l, w), lambda bi, c: (bi, cidx(c), 0))
    direction = 1 if reverse else 0
    state = lambda shape: pl.BlockSpec((1, 1) + shape, lambda bi, c: (bi, direction) + (0,) * len(shape),
                                       pipeline_mode=pl.Buffered(1))
    specs = [tok(e), tok(e), tok(e), tok(LANES),
             pl.BlockSpec((1, 1, 4 * N_HEADS, l), lambda bi, c: (bi, cidx(c), 0, 0)),
             state((N_HEADS, dh, dh)), state((N_HEADS, 1, dh)), state((N_HEADS, 1, 1))]
    scratch = [pltpu.VMEM((N_HEADS, dh, dh), F32), pltpu.VMEM((N_HEADS, 1, dh), F32),
               pltpu.VMEM((N_HEADS, 1, 1), F32)]
    return specs, scratch, tok


def _scan_bwd(q, k, v, cols, rows, c0, n0, m0):
    b, t, e = q.shape
    l = SCAN_CHUNK
    nc = t // l
    specs, scratch, tok = _scan_specs(b, t, e, nc, l, True)
    return pl.pallas_call(
        _scan_bwd_kernel,
        grid=(b, nc),
        in_specs=specs,
        out_specs=tok(e),
        out_shape=jax.ShapeDtypeStruct((b, t, e), BF16),
        scratch_shapes=scratch,
        compiler_params=_cparams(("parallel", "arbitrary")),
        name="mlstm_scan_bwd",
    )(q, k, v, cols, rows, c0, n0, m0)


def _scan_fwd_out(q, k, v, cols, rows, c0, n0, m0, hb, g1, g2, w_out_b, x, mod, post_g):
    b, t, e = q.shape
    d = x.shape[-1]
    l = SCAN_CHUNK
    nc = t // l
    specs, scratch, tok = _scan_specs(b, t, e, nc, l, False)
    specs = specs + [tok(e), tok(e), tok(e), _const_spec(w_out_b.shape), tok(d),
                     pl.BlockSpec((1, 3, d), lambda bi, c: (bi, 0, 0)), _const_spec((1, d))]
    return pl.pallas_call(
        _scan_fwd_out_kernel,
        grid=(b, nc),
        in_specs=specs,
        out_specs=tok(d),
        out_shape=jax.ShapeDtypeStruct((b, t, d), F32),
        scratch_shapes=scratch,
        compiler_params=_cparams(("parallel", "arbitrary")),
        name="mlstm_scan_fwd_out",
    )(q, k, v, cols, rows, c0, n0, m0, hb, g1, g2, w_out_b, x, mod, post_g.reshape(1, d))


def _pool_pre_kernel(x_ref, mod_ref, g_ref, win_ref, u_ref, sz_ref):
    e = u_ref.shape[2]
    hn = _norm_mod(x_ref[0], g_ref[...], mod_ref[0, 0:1, :], mod_ref[0, 1:2, :]).astype(BF16)
    for c in range(e // MXU_DIM):
        cs = slice(c * MXU_DIM, (c + 1) * MXU_DIM)
        u_ref[0, :, cs] = _dot(hn, win_ref[:, cs]).astype(BF16)
        sz_ref[0, :, cs] = _silu(_dot(hn, win_ref[:, e + c * MXU_DIM:e + (c + 1) * MXU_DIM])).astype(BF16)


def _pool_pre(x, mod, pre_g, w_in_b, tm):
    b, t, d = x.shape
    e = w_in_b.shape[1] // 2
    row = lambda bi, i: (bi, i, 0)
    return pl.pallas_call(
        _pool_pre_kernel,
        grid=(b, t // tm),
        in_specs=[pl.BlockSpec((1, tm, d), row), pl.BlockSpec((1, 3, d), lambda bi, i: (bi, 0, 0)),
                  _const_spec((1, d)), _const_spec(w_in_b.shape)],
        out_specs=[pl.BlockSpec((1, tm, e), row)] * 2,
        out_shape=[jax.ShapeDtypeStruct((b, t, e), BF16)] * 2,
        compiler_params=_cparams(("parallel", "parallel")),
        name="pool_pre",
    )(x, mod, pre_g.reshape(1, d), w_in_b)


def _pool_out_kernel(up_ref, uc_ref, un_ref, sz_ref, band_ref, icnt_ref, wpool_ref, pscale_ref,
                     wout_ref, x_ref, mod_ref, postg_ref, o_ref, ext_scr):
    i = pl.program_id(1)
    nt = pl.num_programs(1)
    tm = uc_ref.shape[1]
    gd = wpool_ref.shape[1]
    y = jnp.zeros(o_ref.shape[1:], F32)
    for g, w in enumerate(POOL_WINDOWS):
        gs = slice(g * gd, (g + 1) * gd)
        ext_scr[0:tm, :] = jnp.where(i > 0, up_ref[0, :, gs].astype(F32), 0.0)
        ext_scr[tm:2 * tm, :] = uc_ref[0, :, gs].astype(F32)
        ext_scr[2 * tm:, :] = jnp.where(i < nt - 1, un_ref[0, :, gs].astype(F32), 0.0)
        rsum = jnp.zeros((tm, gd), F32)
        for o in range(-(w // 2), w - w // 2):
            rsum = rsum + ext_scr[tm + o * GRID_W:2 * tm + o * GRID_W, :]
        rsum = rsum.astype(BF16)
        parts = [_dot(band_ref[g], rsum[s * MXU_DIM:(s + 1) * MXU_DIM, :]) for s in range(tm // MXU_DIM)]
        mean = jnp.concatenate(parts, axis=0) * icnt_ref[:, g:g + 1]
        diff = mean - ext_scr[tm:2 * tm, :]
        yg = _dot(diff.astype(BF16), wpool_ref[g]) * pscale_ref[:, gs] * sz_ref[0, :, gs].astype(F32)
        y = y + _dot(yg.astype(BF16), wout_ref[gs, :])
    yn = y * lax.rsqrt(jnp.mean(y * y, axis=-1, keepdims=True) + EPS) * postg_ref[...]
    o_ref[0] = x_ref[0] + mod_ref[0, 2:3, :] * yn


def _pool_constants(t, tm):
    rows = t // GRID_W
    bands = np.zeros((len(POOL_WINDOWS), MXU_DIM, MXU_DIM), np.float32)
    icnt = np.zeros((t, LANES), np.float32)
    for g, w in enumerate(POOL_WINDOWS):
        lo_off, hi_off = -(w // 2), w - w // 2
        for p in range(MXU_DIM):
            c = p % GRID_W
            base = p - c
            lo, hi = max(c + lo_off, 0), min(c + hi_off, GRID_W)
            bands[g, p, base + lo:base + hi] = 1.0
        cc = np.arange(GRID_W)
        cnt_c = np.clip(cc + hi_off, 0, GRID_W) - np.clip(cc + lo_off, 0, GRID_W)
        rr = np.arange(rows)
        cnt_r = np.clip(rr + hi_off, 0, rows) - np.clip(rr + lo_off, 0, rows)
        icnt[:, g] = (1.0 / (cnt_r[:, None] * cnt_c[None, :])).reshape(-1)
    return bands, icnt


def _pool_out(u, sz, w_pool_b, pool_scale, w_out_b, x, mod, post_g, tm):
    b, t, e = u.shape
    d = x.shape[-1]
    nt = t // tm
    gd = e // len(POOL_WINDOWS)
    bands, icnt = _pool_constants(t, tm)
    row = lambda bi, i: (bi, i, 0)
    return pl.pallas_call(
        _pool_out_kernel,
        grid=(b, nt),
        in_specs=[
            pl.BlockSpec((1, tm, e), lambda bi, i: (bi, jnp.maximum(i - 1, 0), 0)),
            pl.BlockSpec((1, tm, e), row),
            pl.BlockSpec((1, tm, e), lambda bi, i: (bi, jnp.minimum(i + 1, nt - 1), 0)),
            pl.BlockSpec((1, tm, e), row),
            _const_spec(bands.shape),
            pl.BlockSpec((tm, LANES), lambda bi, i: (i, 0)),
            _const_spec(w_pool_b.shape),
            _const_spec((1, e)),
            _const_spec(w_out_b.shape),
            pl.BlockSpec((1, tm, d), row),
            pl.BlockSpec((1, 3, d), lambda bi, i: (bi, 0, 0)),
            _const_spec((1, d)),
        ],
        out_specs=pl.BlockSpec((1, tm, d), row),
        out_shape=jax.ShapeDtypeStruct((b, t, d), F32),
        scratch_shapes=[pltpu.VMEM((3 * tm, gd), F32)],
        compiler_params=_cparams(("parallel", "parallel")),
        name="pool_out",
    )(u, u, u, sz, jnp.asarray(bands, BF16), jnp.asarray(icnt), w_pool_b, pool_scale.reshape(1, e),
      w_out_b, x, mod, post_g.reshape(1, d))


def _block_diag_tiles(w):
    nb = w.shape[0]
    per = MXU_DIM // QKV_BLOCK
    w = w.reshape(nb // per, per, QKV_BLOCK, QKV_BLOCK)
    eye = jnp.eye(per, dtype=w.dtype)
    tiles = jnp.einsum("cgio,gh->cgiho", w, eye)
    return tiles.reshape(nb // per, MXU_DIM, MXU_DIM).astype(BF16)


def _gate_weights(w_f, b_f, w_b, b_b):
    k = w_f.shape[0]
    wg = jnp.zeros((k, LANES), F32).at[:, 0:2 * N_HEADS].set(w_f).at[:, 2 * N_HEADS:4 * N_HEADS].set(w_b)
    bg = jnp.zeros((1, LANES), F32).at[0, 0:2 * N_HEADS].set(b_f).at[0, 2 * N_HEADS:4 * N_HEADS].set(b_b)
    return wg.astype(BF16), bg


def kernel(x, c, ctx, c_ctx, w_ada, b_ada, a_norm_pre, a_norm_post, a_w_in, a_conv_w, a_conv_b,
           a_w_q, a_w_k, a_w_v, a_w_gate_f, a_b_gate_f, a_w_gate_b, a_b_gate_b, a_head_norm, a_skip,
           a_w_out, b_norm_pre, b_norm_post, b_w_in, b_w_pool, b_pool_scale, b_w_out):
    bsz, t, d = x.shape
    t_ctx = ctx.shape[1]
    assert t % SCAN_CHUNK == 0 and t_ctx % SCAN_CHUNK == 0 and t % GRID_W == 0

    n_cond = -(-(bsz + 1) // 8) * 8
    cond = jnp.zeros((n_cond, d), F32).at[:bsz].set(c).at[bsz].set(c_ctx)
    mods = _ada(cond, w_ada, b_ada).reshape(w_ada.shape[0], n_cond, 3, d)
    mod_lat = [mods[i, :bsz] for i in range(w_ada.shape[0])]
    mod_ctx = jnp.broadcast_to(mods[0, bsz][None], (bsz, 3, d))

    w_in_b = a_w_in[0].astype(BF16)
    bdq, bdk, bdv = (_block_diag_tiles(w[0]) for w in (a_w_q, a_w_k, a_w_v))
    wg, bg = _gate_weights(a_w_gate_f[0], a_b_gate_f[0], a_w_gate_b[0], a_b_gate_b[0])
    pre_args = (a_norm_pre[0], w_in_b, a_conv_w[0], a_conv_b[0], bdq, bdk, bdv, wg, bg,
                a_head_norm[0], a_skip[0])
    _, kc, vc, _, _, pre_c = _mlstm_pre(ctx, mod_ctx, *pre_args, tm=min(t_ctx, 256))
    cols_c, rows_c = _gate_prep(pre_c, SCAN_CHUNK)
    c0, n0, m0 = _ctx_states(kc, vc, cols_c, rows_c)

    ql, kl, vl, g1, g2, pre_l = _mlstm_pre(x, mod_lat[0], *pre_args, tm=256)
    cols_l, rows_l = _gate_prep(pre_l, SCAN_CHUNK)
    hb = _scan_bwd(ql, kl, vl, cols_l, rows_l, c0, n0, m0)
    x = _scan_fwd_out(ql, kl, vl, cols_l, rows_l, c0, n0, m0, hb, g1, g2,
                      a_w_out[0].astype(BF16), x, mod_lat[0], a_norm_post[0])

    u, sz = _pool_pre(x, mod_lat[1], b_norm_pre[0], b_w_in[0].astype(BF16), tm=512)
    x = _pool_out(u, sz, b_w_pool[0].astype(BF16), b_pool_scale[0], b_w_out[0].astype(BF16),
                  x, mod_lat[1], b_norm_post[0], tm=512)
    return x
```

```python
import functools

import numpy as np
import jax
import jax.numpy as jnp
from jax import lax
from jax.experimental import pallas as pl
from jax.experimental.pallas import tpu as pltpu

EPS = 1e-6
N_HEADS = 4
QKV_BLOCK = 4
GRID_W = 64
POOL_WINDOWS = (2, 4, 8, 16)

LANES = 128
MXU_DIM = 256
HALO = 16
VMEM_LIMIT_BYTES = 56 * 1024 * 1024

SCAN_CHUNK = 256
POOL_TILE = 512
GATE_CHUNKS_PER_STEP = 8

BF16 = jnp.bfloat16
F32 = jnp.float32


def _dot(a, b):
    return jnp.dot(a, b, preferred_element_type=F32)


def _silu(v):
    return v * jax.nn.sigmoid(v)


def _cparams(sem):
    return pltpu.CompilerParams(dimension_semantics=sem, vmem_limit_bytes=VMEM_LIMIT_BYTES)


def _const_spec(shape):
    nd = len(shape)
    return pl.BlockSpec(shape, lambda *_: (0,) * nd, pipeline_mode=pl.Buffered(1))


def _lane_repeat(col, width):
    return jnp.concatenate([col] * (width // LANES), axis=1) if width > LANES else col


def _ada_kernel(cond_t_ref, w_ref, b_ref, o_ref):
    s = _silu(cond_t_ref[...])
    w = w_ref[0]
    n_rows = o_ref.shape[1]
    for r in range(n_rows):
        o_ref[0, r:r + 1, :] = jnp.sum(w * s[:, r:r + 1], axis=0, keepdims=True) + b_ref[0]


def _ada(cond, w_ada, b_ada):
    depth, d, n3 = w_ada.shape
    rows = cond.shape[0]
    tn = max(w for w in range(LANES, 1024 + 1, LANES) if n3 % w == 0)
    return pl.pallas_call(
        _ada_kernel,
        grid=(depth, n3 // tn),
        in_specs=[
            pl.BlockSpec((d, rows), lambda l, j: (0, 0)),
            pl.BlockSpec((1, d, tn), lambda l, j: (l, 0, j)),
            pl.BlockSpec((1, 1, tn), lambda l, j: (l, 0, j)),
        ],
        out_specs=pl.BlockSpec((1, rows, tn), lambda l, j: (l, 0, j)),
        out_shape=jax.ShapeDtypeStruct((depth, rows, n3), F32),
        compiler_params=_cparams(("parallel", "parallel")),
        name="ada_mod",
    )(cond.T, w_ada, b_ada.reshape(depth, 1, n3))


def _norm_mod(xv, g, shift, scale):
    y = xv * lax.rsqrt(jnp.mean(xv * xv, axis=-1, keepdims=True) + EPS) * g
    return y * (1.0 + scale) + shift


def _scan_along_lanes(x, fwd_rows, lane, combine, fill):
    l = x.shape[1]
    step = 1
    while step < l:
        from_left = jnp.where(lane >= step, pltpu.roll(x, step, axis=1), fill)
        from_right = jnp.where(lane < l - step, pltpu.roll(x, l - step, axis=1), fill)
        x = combine(x, jnp.where(fwd_rows, from_left, from_right))
        step *= 2
    return x


def _gate_vectors(pre_chunks):
    h = N_HEADS
    l = pre_chunks[0].shape[0]
    pt = jnp.concatenate([p.T[0:4 * h, :] for p in pre_chunks], axis=0)
    row = lax.broadcasted_iota(jnp.int32, pt.shape, 0) % (4 * h)
    lane = lax.broadcasted_iota(jnp.int32, pt.shape, 1)
    fwd_rows = row < 2 * h
    log_f = jnp.minimum(pt, 0.0) - jnp.log1p(jnp.exp(-jnp.abs(pt)))
    b = _scan_along_lanes(log_f, fwd_rows, lane, jnp.add, 0.0)
    a = pt - pltpu.roll(b, pt.shape[0] - h, axis=0)
    cmax = _scan_along_lanes(a, fwd_rows, lane, jnp.maximum, -jnp.inf)
    rows = jnp.where((row % (2 * h)) < h, a, b)
    out = []
    for ch in range(len(pre_chunks)):
        rs = slice(ch * 4 * h, (ch + 1) * 4 * h)
        stacked = jnp.concatenate([rows[rs], cmax[rs], jnp.zeros((LANES - 8 * h, l), F32)], axis=0)
        cols = stacked.T
        reps = []
        for direction in range(2):
            for head in range(h):
                j = 4 * h + direction * 2 * h + head
                reps.append(jnp.broadcast_to(cols[:, j:j + 1], (l, LANES)))
        out.append((cols, rows[rs], jnp.concatenate(reps, axis=1)))
    return out


def _mlstm_pre_kernel(x_ref, halo_ref, mod_ref, g_ref, win_ref, convw_ref, convb_ref,
                      bdq_ref, bdk_ref, bdv_ref, wg_ref, bg_ref, hnorm_ref, skip_ref,
                      q_ref, k_ref, kt_ref, v_ref, g1_ref, g2_ref, pre_ref,
                      hn_scr, xm_scr, *, k_scale):
    i = pl.program_id(1)
    nt = pl.num_programs(1)
    tm = x_ref.shape[1]
    e = q_ref.shape[2]
    shift = mod_ref[0, 0:1, :]
    scale = mod_ref[0, 1:2, :]
    g = g_ref[...]

    hn_scr[0:HALO, :] = _norm_mod(halo_ref[0, 0, 0:HALO, :], g, shift, scale).astype(BF16)
    hn_scr[HALO:HALO + tm, :] = _norm_mod(x_ref[0], g, shift, scale).astype(BF16)
    hn_scr[HALO + tm:, :] = _norm_mod(halo_ref[0, 0, HALO:, :], g, shift, scale).astype(BF16)

    pre = jnp.zeros((tm, LANES), F32) + bg_ref[...]
    for c in range(e // MXU_DIM):
        cs = slice(c * MXU_DIM, (c + 1) * MXU_DIM)
        xm_ext = _dot(hn_scr[...], win_ref[:, cs])
        xm_scr[0:HALO, :] = jnp.where(i > 0, xm_ext[0:HALO], 0.0)
        xm_scr[HALO:HALO + tm, :] = xm_ext[HALO:HALO + tm]
        xm_scr[HALO + tm:, :] = jnp.where(i < nt - 1, xm_ext[HALO + tm:], 0.0)
        hc = hn_scr[HALO:HALO + tm, :]
        z = _dot(hc, win_ref[:, e + c * MXU_DIM:e + (c + 1) * MXU_DIM])
        og = _dot(hc, win_ref[:, 2 * e + c * MXU_DIM:2 * e + (c + 1) * MXU_DIM])

        xm_all = xm_scr[...]
        xm = xm_all[HALO:HALO + tm]
        xm_prev = pltpu.roll(xm_all, 1, axis=0)[HALO:HALO + tm]
        xm_next = pltpu.roll(xm_all, tm + 2 * HALO - 1, axis=0)[HALO:HALO + tm]
        conv = (xm_prev * convw_ref[0:1, cs] + xm * convw_ref[1:2, cs] + xm_next * convw_ref[2:3, cs]
                + convb_ref[:, cs])
        xc = _silu(conv)
        xc_b = xc.astype(BF16)
        q = _dot(xc_b, bdq_ref[c]).astype(BF16)
        k32 = _dot(xc_b, bdk_ref[c]) * k_scale
        k = k32.astype(BF16)
        v = _dot(xm.astype(BF16), bdv_ref[c]).astype(BF16)
        pre = (pre + _dot(q, wg_ref[cs, :])
               + _dot(k, wg_ref[e + c * MXU_DIM:e + (c + 1) * MXU_DIM, :])
               + _dot(v, wg_ref[2 * e + c * MXU_DIM:2 * e + (c + 1) * MXU_DIM, :]))
        sz = _silu(z)
        q_ref[0, :, cs] = q
        k_ref[0, :, cs] = k
        kt_ref[0, cs, :] = k32.T.astype(BF16)
        v_ref[0, :, cs] = v
        g1_ref[0, :, cs] = (hnorm_ref[:, cs] * jax.nn.sigmoid(og) * sz).astype(BF16)
        g2_ref[0, :, cs] = (skip_ref[:, cs] * xc * sz).astype(BF16)

    pre_ref[0] = pre


def _mlstm_pre(x, mod, pre_g, w_in_b, conv_w, conv_b, bdq, bdk, bdv, wg, bg, head_norm, skip):
    b, t, d = x.shape
    e = w_in_b.shape[1] // 3
    tm = SCAN_CHUNK
    nt = t // tm
    xr = x.reshape(b, nt, tm, d)
    zeros = jnp.zeros((b, 1, HALO, d), x.dtype)
    prev = jnp.concatenate([zeros, xr[:, :-1, tm - HALO:, :]], axis=1)
    nxt = jnp.concatenate([xr[:, 1:, :HALO, :], zeros], axis=1)
    halo = jnp.concatenate([prev, nxt], axis=2)

    row = lambda bi, i: (bi, i, 0)
    big = pl.BlockSpec((1, tm, e), row)
    big_t = pl.BlockSpec((1, e, tm), lambda bi, i: (bi, 0, i))
    kernel = functools.partial(_mlstm_pre_kernel, k_scale=float((e // N_HEADS) ** -0.5))
    tok = jax.ShapeDtypeStruct((b, t, e), BF16)
    return pl.pallas_call(
        kernel,
        grid=(b, nt),
        in_specs=[
            pl.BlockSpec((1, tm, d), row),
            pl.BlockSpec((1, 1, 2 * HALO, d), lambda bi, i: (bi, i, 0, 0)),
            pl.BlockSpec((1, 3, d), lambda bi, i: (bi, 0, 0)),
            _const_spec((1, d)),
            _const_spec(w_in_b.shape),
            _const_spec(conv_w.shape),
            _const_spec((1, e)),
            _const_spec(bdq.shape), _const_spec(bdk.shape), _const_spec(bdv.shape),
            _const_spec(wg.shape), _const_spec((1, LANES)),
            _const_spec((1, e)), _const_spec((1, e)),
        ],
        out_specs=[big, big, big_t, big, big, big, pl.BlockSpec((1, tm, LANES), row)],
        out_shape=[tok, tok, jax.ShapeDtypeStruct((b, e, t), BF16), tok, tok, tok,
                   jax.ShapeDtypeStruct((b, t, LANES), F32)],
        scratch_shapes=[pltpu.VMEM((tm + 2 * HALO, d), BF16),
                        pltpu.VMEM((tm + 2 * HALO, MXU_DIM), F32)],
        compiler_params=_cparams(("parallel", "parallel")),
        name="mlstm_pre",
    )(x, halo, mod, pre_g.reshape(1, d), w_in_b, conv_w, conv_b.reshape(1, e),
      bdq, bdk, bdv, wg, bg, head_norm.reshape(1, e), skip.reshape(1, e))


def _gate_prep_kernel(pre_ref, cols_ref, rows_ref, cmrep_ref):
    nch = pre_ref.shape[1] // SCAN_CHUNK
    chunks = [pre_ref[0, ch * SCAN_CHUNK:(ch + 1) * SCAN_CHUNK, :] for ch in range(nch)]
    for ch, (cols, rows, cmrep) in enumerate(_gate_vectors(chunks)):
        rs = slice(ch * SCAN_CHUNK, (ch + 1) * SCAN_CHUNK)
        cols_ref[0, rs, :] = cols
        rows_ref[0, ch] = rows
        cmrep_ref[0, rs, :] = cmrep


def _gate_prep(pre):
    b, t, _ = pre.shape
    nch = min(GATE_CHUNKS_PER_STEP, t // SCAN_CHUNK)
    tm = nch * SCAN_CHUNK
    row = lambda bi, i: (bi, i, 0)
    return pl.pallas_call(
        _gate_prep_kernel,
        grid=(b, t // tm),
        in_specs=[pl.BlockSpec((1, tm, LANES), row)],
        out_specs=[pl.BlockSpec((1, tm, LANES), row),
                   pl.BlockSpec((1, nch, 4 * N_HEADS, SCAN_CHUNK), lambda bi, i: (bi, i, 0, 0)),
                   pl.BlockSpec((1, tm, 2 * N_HEADS * LANES), row)],
        out_shape=[jax.ShapeDtypeStruct((b, t, LANES), F32),
                   jax.ShapeDtypeStruct((b, t // SCAN_CHUNK, 4 * N_HEADS, SCAN_CHUNK), F32),
                   jax.ShapeDtypeStruct((b, t, 2 * N_HEADS * LANES), F32)],
        compiler_params=_cparams(("parallel", "parallel")),
        name="gate_prep",
    )(pre)


def _scan_chunk(get, cols, rows, cmrep, c_st, n_st, m_st, backward, emit):
    h = N_HEADS
    l = cols.shape[0]
    dh = c_st.shape[-1]
    off = 2 * h if backward else 0
    last = 0 if backward else l - 1
    st = []
    for head in range(h):
        cm_col = cols[:, 4 * h + off + head:4 * h + off + head + 1]
        st.append(dict(
            a_col=cols[:, off + head:off + head + 1],
            b_col=cols[:, off + h + head:off + h + head + 1],
            a_row=rows[off + head:off + head + 1, :],
            total=cols[last:last + 1, off + h + head:off + h + head + 1],
            amax=cm_col[last:last + 1, :],
            m_prev=m_st[head]))

    if emit is not None:
        ti = lax.broadcasted_iota(jnp.int32, (l, l), 0)
        si = lax.broadcasted_iota(jnp.int32, (l, l), 1)
        mask = (si >= ti) if backward else (ti >= si)
        for head in range(h):
            s = st[head]
            q = get("q", head)
            s["u"] = jnp.maximum(s["m_prev"], cmrep[:, head * LANES:(head + 1) * LANES])
            s["p"] = jnp.exp(jnp.where(mask, s["a_row"] - _lane_repeat(s["u"], l), -jnp.inf))
            s["qk"] = _dot(q, get("kt", head))
            s["qc"] = _dot(q, c_st[head].astype(BF16))
        for head in range(h):
            s = st[head]
            sc = s["qk"] * s["p"]
            w_inter = jnp.exp(s["m_prev"] - s["u"])
            num = _dot(sc.astype(BF16), get("v", head)) + _lane_repeat(w_inter, dh) * s["qc"]
            qn = jnp.sum(get("q", head).astype(F32) * n_st[head], axis=-1, keepdims=True)
            den = jnp.sum(sc, axis=-1, keepdims=True) + w_inter[:, 0:1] * qn
            floor = jnp.exp(-(s["b_col"] + s["u"][:, 0:1]))
            emit(head, num / jnp.maximum(jnp.abs(den), floor))

    for head in range(h):
        s = st[head]
        s["u_last"] = jnp.maximum(s["m_prev"], s["amax"])
        s["decay"] = jnp.exp(s["m_prev"] - s["u_last"])
        wkt = (get("kt", head).astype(F32) * jnp.exp(s["a_row"] - s["u_last"])).astype(BF16)
        s["upd"] = _dot(wkt, get("v", head))
    for head in range(h):
        s = st[head]
        wk = get("k", head).astype(F32) * jnp.exp(s["a_col"] - s["u_last"])
        c_st[head] = s["decay"] * c_st[head] + s["upd"]
        n_st[head] = s["decay"] * n_st[head] + jnp.sum(wk, axis=0, keepdims=True)
        m_st[head] = s["total"] + s["u_last"]


def _tile_getter(refs, dh, rows=slice(None)):
    def get(name, head):
        hs = slice(head * dh, (head + 1) * dh)
        if name == "kt":
            return refs["kt"][0, hs, rows]
        return refs[name][0, rows, hs]
    return get


def _ctx_state_kernel(k_ref, kt_ref, v_ref, cols_ref, rows_ref, c_out, n_out, m_out):
    dh = c_out.shape[-1]
    nc = k_ref.shape[1] // SCAN_CHUNK
    c_out[...] = jnp.zeros(c_out.shape, F32)
    n_out[...] = jnp.zeros(n_out.shape, F32)
    m_out[...] = jnp.zeros(m_out.shape, F32)
    refs = dict(k=k_ref, kt=kt_ref, v=v_ref)
    for direction in range(2):
        for step in range(nc):
            c = nc - 1 - step if direction else step
            rs = slice(c * SCAN_CHUNK, (c + 1) * SCAN_CHUNK)
            _scan_chunk(_tile_getter(refs, dh, rs), cols_ref[0, rs, :], rows_ref[0, c], None,
                        c_out.at[0, direction], n_out.at[0, direction], m_out.at[0, direction],
                        bool(direction), None)


def _ctx_states(k, kt, v, cols, rows):
    b, t, e = k.shape
    dh = e // N_HEADS
    nc = t // SCAN_CHUNK
    per_b = lambda nd: (lambda bi: (bi,) + (0,) * (nd - 1))
    return pl.pallas_call(
        _ctx_state_kernel,
        grid=(b,),
        in_specs=[pl.BlockSpec((1, t, e), per_b(3)), pl.BlockSpec((1, e, t), per_b(3)),
                  pl.BlockSpec((1, t, e), per_b(3)),
                  pl.BlockSpec((1, t, LANES), per_b(3)),
                  pl.BlockSpec((1, nc, 4 * N_HEADS, SCAN_CHUNK), per_b(4))],
        out_specs=[pl.BlockSpec((1, 2, N_HEADS, dh, dh), per_b(5)),
                   pl.BlockSpec((1, 2, N_HEADS, 1, dh), per_b(5)),
                   pl.BlockSpec((1, 2, N_HEADS, 1, 1), per_b(5))],
        out_shape=[jax.ShapeDtypeStruct((b, 2, N_HEADS, dh, dh), F32),
                   jax.ShapeDtypeStruct((b, 2, N_HEADS, 1, dh), F32),
                   jax.ShapeDtypeStruct((b, 2, N_HEADS, 1, 1), F32)],
        compiler_params=_cparams(("parallel",)),
        name="ctx_states",
    )(k, kt, v, cols, rows)


def _load_state(c0_ref, n0_ref, m0_ref, c_scr, n_scr, m_scr):
    @pl.when(pl.program_id(1) == 0)
    def _():
        c_scr[...] = c0_ref[0, 0]
        n_scr[...] = n0_ref[0, 0]
        m_scr[...] = m0_ref[0, 0]


def _scan_bwd_kernel(q_ref, k_ref, kt_ref, v_ref, cols_ref, rows_ref, cmrep_ref, c0_ref, n0_ref, m0_ref,
                     hb_ref, c_scr, n_scr, m_scr):
    _load_state(c0_ref, n0_ref, m0_ref, c_scr, n_scr, m_scr)
    dh = c_scr.shape[-1]

    def emit(head, hb):
        hb_ref[0, :, head * dh:(head + 1) * dh] = hb.astype(hb_ref.dtype)

    _scan_chunk(_tile_getter(dict(q=q_ref, k=k_ref, kt=kt_ref, v=v_ref), dh), cols_ref[0], rows_ref[0, 0],
                cmrep_ref[0], c_scr, n_scr, m_scr, True, emit)


def _scan_fwd_out_kernel(q_ref, k_ref, kt_ref, v_ref, cols_ref, rows_ref, cmrep_ref, c0_ref, n0_ref, m0_ref,
                         hb_ref, g1_ref, g2_ref, wout_ref, x_ref, mod_ref, postg_ref,
                         o_ref, c_scr, n_scr, m_scr, gated_scr):
    _load_state(c0_ref, n0_ref, m0_ref, c_scr, n_scr, m_scr)
    dh = c_scr.shape[-1]

    def emit(head, hf):
        hs = slice(head * dh, (head + 1) * dh)
        hsum = hf + hb_ref[0, :, hs].astype(F32)
        cen = hsum - jnp.mean(hsum, axis=-1, keepdims=True)
        hn = cen * lax.rsqrt(jnp.mean(cen * cen, axis=-1, keepdims=True) + EPS)
        gated_scr[:, hs] = (hn * g1_ref[0, :, hs].astype(F32) + g2_ref[0, :, hs].astype(F32)).astype(BF16)

    _scan_chunk(_tile_getter(dict(q=q_ref, k=k_ref, kt=kt_ref, v=v_ref), dh), cols_ref[0], rows_ref[0, 0],
                cmrep_ref[0], c_scr, n_scr, m_scr, False, emit)
    y = _dot(gated_scr[...], wout_ref[...])
    yn = y * lax.rsqrt(jnp.mean(y * y, axis=-1, keepdims=True) + EPS) * postg_ref[...]
    o_ref[0] = x_ref[0] + mod_ref[0, 2:3, :] * yn


def _scan_specs(b, t, e, nc, l, reverse):
    dh = e // N_HEADS
    direction = 1 if reverse else 0
    cidx = (lambda c: nc - 1 - c) if reverse else (lambda c: c)
    tok = lambda w: pl.BlockSpec((1, l, w), lambda bi, c: (bi, cidx(c), 0))
    state = lambda shape: pl.BlockSpec((1, 1) + shape, lambda bi, c: (bi, direction) + (0,) * len(shape),
                                       pipeline_mode=pl.Buffered(1))
    specs = [tok(e), tok(e), pl.BlockSpec((1, e, l), lambda bi, c: (bi, 0, cidx(c))), tok(e), tok(LANES),
             pl.BlockSpec((1, 1, 4 * N_HEADS, l), lambda bi, c: (bi, cidx(c), 0, 0)),
             pl.BlockSpec((1, l, N_HEADS * LANES), lambda bi, c: (bi, cidx(c), direction)),
             state((N_HEADS, dh, dh)), state((N_HEADS, 1, dh)), state((N_HEADS, 1, 1))]
    scratch = [pltpu.VMEM((N_HEADS, dh, dh), F32), pltpu.VMEM((N_HEADS, 1, dh), F32),
               pltpu.VMEM((N_HEADS, 1, 1), F32)]
    return specs, scratch, tok


def _scan_bwd(q, k, kt, v, cols, rows, cmrep, c0, n0, m0):
    b, t, e = q.shape
    l = SCAN_CHUNK
    nc = t // l
    specs, scratch, tok = _scan_specs(b, t, e, nc, l, True)
    return pl.pallas_call(
        _scan_bwd_kernel,
        grid=(b, nc),
        in_specs=specs,
        out_specs=tok(e),
        out_shape=jax.ShapeDtypeStruct((b, t, e), BF16),
        scratch_shapes=scratch,
        compiler_params=_cparams(("parallel", "arbitrary")),
        name="mlstm_scan_bwd",
    )(q, k, kt, v, cols, rows, cmrep, c0, n0, m0)


def _scan_fwd_out(q, k, kt, v, cols, rows, cmrep, c0, n0, m0, hb, g1, g2, w_out_b, x, mod, post_g):
    b, t, e = q.shape
    d = x.shape[-1]
    l = SCAN_CHUNK
    nc = t // l
    specs, scratch, tok = _scan_specs(b, t, e, nc, l, False)
    specs = specs + [tok(e), tok(e), tok(e), _const_spec(w_out_b.shape), tok(d),
                     pl.BlockSpec((1, 3, d), lambda bi, c: (bi, 0, 0)), _const_spec((1, d))]
    return pl.pallas_call(
        _scan_fwd_out_kernel,
        grid=(b, nc),
        in_specs=specs,
        out_specs=tok(d),
        out_shape=jax.ShapeDtypeStruct((b, t, d), F32),
        scratch_shapes=scratch + [pltpu.VMEM((l, e), BF16)],
        compiler_params=_cparams(("parallel", "arbitrary")),
        name="mlstm_scan_fwd_out",
    )(q, k, kt, v, cols, rows, cmrep, c0, n0, m0, hb, g1, g2, w_out_b, x, mod, post_g.reshape(1, d))


def _pool_pre_kernel(x_ref, mod_ref, g_ref, win_ref, u_ref, sz_ref):
    e = u_ref.shape[2]
    hn = _norm_mod(x_ref[0], g_ref[...], mod_ref[0, 0:1, :], mod_ref[0, 1:2, :]).astype(BF16)
    for c in range(e // MXU_DIM):
        cs = slice(c * MXU_DIM, (c + 1) * MXU_DIM)
        u_ref[0, :, cs] = _dot(hn, win_ref[:, cs]).astype(BF16)
        sz_ref[0, :, cs] = _silu(_dot(hn, win_ref[:, e + c * MXU_DIM:e + (c + 1) * MXU_DIM])).astype(BF16)


def _pool_pre(x, mod, pre_g, w_in_b):
    b, t, d = x.shape
    e = w_in_b.shape[1] // 2
    tm = POOL_TILE
    row = lambda bi, i: (bi, i, 0)
    return pl.pallas_call(
        _pool_pre_kernel,
        grid=(b, t // tm),
        in_specs=[pl.BlockSpec((1, tm, d), row), pl.BlockSpec((1, 3, d), lambda bi, i: (bi, 0, 0)),
                  _const_spec((1, d)), _const_spec(w_in_b.shape)],
        out_specs=[pl.BlockSpec((1, tm, e), row)] * 2,
        out_shape=[jax.ShapeDtypeStruct((b, t, e), BF16)] * 2,
        compiler_params=_cparams(("parallel", "parallel")),
        name="pool_pre",
    )(x, mod, pre_g.reshape(1, d), w_in_b)


def _pool_out_kernel(up_ref, uc_ref, un_ref, sz_ref, band_ref, icnt_ref, wpool_ref, pscale_ref,
                     wout_ref, x_ref, mod_ref, postg_ref, o_ref, ext_scr):
    i = pl.program_id(1)
    nt = pl.num_programs(1)
    tm = uc_ref.shape[1]
    gd = wpool_ref.shape[1]
    y = jnp.zeros(o_ref.shape[1:], F32)
    for g, w in enumerate(POOL_WINDOWS):
        gs = slice(g * gd, (g + 1) * gd)
        ext_scr[0:tm, :] = jnp.where(i > 0, up_ref[0, :, gs].astype(F32), 0.0)
        ext_scr[tm:2 * tm, :] = uc_ref[0, :, gs].astype(F32)
        ext_scr[2 * tm:, :] = jnp.where(i < nt - 1, un_ref[0, :, gs].astype(F32), 0.0)
        rsum = jnp.zeros((tm, gd), F32)
        for o in range(-(w // 2), w - w // 2):
            rsum = rsum + ext_scr[tm + o * GRID_W:2 * tm + o * GRID_W, :]
        rsum = rsum.astype(BF16)
        parts = [_dot(band_ref[g], rsum[s * MXU_DIM:(s + 1) * MXU_DIM, :]) for s in range(tm // MXU_DIM)]
        mean = jnp.concatenate(parts, axis=0) * icnt_ref[:, g:g + 1]
        diff = mean - ext_scr[tm:2 * tm, :]
        yg = _dot(diff.astype(BF16), wpool_ref[g]) * pscale_ref[:, gs] * sz_ref[0, :, gs].astype(F32)
        y = y + _dot(yg.astype(BF16), wout_ref[gs, :])
    yn = y * lax.rsqrt(jnp.mean(y * y, axis=-1, keepdims=True) + EPS) * postg_ref[...]
    o_ref[0] = x_ref[0] + mod_ref[0, 2:3, :] * yn


def _pool_constants(t):
    rows = t // GRID_W
    bands = np.zeros((len(POOL_WINDOWS), MXU_DIM, MXU_DIM), np.float32)
    icnt = np.zeros((t, LANES), np.float32)
    for g, w in enumerate(POOL_WINDOWS):
        lo_off, hi_off = -(w // 2), w - w // 2
        for p in range(MXU_DIM):
            c = p % GRID_W
            base = p - c
            lo, hi = max(c + lo_off, 0), min(c + hi_off, GRID_W)
            bands[g, p, base + lo:base + hi] = 1.0
        cc = np.arange(GRID_W)
        cnt_c = np.clip(cc + hi_off, 0, GRID_W) - np.clip(cc + lo_off, 0, GRID_W)
        rr = np.arange(rows)
        cnt_r = np.clip(rr + hi_off, 0, rows) - np.clip(rr + lo_off, 0, rows)
        icnt[:, g] = (1.0 / (cnt_r[:, None] * cnt_c[None, :])).reshape(-1)
    return bands, icnt


def _pool_out(u, sz, w_pool_b, pool_scale, w_out_b, x, mod, post_g):
    b, t, e = u.shape
    d = x.shape[-1]
    tm = POOL_TILE
    assert tm >= (max(POOL_WINDOWS) // 2) * GRID_W and tm % MXU_DIM == 0 and t % tm == 0
    nt = t // tm
    gd = e // len(POOL_WINDOWS)
    bands, icnt = _pool_constants(t)
    row = lambda bi, i: (bi, i, 0)
    return pl.pallas_call(
        _pool_out_kernel,
        grid=(b, nt),
        in_specs=[
            pl.BlockSpec((1, tm, e), lambda bi, i: (bi, jnp.maximum(i - 1, 0), 0)),
            pl.BlockSpec((1, tm, e), row),
            pl.BlockSpec((1, tm, e), lambda bi, i: (bi, jnp.minimum(i + 1, nt - 1), 0)),
            pl.BlockSpec((1, tm, e), row),
            _const_spec(bands.shape),
            pl.BlockSpec((tm, LANES), lambda bi, i: (i, 0)),
            _const_spec(w_pool_b.shape),
            _const_spec((1, e)),
            _const_spec(w_out_b.shape),
            pl.BlockSpec((1, tm, d), row),
            pl.BlockSpec((1, 3, d), lambda bi, i: (bi, 0, 0)),
            _const_spec((1, d)),
        ],
        out_specs=pl.BlockSpec((1, tm, d), row),
        out_shape=jax.ShapeDtypeStruct((b, t, d), F32),
        scratch_shapes=[pltpu.VMEM((3 * tm, gd), F32)],
        compiler_params=_cparams(("parallel", "parallel")),
        name="pool_out",
    )(u, u, u, sz, jnp.asarray(bands, BF16), jnp.asarray(icnt), w_pool_b, pool_scale.reshape(1, e),
      w_out_b, x, mod, post_g.reshape(1, d))


def _block_diag_tiles(w):
    per = MXU_DIM // QKV_BLOCK
    rows = w.reshape(w.shape[0] // per, MXU_DIM, QKV_BLOCK)
    tiled = jnp.tile(rows, (1, 1, per))
    blk = np.arange(MXU_DIM) // QKV_BLOCK
    return jnp.where(blk[:, None] == blk[None, :], tiled, 0.0).astype(BF16)


def _gate_weights(w_f, b_f, w_b, b_b):
    pad = LANES - 4 * N_HEADS
    wg = jnp.pad(jnp.concatenate([w_f, w_b], axis=1), ((0, 0), (0, pad)))
    bg = jnp.pad(jnp.concatenate([b_f, b_b]), (0, pad)).reshape(1, LANES)
    return wg.astype(BF16), bg


def kernel(x, c, ctx, c_ctx, w_ada, b_ada, a_norm_pre, a_norm_post, a_w_in, a_conv_w, a_conv_b,
           a_w_q, a_w_k, a_w_v, a_w_gate_f, a_b_gate_f, a_w_gate_b, a_b_gate_b, a_head_norm, a_skip,
           a_w_out, b_norm_pre, b_norm_post, b_w_in, b_w_pool, b_pool_scale, b_w_out):
    bsz, t, d = x.shape
    t_ctx = ctx.shape[1]
    assert t % SCAN_CHUNK == 0 and t_ctx % SCAN_CHUNK == 0 and t % GRID_W == 0

    n_cond = -(-(bsz + 1) // 8) * 8
    cond = jnp.concatenate([c, c_ctx[None], jnp.zeros((n_cond - bsz - 1, d), F32)], axis=0)
    mods = _ada(cond, w_ada, b_ada).reshape(w_ada.shape[0], n_cond, 3, d)
    mod_lat = [mods[i, :bsz] for i in range(w_ada.shape[0])]
    mod_ctx = jnp.broadcast_to(mods[0, bsz][None], (bsz, 3, d))

    w_in_b = a_w_in[0].astype(BF16)
    bdq, bdk, bdv = (_block_diag_tiles(w[0]) for w in (a_w_q, a_w_k, a_w_v))
    wg, bg = _gate_weights(a_w_gate_f[0], a_b_gate_f[0], a_w_gate_b[0], a_b_gate_b[0])
    pre_args = (a_norm_pre[0], w_in_b, a_conv_w[0], a_conv_b[0], bdq, bdk, bdv, wg, bg,
                a_head_norm[0], a_skip[0])
    _, kc, ktc, vc, _, _, pre_c = _mlstm_pre(ctx, mod_ctx, *pre_args)
    cols_c, rows_c, _ = _gate_prep(pre_c)
    c0, n0, m0 = _ctx_states(kc, ktc, vc, cols_c, rows_c)

    ql, kl, ktl, vl, g1, g2, pre_l = _mlstm_pre(x, mod_lat[0], *pre_args)
    cols_l, rows_l, cmrep_l = _gate_prep(pre_l)
    scan_in = (ql, kl, ktl, vl, cols_l, rows_l, cmrep_l, c0, n0, m0)
    hb = _scan_bwd(*scan_in)
    x = _scan_fwd_out(*scan_in, hb, g1, g2, a_w_out[0].astype(BF16), x, mod_lat[0], a_norm_post[0])

    u, sz = _pool_pre(x, mod_lat[1], b_norm_pre[0], b_w_in[0].astype(BF16))
    x = _pool_out(u, sz, b_w_pool[0].astype(BF16), b_pool_scale[0], b_w_out[0].astype(BF16),
                  x, mod_lat[1], b_norm_post[0])
    return x
```

```python
import functools

import numpy as np
import jax
import jax.numpy as jnp
from jax import lax
from jax.experimental import pallas as pl
from jax.experimental.pallas import tpu as pltpu

EPS = 1e-6
N_HEADS = 4
QKV_BLOCK = 4
GRID_W = 64
POOL_WINDOWS = (2, 4, 8, 16)

LANES = 128
MXU_DIM = 256
HALO = 16
VMEM_LIMIT_BYTES = 56 * 1024 * 1024

SCAN_CHUNK = 256
PRE_TILE = 512
POOL_TILE = 512
GATE_CHUNKS_PER_STEP = 8

BF16 = jnp.bfloat16
F32 = jnp.float32


def _dot(a, b):
    return jnp.dot(a, b, preferred_element_type=F32)


def _silu(v):
    return v * jax.nn.sigmoid(v)


def _cparams(sem):
    return pltpu.CompilerParams(dimension_semantics=sem, vmem_limit_bytes=VMEM_LIMIT_BYTES)


def _const_spec(shape):
    nd = len(shape)
    return pl.BlockSpec(shape, lambda *_: (0,) * nd, pipeline_mode=pl.Buffered(1))


def _lane_repeat(col, width):
    return jnp.concatenate([col] * (width // LANES), axis=1) if width > LANES else col


def _ada_kernel(cond_t_ref, w_ref, b_ref, o_ref):
    s = _silu(cond_t_ref[...])
    w = w_ref[0]
    n_rows = o_ref.shape[1]
    for r in range(n_rows):
        o_ref[0, r:r + 1, :] = jnp.sum(w * s[:, r:r + 1], axis=0, keepdims=True) + b_ref[0]


def _ada(cond, w_ada, b_ada):
    depth, d, n3 = w_ada.shape
    rows = cond.shape[0]
    tn = max(w for w in range(LANES, 1024 + 1, LANES) if n3 % w == 0)
    return pl.pallas_call(
        _ada_kernel,
        grid=(depth, n3 // tn),
        in_specs=[
            pl.BlockSpec((d, rows), lambda l, j: (0, 0)),
            pl.BlockSpec((1, d, tn), lambda l, j: (l, 0, j)),
            pl.BlockSpec((1, 1, tn), lambda l, j: (l, 0, j)),
        ],
        out_specs=pl.BlockSpec((1, rows, tn), lambda l, j: (l, 0, j)),
        out_shape=jax.ShapeDtypeStruct((depth, rows, n3), F32),
        compiler_params=_cparams(("parallel", "parallel")),
        name="ada_mod",
    )(cond.T, w_ada, b_ada.reshape(depth, 1, n3))


def _norm_mod(xv, g, shift, scale):
    y = xv * lax.rsqrt(jnp.mean(xv * xv, axis=-1, keepdims=True) + EPS) * g
    return y * (1.0 + scale) + shift


def _scan_along_lanes(x, fwd_rows, lane, combine, fill):
    l = x.shape[1]
    step = 1
    while step < l:
        from_left = jnp.where(lane >= step, pltpu.roll(x, step, axis=1), fill)
        from_right = jnp.where(lane < l - step, pltpu.roll(x, l - step, axis=1), fill)
        x = combine(x, jnp.where(fwd_rows, from_left, from_right))
        step *= 2
    return x


def _gate_vectors(pre_chunks):
    h = N_HEADS
    l = pre_chunks[0].shape[0]
    pt = jnp.concatenate([p.T[0:4 * h, :] for p in pre_chunks], axis=0)
    row = lax.broadcasted_iota(jnp.int32, pt.shape, 0) % (4 * h)
    lane = lax.broadcasted_iota(jnp.int32, pt.shape, 1)
    fwd_rows = row < 2 * h
    log_f = jnp.minimum(pt, 0.0) - jnp.log1p(jnp.exp(-jnp.abs(pt)))
    b = _scan_along_lanes(log_f, fwd_rows, lane, jnp.add, 0.0)
    a = pt - pltpu.roll(b, pt.shape[0] - h, axis=0)
    cmax = _scan_along_lanes(a, fwd_rows, lane, jnp.maximum, -jnp.inf)
    rows = jnp.where((row % (2 * h)) < h, a, b)
    out = []
    for ch in range(len(pre_chunks)):
        rs = slice(ch * 4 * h, (ch + 1) * 4 * h)
        stacked = jnp.concatenate([rows[rs], cmax[rs], jnp.zeros((LANES - 8 * h, l), F32)], axis=0)
        cols = stacked.T
        reps = []
        for direction in range(2):
            for head in range(h):
                j = 4 * h + direction * 2 * h + head
                reps.append(jnp.broadcast_to(cols[:, j:j + 1], (l, LANES)))
        out.append((cols, rows[rs], jnp.concatenate(reps, axis=1)))
    return out


def _mlstm_pre_kernel(x_ref, halo_ref, mod_ref, g_ref, win_ref, convw_ref, convb_ref,
                      bdq_ref, bdk_ref, bdv_ref, wg_ref, bg_ref, hnorm_ref, skip_ref,
                      q_ref, k_ref, kt_ref, v_ref, g1_ref, g2_ref, pre_ref,
                      hn_scr, xm_scr, *, k_scale):
    i = pl.program_id(1)
    nt = pl.num_programs(1)
    tm = x_ref.shape[1]
    e = q_ref.shape[2]
    shift = mod_ref[0, 0:1, :]
    scale = mod_ref[0, 1:2, :]
    g = g_ref[...]

    hn_scr[0:HALO, :] = _norm_mod(halo_ref[0, 0, 0:HALO, :], g, shift, scale).astype(BF16)
    hn_scr[HALO:HALO + tm, :] = _norm_mod(x_ref[0], g, shift, scale).astype(BF16)
    hn_scr[HALO + tm:, :] = _norm_mod(halo_ref[0, 0, HALO:, :], g, shift, scale).astype(BF16)

    pre = jnp.zeros((tm, LANES), F32) + bg_ref[...]
    for c in range(e // MXU_DIM):
        cs = slice(c * MXU_DIM, (c + 1) * MXU_DIM)
        xm_ext = _dot(hn_scr[...], win_ref[:, cs])
        xm_scr[0:HALO, :] = jnp.where(i > 0, xm_ext[0:HALO], 0.0)
        xm_scr[HALO:HALO + tm, :] = xm_ext[HALO:HALO + tm]
        xm_scr[HALO + tm:, :] = jnp.where(i < nt - 1, xm_ext[HALO + tm:], 0.0)
        hc = hn_scr[HALO:HALO + tm, :]
        z = _dot(hc, win_ref[:, e + c * MXU_DIM:e + (c + 1) * MXU_DIM])
        og = _dot(hc, win_ref[:, 2 * e + c * MXU_DIM:2 * e + (c + 1) * MXU_DIM])

        xm_all = xm_scr[...]
        xm = xm_all[HALO:HALO + tm]
        xm_prev = pltpu.roll(xm_all, 1, axis=0)[HALO:HALO + tm]
        xm_next = pltpu.roll(xm_all, tm + 2 * HALO - 1, axis=0)[HALO:HALO + tm]
        conv = (xm_prev * convw_ref[0:1, cs] + xm * convw_ref[1:2, cs] + xm_next * convw_ref[2:3, cs]
                + convb_ref[:, cs])
        xc = _silu(conv)
        xc_b = xc.astype(BF16)
        q = _dot(xc_b, bdq_ref[c]).astype(BF16)
        k32 = _dot(xc_b, bdk_ref[c]) * k_scale
        k = k32.astype(BF16)
        v = _dot(xm.astype(BF16), bdv_ref[c]).astype(BF16)
        pre = (pre + _dot(q, wg_ref[cs, :])
               + _dot(k, wg_ref[e + c * MXU_DIM:e + (c + 1) * MXU_DIM, :])
               + _dot(v, wg_ref[2 * e + c * MXU_DIM:2 * e + (c + 1) * MXU_DIM, :]))
        sz = _silu(z)
        q_ref[0, :, cs] = q
        k_ref[0, :, cs] = k
        kt_ref[0, cs, :] = k32.T.astype(BF16)
        v_ref[0, :, cs] = v
        g1_ref[0, :, cs] = (hnorm_ref[:, cs] * jax.nn.sigmoid(og) * sz).astype(BF16)
        g2_ref[0, :, cs] = (skip_ref[:, cs] * xc * sz).astype(BF16)

    pre_ref[0] = pre


def _mlstm_pre(x, mod, pre_g, w_in_b, conv_w, conv_b, bdq, bdk, bdv, wg, bg, head_norm, skip):
    b, t, d = x.shape
    e = w_in_b.shape[1] // 3
    tm = min(PRE_TILE, t)
    nt = t // tm
    xr = x.reshape(b, nt, tm, d)
    zeros = jnp.zeros((b, 1, HALO, d), x.dtype)
    prev = jnp.concatenate([zeros, xr[:, :-1, tm - HALO:, :]], axis=1)
    nxt = jnp.concatenate([xr[:, 1:, :HALO, :], zeros], axis=1)
    halo = jnp.concatenate([prev, nxt], axis=2)

    row = lambda bi, i: (bi, i, 0)
    big = pl.BlockSpec((1, tm, e), row)
    big_t = pl.BlockSpec((1, e, tm), lambda bi, i: (bi, 0, i))
    kernel = functools.partial(_mlstm_pre_kernel, k_scale=float((e // N_HEADS) ** -0.5))
    tok = jax.ShapeDtypeStruct((b, t, e), BF16)
    return pl.pallas_call(
        kernel,
        grid=(b, nt),
        in_specs=[
            pl.BlockSpec((1, tm, d), row),
            pl.BlockSpec((1, 1, 2 * HALO, d), lambda bi, i: (bi, i, 0, 0)),
            pl.BlockSpec((1, 3, d), lambda bi, i: (bi, 0, 0)),
            _const_spec((1, d)),
            _const_spec(w_in_b.shape),
            _const_spec(conv_w.shape),
            _const_spec((1, e)),
            _const_spec(bdq.shape), _const_spec(bdk.shape), _const_spec(bdv.shape),
            _const_spec(wg.shape), _const_spec((1, LANES)),
            _const_spec((1, e)), _const_spec((1, e)),
        ],
        out_specs=[big, big, big_t, big, big, big, pl.BlockSpec((1, tm, LANES), row)],
        out_shape=[tok, tok, jax.ShapeDtypeStruct((b, e, t), BF16), tok, tok, tok,
                   jax.ShapeDtypeStruct((b, t, LANES), F32)],
        scratch_shapes=[pltpu.VMEM((tm + 2 * HALO, d), BF16),
                        pltpu.VMEM((tm + 2 * HALO, MXU_DIM), F32)],
        compiler_params=_cparams(("parallel", "parallel")),
        name="mlstm_pre",
    )(x, halo, mod, pre_g.reshape(1, d), w_in_b, conv_w, conv_b.reshape(1, e),
      bdq, bdk, bdv, wg, bg, head_norm.reshape(1, e), skip.reshape(1, e))


def _gate_prep_kernel(pre_ref, cols_ref, rows_ref, cmrep_ref):
    nch = pre_ref.shape[1] // SCAN_CHUNK
    chunks = [pre_ref[0, ch * SCAN_CHUNK:(ch + 1) * SCAN_CHUNK, :] for ch in range(nch)]
    for ch, (cols, rows, cmrep) in enumerate(_gate_vectors(chunks)):
        rs = slice(ch * SCAN_CHUNK, (ch + 1) * SCAN_CHUNK)
        cols_ref[0, rs, :] = cols
        rows_ref[0, ch] = rows
        cmrep_ref[0, rs, :] = cmrep


def _gate_prep(pre):
    b, t, _ = pre.shape
    nch = min(GATE_CHUNKS_PER_STEP, t // SCAN_CHUNK)
    tm = nch * SCAN_CHUNK
    row = lambda bi, i: (bi, i, 0)
    return pl.pallas_call(
        _gate_prep_kernel,
        grid=(b, t // tm),
        in_specs=[pl.BlockSpec((1, tm, LANES), row)],
        out_specs=[pl.BlockSpec((1, tm, LANES), row),
                   pl.BlockSpec((1, nch, 4 * N_HEADS, SCAN_CHUNK), lambda bi, i: (bi, i, 0, 0)),
                   pl.BlockSpec((1, tm, 2 * N_HEADS * LANES), row)],
        out_shape=[jax.ShapeDtypeStruct((b, t, LANES), F32),
                   jax.ShapeDtypeStruct((b, t // SCAN_CHUNK, 4 * N_HEADS, SCAN_CHUNK), F32),
                   jax.ShapeDtypeStruct((b, t, 2 * N_HEADS * LANES), F32)],
        compiler_params=_cparams(("parallel", "parallel")),
        name="gate_prep",
    )(pre)


def _scan_chunk(get, cols, rows, cmrep, c_st, n_st, m_st, backward, emit):
    h = N_HEADS
    l = cols.shape[0]
    dh = c_st.shape[-1]
    off = 2 * h if backward else 0
    last = 0 if backward else l - 1
    st = []
    for head in range(h):
        cm_col = cols[:, 4 * h + off + head:4 * h + off + head + 1]
        st.append(dict(
            a_col=cols[:, off + head:off + head + 1],
            b_col=cols[:, off + h + head:off + h + head + 1],
            a_row=rows[off + head:off + head + 1, :],
            total=cols[last:last + 1, off + h + head:off + h + head + 1],
            amax=cm_col[last:last + 1, :],
            m_prev=m_st[head]))

    if emit is not None:
        ti = lax.broadcasted_iota(jnp.int32, (l, l), 0)
        si = lax.broadcasted_iota(jnp.int32, (l, l), 1)
        mask = (si >= ti) if backward else (ti >= si)
        for head in range(h):
            s = st[head]
            q = get("q", head)
            s["u"] = jnp.maximum(s["m_prev"], cmrep[:, head * LANES:(head + 1) * LANES])
            s["p"] = jnp.exp(jnp.where(mask, s["a_row"] - _lane_repeat(s["u"], l), -jnp.inf))
            s["qk"] = _dot(q, get("kt", head))
            s["qc"] = _dot(q, c_st[head].astype(BF16))
        for head in range(h):
            s = st[head]
            sc = s["qk"] * s["p"]
            w_inter = jnp.exp(s["m_prev"] - s["u"])
            num = _dot(sc.astype(BF16), get("v", head)) + _lane_repeat(w_inter, dh) * s["qc"]
            qn = jnp.sum(get("q", head).astype(F32) * n_st[head], axis=-1, keepdims=True)
            den = jnp.sum(sc, axis=-1, keepdims=True) + w_inter[:, 0:1] * qn
            floor = jnp.exp(-(s["b_col"] + s["u"][:, 0:1]))
            emit(head, num / jnp.maximum(jnp.abs(den), floor))

    for head in range(h):
        s = st[head]
        s["u_last"] = jnp.maximum(s["m_prev"], s["amax"])
        s["decay"] = jnp.exp(s["m_prev"] - s["u_last"])
        wkt = (get("kt", head).astype(F32) * jnp.exp(s["a_row"] - s["u_last"])).astype(BF16)
        s["upd"] = _dot(wkt, get("v", head))
    for head in range(h):
        s = st[head]
        wk = get("k", head).astype(F32) * jnp.exp(s["a_col"] - s["u_last"])
        c_st[head] = s["decay"] * c_st[head] + s["upd"]
        n_st[head] = s["decay"] * n_st[head] + jnp.sum(wk, axis=0, keepdims=True)
        m_st[head] = s["total"] + s["u_last"]


def _tile_getter(refs, dh, rows=slice(None)):
    def get(name, head):
        hs = slice(head * dh, (head + 1) * dh)
        if name == "kt":
            return refs["kt"][0, hs, rows]
        return refs[name][0, rows, hs]
    return get


def _ctx_state_kernel(k_ref, kt_ref, v_ref, cols_ref, rows_ref, c_out, n_out, m_out):
    dh = c_out.shape[-1]
    nc = k_ref.shape[1] // SCAN_CHUNK
    c_out[...] = jnp.zeros(c_out.shape, F32)
    n_out[...] = jnp.zeros(n_out.shape, F32)
    m_out[...] = jnp.zeros(m_out.shape, F32)
    refs = dict(k=k_ref, kt=kt_ref, v=v_ref)
    for direction in range(2):
        for step in range(nc):
            c = nc - 1 - step if direction else step
            rs = slice(c * SCAN_CHUNK, (c + 1) * SCAN_CHUNK)
            _scan_chunk(_tile_getter(refs, dh, rs), cols_ref[0, rs, :], rows_ref[0, c], None,
                        c_out.at[0, direction], n_out.at[0, direction], m_out.at[0, direction],
                        bool(direction), None)


def _ctx_states(k, kt, v, cols, rows):
    b, t, e = k.shape
    dh = e // N_HEADS
    nc = t // SCAN_CHUNK
    per_b = lambda nd: (lambda bi: (bi,) + (0,) * (nd - 1))
    return pl.pallas_call(
        _ctx_state_kernel,
        grid=(b,),
        in_specs=[pl.BlockSpec((1, t, e), per_b(3)), pl.BlockSpec((1, e, t), per_b(3)),
                  pl.BlockSpec((1, t, e), per_b(3)),
                  pl.BlockSpec((1, t, LANES), per_b(3)),
                  pl.BlockSpec((1, nc, 4 * N_HEADS, SCAN_CHUNK), per_b(4))],
        out_specs=[pl.BlockSpec((1, 2, N_HEADS, dh, dh), per_b(5)),
                   pl.BlockSpec((1, 2, N_HEADS, 1, dh), per_b(5)),
                   pl.BlockSpec((1, 2, N_HEADS, 1, 1), per_b(5))],
        out_shape=[jax.ShapeDtypeStruct((b, 2, N_HEADS, dh, dh), F32),
                   jax.ShapeDtypeStruct((b, 2, N_HEADS, 1, dh), F32),
                   jax.ShapeDtypeStruct((b, 2, N_HEADS, 1, 1), F32)],
        compiler_params=_cparams(("parallel",)),
        name="ctx_states",
    )(k, kt, v, cols, rows)


def _load_state(c0_ref, n0_ref, m0_ref, c_scr, n_scr, m_scr):
    @pl.when(pl.program_id(1) == 0)
    def _():
        c_scr[...] = c0_ref[0, 0]
        n_scr[...] = n0_ref[0, 0]
        m_scr[...] = m0_ref[0, 0]


def _scan_bwd_kernel(q_ref, k_ref, kt_ref, v_ref, cols_ref, rows_ref, cmrep_ref, c0_ref, n0_ref, m0_ref,
                     hb_ref, c_scr, n_scr, m_scr):
    _load_state(c0_ref, n0_ref, m0_ref, c_scr, n_scr, m_scr)
    dh = c_scr.shape[-1]

    def emit(head, hb):
        hb_ref[0, :, head * dh:(head + 1) * dh] = hb.astype(hb_ref.dtype)

    _scan_chunk(_tile_getter(dict(q=q_ref, k=k_ref, kt=kt_ref, v=v_ref), dh), cols_ref[0], rows_ref[0, 0],
                cmrep_ref[0], c_scr, n_scr, m_scr, True, emit)


def _scan_fwd_out_kernel(q_ref, k_ref, kt_ref, v_ref, cols_ref, rows_ref, cmrep_ref, c0_ref, n0_ref, m0_ref,
                         hb_ref, g1_ref, g2_ref, wout_ref, x_ref, mod_ref, postg_ref, one_ref,
                         o_ref, c_scr, n_scr, m_scr, hf_scr):
    _load_state(c0_ref, n0_ref, m0_ref, c_scr, n_scr, m_scr)
    dh = c_scr.shape[-1]

    def emit(head, hf):
        hf_scr[:, head * dh:(head + 1) * dh] = hf

    _scan_chunk(_tile_getter(dict(q=q_ref, k=k_ref, kt=kt_ref, v=v_ref), dh), cols_ref[0], rows_ref[0, 0],
                cmrep_ref[0], c_scr, n_scr, m_scr, False, emit)

    @pl.when(one_ref[0] > 0)
    def _():
        y = None
        for head in range(N_HEADS):
            hs = slice(head * dh, (head + 1) * dh)
            hsum = hf_scr[:, hs] + hb_ref[0, :, hs].astype(F32)
            cen = hsum - jnp.mean(hsum, axis=-1, keepdims=True)
            hn = cen * lax.rsqrt(jnp.mean(cen * cen, axis=-1, keepdims=True) + EPS)
            gated = (hn * g1_ref[0, :, hs].astype(F32) + g2_ref[0, :, hs].astype(F32)).astype(BF16)
            part = _dot(gated, wout_ref[hs, :])
            y = part if y is None else y + part
        yn = y * lax.rsqrt(jnp.mean(y * y, axis=-1, keepdims=True) + EPS) * postg_ref[...]
        o_ref[0] = x_ref[0] + mod_ref[0, 2:3, :] * yn


def _scan_specs(b, t, e, nc, l, reverse):
    dh = e // N_HEADS
    direction = 1 if reverse else 0
    cidx = (lambda c: nc - 1 - c) if reverse else (lambda c: c)
    tok = lambda w: pl.BlockSpec((1, l, w), lambda bi, c: (bi, cidx(c), 0))
    state = lambda shape: pl.BlockSpec((1, 1) + shape, lambda bi, c: (bi, direction) + (0,) * len(shape),
                                       pipeline_mode=pl.Buffered(1))
    specs = [tok(e), tok(e), pl.BlockSpec((1, e, l), lambda bi, c: (bi, 0, cidx(c))), tok(e), tok(LANES),
             pl.BlockSpec((1, 1, 4 * N_HEADS, l), lambda bi, c: (bi, cidx(c), 0, 0)),
             pl.BlockSpec((1, l, N_HEADS * LANES), lambda bi, c: (bi, cidx(c), direction)),
             state((N_HEADS, dh, dh)), state((N_HEADS, 1, dh)), state((N_HEADS, 1, 1))]
    scratch = [pltpu.VMEM((N_HEADS, dh, dh), F32), pltpu.VMEM((N_HEADS, 1, dh), F32),
               pltpu.VMEM((N_HEADS, 1, 1), F32)]
    return specs, scratch, tok


def _scan_bwd(q, k, kt, v, cols, rows, cmrep, c0, n0, m0):
    b, t, e = q.shape
    l = SCAN_CHUNK
    nc = t // l
    specs, scratch, tok = _scan_specs(b, t, e, nc, l, True)
    return pl.pallas_call(
        _scan_bwd_kernel,
        grid=(b, nc),
        in_specs=specs,
        out_specs=tok(e),
        out_shape=jax.ShapeDtypeStruct((b, t, e), BF16),
        scratch_shapes=scratch,
        compiler_params=_cparams(("parallel", "arbitrary")),
        name="mlstm_scan_bwd",
    )(q, k, kt, v, cols, rows, cmrep, c0, n0, m0)


def _scan_fwd_out(q, k, kt, v, cols, rows, cmrep, c0, n0, m0, hb, g1, g2, w_out_b, x, mod, post_g):
    b, t, e = q.shape
    d = x.shape[-1]
    l = SCAN_CHUNK
    nc = t // l
    specs, scratch, tok = _scan_specs(b, t, e, nc, l, False)
    specs = specs + [tok(e), tok(e), tok(e), _const_spec(w_out_b.shape), tok(d),
                     pl.BlockSpec((1, 3, d), lambda bi, c: (bi, 0, 0)), _const_spec((1, d)),
                     pl.BlockSpec(memory_space=pltpu.SMEM)]
    return pl.pallas_call(
        _scan_fwd_out_kernel,
        grid=(b, nc),
        in_specs=specs,
        out_specs=tok(d),
        out_shape=jax.ShapeDtypeStruct((b, t, d), F32),
        scratch_shapes=scratch + [pltpu.VMEM((l, e), F32)],
        compiler_params=_cparams(("parallel", "arbitrary")),
        name="mlstm_scan_fwd_out",
    )(q, k, kt, v, cols, rows, cmrep, c0, n0, m0, hb, g1, g2, w_out_b, x, mod, post_g.reshape(1, d),
      jnp.ones((1,), jnp.int32))


def _pool_pre_kernel(x_ref, mod_ref, g_ref, win_ref, u_ref, sz_ref):
    e = u_ref.shape[2]
    hn = _norm_mod(x_ref[0], g_ref[...], mod_ref[0, 0:1, :], mod_ref[0, 1:2, :]).astype(BF16)
    for c in range(e // MXU_DIM):
        cs = slice(c * MXU_DIM, (c + 1) * MXU_DIM)
        u_ref[0, :, cs] = _dot(hn, win_ref[:, cs]).astype(BF16)
        sz_ref[0, :, cs] = _silu(_dot(hn, win_ref[:, e + c * MXU_DIM:e + (c + 1) * MXU_DIM])).astype(BF16)


def _pool_pre(x, mod, pre_g, w_in_b):
    b, t, d = x.shape
    e = w_in_b.shape[1] // 2
    tm = POOL_TILE
    row = lambda bi, i: (bi, i, 0)
    return pl.pallas_call(
        _pool_pre_kernel,
        grid=(b, t // tm),
        in_specs=[pl.BlockSpec((1, tm, d), row), pl.BlockSpec((1, 3, d), lambda bi, i: (bi, 0, 0)),
                  _const_spec((1, d)), _const_spec(w_in_b.shape)],
        out_specs=[pl.BlockSpec((1, tm, e), row)] * 2,
        out_shape=[jax.ShapeDtypeStruct((b, t, e), BF16)] * 2,
        compiler_params=_cparams(("parallel", "parallel")),
        name="pool_pre",
    )(x, mod, pre_g.reshape(1, d), w_in_b)


def _pool_out_kernel(up_ref, uc_ref, un_ref, sz_ref, band_ref, icnt_ref, wpool_ref, pscale_ref,
                     wout_ref, x_ref, mod_ref, postg_ref, o_ref, ext_scr):
    i = pl.program_id(1)
    nt = pl.num_programs(1)
    tm = uc_ref.shape[1]
    gd = wpool_ref.shape[1]
    y = jnp.zeros(o_ref.shape[1:], F32)
    for g, w in enumerate(POOL_WINDOWS):
        gs = slice(g * gd, (g + 1) * gd)
        ext_scr[0:tm, :] = jnp.where(i > 0, up_ref[0, :, gs].astype(F32), 0.0)
        ext_scr[tm:2 * tm, :] = uc_ref[0, :, gs].astype(F32)
        ext_scr[2 * tm:, :] = jnp.where(i < nt - 1, un_ref[0, :, gs].astype(F32), 0.0)
        rsum = jnp.zeros((tm, gd), F32)
        for o in range(-(w // 2), w - w // 2):
            rsum = rsum + ext_scr[tm + o * GRID_W:2 * tm + o * GRID_W, :]
        rsum = rsum.astype(BF16)
        parts = [_dot(band_ref[g], rsum[s * MXU_DIM:(s + 1) * MXU_DIM, :]) for s in range(tm // MXU_DIM)]
        mean = jnp.concatenate(parts, axis=0) * icnt_ref[:, g:g + 1]
        diff = mean - ext_scr[tm:2 * tm, :]
        yg = _dot(diff.astype(BF16), wpool_ref[g]) * pscale_ref[:, gs] * sz_ref[0, :, gs].astype(F32)
        y = y + _dot(yg.astype(BF16), wout_ref[gs, :])
    yn = y * lax.rsqrt(jnp.mean(y * y, axis=-1, keepdims=True) + EPS) * postg_ref[...]
    o_ref[0] = x_ref[0] + mod_ref[0, 2:3, :] * yn


def _pool_constants(t):
    rows = t // GRID_W
    bands = np.zeros((len(POOL_WINDOWS), MXU_DIM, MXU_DIM), np.float32)
    icnt = np.zeros((t, LANES), np.float32)
    for g, w in enumerate(POOL_WINDOWS):
        lo_off, hi_off = -(w // 2), w - w // 2
        for p in range(MXU_DIM):
            c = p % GRID_W
            base = p - c
            lo, hi = max(c + lo_off, 0), min(c + hi_off, GRID_W)
            bands[g, p, base + lo:base + hi] = 1.0
        cc = np.arange(GRID_W)
        cnt_c = np.clip(cc + hi_off, 0, GRID_W) - np.clip(cc + lo_off, 0, GRID_W)
        rr = np.arange(rows)
        cnt_r = np.clip(rr + hi_off, 0, rows) - np.clip(rr + lo_off, 0, rows)
        icnt[:, g] = (1.0 / (cnt_r[:, None] * cnt_c[None, :])).reshape(-1)
    return bands, icnt


def _pool_out(u, sz, w_pool_b, pool_scale, w_out_b, x, mod, post_g):
    b, t, e = u.shape
    d = x.shape[-1]
    tm = POOL_TILE
    assert tm >= (max(POOL_WINDOWS) // 2) * GRID_W and tm % MXU_DIM == 0 and t % tm == 0
    nt = t // tm
    gd = e // len(POOL_WINDOWS)
    bands, icnt = _pool_constants(t)
    row = lambda bi, i: (bi, i, 0)
    return pl.pallas_call(
        _pool_out_kernel,
        grid=(b, nt),
        in_specs=[
            pl.BlockSpec((1, tm, e), lambda bi, i: (bi, jnp.maximum(i - 1, 0), 0)),
            pl.BlockSpec((1, tm, e), row),
            pl.BlockSpec((1, tm, e), lambda bi, i: (bi, jnp.minimum(i + 1, nt - 1), 0)),
            pl.BlockSpec((1, tm, e), row),
            _const_spec(bands.shape),
            pl.BlockSpec((tm, LANES), lambda bi, i: (i, 0)),
            _const_spec(w_pool_b.shape),
            _const_spec((1, e)),
            _const_spec(w_out_b.shape),
            pl.BlockSpec((1, tm, d), row),
            pl.BlockSpec((1, 3, d), lambda bi, i: (bi, 0, 0)),
            _const_spec((1, d)),
        ],
        out_specs=pl.BlockSpec((1, tm, d), row),
        out_shape=jax.ShapeDtypeStruct((b, t, d), F32),
        scratch_shapes=[pltpu.VMEM((3 * tm, gd), F32)],
        compiler_params=_cparams(("parallel", "parallel")),
        name="pool_out",
    )(u, u, u, sz, jnp.asarray(bands, BF16), jnp.asarray(icnt), w_pool_b, pool_scale.reshape(1, e),
      w_out_b, x, mod, post_g.reshape(1, d))


def _block_diag_tiles(w):
    per = MXU_DIM // QKV_BLOCK
    rows = w.reshape(w.shape[0] // per, MXU_DIM, QKV_BLOCK)
    tiled = jnp.tile(rows, (1, 1, per))
    blk = np.arange(MXU_DIM) // QKV_BLOCK
    return jnp.where(blk[:, None] == blk[None, :], tiled, 0.0).astype(BF16)


def _gate_weights(w_f, b_f, w_b, b_b):
    pad = LANES - 4 * N_HEADS
    wg = jnp.pad(jnp.concatenate([w_f, w_b], axis=1), ((0, 0), (0, pad)))
    bg = jnp.pad(jnp.concatenate([b_f, b_b]), (0, pad)).reshape(1, LANES)
    return wg.astype(BF16), bg


def kernel(x, c, ctx, c_ctx, w_ada, b_ada, a_norm_pre, a_norm_post, a_w_in, a_conv_w, a_conv_b,
           a_w_q, a_w_k, a_w_v, a_w_gate_f, a_b_gate_f, a_w_gate_b, a_b_gate_b, a_head_norm, a_skip,
           a_w_out, b_norm_pre, b_norm_post, b_w_in, b_w_pool, b_pool_scale, b_w_out):
    bsz, t, d = x.shape
    t_ctx = ctx.shape[1]
    assert t % SCAN_CHUNK == 0 and t_ctx % SCAN_CHUNK == 0 and t % GRID_W == 0

    n_cond = -(-(bsz + 1) // 8) * 8
    cond = jnp.concatenate([c, c_ctx[None], jnp.zeros((n_cond - bsz - 1, d), F32)], axis=0)
    mods = _ada(cond, w_ada, b_ada).reshape(w_ada.shape[0], n_cond, 3, d)
    mod_lat = [mods[i, :bsz] for i in range(w_ada.shape[0])]
    mod_ctx = jnp.broadcast_to(mods[0, bsz][None], (bsz, 3, d))

    w_in_b = a_w_in[0].astype(BF16)
    bdq, bdk, bdv = (_block_diag_tiles(w[0]) for w in (a_w_q, a_w_k, a_w_v))
    wg, bg = _gate_weights(a_w_gate_f[0], a_b_gate_f[0], a_w_gate_b[0], a_b_gate_b[0])
    pre_args = (a_norm_pre[0], w_in_b, a_conv_w[0], a_conv_b[0], bdq, bdk, bdv, wg, bg,
                a_head_norm[0], a_skip[0])
    _, kc, ktc, vc, _, _, pre_c = _mlstm_pre(ctx, mod_ctx, *pre_args)
    cols_c, rows_c, _ = _gate_prep(pre_c)
    c0, n0, m0 = _ctx_states(kc, ktc, vc, cols_c, rows_c)

    ql, kl, ktl, vl, g1, g2, pre_l = _mlstm_pre(x, mod_lat[0], *pre_args)
    cols_l, rows_l, cmrep_l = _gate_prep(pre_l)
    scan_in = (ql, kl, ktl, vl, cols_l, rows_l, cmrep_l, c0, n0, m0)
    hb = _scan_bwd(*scan_in)
    x = _scan_fwd_out(*scan_in, hb, g1, g2, a_w_out[0].astype(BF16), x, mod_lat[0], a_norm_post[0])

    u, sz = _pool_pre(x, mod_lat[1], b_norm_pre[0], b_w_in[0].astype(BF16))
    x = _pool_out(u, sz, b_w_pool[0].astype(BF16), b_pool_scale[0], b_w_out[0].astype(BF16),
                  x, mod_lat[1], b_norm_post[0])
    return x
```

```python
import functools

import numpy as np
import jax
import jax.numpy as jnp
from jax import lax
from jax.experimental import pallas as pl
from jax.experimental.pallas import tpu as pltpu

EPS = 1e-6
N_HEADS = 4
QKV_BLOCK = 4
GRID_W = 64
POOL_WINDOWS = (2, 4, 8, 16)

LANES = 128
MXU_DIM = 256
HALO = 16
VMEM_LIMIT_BYTES = 56 * 1024 * 1024

SCAN_CHUNK = 256
PRE_TILE = 512
POOL_TILE = 512
GATE_CHUNKS_PER_STEP = 8
BWD_CHUNKS_PER_STEP = 2

BF16 = jnp.bfloat16
F32 = jnp.float32


def _dot(a, b):
    return jnp.dot(a, b, preferred_element_type=F32)


def _silu(v):
    return v * jax.nn.sigmoid(v)


def _cparams(sem):
    return pltpu.CompilerParams(dimension_semantics=sem, vmem_limit_bytes=VMEM_LIMIT_BYTES)


def _const_spec(shape):
    nd = len(shape)
    return pl.BlockSpec(shape, lambda *_: (0,) * nd, pipeline_mode=pl.Buffered(1))


def _lane_repeat(col, width):
    return jnp.concatenate([col] * (width // LANES), axis=1) if width > LANES else col


def _ada_kernel(cond_t_ref, w_ref, b_ref, o_ref):
    s = _silu(cond_t_ref[...])
    w = w_ref[0]
    n_rows = o_ref.shape[1]
    for r in range(n_rows):
        o_ref[0, r:r + 1, :] = jnp.sum(w * s[:, r:r + 1], axis=0, keepdims=True) + b_ref[0]


def _ada(cond, w_ada, b_ada):
    depth, d, n3 = w_ada.shape
    rows = cond.shape[0]
    tn = max(w for w in range(LANES, 1024 + 1, LANES) if n3 % w == 0)
    return pl.pallas_call(
        _ada_kernel,
        grid=(depth, n3 // tn),
        in_specs=[
            pl.BlockSpec((d, rows), lambda l, j: (0, 0)),
            pl.BlockSpec((1, d, tn), lambda l, j: (l, 0, j)),
            pl.BlockSpec((1, 1, tn), lambda l, j: (l, 0, j)),
        ],
        out_specs=pl.BlockSpec((1, rows, tn), lambda l, j: (l, 0, j)),
        out_shape=jax.ShapeDtypeStruct((depth, rows, n3), F32),
        compiler_params=_cparams(("parallel", "parallel")),
        name="ada_mod",
    )(cond.T, w_ada, b_ada.reshape(depth, 1, n3))


def _norm_mod(xv, g, shift, scale):
    y = xv * lax.rsqrt(jnp.mean(xv * xv, axis=-1, keepdims=True) + EPS) * g
    return y * (1.0 + scale) + shift


def _scan_along_lanes(x, fwd_rows, lane, combine, fill):
    l = x.shape[1]
    step = 1
    while step < l:
        from_left = jnp.where(lane >= step, pltpu.roll(x, step, axis=1), fill)
        from_right = jnp.where(lane < l - step, pltpu.roll(x, l - step, axis=1), fill)
        x = combine(x, jnp.where(fwd_rows, from_left, from_right))
        step *= 2
    return x


def _gate_vectors(pre_chunks):
    h = N_HEADS
    l = pre_chunks[0].shape[0]
    pt = jnp.concatenate([p.T[0:4 * h, :] for p in pre_chunks], axis=0)
    row = lax.broadcasted_iota(jnp.int32, pt.shape, 0) % (4 * h)
    lane = lax.broadcasted_iota(jnp.int32, pt.shape, 1)
    fwd_rows = row < 2 * h
    log_f = jnp.minimum(pt, 0.0) - jnp.log1p(jnp.exp(-jnp.abs(pt)))
    b = _scan_along_lanes(log_f, fwd_rows, lane, jnp.add, 0.0)
    a = pt - pltpu.roll(b, pt.shape[0] - h, axis=0)
    cmax = _scan_along_lanes(a, fwd_rows, lane, jnp.maximum, -jnp.inf)
    rows = jnp.where((row % (2 * h)) < h, a, b)
    out = []
    for ch in range(len(pre_chunks)):
        rs = slice(ch * 4 * h, (ch + 1) * 4 * h)
        stacked = jnp.concatenate([rows[rs], cmax[rs], jnp.zeros((LANES - 8 * h, l), F32)], axis=0)
        cols = stacked.T
        reps = []
        for direction in range(2):
            for head in range(h):
                j = 4 * h + direction * 2 * h + head
                reps.append(jnp.broadcast_to(cols[:, j:j + 1], (l, LANES)))
        out.append((cols, rows[rs], jnp.concatenate(reps, axis=1)))
    return out


def _fold_weights_kernel(bdq_ref, bdk_ref, bdv_ref, wgq_ref, wgk_ref, wgv_ref, bdkq_ref, aqk_ref, av_ref,
                         *, k_scale):
    hp = lax.Precision.HIGHEST
    dot = lambda a, b: jnp.dot(a, b, precision=hp, preferred_element_type=F32)
    bdq = bdq_ref[0]
    bdk = bdk_ref[0] * k_scale
    bdkq = lax.dot_general(bdk, bdq, (((1,), (1,)), ((), ())), precision=hp, preferred_element_type=F32)
    bdkq_ref[0] = bdkq.astype(BF16)
    aqk_ref[0] = (dot(bdq, wgq_ref[0]) + dot(bdk, wgk_ref[0])).astype(BF16)
    av_ref[0] = dot(bdv_ref[0], wgv_ref[0]).astype(BF16)


def _fold_weights(bdq, bdk, bdv, wg, k_scale):
    nt = bdq.shape[0]
    wg3 = wg.reshape(3, nt, MXU_DIM, LANES)
    tile = pl.BlockSpec((1, MXU_DIM, MXU_DIM), lambda c: (c, 0, 0))
    gate = pl.BlockSpec((1, MXU_DIM, LANES), lambda c: (c, 0, 0))
    return pl.pallas_call(
        functools.partial(_fold_weights_kernel, k_scale=k_scale),
        grid=(nt,),
        in_specs=[tile, tile, tile, gate, gate, gate],
        out_specs=[tile, gate, gate],
        out_shape=[jax.ShapeDtypeStruct((nt, MXU_DIM, MXU_DIM), BF16),
                   jax.ShapeDtypeStruct((nt, MXU_DIM, LANES), BF16),
                   jax.ShapeDtypeStruct((nt, MXU_DIM, LANES), BF16)],
        compiler_params=_cparams(("parallel",)),
        name="fold_weights",
    )(bdq, bdk, bdv, wg3[0], wg3[1], wg3[2])


def _mlstm_pre_kernel(x_ref, halo_ref, mod_ref, g_ref, win_ref, convw_ref, convb_ref,
                      bdkq_ref, bdv_ref, aqk_ref, av_ref, bg_ref, hnorm_ref, skip_ref,
                      q_ref, k_ref, kt_ref, v_ref, g1_ref, g2_ref, pre_ref,
                      hn_scr):
    i = pl.program_id(1)
    nt = pl.num_programs(1)
    tm = x_ref.shape[1]
    e = q_ref.shape[2]
    shift = mod_ref[0, 0:1, :]
    scale = mod_ref[0, 1:2, :]
    g = g_ref[...]

    hn_scr[0:HALO, :] = _norm_mod(halo_ref[0, 0, 0:HALO, :], g, shift, scale).astype(BF16)
    hn_scr[HALO:HALO + tm, :] = _norm_mod(x_ref[0], g, shift, scale).astype(BF16)
    hn_scr[HALO + tm:, :] = _norm_mod(halo_ref[0, 0, HALO:, :], g, shift, scale).astype(BF16)

    def project(c):
        cs = slice(c * MXU_DIM, (c + 1) * MXU_DIM)
        hc = hn_scr[HALO:HALO + tm, :]
        return (_dot(hn_scr[...], win_ref[:, cs]),
                _dot(hc, win_ref[:, e + c * MXU_DIM:e + (c + 1) * MXU_DIM]),
                _dot(hc, win_ref[:, 2 * e + c * MXU_DIM:2 * e + (c + 1) * MXU_DIM]))

    pre = jnp.zeros((tm, LANES), F32) + bg_ref[...]
    n_chunks = e // MXU_DIM
    queue = [project(c) for c in range(min(2, n_chunks))]
    for c in range(n_chunks):
        cs = slice(c * MXU_DIM, (c + 1) * MXU_DIM)
        xm_ext, z, og = queue.pop(0)
        if c + 2 < n_chunks:
            queue.append(project(c + 2))
        xm_all = jnp.concatenate([jnp.where(i > 0, xm_ext[0:HALO], 0.0), xm_ext[HALO:HALO + tm],
                                  jnp.where(i < nt - 1, xm_ext[HALO + tm:], 0.0)], axis=0)
        xm = xm_all[HALO:HALO + tm]
        xm_prev = pltpu.roll(xm_all, 1, axis=0)[HALO:HALO + tm]
        xm_next = pltpu.roll(xm_all, tm + 2 * HALO - 1, axis=0)[HALO:HALO + tm]
        conv = (xm_prev * convw_ref[0:1, cs] + xm * convw_ref[1:2, cs] + xm_next * convw_ref[2:3, cs]
                + convb_ref[:, cs])
        xc = _silu(conv)
        xc_b = xc.astype(BF16)
        xm_b = xm.astype(BF16)
        k32 = _dot(xc_b, bdkq_ref[c])
        pre = pre + _dot(xc_b, aqk_ref[c]) + _dot(xm_b, av_ref[c])
        sz = _silu(z)
        q_ref[0, :, cs] = xc_b
        k_ref[0, :, cs] = k32.astype(BF16)
        kt_ref[0, cs, :] = k32.T.astype(BF16)
        v_ref[0, :, cs] = _dot(xm_b, bdv_ref[c]).astype(BF16)
        g1_ref[0, :, cs] = (hnorm_ref[:, cs] * jax.nn.sigmoid(og) * sz).astype(BF16)
        g2_ref[0, :, cs] = (skip_ref[:, cs] * xc * sz).astype(BF16)

    pre_ref[0] = pre


def _mlstm_pre(x, mod, pre_g, w_in_b, conv_w, conv_b, bdkq, bdv, aqk, av, bg, head_norm, skip):
    b, t, d = x.shape
    e = w_in_b.shape[1] // 3
    tm = min(PRE_TILE, t)
    nt = t // tm
    xr = x.reshape(b, nt, tm, d)
    zeros = jnp.zeros((b, 1, HALO, d), x.dtype)
    prev = jnp.concatenate([zeros, xr[:, :-1, tm - HALO:, :]], axis=1)
    nxt = jnp.concatenate([xr[:, 1:, :HALO, :], zeros], axis=1)
    halo = jnp.concatenate([prev, nxt], axis=2)

    row = lambda bi, i: (bi, i, 0)
    big = pl.BlockSpec((1, tm, e), row)
    big_t = pl.BlockSpec((1, e, tm), lambda bi, i: (bi, 0, i))
    tok = jax.ShapeDtypeStruct((b, t, e), BF16)
    return pl.pallas_call(
        _mlstm_pre_kernel,
        grid=(b, nt),
        in_specs=[
            pl.BlockSpec((1, tm, d), row),
            pl.BlockSpec((1, 1, 2 * HALO, d), lambda bi, i: (bi, i, 0, 0)),
            pl.BlockSpec((1, 3, d), lambda bi, i: (bi, 0, 0)),
            _const_spec((1, d)),
            _const_spec(w_in_b.shape),
            _const_spec(conv_w.shape),
            _const_spec((1, e)),
            _const_spec(bdkq.shape), _const_spec(bdv.shape), _const_spec(aqk.shape), _const_spec(av.shape),
            _const_spec((1, LANES)),
            _const_spec((1, e)), _const_spec((1, e)),
        ],
        out_specs=[big, big, big_t, big, big, big, pl.BlockSpec((1, tm, LANES), row)],
        out_shape=[tok, tok, jax.ShapeDtypeStruct((b, e, t), BF16), tok, tok, tok,
                   jax.ShapeDtypeStruct((b, t, LANES), F32)],
        scratch_shapes=[pltpu.VMEM((tm + 2 * HALO, d), BF16)],
        compiler_params=_cparams(("parallel", "parallel")),
        name="mlstm_pre",
    )(x, halo, mod, pre_g.reshape(1, d), w_in_b, conv_w, conv_b.reshape(1, e),
      bdkq, bdv, aqk, av, bg, head_norm.reshape(1, e), skip.reshape(1, e))


def _gate_prep_kernel(pre_ref, cols_ref, rows_ref, cmrep_ref):
    nch = pre_ref.shape[1] // SCAN_CHUNK
    chunks = [pre_ref[0, ch * SCAN_CHUNK:(ch + 1) * SCAN_CHUNK, :] for ch in range(nch)]
    for ch, (cols, rows, cmrep) in enumerate(_gate_vectors(chunks)):
        rs = slice(ch * SCAN_CHUNK, (ch + 1) * SCAN_CHUNK)
        cols_ref[0, rs, :] = cols
        rows_ref[0, ch] = rows
        cmrep_ref[0, rs, :] = cmrep


def _gate_prep(pre):
    b, t, _ = pre.shape
    nch = min(GATE_CHUNKS_PER_STEP, t // SCAN_CHUNK)
    tm = nch * SCAN_CHUNK
    row = lambda bi, i: (bi, i, 0)
    return pl.pallas_call(
        _gate_prep_kernel,
        grid=(b, t // tm),
        in_specs=[pl.BlockSpec((1, tm, LANES), row)],
        out_specs=[pl.BlockSpec((1, tm, LANES), row),
                   pl.BlockSpec((1, nch, 4 * N_HEADS, SCAN_CHUNK), lambda bi, i: (bi, i, 0, 0)),
                   pl.BlockSpec((1, tm, 2 * N_HEADS * LANES), row)],
        out_shape=[jax.ShapeDtypeStruct((b, t, LANES), F32),
                   jax.ShapeDtypeStruct((b, t // SCAN_CHUNK, 4 * N_HEADS, SCAN_CHUNK), F32),
                   jax.ShapeDtypeStruct((b, t, 2 * N_HEADS * LANES), F32)],
        compiler_params=_cparams(("parallel", "parallel")),
        name="gate_prep",
    )(pre)


def _scan_chunk(get, cols, rows, cmrep, c_st, n_st, m_st, backward, emit):
    h = N_HEADS
    l = cols.shape[0]
    dh = c_st.shape[-1]
    off = 2 * h if backward else 0
    last = 0 if backward else l - 1
    st = []
    for head in range(h):
        cm_col = cols[:, 4 * h + off + head:4 * h + off + head + 1]
        st.append(dict(
            a_col=cols[:, off + head:off + head + 1],
            b_col=cols[:, off + h + head:off + h + head + 1],
            a_row=rows[off + head:off + head + 1, :],
            total=cols[last:last + 1, off + h + head:off + h + head + 1],
            amax=cm_col[last:last + 1, :],
            m_prev=m_st[head]))

    if emit is not None:
        ti = lax.broadcasted_iota(jnp.int32, (l, l), 0)
        si = lax.broadcasted_iota(jnp.int32, (l, l), 1)
        mask = (si >= ti) if backward else (ti >= si)
        for head in range(h):
            s = st[head]
            q = get("q", head)
            s["u"] = jnp.maximum(s["m_prev"], cmrep[:, head * LANES:(head + 1) * LANES])
            s["p"] = jnp.exp(jnp.where(mask, s["a_row"] - _lane_repeat(s["u"], l), -jnp.inf))
            s["qk"] = _dot(q, get("kt", head))
            s["qc"] = _dot(q, c_st[head].astype(BF16))
        for head in range(h):
            s = st[head]
            sc = s["qk"] * s["p"]
            w_inter = jnp.exp(s["m_prev"] - s["u"])
            num = _dot(sc.astype(BF16), get("v", head)) + _lane_repeat(w_inter, dh) * s["qc"]
            qn = jnp.sum(get("q", head).astype(F32) * n_st[head], axis=-1, keepdims=True)
            den = jnp.sum(sc, axis=-1, keepdims=True) + w_inter[:, 0:1] * qn
            floor = jnp.exp(-(s["b_col"] + s["u"][:, 0:1]))
            emit(head, num / jnp.maximum(jnp.abs(den), floor))

    for head in range(h):
        s = st[head]
        s["u_last"] = jnp.maximum(s["m_prev"], s["amax"])
        s["decay"] = jnp.exp(s["m_prev"] - s["u_last"])
        wkt = (get("kt", head).astype(F32) * jnp.exp(s["a_row"] - s["u_last"])).astype(BF16)
        s["upd"] = _dot(wkt, get("v", head))
    for head in range(h):
        s = st[head]
        wk = get("k", head).astype(F32) * jnp.exp(s["a_col"] - s["u_last"])
        c_st[head] = s["decay"] * c_st[head] + s["upd"]
        n_st[head] = s["decay"] * n_st[head] + jnp.sum(wk, axis=0, keepdims=True)
        m_st[head] = s["total"] + s["u_last"]


def _tile_getter(refs, dh, rows=slice(None)):
    def get(name, head):
        hs = slice(head * dh, (head + 1) * dh)
        if name == "kt":
            return refs["kt"][0, hs, rows]
        return refs[name][0, rows, hs]
    return get


def _ctx_state_kernel(k_ref, kt_ref, v_ref, cols_ref, rows_ref, c_out, n_out, m_out):
    dh = c_out.shape[-1]
    nc = k_ref.shape[1] // SCAN_CHUNK
    c_out[...] = jnp.zeros(c_out.shape, F32)
    n_out[...] = jnp.zeros(n_out.shape, F32)
    m_out[...] = jnp.zeros(m_out.shape, F32)
    refs = dict(k=k_ref, kt=kt_ref, v=v_ref)
    for direction in range(2):
        for step in range(nc):
            c = nc - 1 - step if direction else step
            rs = slice(c * SCAN_CHUNK, (c + 1) * SCAN_CHUNK)
            _scan_chunk(_tile_getter(refs, dh, rs), cols_ref[0, rs, :], rows_ref[0, c], None,
                        c_out.at[0, direction], n_out.at[0, direction], m_out.at[0, direction],
                        bool(direction), None)


def _ctx_states(k, kt, v, cols, rows):
    b, t, e = k.shape
    dh = e // N_HEADS
    nc = t // SCAN_CHUNK
    per_b = lambda nd: (lambda bi: (bi,) + (0,) * (nd - 1))
    return pl.pallas_call(
        _ctx_state_kernel,
        grid=(b,),
        in_specs=[pl.BlockSpec((1, t, e), per_b(3)), pl.BlockSpec((1, e, t), per_b(3)),
                  pl.BlockSpec((1, t, e), per_b(3)),
                  pl.BlockSpec((1, t, LANES), per_b(3)),
                  pl.BlockSpec((1, nc, 4 * N_HEADS, SCAN_CHUNK), per_b(4))],
        out_specs=[pl.BlockSpec((1, 2, N_HEADS, dh, dh), per_b(5)),
                   pl.BlockSpec((1, 2, N_HEADS, 1, dh), per_b(5)),
                   pl.BlockSpec((1, 2, N_HEADS, 1, 1), per_b(5))],
        out_shape=[jax.ShapeDtypeStruct((b, 2, N_HEADS, dh, dh), F32),
                   jax.ShapeDtypeStruct((b, 2, N_HEADS, 1, dh), F32),
                   jax.ShapeDtypeStruct((b, 2, N_HEADS, 1, 1), F32)],
        compiler_params=_cparams(("parallel",)),
        name="ctx_states",
    )(k, kt, v, cols, rows)


def _load_state(c0_ref, n0_ref, m0_ref, c_scr, n_scr, m_scr):
    @pl.when(pl.program_id(1) == 0)
    def _():
        c_scr[...] = c0_ref[0, 0]
        n_scr[...] = n0_ref[0, 0]
        m_scr[...] = m0_ref[0, 0]


def _scan_bwd_kernel(q_ref, k_ref, kt_ref, v_ref, cols_ref, rows_ref, cmrep_ref, c0_ref, n0_ref, m0_ref,
                     hb_ref, c_scr, n_scr, m_scr):
    _load_state(c0_ref, n0_ref, m0_ref, c_scr, n_scr, m_scr)
    dh = c_scr.shape[-1]
    refs = dict(q=q_ref, k=k_ref, kt=kt_ref, v=v_ref)
    for j in reversed(range(rows_ref.shape[1])):
        rs = slice(j * SCAN_CHUNK, (j + 1) * SCAN_CHUNK)

        def emit(head, hb, rs=rs):
            hb_ref[0, rs, head * dh:(head + 1) * dh] = hb.astype(hb_ref.dtype)

        _scan_chunk(_tile_getter(refs, dh, rs), cols_ref[0, rs, :], rows_ref[0, j], cmrep_ref[0, rs, :],
                    c_scr, n_scr, m_scr, True, emit)


def _scan_fwd_out_kernel(q_ref, k_ref, kt_ref, v_ref, cols_ref, rows_ref, cmrep_ref, c0_ref, n0_ref, m0_ref,
                         hb_ref, g1_ref, g2_ref, wout_ref, x_ref, mod_ref, postg_ref, one_ref,
                         o_ref, c_scr, n_scr, m_scr, hf_scr):
    _load_state(c0_ref, n0_ref, m0_ref, c_scr, n_scr, m_scr)
    dh = c_scr.shape[-1]

    def emit(head, hf):
        hf_scr[:, head * dh:(head + 1) * dh] = hf

    _scan_chunk(_tile_getter(dict(q=q_ref, k=k_ref, kt=kt_ref, v=v_ref), dh), cols_ref[0], rows_ref[0, 0],
                cmrep_ref[0], c_scr, n_scr, m_scr, False, emit)

    @pl.when(one_ref[0] > 0)
    def _():
        y = None
        for head in range(N_HEADS):
            hs = slice(head * dh, (head + 1) * dh)
            hsum = hf_scr[:, hs] + hb_ref[0, :, hs].astype(F32)
            cen = hsum - jnp.mean(hsum, axis=-1, keepdims=True)
            hn = cen * lax.rsqrt(jnp.mean(cen * cen, axis=-1, keepdims=True) + EPS)
            gated = (hn * g1_ref[0, :, hs].astype(F32) + g2_ref[0, :, hs].astype(F32)).astype(BF16)
            part = _dot(gated, wout_ref[hs, :])
            y = part if y is None else y + part
        yn = y * lax.rsqrt(jnp.mean(y * y, axis=-1, keepdims=True) + EPS) * postg_ref[...]
        o_ref[0] = x_ref[0] + mod_ref[0, 2:3, :] * yn


def _scan_specs(b, t, e, chunks_per_step, reverse):
    dh = e // N_HEADS
    direction = 1 if reverse else 0
    l = SCAN_CHUNK * chunks_per_step
    ns = t // l
    cidx = (lambda c: ns - 1 - c) if reverse else (lambda c: c)
    tok = lambda w: pl.BlockSpec((1, l, w), lambda bi, c: (bi, cidx(c), 0))
    state = lambda shape: pl.BlockSpec((1, 1) + shape, lambda bi, c: (bi, direction) + (0,) * len(shape),
                                       pipeline_mode=pl.Buffered(1))
    specs = [tok(e), tok(e), pl.BlockSpec((1, e, l), lambda bi, c: (bi, 0, cidx(c))), tok(e), tok(LANES),
             pl.BlockSpec((1, chunks_per_step, 4 * N_HEADS, SCAN_CHUNK), lambda bi, c: (bi, cidx(c), 0, 0)),
             pl.BlockSpec((1, l, N_HEADS * LANES), lambda bi, c: (bi, cidx(c), direction)),
             state((N_HEADS, dh, dh)), state((N_HEADS, 1, dh)), state((N_HEADS, 1, 1))]
    scratch = [pltpu.VMEM((N_HEADS, dh, dh), F32), pltpu.VMEM((N_HEADS, 1, dh), F32),
               pltpu.VMEM((N_HEADS, 1, 1), F32)]
    return specs, scratch, tok, ns


def _scan_bwd(q, k, kt, v, cols, rows, cmrep, c0, n0, m0):
    b, t, e = q.shape
    specs, scratch, tok, ns = _scan_specs(b, t, e, BWD_CHUNKS_PER_STEP, True)
    return pl.pallas_call(
        _scan_bwd_kernel,
        grid=(b, ns),
        in_specs=specs,
        out_specs=tok(e),
        out_shape=jax.ShapeDtypeStruct((b, t, e), BF16),
        scratch_shapes=scratch,
        compiler_params=_cparams(("parallel", "arbitrary")),
        name="mlstm_scan_bwd",
    )(q, k, kt, v, cols, rows, cmrep, c0, n0, m0)


def _scan_fwd_out(q, k, kt, v, cols, rows, cmrep, c0, n0, m0, hb, g1, g2, w_out_b, x, mod, post_g):
    b, t, e = q.shape
    d = x.shape[-1]
    l = SCAN_CHUNK
    specs, scratch, tok, ns = _scan_specs(b, t, e, 1, False)
    specs = specs + [tok(e), tok(e), tok(e), _const_spec(w_out_b.shape), tok(d),
                     pl.BlockSpec((1, 3, d), lambda bi, c: (bi, 0, 0)), _const_spec((1, d)),
                     pl.BlockSpec(memory_space=pltpu.SMEM)]
    return pl.pallas_call(
        _scan_fwd_out_kernel,
        grid=(b, ns),
        in_specs=specs,
        out_specs=tok(d),
        out_shape=jax.ShapeDtypeStruct((b, t, d), F32),
        scratch_shapes=scratch + [pltpu.VMEM((l, e), F32)],
        compiler_params=_cparams(("parallel", "arbitrary")),
        name="mlstm_scan_fwd_out",
    )(q, k, kt, v, cols, rows, cmrep, c0, n0, m0, hb, g1, g2, w_out_b, x, mod, post_g.reshape(1, d),
      jnp.ones((1,), jnp.int32))


def _pool_pre_kernel(x_ref, mod_ref, g_ref, win_ref, u_ref, sz_ref):
    e = u_ref.shape[2]
    hn = _norm_mod(x_ref[0], g_ref[...], mod_ref[0, 0:1, :], mod_ref[0, 1:2, :]).astype(BF16)
    for c in range(e // MXU_DIM):
        cs = slice(c * MXU_DIM, (c + 1) * MXU_DIM)
        u_ref[0, :, cs] = _dot(hn, win_ref[:, cs]).astype(BF16)
        sz_ref[0, :, cs] = _silu(_dot(hn, win_ref[:, e + c * MXU_DIM:e + (c + 1) * MXU_DIM])).astype(BF16)


def _pool_pre(x, mod, pre_g, w_in_b):
    b, t, d = x.shape
    e = w_in_b.shape[1] // 2
    tm = POOL_TILE
    row = lambda bi, i: (bi, i, 0)
    return pl.pallas_call(
        _pool_pre_kernel,
        grid=(b, t // tm),
        in_specs=[pl.BlockSpec((1, tm, d), row), pl.BlockSpec((1, 3, d), lambda bi, i: (bi, 0, 0)),
                  _const_spec((1, d)), _const_spec(w_in_b.shape)],
        out_specs=[pl.BlockSpec((1, tm, e), row)] * 2,
        out_shape=[jax.ShapeDtypeStruct((b, t, e), BF16)] * 2,
        compiler_params=_cparams(("parallel", "parallel")),
        name="pool_pre",
    )(x, mod, pre_g.reshape(1, d), w_in_b)


def _pool_out_kernel(up_ref, uc_ref, un_ref, sz_ref, band_ref, icnt_ref, wpool_ref, pscale_ref,
                     wout_ref, x_ref, mod_ref, postg_ref, o_ref, ext_scr):
    i = pl.program_id(1)
    nt = pl.num_programs(1)
    tm = uc_ref.shape[1]
    gd = wpool_ref.shape[1]
    ext_scr[0:tm, :] = jnp.where(i > 0, up_ref[0].astype(F32), 0.0)
    ext_scr[tm:2 * tm, :] = uc_ref[0].astype(F32)
    ext_scr[2 * tm:, :] = jnp.where(i < nt - 1, un_ref[0].astype(F32), 0.0)

    y = None
    for g, w in enumerate(POOL_WINDOWS):
        gs = slice(g * gd, (g + 1) * gd)
        n = tm + (w - 1) * GRID_W
        lo = tm - (w // 2) * GRID_W
        rsum = ext_scr[lo:lo + n, gs]
        span = 1
        while span < w:
            n -= span * GRID_W
            rsum = rsum[0:n] + rsum[span * GRID_W:span * GRID_W + n]
            span *= 2
        rsum = rsum.astype(BF16)
        parts = [_dot(band_ref[g], rsum[s * MXU_DIM:(s + 1) * MXU_DIM, :]) for s in range(tm // MXU_DIM)]
        mean = jnp.concatenate(parts, axis=0) * icnt_ref[:, g:g + 1]
        diff = mean - ext_scr[tm:2 * tm, gs]
        yg = _dot(diff.astype(BF16), wpool_ref[g]) * pscale_ref[:, gs] * sz_ref[0, :, gs].astype(F32)
        part = _dot(yg.astype(BF16), wout_ref[gs, :])
        y = part if y is None else y + part
    yn = y * lax.rsqrt(jnp.mean(y * y, axis=-1, keepdims=True) + EPS) * postg_ref[...]
    o_ref[0] = x_ref[0] + mod_ref[0, 2:3, :] * yn


def _pool_constants(t):
    rows = t // GRID_W
    bands = np.zeros((len(POOL_WINDOWS), MXU_DIM, MXU_DIM), np.float32)
    icnt = np.zeros((t, LANES), np.float32)
    for g, w in enumerate(POOL_WINDOWS):
        lo_off, hi_off = -(w // 2), w - w // 2
        for p in range(MXU_DIM):
            c = p % GRID_W
            base = p - c
            lo, hi = max(c + lo_off, 0), min(c + hi_off, GRID_W)
            bands[g, p, base + lo:base + hi] = 1.0
        cc = np.arange(GRID_W)
        cnt_c = np.clip(cc + hi_off, 0, GRID_W) - np.clip(cc + lo_off, 0, GRID_W)
        rr = np.arange(rows)
        cnt_r = np.clip(rr + hi_off, 0, rows) - np.clip(rr + lo_off, 0, rows)
        icnt[:, g] = (1.0 / (cnt_r[:, None] * cnt_c[None, :])).reshape(-1)
    return bands, icnt


def _pool_out(u, sz, w_pool_b, pool_scale, w_out_b, x, mod, post_g):
    b, t, e = u.shape
    d = x.shape[-1]
    tm = POOL_TILE
    assert tm >= (max(POOL_WINDOWS) // 2) * GRID_W and tm % MXU_DIM == 0 and t % tm == 0
    nt = t // tm
    gd = e // len(POOL_WINDOWS)
    bands, icnt = _pool_constants(t)
    row = lambda bi, i: (bi, i, 0)
    return pl.pallas_call(
        _pool_out_kernel,
        grid=(b, nt),
        in_specs=[
            pl.BlockSpec((1, tm, e), lambda bi, i: (bi, jnp.maximum(i - 1, 0), 0)),
            pl.BlockSpec((1, tm, e), row),
            pl.BlockSpec((1, tm, e), lambda bi, i: (bi, jnp.minimum(i + 1, nt - 1), 0)),
            pl.BlockSpec((1, tm, e), row),
            _const_spec(bands.shape),
            pl.BlockSpec((tm, LANES), lambda bi, i: (i, 0)),
            _const_spec(w_pool_b.shape),
            _const_spec((1, e)),
            _const_spec(w_out_b.shape),
            pl.BlockSpec((1, tm, d), row),
            pl.BlockSpec((1, 3, d), lambda bi, i: (bi, 0, 0)),
            _const_spec((1, d)),
        ],
        out_specs=pl.BlockSpec((1, tm, d), row),
        out_shape=jax.ShapeDtypeStruct((b, t, d), F32),
        scratch_shapes=[pltpu.VMEM((3 * tm, e), F32)],
        compiler_params=_cparams(("parallel", "parallel")),
        name="pool_out",
    )(u, u, u, sz, jnp.asarray(bands, BF16), jnp.asarray(icnt), w_pool_b, pool_scale.reshape(1, e),
      w_out_b, x, mod, post_g.reshape(1, d))


def _block_diag_tiles(w):
    per = MXU_DIM // QKV_BLOCK
    rows = w.reshape(w.shape[0] // per, MXU_DIM, QKV_BLOCK)
    tiled = jnp.tile(rows, (1, 1, per))
    blk = np.arange(MXU_DIM) // QKV_BLOCK
    return jnp.where(blk[:, None] == blk[None, :], tiled, 0.0)


def _gate_weights(w_f, b_f, w_b, b_b):
    pad = LANES - 4 * N_HEADS
    wg = jnp.pad(jnp.concatenate([w_f, w_b], axis=1), ((0, 0), (0, pad)))
    bg = jnp.pad(jnp.concatenate([b_f, b_b]), (0, pad)).reshape(1, LANES)
    return wg, bg


def kernel(x, c, ctx, c_ctx, w_ada, b_ada, a_norm_pre, a_norm_post, a_w_in, a_conv_w, a_conv_b,
           a_w_q, a_w_k, a_w_v, a_w_gate_f, a_b_gate_f, a_w_gate_b, a_b_gate_b, a_head_norm, a_skip,
           a_w_out, b_norm_pre, b_norm_post, b_w_in, b_w_pool, b_pool_scale, b_w_out):
    bsz, t, d = x.shape
    t_ctx = ctx.shape[1]
    assert t % SCAN_CHUNK == 0 and t_ctx % SCAN_CHUNK == 0 and t % GRID_W == 0

    n_cond = -(-(bsz + 1) // 8) * 8
    cond = jnp.concatenate([c, c_ctx[None], jnp.zeros((n_cond - bsz - 1, d), F32)], axis=0)
    mods = _ada(cond, w_ada, b_ada).reshape(w_ada.shape[0], n_cond, 3, d)
    mod_lat = [mods[i, :bsz] for i in range(w_ada.shape[0])]
    mod_ctx = jnp.broadcast_to(mods[0, bsz][None], (bsz, 3, d))

    w_in_b = a_w_in[0].astype(BF16)
    bdq, bdk, bdv = (_block_diag_tiles(w[0]) for w in (a_w_q, a_w_k, a_w_v))
    wg, bg = _gate_weights(a_w_gate_f[0], a_b_gate_f[0], a_w_gate_b[0], a_b_gate_b[0])
    k_scale = float((a_w_in.shape[2] // 3 // N_HEADS) ** -0.5)
    bdkq, aqk, av = _fold_weights(bdq, bdk, bdv, wg, k_scale)
    pre_args = (a_norm_pre[0], w_in_b, a_conv_w[0], a_conv_b[0], bdkq, bdv.astype(BF16), aqk, av, bg,
                a_head_norm[0], a_skip[0])
    _, kc, ktc, vc, _, _, pre_c = _mlstm_pre(ctx, mod_ctx, *pre_args)
    cols_c, rows_c, _ = _gate_prep(pre_c)
    c0, n0, m0 = _ctx_states(kc, ktc, vc, cols_c, rows_c)

    ql, kl, ktl, vl, g1, g2, pre_l = _mlstm_pre(x, mod_lat[0], *pre_args)
    cols_l, rows_l, cmrep_l = _gate_prep(pre_l)
    scan_in = (ql, kl, ktl, vl, cols_l, rows_l, cmrep_l, c0, n0, m0)
    hb = _scan_bwd(*scan_in)
    x = _scan_fwd_out(*scan_in, hb, g1, g2, a_w_out[0].astype(BF16), x, mod_lat[0], a_norm_post[0])

    u, sz = _pool_pre(x, mod_lat[1], b_norm_pre[0], b_w_in[0].astype(BF16))
    x = _pool_out(u, sz, b_w_pool[0].astype(BF16), b_pool_scale[0], b_w_out[0].astype(BF16),
                  x, mod_lat[1], b_norm_post[0])
    return x
```

```python
import functools

import numpy as np
import jax
import jax.numpy as jnp
from jax import lax
from jax.experimental import pallas as pl
from jax.experimental.pallas import tpu as pltpu

EPS = 1e-6
N_HEADS = 4
QKV_BLOCK = 4
GRID_W = 64
POOL_WINDOWS = (2, 4, 8, 16)

LANES = 128
MXU_DIM = 256
HALO = 16
VMEM_LIMIT_BYTES = 56 * 1024 * 1024

SCAN_CHUNK = 256
PRE_TILE = 512
POOL_TILE = 512
GATE_CHUNKS_PER_STEP = 8
SCAN_CHUNKS_PER_STEP = 2

BF16 = jnp.bfloat16
F32 = jnp.float32


def _dot(a, b):
    return jnp.dot(a, b, preferred_element_type=F32)


def _silu(v):
    return v * jax.nn.sigmoid(v)


def _cparams(sem):
    return pltpu.CompilerParams(dimension_semantics=sem, vmem_limit_bytes=VMEM_LIMIT_BYTES)


def _const_spec(shape):
    nd = len(shape)
    return pl.BlockSpec(shape, lambda *_: (0,) * nd, pipeline_mode=pl.Buffered(1))


def _lane_repeat(col, width):
    return jnp.concatenate([col] * (width // LANES), axis=1) if width > LANES else col


def _ada_kernel(cond_t_ref, w_ref, b_ref, o_ref, *, n_used):
    s = _silu(cond_t_ref[...])
    w = w_ref[0]
    for r in range(n_used):
        o_ref[0, r:r + 1, :] = jnp.sum(w * s[:, r:r + 1], axis=0, keepdims=True) + b_ref[0]
    o_ref[0, n_used:, :] = jnp.zeros((o_ref.shape[1] - n_used, o_ref.shape[2]), F32)


def _ada(cond, n_used, w_ada, b_ada):
    depth, d, n3 = w_ada.shape
    rows = cond.shape[0]
    tn = max(w for w in range(LANES, 1024 + 1, LANES) if n3 % w == 0)
    return pl.pallas_call(
        functools.partial(_ada_kernel, n_used=n_used),
        grid=(depth, n3 // tn),
        in_specs=[
            pl.BlockSpec((d, rows), lambda l, j: (0, 0)),
            pl.BlockSpec((1, d, tn), lambda l, j: (l, 0, j)),
            pl.BlockSpec((1, 1, tn), lambda l, j: (l, 0, j)),
        ],
        out_specs=pl.BlockSpec((1, rows, tn), lambda l, j: (l, 0, j)),
        out_shape=jax.ShapeDtypeStruct((depth, rows, n3), F32),
        compiler_params=_cparams(("parallel", "parallel")),
        name="ada_mod",
    )(cond.T, w_ada, b_ada.reshape(depth, 1, n3))


def _norm_mod(xv, g, shift, scale):
    y = xv * lax.rsqrt(jnp.mean(xv * xv, axis=-1, keepdims=True) + EPS) * g
    return y * (1.0 + scale) + shift


def _scan_along_lanes(x, fwd_rows, lane, combine, fill):
    l = x.shape[1]
    step = 1
    while step < l:
        from_left = jnp.where(lane >= step, pltpu.roll(x, step, axis=1), fill)
        from_right = jnp.where(lane < l - step, pltpu.roll(x, l - step, axis=1), fill)
        x = combine(x, jnp.where(fwd_rows, from_left, from_right))
        step *= 2
    return x


def _gate_vectors(pre_chunks):
    h = N_HEADS
    l = pre_chunks[0].shape[0]
    pt = jnp.concatenate([p.T[0:4 * h, :] for p in pre_chunks], axis=0)
    row = lax.broadcasted_iota(jnp.int32, pt.shape, 0) % (4 * h)
    lane = lax.broadcasted_iota(jnp.int32, pt.shape, 1)
    fwd_rows = row < 2 * h
    log_f = jnp.minimum(pt, 0.0) - jnp.log1p(jnp.exp(-jnp.abs(pt)))
    b = _scan_along_lanes(log_f, fwd_rows, lane, jnp.add, 0.0)
    a = pt - pltpu.roll(b, pt.shape[0] - h, axis=0)
    cmax = _scan_along_lanes(a, fwd_rows, lane, jnp.maximum, -jnp.inf)
    rows = jnp.where((row % (2 * h)) < h, a, b)
    out = []
    for ch in range(len(pre_chunks)):
        rs = slice(ch * 4 * h, (ch + 1) * 4 * h)
        stacked = jnp.concatenate([rows[rs], cmax[rs], jnp.zeros((LANES - 8 * h, l), F32)], axis=0)
        cols = stacked.T
        reps = []
        for direction in range(2):
            for head in range(h):
                j = 4 * h + direction * 2 * h + head
                reps.append(jnp.broadcast_to(cols[:, j:j + 1], (l, LANES)))
        out.append((cols, rows[rs], jnp.concatenate(reps, axis=1)))
    return out


def _fold_weights_kernel(bdq_ref, bdk_ref, bdv_ref, wgq_ref, wgk_ref, wgv_ref, bdkq_ref, aqk_ref, av_ref,
                         *, k_scale):
    hp = lax.Precision.HIGHEST
    dot = lambda a, b: jnp.dot(a, b, precision=hp, preferred_element_type=F32)
    for c in range(bdq_ref.shape[0]):
        bdq = bdq_ref[c]
        bdk = bdk_ref[c] * k_scale
        bdkq = lax.dot_general(bdk, bdq, (((1,), (1,)), ((), ())), precision=hp, preferred_element_type=F32)
        bdkq_ref[c] = bdkq.astype(BF16)
        aqk_ref[c] = (dot(bdq, wgq_ref[c]) + dot(bdk, wgk_ref[c])).astype(BF16)
        av_ref[c] = dot(bdv_ref[c], wgv_ref[c]).astype(BF16)


def _fold_weights(bdq, bdk, bdv, wg, k_scale):
    nt = bdq.shape[0]
    wg3 = wg.reshape(3, nt, MXU_DIM, LANES)
    tile = pl.BlockSpec((nt, MXU_DIM, MXU_DIM), lambda c: (0, 0, 0))
    gate = pl.BlockSpec((nt, MXU_DIM, LANES), lambda c: (0, 0, 0))
    return pl.pallas_call(
        functools.partial(_fold_weights_kernel, k_scale=k_scale),
        grid=(1,),
        in_specs=[tile, tile, tile, gate, gate, gate],
        out_specs=[tile, gate, gate],
        out_shape=[jax.ShapeDtypeStruct((nt, MXU_DIM, MXU_DIM), BF16),
                   jax.ShapeDtypeStruct((nt, MXU_DIM, LANES), BF16),
                   jax.ShapeDtypeStruct((nt, MXU_DIM, LANES), BF16)],
        compiler_params=_cparams(("parallel",)),
        name="fold_weights",
    )(bdq, bdk, bdv, wg3[0], wg3[1], wg3[2])


def _mlstm_pre_kernel(x_ref, halo_ref, mod_ref, g_ref, win_ref, convw_ref, convb_ref,
                      bdkq_ref, bdv_ref, aqk_ref, av_ref, bg_ref, hnorm_ref, skip_ref, *out_and_scratch,
                      states_only):
    if states_only:
        k_ref, kt_ref, v_ref, pre_ref, hn_scr = out_and_scratch
    else:
        q_ref, k_ref, kt_ref, v_ref, g1_ref, g2_ref, pre_ref, hn_scr = out_and_scratch
    i = pl.program_id(1)
    nt = pl.num_programs(1)
    tm = x_ref.shape[1]
    e = k_ref.shape[2]
    shift = mod_ref[0, 0:1, :]
    scale = mod_ref[0, 1:2, :]
    g = g_ref[...]

    hn_scr[0:HALO, :] = _norm_mod(halo_ref[0, 0, 0:HALO, :], g, shift, scale).astype(BF16)
    hn_scr[HALO:HALO + tm, :] = _norm_mod(x_ref[0], g, shift, scale).astype(BF16)
    hn_scr[HALO + tm:, :] = _norm_mod(halo_ref[0, 0, HALO:, :], g, shift, scale).astype(BF16)

    def project(c):
        cs = slice(c * MXU_DIM, (c + 1) * MXU_DIM)
        xm_ext = _dot(hn_scr[...], win_ref[:, cs])
        if states_only:
            return xm_ext, None, None
        hc = hn_scr[HALO:HALO + tm, :]
        return (xm_ext, _dot(hc, win_ref[:, e + c * MXU_DIM:e + (c + 1) * MXU_DIM]),
                _dot(hc, win_ref[:, 2 * e + c * MXU_DIM:2 * e + (c + 1) * MXU_DIM]))

    pre = jnp.zeros((tm, LANES), F32) + bg_ref[...]
    n_chunks = e // MXU_DIM
    queue = [project(c) for c in range(min(2, n_chunks))]
    for c in range(n_chunks):
        cs = slice(c * MXU_DIM, (c + 1) * MXU_DIM)
        xm_ext, z, og = queue.pop(0)
        if c + 2 < n_chunks:
            queue.append(project(c + 2))
        xm_all = jnp.concatenate([jnp.where(i > 0, xm_ext[0:HALO], 0.0), xm_ext[HALO:HALO + tm],
                                  jnp.where(i < nt - 1, xm_ext[HALO + tm:], 0.0)], axis=0)
        xm = xm_all[HALO:HALO + tm]
        xm_prev = pltpu.roll(xm_all, 1, axis=0)[HALO:HALO + tm]
        xm_next = pltpu.roll(xm_all, tm + 2 * HALO - 1, axis=0)[HALO:HALO + tm]
        conv = (xm_prev * convw_ref[0:1, cs] + xm * convw_ref[1:2, cs] + xm_next * convw_ref[2:3, cs]
                + convb_ref[:, cs])
        xc = _silu(conv)
        xc_b = xc.astype(BF16)
        xm_b = xm.astype(BF16)
        k32 = _dot(xc_b, bdkq_ref[c])
        pre = pre + _dot(xc_b, aqk_ref[c]) + _dot(xm_b, av_ref[c])
        k_ref[0, :, cs] = k32.astype(BF16)
        kt_ref[0, cs, :] = k32.T.astype(BF16)
        v_ref[0, :, cs] = _dot(xm_b, bdv_ref[c]).astype(BF16)
        if not states_only:
            sz = _silu(z)
            q_ref[0, :, cs] = xc_b
            g1_ref[0, :, cs] = (hnorm_ref[:, cs] * jax.nn.sigmoid(og) * sz).astype(BF16)
            g2_ref[0, :, cs] = (skip_ref[:, cs] * xc * sz).astype(BF16)

    pre_ref[0] = pre


def _mlstm_pre(x, mod, pre_g, w_in_b, conv_w, conv_b, bdkq, bdv, aqk, av, bg, head_norm, skip, states_only):
    b, t, d = x.shape
    e = w_in_b.shape[1] // 3
    tm = min(PRE_TILE, t)
    nt = t // tm
    xr = x.reshape(b, nt, tm, d)
    zeros = jnp.zeros((b, 1, HALO, d), x.dtype)
    prev = jnp.concatenate([zeros, xr[:, :-1, tm - HALO:, :]], axis=1)
    nxt = jnp.concatenate([xr[:, 1:, :HALO, :], zeros], axis=1)
    halo = jnp.concatenate([prev, nxt], axis=2)

    row = lambda bi, i: (bi, i, 0)
    big = pl.BlockSpec((1, tm, e), row)
    big_t = pl.BlockSpec((1, e, tm), lambda bi, i: (bi, 0, i))
    tok = jax.ShapeDtypeStruct((b, t, e), BF16)
    tok_t = jax.ShapeDtypeStruct((b, e, t), BF16)
    pre = jax.ShapeDtypeStruct((b, t, LANES), F32)
    pre_spec = pl.BlockSpec((1, tm, LANES), row)
    if states_only:
        out_specs, out_shape = [big, big_t, big, pre_spec], [tok, tok_t, tok, pre]
    else:
        out_specs = [big, big, big_t, big, big, big, pre_spec]
        out_shape = [tok, tok, tok_t, tok, tok, tok, pre]
    return pl.pallas_call(
        functools.partial(_mlstm_pre_kernel, states_only=states_only),
        grid=(b, nt),
        in_specs=[
            pl.BlockSpec((1, tm, d), row),
            pl.BlockSpec((1, 1, 2 * HALO, d), lambda bi, i: (bi, i, 0, 0)),
            pl.BlockSpec((1, 3, d), lambda bi, i: (bi, 0, 0)),
            _const_spec((1, d)),
            _const_spec(w_in_b.shape),
            _const_spec(conv_w.shape),
            _const_spec((1, e)),
            _const_spec(bdkq.shape), _const_spec(bdv.shape), _const_spec(aqk.shape), _const_spec(av.shape),
            _const_spec((1, LANES)),
            _const_spec((1, e)), _const_spec((1, e)),
        ],
        out_specs=out_specs,
        out_shape=out_shape,
        scratch_shapes=[pltpu.VMEM((tm + 2 * HALO, d), BF16)],
        compiler_params=_cparams(("parallel", "parallel")),
        name="mlstm_pre",
    )(x, halo, mod, pre_g.reshape(1, d), w_in_b, conv_w, conv_b.reshape(1, e),
      bdkq, bdv, aqk, av, bg, head_norm.reshape(1, e), skip.reshape(1, e))


def _gate_prep_kernel(pre_ref, cols_ref, rows_ref, cmrep_ref):
    nch = pre_ref.shape[1] // SCAN_CHUNK
    chunks = [pre_ref[0, ch * SCAN_CHUNK:(ch + 1) * SCAN_CHUNK, :] for ch in range(nch)]
    for ch, (cols, rows, cmrep) in enumerate(_gate_vectors(chunks)):
        rs = slice(ch * SCAN_CHUNK, (ch + 1) * SCAN_CHUNK)
        cols_ref[0, rs, :] = cols
        rows_ref[0, ch] = rows
        cmrep_ref[0, rs, :] = cmrep


def _gate_prep(pre):
    b, t, _ = pre.shape
    nch = min(GATE_CHUNKS_PER_STEP, t // SCAN_CHUNK)
    tm = nch * SCAN_CHUNK
    row = lambda bi, i: (bi, i, 0)
    return pl.pallas_call(
        _gate_prep_kernel,
        grid=(b, t // tm),
        in_specs=[pl.BlockSpec((1, tm, LANES), row)],
        out_specs=[pl.BlockSpec((1, tm, LANES), row),
                   pl.BlockSpec((1, nch, 4 * N_HEADS, SCAN_CHUNK), lambda bi, i: (bi, i, 0, 0)),
                   pl.BlockSpec((1, tm, 2 * N_HEADS * LANES), row)],
        out_shape=[jax.ShapeDtypeStruct((b, t, LANES), F32),
                   jax.ShapeDtypeStruct((b, t // SCAN_CHUNK, 4 * N_HEADS, SCAN_CHUNK), F32),
                   jax.ShapeDtypeStruct((b, t, 2 * N_HEADS * LANES), F32)],
        compiler_params=_cparams(("parallel", "parallel")),
        name="gate_prep",
    )(pre)


def _scan_chunk(get, cols, rows, cmrep, c_st, n_st, m_st, backward, emit):
    h = N_HEADS
    l = cols.shape[0]
    dh = c_st.shape[-1]
    off = 2 * h if backward else 0
    last = 0 if backward else l - 1
    st = []
    for head in range(h):
        cm_col = cols[:, 4 * h + off + head:4 * h + off + head + 1]
        st.append(dict(
            a_col=cols[:, off + head:off + head + 1],
            b_col=cols[:, off + h + head:off + h + head + 1],
            a_row=rows[off + head:off + head + 1, :],
            total=cols[last:last + 1, off + h + head:off + h + head + 1],
            amax=cm_col[last:last + 1, :],
            m_prev=m_st[head]))

    if emit is not None:
        ti = lax.broadcasted_iota(jnp.int32, (l, l), 0)
        si = lax.broadcasted_iota(jnp.int32, (l, l), 1)
        mask = (si >= ti) if backward else (ti >= si)
        for head in range(h):
            s = st[head]
            q = get("q", head)
            s["u"] = jnp.maximum(s["m_prev"], cmrep[:, head * LANES:(head + 1) * LANES])
            s["p"] = jnp.exp(jnp.where(mask, s["a_row"] - _lane_repeat(s["u"], l), -jnp.inf))
            s["qk"] = _dot(q, get("kt", head))
            s["qc"] = _dot(q, c_st[head].astype(BF16))
        for head in range(h):
            s = st[head]
            sc = s["qk"] * s["p"]
            w_inter = jnp.exp(s["m_prev"] - s["u"])
            num = _dot(sc.astype(BF16), get("v", head)) + _lane_repeat(w_inter, dh) * s["qc"]
            qn = jnp.sum(get("q", head).astype(F32) * n_st[head], axis=-1, keepdims=True)
            den = jnp.sum(sc, axis=-1, keepdims=True) + w_inter[:, 0:1] * qn
            floor = jnp.exp(-(s["b_col"] + s["u"][:, 0:1]))
            emit(head, num / jnp.maximum(jnp.abs(den), floor))

    for head in range(h):
        s = st[head]
        s["u_last"] = jnp.maximum(s["m_prev"], s["amax"])
        s["decay"] = jnp.exp(s["m_prev"] - s["u_last"])
        wkt = (get("kt", head).astype(F32) * jnp.exp(s["a_row"] - s["u_last"])).astype(BF16)
        s["upd"] = _dot(wkt, get("v", head))
    for head in range(h):
        s = st[head]
        wk = get("k", head).astype(F32) * jnp.exp(s["a_col"] - s["u_last"])
        c_st[head] = s["decay"] * c_st[head] + s["upd"]
        n_st[head] = s["decay"] * n_st[head] + jnp.sum(wk, axis=0, keepdims=True)
        m_st[head] = s["total"] + s["u_last"]


def _tile_getter(refs, dh, rows=slice(None)):
    def get(name, head):
        hs = slice(head * dh, (head + 1) * dh)
        if name == "kt":
            return refs["kt"][0, hs, rows]
        return refs[name][0, rows, hs]
    return get


def _ctx_state_kernel(k_ref, kt_ref, v_ref, cols_ref, rows_ref, c_out, n_out, m_out):
    dh = c_out.shape[-1]
    nc = k_ref.shape[1] // SCAN_CHUNK
    c_out[...] = jnp.zeros(c_out.shape, F32)
    n_out[...] = jnp.zeros(n_out.shape, F32)
    m_out[...] = jnp.zeros(m_out.shape, F32)
    refs = dict(k=k_ref, kt=kt_ref, v=v_ref)
    for direction in range(2):
        for step in range(nc):
            c = nc - 1 - step if direction else step
            rs = slice(c * SCAN_CHUNK, (c + 1) * SCAN_CHUNK)
            _scan_chunk(_tile_getter(refs, dh, rs), cols_ref[0, rs, :], rows_ref[0, c], None,
                        c_out.at[0, direction], n_out.at[0, direction], m_out.at[0, direction],
                        bool(direction), None)


def _ctx_states(k, kt, v, cols, rows):
    b, t, e = k.shape
    dh = e // N_HEADS
    nc = t // SCAN_CHUNK
    per_b = lambda nd: (lambda bi: (bi,) + (0,) * (nd - 1))
    return pl.pallas_call(
        _ctx_state_kernel,
        grid=(b,),
        in_specs=[pl.BlockSpec((1, t, e), per_b(3)), pl.BlockSpec((1, e, t), per_b(3)),
                  pl.BlockSpec((1, t, e), per_b(3)),
                  pl.BlockSpec((1, t, LANES), per_b(3)),
                  pl.BlockSpec((1, nc, 4 * N_HEADS, SCAN_CHUNK), per_b(4))],
        out_specs=[pl.BlockSpec((1, 2, N_HEADS, dh, dh), per_b(5)),
                   pl.BlockSpec((1, 2, N_HEADS, 1, dh), per_b(5)),
                   pl.BlockSpec((1, 2, N_HEADS, 1, 1), per_b(5))],
        out_shape=[jax.ShapeDtypeStruct((b, 2, N_HEADS, dh, dh), F32),
                   jax.ShapeDtypeStruct((b, 2, N_HEADS, 1, dh), F32),
                   jax.ShapeDtypeStruct((b, 2, N_HEADS, 1, 1), F32)],
        compiler_params=_cparams(("parallel",)),
        name="ctx_states",
    )(k, kt, v, cols, rows)


def _load_state(c0_ref, n0_ref, m0_ref, c_scr, n_scr, m_scr):
    @pl.when(pl.program_id(1) == 0)
    def _():
        c_scr[...] = c0_ref[0, 0]
        n_scr[...] = n0_ref[0, 0]
        m_scr[...] = m0_ref[0, 0]


def _scan_kernel(q_ref, k_ref, kt_ref, v_ref, cols_ref, rows_ref, cmrep_ref, c0_ref, n0_ref, m0_ref,
                 h_ref, c_scr, n_scr, m_scr, *, backward):
    _load_state(c0_ref, n0_ref, m0_ref, c_scr, n_scr, m_scr)
    dh = c_scr.shape[-1]
    refs = dict(q=q_ref, k=k_ref, kt=kt_ref, v=v_ref)
    chunks = range(rows_ref.shape[1])
    for j in (reversed(chunks) if backward else chunks):
        rs = slice(j * SCAN_CHUNK, (j + 1) * SCAN_CHUNK)

        def emit(head, h, rs=rs):
            h_ref[0, rs, head * dh:(head + 1) * dh] = h.astype(h_ref.dtype)

        _scan_chunk(_tile_getter(refs, dh, rs), cols_ref[0, rs, :], rows_ref[0, j], cmrep_ref[0, rs, :],
                    c_scr, n_scr, m_scr, backward, emit)


def _mixer_out_kernel(hf_ref, hb_ref, g1_ref, g2_ref, wout_ref, x_ref, mod0_ref, postg_ref,
                      mod1_ref, preg1_ref, win1_ref, x1_ref, u_ref, sz_ref):
    e = hf_ref.shape[2]
    dh = e // N_HEADS
    y = None
    for head in range(N_HEADS):
        hs = slice(head * dh, (head + 1) * dh)
        hsum = hf_ref[0, :, hs].astype(F32) + hb_ref[0, :, hs].astype(F32)
        cen = hsum - jnp.mean(hsum, axis=-1, keepdims=True)
        hn = cen * lax.rsqrt(jnp.mean(cen * cen, axis=-1, keepdims=True) + EPS)
        gated = (hn * g1_ref[0, :, hs].astype(F32) + g2_ref[0, :, hs].astype(F32)).astype(BF16)
        part = _dot(gated, wout_ref[hs, :])
        y = part if y is None else y + part
    yn = y * lax.rsqrt(jnp.mean(y * y, axis=-1, keepdims=True) + EPS) * postg_ref[...]
    x1 = x_ref[0] + mod0_ref[0, 2:3, :] * yn
    x1_ref[0] = x1
    _pool_in_proj(x1, mod1_ref, preg1_ref, win1_ref, u_ref, sz_ref)


def _scan_specs(b, t, e, chunks_per_step, reverse):
    dh = e // N_HEADS
    direction = 1 if reverse else 0
    l = SCAN_CHUNK * chunks_per_step
    ns = t // l
    cidx = (lambda c: ns - 1 - c) if reverse else (lambda c: c)
    tok = lambda w: pl.BlockSpec((1, l, w), lambda bi, c: (bi, cidx(c), 0))
    state = lambda shape: pl.BlockSpec((1, 1) + shape, lambda bi, c: (bi, direction) + (0,) * len(shape),
                                       pipeline_mode=pl.Buffered(1))
    specs = [tok(e), tok(e), pl.BlockSpec((1, e, l), lambda bi, c: (bi, 0, cidx(c))), tok(e), tok(LANES),
             pl.BlockSpec((1, chunks_per_step, 4 * N_HEADS, SCAN_CHUNK), lambda bi, c: (bi, cidx(c), 0, 0)),
             pl.BlockSpec((1, l, N_HEADS * LANES), lambda bi, c: (bi, cidx(c), direction)),
             state((N_HEADS, dh, dh)), state((N_HEADS, 1, dh)), state((N_HEADS, 1, 1))]
    scratch = [pltpu.VMEM((N_HEADS, dh, dh), F32), pltpu.VMEM((N_HEADS, 1, dh), F32),
               pltpu.VMEM((N_HEADS, 1, 1), F32)]
    return specs, scratch, tok, ns


def _scan(q, k, kt, v, cols, rows, cmrep, c0, n0, m0, backward):
    b, t, e = q.shape
    specs, scratch, tok, ns = _scan_specs(b, t, e, SCAN_CHUNKS_PER_STEP, backward)
    return pl.pallas_call(
        functools.partial(_scan_kernel, backward=backward),
        grid=(b, ns),
        in_specs=specs,
        out_specs=tok(e),
        out_shape=jax.ShapeDtypeStruct((b, t, e), BF16),
        scratch_shapes=scratch,
        compiler_params=_cparams(("parallel", "arbitrary")),
        name="mlstm_scan_bwd" if backward else "mlstm_scan_fwd",
    )(q, k, kt, v, cols, rows, cmrep, c0, n0, m0)


def _mixer_out(hf, hb, g1, g2, w_out_b, x, mod0, post_g, mod1, pre_g1, w_in1_b):
    b, t, e = hf.shape
    d = x.shape[-1]
    tm = POOL_TILE
    row = lambda bi, i: (bi, i, 0)
    tok = lambda w: pl.BlockSpec((1, tm, w), row)
    modspec = pl.BlockSpec((1, 3, d), lambda bi, i: (bi, 0, 0))
    return pl.pallas_call(
        _mixer_out_kernel,
        grid=(b, t // tm),
        in_specs=[tok(e), tok(e), tok(e), tok(e), _const_spec(w_out_b.shape), tok(d), modspec,
                  _const_spec((1, d)), modspec, _const_spec((1, d)), _const_spec(w_in1_b.shape)],
        out_specs=[tok(d), tok(e), tok(e)],
        out_shape=[jax.ShapeDtypeStruct((b, t, d), F32), jax.ShapeDtypeStruct((b, t, e), BF16),
                   jax.ShapeDtypeStruct((b, t, e), BF16)],
        compiler_params=_cparams(("parallel", "parallel")),
        name="mixer_out_pool_in",
    )(hf, hb, g1, g2, w_out_b, x, mod0, post_g.reshape(1, d), mod1, pre_g1.reshape(1, d), w_in1_b)


def _pool_in_proj(x, mod_ref, g_ref, win_ref, u_ref, sz_ref):
    e = u_ref.shape[2]
    hn = _norm_mod(x, g_ref[...], mod_ref[0, 0:1, :], mod_ref[0, 1:2, :]).astype(BF16)
    for c in range(e // MXU_DIM):
        cs = slice(c * MXU_DIM, (c + 1) * MXU_DIM)
        u_ref[0, :, cs] = _dot(hn, win_ref[:, cs]).astype(BF16)
        sz_ref[0, :, cs] = _silu(_dot(hn, win_ref[:, e + c * MXU_DIM:e + (c + 1) * MXU_DIM])).astype(BF16)


def _pool_out_kernel(up_ref, uc_ref, un_ref, sz_ref, band_ref, icnt_ref, wpool_ref, pscale_ref,
                     wout_ref, x_ref, mod_ref, postg_ref, o_ref, ext_scr):
    i = pl.program_id(1)
    nt = pl.num_programs(1)
    tm = uc_ref.shape[1]
    gd = wpool_ref.shape[1]
    ext_scr[0:tm, :] = jnp.where(i > 0, up_ref[0].astype(F32), 0.0)
    ext_scr[tm:2 * tm, :] = uc_ref[0].astype(F32)
    ext_scr[2 * tm:, :] = jnp.where(i < nt - 1, un_ref[0].astype(F32), 0.0)

    y = None
    for g, w in enumerate(POOL_WINDOWS):
        gs = slice(g * gd, (g + 1) * gd)
        n = tm + (w - 1) * GRID_W
        lo = tm - (w // 2) * GRID_W
        rsum = ext_scr[lo:lo + n, gs]
        span = 1
        while span < w:
            n -= span * GRID_W
            rsum = rsum[0:n] + rsum[span * GRID_W:span * GRID_W + n]
            span *= 2
        rsum = rsum.astype(BF16)
        parts = [_dot(band_ref[g], rsum[s * MXU_DIM:(s + 1) * MXU_DIM, :]) for s in range(tm // MXU_DIM)]
        mean = jnp.concatenate(parts, axis=0) * icnt_ref[:, g:g + 1]
        diff = mean - ext_scr[tm:2 * tm, gs]
        yg = _dot(diff.astype(BF16), wpool_ref[g]) * pscale_ref[:, gs] * sz_ref[0, :, gs].astype(F32)
        part = _dot(yg.astype(BF16), wout_ref[gs, :])
        y = part if y is None else y + part
    yn = y * lax.rsqrt(jnp.mean(y * y, axis=-1, keepdims=True) + EPS) * postg_ref[...]
    o_ref[0] = x_ref[0] + mod_ref[0, 2:3, :] * yn


def _pool_constants(t):
    rows = t // GRID_W
    bands = np.zeros((len(POOL_WINDOWS), MXU_DIM, MXU_DIM), np.float32)
    icnt = np.zeros((t, LANES), np.float32)
    for g, w in enumerate(POOL_WINDOWS):
        lo_off, hi_off = -(w // 2), w - w // 2
        for p in range(MXU_DIM):
            c = p % GRID_W
            base = p - c
            lo, hi = max(c + lo_off, 0), min(c + hi_off, GRID_W)
            bands[g, p, base + lo:base + hi] = 1.0
        cc = np.arange(GRID_W)
        cnt_c = np.clip(cc + hi_off, 0, GRID_W) - np.clip(cc + lo_off, 0, GRID_W)
        rr = np.arange(rows)
        cnt_r = np.clip(rr + hi_off, 0, rows) - np.clip(rr + lo_off, 0, rows)
        icnt[:, g] = (1.0 / (cnt_r[:, None] * cnt_c[None, :])).reshape(-1)
    return bands, icnt


def _pool_out(u, sz, w_pool_b, pool_scale, w_out_b, x, mod, post_g):
    b, t, e = u.shape
    d = x.shape[-1]
    tm = POOL_TILE
    assert tm >= (max(POOL_WINDOWS) // 2) * GRID_W and tm % MXU_DIM == 0 and t % tm == 0
    nt = t // tm
    gd = e // len(POOL_WINDOWS)
    bands, icnt = _pool_constants(t)
    row = lambda bi, i: (bi, i, 0)
    return pl.pallas_call(
        _pool_out_kernel,
        grid=(b, nt),
        in_specs=[
            pl.BlockSpec((1, tm, e), lambda bi, i: (bi, jnp.maximum(i - 1, 0), 0)),
            pl.BlockSpec((1, tm, e), row),
            pl.BlockSpec((1, tm, e), lambda bi, i: (bi, jnp.minimum(i + 1, nt - 1), 0)),
            pl.BlockSpec((1, tm, e), row),
            _const_spec(bands.shape),
            pl.BlockSpec((tm, LANES), lambda bi, i: (i, 0)),
            _const_spec(w_pool_b.shape),
            _const_spec((1, e)),
            _const_spec(w_out_b.shape),
            pl.BlockSpec((1, tm, d), row),
            pl.BlockSpec((1, 3, d), lambda bi, i: (bi, 0, 0)),
            _const_spec((1, d)),
        ],
        out_specs=pl.BlockSpec((1, tm, d), row),
        out_shape=jax.ShapeDtypeStruct((b, t, d), F32),
        scratch_shapes=[pltpu.VMEM((3 * tm, e), F32)],
        compiler_params=_cparams(("parallel", "parallel")),
        name="pool_out",
    )(u, u, u, sz, jnp.asarray(bands, BF16), jnp.asarray(icnt), w_pool_b, pool_scale.reshape(1, e),
      w_out_b, x, mod, post_g.reshape(1, d))


def _block_diag_tiles(w):
    per = MXU_DIM // QKV_BLOCK
    rows = w.reshape(w.shape[0] // per, MXU_DIM, QKV_BLOCK)
    tiled = jnp.tile(rows, (1, 1, per))
    blk = np.arange(MXU_DIM) // QKV_BLOCK
    return jnp.where(blk[:, None] == blk[None, :], tiled, 0.0)


def _gate_weights(w_f, b_f, w_b, b_b):
    pad = LANES - 4 * N_HEADS
    wg = jnp.pad(jnp.concatenate([w_f, w_b], axis=1), ((0, 0), (0, pad)))
    bg = jnp.pad(jnp.concatenate([b_f, b_b]), (0, pad)).reshape(1, LANES)
    return wg, bg


def kernel(x, c, ctx, c_ctx, w_ada, b_ada, a_norm_pre, a_norm_post, a_w_in, a_conv_w, a_conv_b,
           a_w_q, a_w_k, a_w_v, a_w_gate_f, a_b_gate_f, a_w_gate_b, a_b_gate_b, a_head_norm, a_skip,
           a_w_out, b_norm_pre, b_norm_post, b_w_in, b_w_pool, b_pool_scale, b_w_out):
    bsz, t, d = x.shape
    t_ctx = ctx.shape[1]
    assert t % SCAN_CHUNK == 0 and t_ctx % SCAN_CHUNK == 0 and t % GRID_W == 0

    n_cond = -(-(bsz + 1) // 8) * 8
    cond = jnp.concatenate([c, c_ctx[None], jnp.zeros((n_cond - bsz - 1, d), F32)], axis=0)
    mods = _ada(cond, bsz + 1, w_ada, b_ada).reshape(w_ada.shape[0], n_cond, 3, d)
    mod_lat = [mods[i, :bsz] for i in range(w_ada.shape[0])]
    mod_ctx = jnp.broadcast_to(mods[0, bsz][None], (bsz, 3, d))

    w_in_b = a_w_in[0].astype(BF16)
    bdq, bdk, bdv = (_block_diag_tiles(w[0]) for w in (a_w_q, a_w_k, a_w_v))
    wg, bg = _gate_weights(a_w_gate_f[0], a_b_gate_f[0], a_w_gate_b[0], a_b_gate_b[0])
    k_scale = float((a_w_in.shape[2] // 3 // N_HEADS) ** -0.5)
    bdkq, aqk, av = _fold_weights(bdq, bdk, bdv, wg, k_scale)
    pre_args = (a_norm_pre[0], w_in_b, a_conv_w[0], a_conv_b[0], bdkq, bdv.astype(BF16), aqk, av, bg,
                a_head_norm[0], a_skip[0])
    kc, ktc, vc, pre_c = _mlstm_pre(ctx, mod_ctx, *pre_args, states_only=True)
    cols_c, rows_c, _ = _gate_prep(pre_c)
    c0, n0, m0 = _ctx_states(kc, ktc, vc, cols_c, rows_c)

    ql, kl, ktl, vl, g1, g2, pre_l = _mlstm_pre(x, mod_lat[0], *pre_args, states_only=False)
    cols_l, rows_l, cmrep_l = _gate_prep(pre_l)
    scan_in = (ql, kl, ktl, vl, cols_l, rows_l, cmrep_l, c0, n0, m0)
    hb = _scan(*scan_in, backward=True)
    hf = _scan(*scan_in, backward=False)

    x, u, sz = _mixer_out(hf, hb, g1, g2, a_w_out[0].astype(BF16), x, mod_lat[0], a_norm_post[0],
                          mod_lat[1], b_norm_pre[0], b_w_in[0].astype(BF16))
    x = _pool_out(u, sz, b_w_pool[0].astype(BF16), b_pool_scale[0], b_w_out[0].astype(BF16),
                  x, mod_lat[1], b_norm_post[0])
    return x
```

```python
import functools

import numpy as np
import jax
import jax.numpy as jnp
from jax import lax
from jax.experimental import pallas as pl
from jax.experimental.pallas import tpu as pltpu

EPS = 1e-6
N_HEADS = 4
QKV_BLOCK = 4
GRID_W = 64
POOL_WINDOWS = (2, 4, 8, 16)

LANES = 128
MXU_DIM = 256
HALO = 16
VMEM_LIMIT_BYTES = 56 * 1024 * 1024

SCAN_CHUNK = 256
PRE_TILE = 512
POOL_TILE = 512
GATE_CHUNKS_PER_STEP = 8
SCAN_CHUNKS_PER_STEP = 2

BF16 = jnp.bfloat16
F32 = jnp.float32


def _dot(a, b):
    return jnp.dot(a, b, preferred_element_type=F32)


def _silu(v):
    return v * jax.nn.sigmoid(v)


def _cparams(sem):
    return pltpu.CompilerParams(dimension_semantics=sem, vmem_limit_bytes=VMEM_LIMIT_BYTES)


def _const_spec(shape):
    nd = len(shape)
    return pl.BlockSpec(shape, lambda *_: (0,) * nd, pipeline_mode=pl.Buffered(1))


def _lane_repeat(col, width):
    return jnp.concatenate([col] * (width // LANES), axis=1) if width > LANES else col


def _ada_kernel(cond_t_ref, w_ref, b_ref, o_ref, *, n_used):
    s = _silu(cond_t_ref[...])
    w = w_ref[0]
    for r in range(n_used):
        o_ref[0, r:r + 1, :] = jnp.sum(w * s[:, r:r + 1], axis=0, keepdims=True) + b_ref[0]
    o_ref[0, n_used:, :] = jnp.zeros((o_ref.shape[1] - n_used, o_ref.shape[2]), F32)


def _ada(cond, n_used, w_ada, b_ada):
    depth, d, n3 = w_ada.shape
    rows = cond.shape[0]
    tn = max(w for w in range(LANES, 1024 + 1, LANES) if n3 % w == 0)
    return pl.pallas_call(
        functools.partial(_ada_kernel, n_used=n_used),
        grid=(depth, n3 // tn),
        in_specs=[
            pl.BlockSpec((d, rows), lambda l, j: (0, 0)),
            pl.BlockSpec((1, d, tn), lambda l, j: (l, 0, j)),
            pl.BlockSpec((1, 1, tn), lambda l, j: (l, 0, j)),
        ],
        out_specs=pl.BlockSpec((1, rows, tn), lambda l, j: (l, 0, j)),
        out_shape=jax.ShapeDtypeStruct((depth, rows, n3), F32),
        compiler_params=_cparams(("parallel", "parallel")),
        name="ada_mod",
    )(cond.T, w_ada, b_ada.reshape(depth, 1, n3))


def _norm_mod(xv, g, shift, scale):
    y = xv * lax.rsqrt(jnp.mean(xv * xv, axis=-1, keepdims=True) + EPS) * g
    return y * (1.0 + scale) + shift


def _scan_along_lanes(x, fwd_rows, lane, combine, fill):
    l = x.shape[1]
    step = 1
    while step < l:
        from_left = jnp.where(lane >= step, pltpu.roll(x, step, axis=1), fill)
        from_right = jnp.where(lane < l - step, pltpu.roll(x, l - step, axis=1), fill)
        x = combine(x, jnp.where(fwd_rows, from_left, from_right))
        step *= 2
    return x


def _gate_vectors(pre_chunks):
    h = N_HEADS
    l = pre_chunks[0].shape[0]
    pt = jnp.concatenate([p.T[0:4 * h, :] for p in pre_chunks], axis=0)
    row = lax.broadcasted_iota(jnp.int32, pt.shape, 0) % (4 * h)
    lane = lax.broadcasted_iota(jnp.int32, pt.shape, 1)
    fwd_rows = row < 2 * h
    log_f = jnp.minimum(pt, 0.0) - jnp.log1p(jnp.exp(-jnp.abs(pt)))
    b = _scan_along_lanes(log_f, fwd_rows, lane, jnp.add, 0.0)
    a = pt - pltpu.roll(b, pt.shape[0] - h, axis=0)
    cmax = _scan_along_lanes(a, fwd_rows, lane, jnp.maximum, -jnp.inf)
    rows = jnp.where((row % (2 * h)) < h, a, b)
    out = []
    for ch in range(len(pre_chunks)):
        rs = slice(ch * 4 * h, (ch + 1) * 4 * h)
        stacked = jnp.concatenate([rows[rs], cmax[rs], jnp.zeros((LANES - 8 * h, l), F32)], axis=0)
        cols = stacked.T
        reps = []
        for direction in range(2):
            for head in range(h):
                j = 4 * h + direction * 2 * h + head
                reps.append(jnp.broadcast_to(cols[:, j:j + 1], (l, LANES)))
        out.append((cols, rows[rs], jnp.concatenate(reps, axis=1)))
    return out


def _fold_weights_kernel(bdq_ref, bdk_ref, bdv_ref, wgq_ref, wgk_ref, wgv_ref, bdkq_ref, aqk_ref, av_ref,
                         *, k_scale):
    hp = lax.Precision.HIGHEST
    dot = lambda a, b: jnp.dot(a, b, precision=hp, preferred_element_type=F32)
    for c in range(bdq_ref.shape[0]):
        bdq = bdq_ref[c]
        bdk = bdk_ref[c] * k_scale
        bdkq = lax.dot_general(bdk, bdq, (((1,), (1,)), ((), ())), precision=hp, preferred_element_type=F32)
        bdkq_ref[c] = bdkq.astype(BF16)
        aqk_ref[c] = (dot(bdq, wgq_ref[c]) + dot(bdk, wgk_ref[c])).astype(BF16)
        av_ref[c] = dot(bdv_ref[c], wgv_ref[c]).astype(BF16)


def _fold_weights(bdq, bdk, bdv, wg, k_scale):
    nt = bdq.shape[0]
    wg3 = wg.reshape(3, nt, MXU_DIM, LANES)
    tile = pl.BlockSpec((nt, MXU_DIM, MXU_DIM), lambda c: (0, 0, 0))
    gate = pl.BlockSpec((nt, MXU_DIM, LANES), lambda c: (0, 0, 0))
    return pl.pallas_call(
        functools.partial(_fold_weights_kernel, k_scale=k_scale),
        grid=(1,),
        in_specs=[tile, tile, tile, gate, gate, gate],
        out_specs=[tile, gate, gate],
        out_shape=[jax.ShapeDtypeStruct((nt, MXU_DIM, MXU_DIM), BF16),
                   jax.ShapeDtypeStruct((nt, MXU_DIM, LANES), BF16),
                   jax.ShapeDtypeStruct((nt, MXU_DIM, LANES), BF16)],
        compiler_params=_cparams(("parallel",)),
        name="fold_weights",
    )(bdq, bdk, bdv, wg3[0], wg3[1], wg3[2])


def _mlstm_pre_kernel(x_ref, halo_ref, mod_ref, g_ref, win_ref, convw_ref, convb_ref,
                      bdkq_ref, bdv_ref, aqk_ref, av_ref, bg_ref, hnorm_ref, skip_ref, *out_and_scratch,
                      states_only):
    if states_only:
        k_ref, kt_ref, v_ref, pre_ref, hn_scr = out_and_scratch
    else:
        q_ref, k_ref, kt_ref, v_ref, g1_ref, g2_ref, pre_ref, hn_scr = out_and_scratch
    i = pl.program_id(1)
    nt = pl.num_programs(1)
    tm = x_ref.shape[1]
    e = k_ref.shape[2]
    shift = mod_ref[0, 0:1, :]
    scale = mod_ref[0, 1:2, :]
    g = g_ref[...]

    hn_scr[0:HALO, :] = _norm_mod(halo_ref[0, 0, 0:HALO, :], g, shift, scale).astype(BF16)
    hn_scr[HALO:HALO + tm, :] = _norm_mod(x_ref[0], g, shift, scale).astype(BF16)
    hn_scr[HALO + tm:, :] = _norm_mod(halo_ref[0, 0, HALO:, :], g, shift, scale).astype(BF16)

    def project(c):
        cs = slice(c * MXU_DIM, (c + 1) * MXU_DIM)
        xm_ext = _dot(hn_scr[...], win_ref[:, cs])
        if states_only:
            return xm_ext, None, None
        hc = hn_scr[HALO:HALO + tm, :]
        return (xm_ext, _dot(hc, win_ref[:, e + c * MXU_DIM:e + (c + 1) * MXU_DIM]),
                _dot(hc, win_ref[:, 2 * e + c * MXU_DIM:2 * e + (c + 1) * MXU_DIM]))

    pre = jnp.zeros((tm, LANES), F32) + bg_ref[...]
    n_chunks = e // MXU_DIM
    queue = [project(c) for c in range(min(2, n_chunks))]
    for c in range(n_chunks):
        cs = slice(c * MXU_DIM, (c + 1) * MXU_DIM)
        xm_ext, z, og = queue.pop(0)
        if c + 2 < n_chunks:
            queue.append(project(c + 2))
        xm_all = jnp.concatenate([jnp.where(i > 0, xm_ext[0:HALO], 0.0), xm_ext[HALO:HALO + tm],
                                  jnp.where(i < nt - 1, xm_ext[HALO + tm:], 0.0)], axis=0)
        xm = xm_all[HALO:HALO + tm]
        xm_prev = pltpu.roll(xm_all, 1, axis=0)[HALO:HALO + tm]
        xm_next = pltpu.roll(xm_all, tm + 2 * HALO - 1, axis=0)[HALO:HALO + tm]
        conv = (xm_prev * convw_ref[0:1, cs] + xm * convw_ref[1:2, cs] + xm_next * convw_ref[2:3, cs]
                + convb_ref[:, cs])
        xc = _silu(conv)
        xc_b = xc.astype(BF16)
        xm_b = xm.astype(BF16)
        k32 = _dot(xc_b, bdkq_ref[c])
        pre = pre + _dot(xc_b, aqk_ref[c]) + _dot(xm_b, av_ref[c])
        k_ref[0, :, cs] = k32.astype(BF16)
        k_t = k32.T.astype(BF16)
        for ch in range(tm // SCAN_CHUNK):
            kt_ref[0, ch, cs, :] = k_t[:, ch * SCAN_CHUNK:(ch + 1) * SCAN_CHUNK]
        v_ref[0, :, cs] = _dot(xm_b, bdv_ref[c]).astype(BF16)
        if not states_only:
            sz = _silu(z)
            q_ref[0, :, cs] = xc_b
            g1_ref[0, :, cs] = (hnorm_ref[:, cs] * jax.nn.sigmoid(og) * sz).astype(BF16)
            g2_ref[0, :, cs] = (skip_ref[:, cs] * xc * sz).astype(BF16)

    pre_ref[0] = pre


def _mlstm_pre(x, mod, pre_g, w_in_b, conv_w, conv_b, bdkq, bdv, aqk, av, bg, head_norm, skip, states_only):
    b, t, d = x.shape
    e = w_in_b.shape[1] // 3
    tm = min(PRE_TILE, t)
    nt = t // tm
    xr = x.reshape(b, nt, tm, d)
    zeros = jnp.zeros((b, 1, HALO, d), x.dtype)
    prev = jnp.concatenate([zeros, xr[:, :-1, tm - HALO:, :]], axis=1)
    nxt = jnp.concatenate([xr[:, 1:, :HALO, :], zeros], axis=1)
    halo = jnp.concatenate([prev, nxt], axis=2)

    row = lambda bi, i: (bi, i, 0)
    big = pl.BlockSpec((1, tm, e), row)
    big_t = pl.BlockSpec((1, tm // SCAN_CHUNK, e, SCAN_CHUNK), lambda bi, i: (bi, i, 0, 0))
    tok = jax.ShapeDtypeStruct((b, t, e), BF16)
    tok_t = jax.ShapeDtypeStruct((b, t // SCAN_CHUNK, e, SCAN_CHUNK), BF16)
    pre = jax.ShapeDtypeStruct((b, t, LANES), F32)
    pre_spec = pl.BlockSpec((1, tm, LANES), row)
    if states_only:
        out_specs, out_shape = [big, big_t, big, pre_spec], [tok, tok_t, tok, pre]
    else:
        out_specs = [big, big, big_t, big, big, big, pre_spec]
        out_shape = [tok, tok, tok_t, tok, tok, tok, pre]
    return pl.pallas_call(
        functools.partial(_mlstm_pre_kernel, states_only=states_only),
        grid=(b, nt),
        in_specs=[
            pl.BlockSpec((1, tm, d), row),
            pl.BlockSpec((1, 1, 2 * HALO, d), lambda bi, i: (bi, i, 0, 0)),
            pl.BlockSpec((1, 3, d), lambda bi, i: (bi, 0, 0)),
            _const_spec((1, d)),
            _const_spec(w_in_b.shape),
            _const_spec(conv_w.shape),
            _const_spec((1, e)),
            _const_spec(bdkq.shape), _const_spec(bdv.shape), _const_spec(aqk.shape), _const_spec(av.shape),
            _const_spec((1, LANES)),
            _const_spec((1, e)), _const_spec((1, e)),
        ],
        out_specs=out_specs,
        out_shape=out_shape,
        scratch_shapes=[pltpu.VMEM((tm + 2 * HALO, d), BF16)],
        compiler_params=_cparams(("parallel", "parallel")),
        name="mlstm_pre",
    )(x, halo, mod, pre_g.reshape(1, d), w_in_b, conv_w, conv_b.reshape(1, e),
      bdkq, bdv, aqk, av, bg, head_norm.reshape(1, e), skip.reshape(1, e))


def _gate_prep_kernel(pre_ref, cols_ref, rows_ref, cmrep_ref):
    nch = pre_ref.shape[1] // SCAN_CHUNK
    chunks = [pre_ref[0, ch * SCAN_CHUNK:(ch + 1) * SCAN_CHUNK, :] for ch in range(nch)]
    for ch, (cols, rows, cmrep) in enumerate(_gate_vectors(chunks)):
        rs = slice(ch * SCAN_CHUNK, (ch + 1) * SCAN_CHUNK)
        cols_ref[0, rs, :] = cols
        rows_ref[0, ch] = rows
        half = N_HEADS * LANES
        cmrep_ref[0, 0, rs, :] = cmrep[:, 0:half]
        cmrep_ref[0, 1, rs, :] = cmrep[:, half:]


def _gate_prep(pre):
    b, t, _ = pre.shape
    nch = min(GATE_CHUNKS_PER_STEP, t // SCAN_CHUNK)
    tm = nch * SCAN_CHUNK
    row = lambda bi, i: (bi, i, 0)
    return pl.pallas_call(
        _gate_prep_kernel,
        grid=(b, t // tm),
        in_specs=[pl.BlockSpec((1, tm, LANES), row)],
        out_specs=[pl.BlockSpec((1, tm, LANES), row),
                   pl.BlockSpec((1, nch, 4 * N_HEADS, SCAN_CHUNK), lambda bi, i: (bi, i, 0, 0)),
                   pl.BlockSpec((1, 2, tm, N_HEADS * LANES), lambda bi, i: (bi, 0, i, 0))],
        out_shape=[jax.ShapeDtypeStruct((b, t, LANES), F32),
                   jax.ShapeDtypeStruct((b, t // SCAN_CHUNK, 4 * N_HEADS, SCAN_CHUNK), F32),
                   jax.ShapeDtypeStruct((b, 2, t, N_HEADS * LANES), F32)],
        compiler_params=_cparams(("parallel", "parallel")),
        name="gate_prep",
    )(pre)


def _scan_chunk(get, cols, rows, cmrep, c_st, n_st, m_st, backward, emit):
    h = N_HEADS
    l = cols.shape[0]
    dh = c_st.shape[-1]
    off = 2 * h if backward else 0
    last = 0 if backward else l - 1
    st = []
    for head in range(h):
        cm_col = cols[:, 4 * h + off + head:4 * h + off + head + 1]
        st.append(dict(
            a_col=cols[:, off + head:off + head + 1],
            b_col=cols[:, off + h + head:off + h + head + 1],
            a_row=rows[off + head:off + head + 1, :],
            total=cols[last:last + 1, off + h + head:off + h + head + 1],
            amax=cm_col[last:last + 1, :],
            m_prev=m_st[head]))

    if emit is not None:
        ti = lax.broadcasted_iota(jnp.int32, (l, l), 0)
        si = lax.broadcasted_iota(jnp.int32, (l, l), 1)
        mask = (si >= ti) if backward else (ti >= si)
        for head in range(h):
            s = st[head]
            q = get("q", head)
            s["u"] = jnp.maximum(s["m_prev"], cmrep[:, head * LANES:(head + 1) * LANES])
            s["p"] = jnp.exp(jnp.where(mask, s["a_row"] - _lane_repeat(s["u"], l), -jnp.inf))
            s["qk"] = _dot(q, get("kt", head))
            s["qc"] = _dot(q, c_st[head].astype(BF16))
            sc = s["qk"] * s["p"]
            w_inter = jnp.exp(s["m_prev"] - s["u"])
            num = _dot(sc.astype(BF16), get("v", head)) + _lane_repeat(w_inter, dh) * s["qc"]
            qn = jnp.sum(get("q", head).astype(F32) * n_st[head], axis=-1, keepdims=True)
            den = jnp.sum(sc, axis=-1, keepdims=True) + w_inter[:, 0:1] * qn
            floor = jnp.exp(-(s["b_col"] + s["u"][:, 0:1]))
            emit(head, num / jnp.maximum(jnp.abs(den), floor))

    for head in range(h):
        s = st[head]
        s["u_last"] = jnp.maximum(s["m_prev"], s["amax"])
        s["decay"] = jnp.exp(s["m_prev"] - s["u_last"])
        wkt = (get("kt", head).astype(F32) * jnp.exp(s["a_row"] - s["u_last"])).astype(BF16)
        s["upd"] = _dot(wkt, get("v", head))
    for head in range(h):
        s = st[head]
        wk = get("k", head).astype(F32) * jnp.exp(s["a_col"] - s["u_last"])
        c_st[head] = s["decay"] * c_st[head] + s["upd"]
        n_st[head] = s["decay"] * n_st[head] + jnp.sum(wk, axis=0, keepdims=True)
        m_st[head] = s["total"] + s["u_last"]


def _tile_getter(refs, dh, chunk):
    rows = slice(chunk * SCAN_CHUNK, (chunk + 1) * SCAN_CHUNK)

    def get(name, head):
        hs = slice(head * dh, (head + 1) * dh)
        if name == "kt":
            return refs["kt"][0, chunk, hs, :]
        return refs[name][0, rows, hs]
    return get


def _ctx_state_kernel(k_ref, kt_ref, v_ref, cols_ref, rows_ref, c_out, n_out, m_out):
    dh = c_out.shape[-1]
    nc = k_ref.shape[1] // SCAN_CHUNK
    c_out[...] = jnp.zeros(c_out.shape, F32)
    n_out[...] = jnp.zeros(n_out.shape, F32)
    m_out[...] = jnp.zeros(m_out.shape, F32)
    refs = dict(k=k_ref, kt=kt_ref, v=v_ref)
    for direction in range(2):
        for step in range(nc):
            c = nc - 1 - step if direction else step
            rs = slice(c * SCAN_CHUNK, (c + 1) * SCAN_CHUNK)
            _scan_chunk(_tile_getter(refs, dh, c), cols_ref[0, rs, :], rows_ref[0, c], None,
                        c_out.at[0, direction], n_out.at[0, direction], m_out.at[0, direction],
                        bool(direction), None)


def _ctx_states(k, kt, v, cols, rows):
    b, t, e = k.shape
    dh = e // N_HEADS
    nc = t // SCAN_CHUNK
    per_b = lambda nd: (lambda bi: (bi,) + (0,) * (nd - 1))
    return pl.pallas_call(
        _ctx_state_kernel,
        grid=(b,),
        in_specs=[pl.BlockSpec((1, t, e), per_b(3)), pl.BlockSpec((1, nc, e, SCAN_CHUNK), per_b(4)),
                  pl.BlockSpec((1, t, e), per_b(3)),
                  pl.BlockSpec((1, t, LANES), per_b(3)),
                  pl.BlockSpec((1, nc, 4 * N_HEADS, SCAN_CHUNK), per_b(4))],
        out_specs=[pl.BlockSpec((1, 2, N_HEADS, dh, dh), per_b(5)),
                   pl.BlockSpec((1, 2, N_HEADS, 1, dh), per_b(5)),
                   pl.BlockSpec((1, 2, N_HEADS, 1, 1), per_b(5))],
        out_shape=[jax.ShapeDtypeStruct((b, 2, N_HEADS, dh, dh), F32),
                   jax.ShapeDtypeStruct((b, 2, N_HEADS, 1, dh), F32),
                   jax.ShapeDtypeStruct((b, 2, N_HEADS, 1, 1), F32)],
        compiler_params=_cparams(("parallel",)),
        name="ctx_states",
    )(k, kt, v, cols, rows)


def _load_state(c0_ref, n0_ref, m0_ref, c_scr, n_scr, m_scr):
    @pl.when(pl.program_id(1) == 0)
    def _():
        c_scr[...] = c0_ref[0, 0]
        n_scr[...] = n0_ref[0, 0]
        m_scr[...] = m0_ref[0, 0]


def _scan_kernel(q_ref, k_ref, kt_ref, v_ref, cols_ref, rows_ref, cmrep_ref, c0_ref, n0_ref, m0_ref,
                 h_ref, c_scr, n_scr, m_scr, *, backward):
    _load_state(c0_ref, n0_ref, m0_ref, c_scr, n_scr, m_scr)
    dh = c_scr.shape[-1]
    refs = dict(q=q_ref, k=k_ref, kt=kt_ref, v=v_ref)
    chunks = range(rows_ref.shape[1])
    for j in (reversed(chunks) if backward else chunks):
        rs = slice(j * SCAN_CHUNK, (j + 1) * SCAN_CHUNK)

        def emit(head, h, rs=rs):
            h_ref[0, rs, head * dh:(head + 1) * dh] = h.astype(h_ref.dtype)

        _scan_chunk(_tile_getter(refs, dh, j), cols_ref[0, rs, :], rows_ref[0, j], cmrep_ref[0, 0, rs, :],
                    c_scr, n_scr, m_scr, backward, emit)


def _mixer_out_kernel(hf_ref, hb_ref, g1_ref, g2_ref, wout_ref, x_ref, mod0_ref, postg_ref,
                      mod1_ref, preg1_ref, win1_ref, x1_ref, u_ref, sz_ref):
    e = hf_ref.shape[2]
    dh = e // N_HEADS
    y = None
    for head in range(N_HEADS):
        hs = slice(head * dh, (head + 1) * dh)
        hsum = hf_ref[0, :, hs].astype(F32) + hb_ref[0, :, hs].astype(F32)
        cen = hsum - jnp.mean(hsum, axis=-1, keepdims=True)
        hn = cen * lax.rsqrt(jnp.mean(cen * cen, axis=-1, keepdims=True) + EPS)
        gated = (hn * g1_ref[0, :, hs].astype(F32) + g2_ref[0, :, hs].astype(F32)).astype(BF16)
        part = _dot(gated, wout_ref[hs, :])
        y = part if y is None else y + part
    yn = y * lax.rsqrt(jnp.mean(y * y, axis=-1, keepdims=True) + EPS) * postg_ref[...]
    x1 = x_ref[0] + mod0_ref[0, 2:3, :] * yn
    x1_ref[0] = x1
    _pool_in_proj(x1, mod1_ref, preg1_ref, win1_ref, u_ref, sz_ref)


def _scan_specs(b, t, e, chunks_per_step, reverse):
    dh = e // N_HEADS
    direction = 1 if reverse else 0
    l = SCAN_CHUNK * chunks_per_step
    ns = t // l
    cidx = (lambda c: ns - 1 - c) if reverse else (lambda c: c)
    tok = lambda w: pl.BlockSpec((1, l, w), lambda bi, c: (bi, cidx(c), 0))
    state = lambda shape: pl.BlockSpec((1, 1) + shape, lambda bi, c: (bi, direction) + (0,) * len(shape),
                                       pipeline_mode=pl.Buffered(1))
    specs = [tok(e), tok(e),
             pl.BlockSpec((1, chunks_per_step, e, SCAN_CHUNK), lambda bi, c: (bi, cidx(c), 0, 0)),
             tok(e), tok(LANES),
             pl.BlockSpec((1, chunks_per_step, 4 * N_HEADS, SCAN_CHUNK), lambda bi, c: (bi, cidx(c), 0, 0)),
             pl.BlockSpec((1, 1, l, N_HEADS * LANES), lambda bi, c: (bi, direction, cidx(c), 0)),
             state((N_HEADS, dh, dh)), state((N_HEADS, 1, dh)), state((N_HEADS, 1, 1))]
    scratch = [pltpu.VMEM((N_HEADS, dh, dh), F32), pltpu.VMEM((N_HEADS, 1, dh), F32),
               pltpu.VMEM((N_HEADS, 1, 1), F32)]
    return specs, scratch, tok, ns


def _scan(q, k, kt, v, cols, rows, cmrep, c0, n0, m0, backward):
    b, t, e = q.shape
    specs, scratch, tok, ns = _scan_specs(b, t, e, SCAN_CHUNKS_PER_STEP, backward)
    return pl.pallas_call(
        functools.partial(_scan_kernel, backward=backward),
        grid=(b, ns),
        in_specs=specs,
        out_specs=tok(e),
        out_shape=jax.ShapeDtypeStruct((b, t, e), BF16),
        scratch_shapes=scratch,
        compiler_params=_cparams(("parallel", "arbitrary")),
        name="mlstm_scan_bwd" if backward else "mlstm_scan_fwd",
    )(q, k, kt, v, cols, rows, cmrep, c0, n0, m0)


def _mixer_out(hf, hb, g1, g2, w_out_b, x, mod0, post_g, mod1, pre_g1, w_in1_b):
    b, t, e = hf.shape
    d = x.shape[-1]
    tm = POOL_TILE
    row = lambda bi, i: (bi, i, 0)
    tok = lambda w: pl.BlockSpec((1, tm, w), row)
    modspec = pl.BlockSpec((1, 3, d), lambda bi, i: (bi, 0, 0))
    return pl.pallas_call(
        _mixer_out_kernel,
        grid=(b, t // tm),
        in_specs=[tok(e), tok(e), tok(e), tok(e), _const_spec(w_out_b.shape), tok(d), modspec,
                  _const_spec((1, d)), modspec, _const_spec((1, d)), _const_spec(w_in1_b.shape)],
        out_specs=[tok(d), tok(e), tok(e)],
        out_shape=[jax.ShapeDtypeStruct((b, t, d), F32), jax.ShapeDtypeStruct((b, t, e), BF16),
                   jax.ShapeDtypeStruct((b, t, e), BF16)],
        compiler_params=_cparams(("parallel", "parallel")),
        name="mixer_out_pool_in",
    )(hf, hb, g1, g2, w_out_b, x, mod0, post_g.reshape(1, d), mod1, pre_g1.reshape(1, d), w_in1_b)


def _pool_in_proj(x, mod_ref, g_ref, win_ref, u_ref, sz_ref):
    e = u_ref.shape[2]
    hn = _norm_mod(x, g_ref[...], mod_ref[0, 0:1, :], mod_ref[0, 1:2, :]).astype(BF16)
    for c in range(e // MXU_DIM):
        cs = slice(c * MXU_DIM, (c + 1) * MXU_DIM)
        u_ref[0, :, cs] = _dot(hn, win_ref[:, cs]).astype(BF16)
        sz_ref[0, :, cs] = _silu(_dot(hn, win_ref[:, e + c * MXU_DIM:e + (c + 1) * MXU_DIM])).astype(BF16)


def _pool_out_kernel(up_ref, uc_ref, un_ref, sz_ref, band_ref, icnt_ref, wpool_ref, pscale_ref,
                     wout_ref, x_ref, mod_ref, postg_ref, o_ref, ext_scr):
    i = pl.program_id(1)
    nt = pl.num_programs(1)
    tm = uc_ref.shape[1]
    gd = wpool_ref.shape[1]
    ext_scr[0:tm, :] = jnp.where(i > 0, up_ref[0].astype(F32), 0.0)
    ext_scr[tm:2 * tm, :] = uc_ref[0].astype(F32)
    ext_scr[2 * tm:, :] = jnp.where(i < nt - 1, un_ref[0].astype(F32), 0.0)

    y = None
    for g, w in enumerate(POOL_WINDOWS):
        gs = slice(g * gd, (g + 1) * gd)
        n = tm + (w - 1) * GRID_W
        lo = tm - (w // 2) * GRID_W
        rsum = ext_scr[lo:lo + n, gs]
        span = 1
        while span < w:
            n -= span * GRID_W
            rsum = rsum[0:n] + rsum[span * GRID_W:span * GRID_W + n]
            span *= 2
        rsum = rsum.astype(BF16)
        parts = [_dot(band_ref[g], rsum[s * MXU_DIM:(s + 1) * MXU_DIM, :]) for s in range(tm // MXU_DIM)]
        mean = jnp.concatenate(parts, axis=0) * icnt_ref[:, g:g + 1]
        diff = mean - ext_scr[tm:2 * tm, gs]
        yg = _dot(diff.astype(BF16), wpool_ref[g]) * pscale_ref[:, gs] * sz_ref[0, :, gs].astype(F32)
        part = _dot(yg.astype(BF16), wout_ref[gs, :])
        y = part if y is None else y + part
    yn = y * lax.rsqrt(jnp.mean(y * y, axis=-1, keepdims=True) + EPS) * postg_ref[...]
    o_ref[0] = x_ref[0] + mod_ref[0, 2:3, :] * yn


def _pool_constants(t):
    rows = t // GRID_W
    bands = np.zeros((len(POOL_WINDOWS), MXU_DIM, MXU_DIM), np.float32)
    icnt = np.zeros((t, LANES), np.float32)
    for g, w in enumerate(POOL_WINDOWS):
        lo_off, hi_off = -(w // 2), w - w // 2
        for p in range(MXU_DIM):
            c = p % GRID_W
            base = p - c
            lo, hi = max(c + lo_off, 0), min(c + hi_off, GRID_W)
            bands[g, p, base + lo:base + hi] = 1.0
        cc = np.arange(GRID_W)
        cnt_c = np.clip(cc + hi_off, 0, GRID_W) - np.clip(cc + lo_off, 0, GRID_W)
        rr = np.arange(rows)
        cnt_r = np.clip(rr + hi_off, 0, rows) - np.clip(rr + lo_off, 0, rows)
        icnt[:, g] = (1.0 / (cnt_r[:, None] * cnt_c[None, :])).reshape(-1)
    return bands, icnt


def _pool_out(u, sz, w_pool_b, pool_scale, w_out_b, x, mod, post_g):
    b, t, e = u.shape
    d = x.shape[-1]
    tm = POOL_TILE
    assert tm >= (max(POOL_WINDOWS) // 2) * GRID_W and tm % MXU_DIM == 0 and t % tm == 0
    nt = t // tm
    gd = e // len(POOL_WINDOWS)
    bands, icnt = _pool_constants(t)
    row = lambda bi, i: (bi, i, 0)
    return pl.pallas_call(
        _pool_out_kernel,
        grid=(b, nt),
        in_specs=[
            pl.BlockSpec((1, tm, e), lambda bi, i: (bi, jnp.maximum(i - 1, 0), 0)),
            pl.BlockSpec((1, tm, e), row),
            pl.BlockSpec((1, tm, e), lambda bi, i: (bi, jnp.minimum(i + 1, nt - 1), 0)),
            pl.BlockSpec((1, tm, e), row),
            _const_spec(bands.shape),
            pl.BlockSpec((tm, LANES), lambda bi, i: (i, 0)),
            _const_spec(w_pool_b.shape),
            _const_spec((1, e)),
            _const_spec(w_out_b.shape),
            pl.BlockSpec((1, tm, d), row),
            pl.BlockSpec((1, 3, d), lambda bi, i: (bi, 0, 0)),
            _const_spec((1, d)),
        ],
        out_specs=pl.BlockSpec((1, tm, d), row),
        out_shape=jax.ShapeDtypeStruct((b, t, d), F32),
        scratch_shapes=[pltpu.VMEM((3 * tm, e), F32)],
        compiler_params=_cparams(("parallel", "parallel")),
        name="pool_out",
    )(u, u, u, sz, jnp.asarray(bands, BF16), jnp.asarray(icnt), w_pool_b, pool_scale.reshape(1, e),
      w_out_b, x, mod, post_g.reshape(1, d))


def _block_diag_tiles(w):
    per = MXU_DIM // QKV_BLOCK
    rows = w.reshape(w.shape[0] // per, MXU_DIM, QKV_BLOCK)
    tiled = jnp.tile(rows, (1, 1, per))
    blk = np.arange(MXU_DIM) // QKV_BLOCK
    return jnp.where(blk[:, None] == blk[None, :], tiled, 0.0)


def _gate_weights(w_f, b_f, w_b, b_b):
    pad = LANES - 4 * N_HEADS
    wg = jnp.pad(jnp.concatenate([w_f, w_b], axis=1), ((0, 0), (0, pad)))
    bg = jnp.pad(jnp.concatenate([b_f, b_b]), (0, pad)).reshape(1, LANES)
    return wg, bg


def kernel(x, c, ctx, c_ctx, w_ada, b_ada, a_norm_pre, a_norm_post, a_w_in, a_conv_w, a_conv_b,
           a_w_q, a_w_k, a_w_v, a_w_gate_f, a_b_gate_f, a_w_gate_b, a_b_gate_b, a_head_norm, a_skip,
           a_w_out, b_norm_pre, b_norm_post, b_w_in, b_w_pool, b_pool_scale, b_w_out):
    bsz, t, d = x.shape
    t_ctx = ctx.shape[1]
    assert t % SCAN_CHUNK == 0 and t_ctx % SCAN_CHUNK == 0 and t % GRID_W == 0

    n_cond = -(-(bsz + 1) // 8) * 8
    cond = jnp.concatenate([c, c_ctx[None], jnp.zeros((n_cond - bsz - 1, d), F32)], axis=0)
    mods = _ada(cond, bsz + 1, w_ada, b_ada).reshape(w_ada.shape[0], n_cond, 3, d)
    mod_lat = [mods[i, :bsz] for i in range(w_ada.shape[0])]
    mod_ctx = jnp.broadcast_to(mods[0, bsz][None], (bsz, 3, d))

    w_in_b = a_w_in[0].astype(BF16)
    bdq, bdk, bdv = (_block_diag_tiles(w[0]) for w in (a_w_q, a_w_k, a_w_v))
    wg, bg = _gate_weights(a_w_gate_f[0], a_b_gate_f[0], a_w_gate_b[0], a_b_gate_b[0])
    k_scale = float((a_w_in.shape[2] // 3 // N_HEADS) ** -0.5)
    bdkq, aqk, av = _fold_weights(bdq, bdk, bdv, wg, k_scale)
    pre_args = (a_norm_pre[0], w_in_b, a_conv_w[0], a_conv_b[0], bdkq, bdv.astype(BF16), aqk, av, bg,
                a_head_norm[0], a_skip[0])
    kc, ktc, vc, pre_c = _mlstm_pre(ctx, mod_ctx, *pre_args, states_only=True)
    cols_c, rows_c, _ = _gate_prep(pre_c)
    c0, n0, m0 = _ctx_states(kc, ktc, vc, cols_c, rows_c)

    ql, kl, ktl, vl, g1, g2, pre_l = _mlstm_pre(x, mod_lat[0], *pre_args, states_only=False)
    cols_l, rows_l, cmrep_l = _gate_prep(pre_l)
    scan_in = (ql, kl, ktl, vl, cols_l, rows_l, cmrep_l, c0, n0, m0)
    hb = _scan(*scan_in, backward=True)
    hf = _scan(*scan_in, backward=False)

    x, u, sz = _mixer_out(hf, hb, g1, g2, a_w_out[0].astype(BF16), x, mod_lat[0], a_norm_post[0],
                          mod_lat[1], b_norm_pre[0], b_w_in[0].astype(BF16))
    x = _pool_out(u, sz, b_w_pool[0].astype(BF16), b_pool_scale[0], b_w_out[0].astype(BF16),
                  x, mod_lat[1], b_norm_post[0])
    return x
```

```python
import functools

import numpy as np
import jax
import jax.numpy as jnp
from jax import lax
from jax.experimental import pallas as pl
from jax.experimental.pallas import tpu as pltpu

EPS = 1e-6
N_HEADS = 4
QKV_BLOCK = 4
GRID_W = 64
POOL_WINDOWS = (2, 4, 8, 16)

LANES = 128
MXU_DIM = 256
HALO = 16
VMEM_LIMIT_BYTES = 56 * 1024 * 1024

SCAN_CHUNK = 256
PRE_TILE = 512
PROJ_AHEAD = 3
POOL_TILE = 512
SCAN_CHUNKS_PER_STEP = 2

BF16 = jnp.bfloat16
F32 = jnp.float32


def _dot(a, b):
    return jnp.dot(a, b, preferred_element_type=F32)


def _silu(v):
    return v * jax.nn.sigmoid(v)


def _cparams(sem):
    return pltpu.CompilerParams(dimension_semantics=sem, vmem_limit_bytes=VMEM_LIMIT_BYTES)


def _const_spec(shape):
    nd = len(shape)
    return pl.BlockSpec(shape, lambda *_: (0,) * nd, pipeline_mode=pl.Buffered(1))


def _lane_repeat(col, width):
    return jnp.concatenate([col] * (width // LANES), axis=1) if width > LANES else col


def _ada_kernel(cond_t_ref, w_ref, b_ref, o_ref, *, n_used):
    s = _silu(cond_t_ref[...])
    w = w_ref[0]
    for r in range(n_used):
        o_ref[0, r:r + 1, :] = jnp.sum(w * s[:, r:r + 1], axis=0, keepdims=True) + b_ref[0]
    o_ref[0, n_used:, :] = jnp.zeros((o_ref.shape[1] - n_used, o_ref.shape[2]), F32)


def _ada(cond, n_used, w_ada, b_ada):
    depth, d, n3 = w_ada.shape
    rows = cond.shape[0]
    tn = max(w for w in range(LANES, 1024 + 1, LANES) if n3 % w == 0)
    return pl.pallas_call(
        functools.partial(_ada_kernel, n_used=n_used),
        grid=(depth, n3 // tn),
        in_specs=[
            pl.BlockSpec((d, rows), lambda l, j: (0, 0)),
            pl.BlockSpec((1, d, tn), lambda l, j: (l, 0, j)),
            pl.BlockSpec((1, 1, tn), lambda l, j: (l, 0, j)),
        ],
        out_specs=pl.BlockSpec((1, rows, tn), lambda l, j: (l, 0, j)),
        out_shape=jax.ShapeDtypeStruct((depth, rows, n3), F32),
        compiler_params=_cparams(("parallel", "parallel")),
        name="ada_mod",
    )(cond.T, w_ada, b_ada.reshape(depth, 1, n3))


def _norm_mod(xv, g, shift, scale):
    y = xv * lax.rsqrt(jnp.mean(xv * xv, axis=-1, keepdims=True) + EPS) * g
    return y * (1.0 + scale) + shift


def _scan_along_lanes(x, fwd_rows, lane, combine, fill):
    l = x.shape[1]
    step = 1
    while step < l:
        from_left = jnp.where(lane >= step, pltpu.roll(x, step, axis=1), fill)
        from_right = jnp.where(lane < l - step, pltpu.roll(x, l - step, axis=1), fill)
        x = combine(x, jnp.where(fwd_rows, from_left, from_right))
        step *= 2
    return x


def _gate_vectors(pre_chunks):
    h = N_HEADS
    l = pre_chunks[0].shape[0]
    pt = jnp.concatenate([p.T[0:4 * h, :] for p in pre_chunks], axis=0)
    row = lax.broadcasted_iota(jnp.int32, pt.shape, 0) % (4 * h)
    lane = lax.broadcasted_iota(jnp.int32, pt.shape, 1)
    fwd_rows = row < 2 * h
    log_f = jnp.minimum(pt, 0.0) - jnp.log1p(jnp.exp(-jnp.abs(pt)))
    b = _scan_along_lanes(log_f, fwd_rows, lane, jnp.add, 0.0)
    a = pt - pltpu.roll(b, pt.shape[0] - h, axis=0)
    cmax = _scan_along_lanes(a, fwd_rows, lane, jnp.maximum, -jnp.inf)
    rows = jnp.where((row % (2 * h)) < h, a, b)
    out = []
    for ch in range(len(pre_chunks)):
        rs = slice(ch * 4 * h, (ch + 1) * 4 * h)
        stacked = jnp.concatenate([rows[rs], cmax[rs], jnp.zeros((LANES - 8 * h, l), F32)], axis=0)
        cols = stacked.T
        reps = []
        for direction in range(2):
            for head in range(h):
                j = 4 * h + direction * 2 * h + head
                reps.append(jnp.broadcast_to(cols[:, j:j + 1], (l, LANES)))
        out.append((cols, rows[rs], jnp.concatenate(reps, axis=1)))
    return out


def _fold_weights_kernel(bdq_ref, bdk_ref, bdv_ref, wgq_ref, wgk_ref, wgv_ref, bdkq_ref, aqk_ref, av_ref,
                         *, k_scale):
    hp = lax.Precision.HIGHEST
    dot = lambda a, b: jnp.dot(a, b, precision=hp, preferred_element_type=F32)
    for c in range(bdq_ref.shape[0]):
        bdq = bdq_ref[c]
        bdk = bdk_ref[c] * k_scale
        bdkq = lax.dot_general(bdk, bdq, (((1,), (1,)), ((), ())), precision=hp, preferred_element_type=F32)
        bdkq_ref[c] = bdkq.astype(BF16)
        aqk_ref[c] = (dot(bdq, wgq_ref[c]) + dot(bdk, wgk_ref[c])).astype(BF16)
        av_ref[c] = dot(bdv_ref[c], wgv_ref[c]).astype(BF16)


def _fold_weights(bdq, bdk, bdv, wg, k_scale):
    nt = bdq.shape[0]
    wg3 = wg.reshape(3, nt, MXU_DIM, LANES)
    tile = pl.BlockSpec((nt, MXU_DIM, MXU_DIM), lambda c: (0, 0, 0))
    gate = pl.BlockSpec((nt, MXU_DIM, LANES), lambda c: (0, 0, 0))
    return pl.pallas_call(
        functools.partial(_fold_weights_kernel, k_scale=k_scale),
        grid=(1,),
        in_specs=[tile, tile, tile, gate, gate, gate],
        out_specs=[tile, gate, gate],
        out_shape=[jax.ShapeDtypeStruct((nt, MXU_DIM, MXU_DIM), BF16),
                   jax.ShapeDtypeStruct((nt, MXU_DIM, LANES), BF16),
                   jax.ShapeDtypeStruct((nt, MXU_DIM, LANES), BF16)],
        compiler_params=_cparams(("parallel",)),
        name="fold_weights",
    )(bdq, bdk, bdv, wg3[0], wg3[1], wg3[2])


def _mlstm_pre_kernel(x_ref, halo_ref, mod_ref, g_ref, win_ref, convw_ref, convb_ref,
                      bdkq_ref, bdv_ref, aqk_ref, av_ref, bg_ref, hnorm_ref, skip_ref, *out_and_scratch,
                      states_only, tiles_per_seq):
    if states_only:
        k_ref, kt_ref, v_ref, cols_ref, rows_ref, cmrep_ref, hn_scr, pre_scr = out_and_scratch
    else:
        (q_ref, k_ref, kt_ref, v_ref, g1_ref, g2_ref, cols_ref, rows_ref, cmrep_ref,
         hn_scr, pre_scr) = out_and_scratch
    step = pl.program_id(0)
    n_tiles = pl.num_programs(0) - 1
    tm, e = k_ref.shape

    def gate_stage():
        nch = tm // SCAN_CHUNK
        chunks = [pre_scr[ch * SCAN_CHUNK:(ch + 1) * SCAN_CHUNK, :] for ch in range(nch)]
        for ch, (cols, rows, cmrep) in enumerate(_gate_vectors(chunks)):
            rs = slice(ch * SCAN_CHUNK, (ch + 1) * SCAN_CHUNK)
            cols_ref[rs, :] = cols
            rows_ref[ch] = rows
            half = N_HEADS * LANES
            cmrep_ref[0, 0, rs, :] = cmrep[:, 0:half]
            cmrep_ref[0, 1, rs, :] = cmrep[:, half:]

    @pl.when(step == 0)
    def _():
        pre_scr[...] = jnp.zeros(pre_scr.shape, F32)

    @pl.when(step == n_tiles)
    def _():
        gate_stage()

    @pl.when(step < n_tiles)
    def _():
        gate_stage()
        _mlstm_pre_tile(x_ref, halo_ref, mod_ref, g_ref, win_ref, convw_ref, convb_ref, bdkq_ref, bdv_ref,
                        aqk_ref, av_ref, bg_ref, hnorm_ref, skip_ref,
                        None if states_only else (q_ref, g1_ref, g2_ref), k_ref, kt_ref, v_ref,
                        hn_scr, pre_scr, step % tiles_per_seq, tiles_per_seq)


def _mlstm_pre_tile(x_ref, halo_ref, mod_ref, g_ref, win_ref, convw_ref, convb_ref, bdkq_ref, bdv_ref,
                    aqk_ref, av_ref, bg_ref, hnorm_ref, skip_ref, gate_outs, k_ref, kt_ref, v_ref,
                    hn_scr, pre_scr, i, nt):
    states_only = gate_outs is None
    if not states_only:
        q_ref, g1_ref, g2_ref = gate_outs
    tm, e = k_ref.shape
    shift = mod_ref[0, 0:1, :]
    scale = mod_ref[0, 1:2, :]
    g = g_ref[...]

    hn_scr[0:HALO, :] = _norm_mod(halo_ref[0, 0:HALO, :], g, shift, scale).astype(BF16)
    hn_scr[HALO:HALO + tm, :] = _norm_mod(x_ref[...], g, shift, scale).astype(BF16)
    hn_scr[HALO + tm:, :] = _norm_mod(halo_ref[0, HALO:, :], g, shift, scale).astype(BF16)

    def project(c):
        cs = slice(c * MXU_DIM, (c + 1) * MXU_DIM)
        xm_ext = _dot(hn_scr[...], win_ref[:, cs])
        if states_only:
            return xm_ext, None, None
        hc = hn_scr[HALO:HALO + tm, :]
        return (xm_ext, _dot(hc, win_ref[:, e + c * MXU_DIM:e + (c + 1) * MXU_DIM]),
                _dot(hc, win_ref[:, 2 * e + c * MXU_DIM:2 * e + (c + 1) * MXU_DIM]))

    pre = jnp.zeros((tm, LANES), F32) + bg_ref[...]
    n_chunks = e // MXU_DIM
    queue = [project(c) for c in range(min(PROJ_AHEAD, n_chunks))]
    for c in range(n_chunks):
        cs = slice(c * MXU_DIM, (c + 1) * MXU_DIM)
        xm_ext, z, og = queue.pop(0)
        if c + PROJ_AHEAD < n_chunks:
            queue.append(project(c + PROJ_AHEAD))
        xm_all = jnp.concatenate([jnp.where(i > 0, xm_ext[0:HALO], 0.0), xm_ext[HALO:HALO + tm],
                                  jnp.where(i < nt - 1, xm_ext[HALO + tm:], 0.0)], axis=0)
        xm = xm_all[HALO:HALO + tm]
        xm_prev = pltpu.roll(xm_all, 1, axis=0)[HALO:HALO + tm]
        xm_next = pltpu.roll(xm_all, tm + 2 * HALO - 1, axis=0)[HALO:HALO + tm]
        conv = (xm_prev * convw_ref[0:1, cs] + xm * convw_ref[1:2, cs] + xm_next * convw_ref[2:3, cs]
                + convb_ref[:, cs])
        xc = _silu(conv)
        xc_b = xc.astype(BF16)
        xm_b = xm.astype(BF16)
        k32 = _dot(xc_b, bdkq_ref[c])
        pre = pre + _dot(xc_b, aqk_ref[c]) + _dot(xm_b, av_ref[c])
        k_ref[:, cs] = k32.astype(BF16)
        k_t = k32.T.astype(BF16)
        for ch in range(tm // SCAN_CHUNK):
            kt_ref[ch, cs, :] = k_t[:, ch * SCAN_CHUNK:(ch + 1) * SCAN_CHUNK]
        v_ref[:, cs] = _dot(xm_b, bdv_ref[c]).astype(BF16)
        if not states_only:
            sz = _silu(z)
            q_ref[:, cs] = xc_b
            g1_ref[:, cs] = (hnorm_ref[:, cs] * jax.nn.sigmoid(og) * sz).astype(BF16)
            g2_ref[:, cs] = (skip_ref[:, cs] * xc * sz).astype(BF16)

    pre_scr[...] = pre


def _mlstm_pre(x, mod, pre_g, w_in_b, conv_w, conv_b, bdkq, bdv, aqk, av, bg, head_norm, skip, states_only):
    b, t, d = x.shape
    e = w_in_b.shape[1] // 3
    tm = min(PRE_TILE, t)
    nt = t // tm
    xr = x.reshape(b, nt, tm, d)
    zeros = jnp.zeros((b, 1, HALO, d), x.dtype)
    prev = jnp.concatenate([zeros, xr[:, :-1, tm - HALO:, :]], axis=1)
    nxt = jnp.concatenate([xr[:, 1:, :HALO, :], zeros], axis=1)
    halo = jnp.concatenate([prev, nxt], axis=2).reshape(b * nt, 2 * HALO, d)

    n = b * nt
    nch = tm // SCAN_CHUNK
    cur = lambda i: jnp.minimum(i, n - 1)
    prv = lambda i: jnp.maximum(i - 1, 0)
    big = pl.BlockSpec((tm, e), lambda i: (cur(i), 0))
    big_t = pl.BlockSpec((nch, e, SCAN_CHUNK), lambda i: (cur(i), 0, 0))
    tok = jax.ShapeDtypeStruct((b * t, e), BF16)
    tok_t = jax.ShapeDtypeStruct((b * t // SCAN_CHUNK, e, SCAN_CHUNK), BF16)
    gate_specs = [pl.BlockSpec((tm, LANES), lambda i: (prv(i), 0)),
                  pl.BlockSpec((nch, 4 * N_HEADS, SCAN_CHUNK), lambda i: (prv(i), 0, 0)),
                  pl.BlockSpec((1, 2, tm, N_HEADS * LANES), lambda i: (prv(i) // nt, 0, prv(i) % nt, 0))]
    gate_shapes = [jax.ShapeDtypeStruct((b * t, LANES), F32),
                   jax.ShapeDtypeStruct((b * t // SCAN_CHUNK, 4 * N_HEADS, SCAN_CHUNK), F32),
                   jax.ShapeDtypeStruct((b, 2, t, N_HEADS * LANES), F32)]
    if states_only:
        out_specs, out_shape = [big, big_t, big] + gate_specs, [tok, tok_t, tok] + gate_shapes
    else:
        out_specs = [big, big, big_t, big, big, big] + gate_specs
        out_shape = [tok, tok, tok_t, tok, tok, tok] + gate_shapes
    outs = pl.pallas_call(
        functools.partial(_mlstm_pre_kernel, states_only=states_only, tiles_per_seq=nt),
        grid=(n + 1,),
        in_specs=[
            pl.BlockSpec((tm, d), lambda i: (cur(i), 0)),
            pl.BlockSpec((1, 2 * HALO, d), lambda i: (cur(i), 0, 0)),
            pl.BlockSpec((1, 3, d), lambda i: (cur(i) // nt, 0, 0)),
            _const_spec((1, d)),
            _const_spec(w_in_b.shape),
            _const_spec(conv_w.shape),
            _const_spec((1, e)),
            _const_spec(bdkq.shape), _const_spec(bdv.shape), _const_spec(aqk.shape), _const_spec(av.shape),
            _const_spec((1, LANES)),
            _const_spec((1, e)), _const_spec((1, e)),
        ],
        out_specs=out_specs,
        out_shape=out_shape,
        scratch_shapes=[pltpu.VMEM((tm + 2 * HALO, d), BF16), pltpu.VMEM((tm, LANES), F32)],
        compiler_params=_cparams(("arbitrary",)),
        name="mlstm_pre",
    )(x.reshape(b * t, d), halo, mod, pre_g.reshape(1, d), w_in_b, conv_w, conv_b.reshape(1, e),
      bdkq, bdv, aqk, av, bg, head_norm.reshape(1, e), skip.reshape(1, e))
    by_token = lambda o: o.reshape(b, t, o.shape[-1])
    by_chunk = lambda o: o.reshape((b, t // SCAN_CHUNK) + o.shape[1:])
    *main, cols, rows, cmrep = outs
    kt_pos = 1 if states_only else 2
    main = [by_chunk(o) if j == kt_pos else by_token(o) for j, o in enumerate(main)]
    return main + [by_token(cols), by_chunk(rows), cmrep]


def _scan_chunk(get, cols, rows, cmrep, c_st, n_st, m_st, backward, emit):
    h = N_HEADS
    l = cols.shape[0]
    dh = c_st.shape[-1]
    off = 2 * h if backward else 0
    last = 0 if backward else l - 1
    st = []
    for head in range(h):
        cm_col = cols[:, 4 * h + off + head:4 * h + off + head + 1]
        st.append(dict(
            a_col=cols[:, off + head:off + head + 1],
            b_col=cols[:, off + h + head:off + h + head + 1],
            a_row=rows[off + head:off + head + 1, :],
            total=cols[last:last + 1, off + h + head:off + h + head + 1],
            amax=cm_col[last:last + 1, :],
            m_prev=m_st[head]))

    if emit is not None:
        ti = lax.broadcasted_iota(jnp.int32, (l, l), 0)
        si = lax.broadcasted_iota(jnp.int32, (l, l), 1)
        mask = (si >= ti) if backward else (ti >= si)
        for head in range(h):
            s = st[head]
            q = get("q", head)
            s["u"] = jnp.maximum(s["m_prev"], cmrep[:, head * LANES:(head + 1) * LANES])
            s["p"] = jnp.exp(jnp.where(mask, s["a_row"] - _lane_repeat(s["u"], l), -jnp.inf))
            s["qk"] = _dot(q, get("kt", head))
            s["qc"] = _dot(q, c_st[head].astype(BF16))
            sc = s["qk"] * s["p"]
            w_inter = jnp.exp(s["m_prev"] - s["u"])
            num = _dot(sc.astype(BF16), get("v", head)) + _lane_repeat(w_inter, dh) * s["qc"]
            qn = jnp.sum(get("q", head).astype(F32) * n_st[head], axis=-1, keepdims=True)
            den = jnp.sum(sc, axis=-1, keepdims=True) + w_inter[:, 0:1] * qn
            floor = jnp.exp(-(s["b_col"] + s["u"][:, 0:1]))
            emit(head, num / jnp.maximum(jnp.abs(den), floor))

    for head in range(h):
        s = st[head]
        s["u_last"] = jnp.maximum(s["m_prev"], s["amax"])
        s["decay"] = jnp.exp(s["m_prev"] - s["u_last"])
        wkt = (get("kt", head).astype(F32) * jnp.exp(s["a_row"] - s["u_last"])).astype(BF16)
        s["upd"] = _dot(wkt, get("v", head))
    for head in range(h):
        s = st[head]
        wk = get("k", head).astype(F32) * jnp.exp(s["a_col"] - s["u_last"])
        c_st[head] = s["decay"] * c_st[head] + s["upd"]
        n_st[head] = s["decay"] * n_st[head] + jnp.sum(wk, axis=0, keepdims=True)
        m_st[head] = s["total"] + s["u_last"]


def _tile_getter(refs, dh, chunk):
    rows = slice(chunk * SCAN_CHUNK, (chunk + 1) * SCAN_CHUNK)

    def get(name, head):
        hs = slice(head * dh, (head + 1) * dh)
        if name == "kt":
            return refs["kt"][0, chunk, hs, :]
        return refs[name][0, rows, hs]
    return get


def _ctx_state_kernel(k_ref, kt_ref, v_ref, cols_ref, rows_ref, c_out, n_out, m_out):
    dh = c_out.shape[-1]
    nc = k_ref.shape[1] // SCAN_CHUNK
    c_out[...] = jnp.zeros(c_out.shape, F32)
    n_out[...] = jnp.zeros(n_out.shape, F32)
    m_out[...] = jnp.zeros(m_out.shape, F32)
    refs = dict(k=k_ref, kt=kt_ref, v=v_ref)
    for direction in range(2):
        for step in range(nc):
            c = nc - 1 - step if direction else step
            rs = slice(c * SCAN_CHUNK, (c + 1) * SCAN_CHUNK)
            _scan_chunk(_tile_getter(refs, dh, c), cols_ref[0, rs, :], rows_ref[0, c], None,
                        c_out.at[0, direction], n_out.at[0, direction], m_out.at[0, direction],
                        bool(direction), None)


def _ctx_states(k, kt, v, cols, rows):
    b, t, e = k.shape
    dh = e // N_HEADS
    nc = t // SCAN_CHUNK
    per_b = lambda nd: (lambda bi: (bi,) + (0,) * (nd - 1))
    return pl.pallas_call(
        _ctx_state_kernel,
        grid=(b,),
        in_specs=[pl.BlockSpec((1, t, e), per_b(3)), pl.BlockSpec((1, nc, e, SCAN_CHUNK), per_b(4)),
                  pl.BlockSpec((1, t, e), per_b(3)),
                  pl.BlockSpec((1, t, LANES), per_b(3)),
                  pl.BlockSpec((1, nc, 4 * N_HEADS, SCAN_CHUNK), per_b(4))],
        out_specs=[pl.BlockSpec((1, 2, N_HEADS, dh, dh), per_b(5)),
                   pl.BlockSpec((1, 2, N_HEADS, 1, dh), per_b(5)),
                   pl.BlockSpec((1, 2, N_HEADS, 1, 1), per_b(5))],
        out_shape=[jax.ShapeDtypeStruct((b, 2, N_HEADS, dh, dh), F32),
                   jax.ShapeDtypeStruct((b, 2, N_HEADS, 1, dh), F32),
                   jax.ShapeDtypeStruct((b, 2, N_HEADS, 1, 1), F32)],
        compiler_params=_cparams(("parallel",)),
        name="ctx_states",
    )(k, kt, v, cols, rows)


def _load_state(c0_ref, n0_ref, m0_ref, c_scr, n_scr, m_scr):
    @pl.when(pl.program_id(1) == 0)
    def _():
        c_scr[...] = c0_ref[0, 0]
        n_scr[...] = n0_ref[0, 0]
        m_scr[...] = m0_ref[0, 0]


def _scan_kernel(q_ref, k_ref, kt_ref, v_ref, cols_ref, rows_ref, cmrep_ref, c0_ref, n0_ref, m0_ref,
                 h_ref, c_scr, n_scr, m_scr, *, backward):
    _load_state(c0_ref, n0_ref, m0_ref, c_scr, n_scr, m_scr)
    dh = c_scr.shape[-1]
    refs = dict(q=q_ref, k=k_ref, kt=kt_ref, v=v_ref)
    chunks = range(rows_ref.shape[1])
    for j in (reversed(chunks) if backward else chunks):
        rs = slice(j * SCAN_CHUNK, (j + 1) * SCAN_CHUNK)

        def emit(head, h, rs=rs):
            h_ref[0, rs, head * dh:(head + 1) * dh] = h.astype(h_ref.dtype)

        _scan_chunk(_tile_getter(refs, dh, j), cols_ref[0, rs, :], rows_ref[0, j], cmrep_ref[0, 0, rs, :],
                    c_scr, n_scr, m_scr, backward, emit)


def _mixer_out_kernel(hf_ref, hb_ref, g1_ref, g2_ref, wout_ref, x_ref, mod0_ref, postg_ref,
                      mod1_ref, preg1_ref, win1_ref, x1_ref, u_ref, sz_ref):
    e = hf_ref.shape[2]
    dh = e // N_HEADS
    y = None
    for head in range(N_HEADS):
        hs = slice(head * dh, (head + 1) * dh)
        hsum = hf_ref[0, :, hs].astype(F32) + hb_ref[0, :, hs].astype(F32)
        cen = hsum - jnp.mean(hsum, axis=-1, keepdims=True)
        hn = cen * lax.rsqrt(jnp.mean(cen * cen, axis=-1, keepdims=True) + EPS)
        gated = (hn * g1_ref[0, :, hs].astype(F32) + g2_ref[0, :, hs].astype(F32)).astype(BF16)
        part = _dot(gated, wout_ref[hs, :])
        y = part if y is None else y + part
    yn = y * lax.rsqrt(jnp.mean(y * y, axis=-1, keepdims=True) + EPS) * postg_ref[...]
    x1 = x_ref[0] + mod0_ref[0, 2:3, :] * yn
    x1_ref[0] = x1
    _pool_in_proj(x1, mod1_ref, preg1_ref, win1_ref, u_ref, sz_ref)


def _scan_specs(b, t, e, chunks_per_step, reverse):
    dh = e // N_HEADS
    direction = 1 if reverse else 0
    l = SCAN_CHUNK * chunks_per_step
    ns = t // l
    cidx = (lambda c: ns - 1 - c) if reverse else (lambda c: c)
    tok = lambda w: pl.BlockSpec((1, l, w), lambda bi, c: (bi, cidx(c), 0))
    state = lambda shape: pl.BlockSpec((1, 1) + shape, lambda bi, c: (bi, direction) + (0,) * len(shape),
                                       pipeline_mode=pl.Buffered(1))
    specs = [tok(e), tok(e),
             pl.BlockSpec((1, chunks_per_step, e, SCAN_CHUNK), lambda bi, c: (bi, cidx(c), 0, 0)),
             tok(e), tok(LANES),
             pl.BlockSpec((1, chunks_per_step, 4 * N_HEADS, SCAN_CHUNK), lambda bi, c: (bi, cidx(c), 0, 0)),
             pl.BlockSpec((1, 1, l, N_HEADS * LANES), lambda bi, c: (bi, direction, cidx(c), 0)),
             state((N_HEADS, dh, dh)), state((N_HEADS, 1, dh)), state((N_HEADS, 1, 1))]
    scratch = [pltpu.VMEM((N_HEADS, dh, dh), F32), pltpu.VMEM((N_HEADS, 1, dh), F32),
               pltpu.VMEM((N_HEADS, 1, 1), F32)]
    return specs, scratch, tok, ns


def _scan(q, k, kt, v, cols, rows, cmrep, c0, n0, m0, backward):
    b, t, e = q.shape
    specs, scratch, tok, ns = _scan_specs(b, t, e, SCAN_CHUNKS_PER_STEP, backward)
    return pl.pallas_call(
        functools.partial(_scan_kernel, backward=backward),
        grid=(b, ns),
        in_specs=specs,
        out_specs=tok(e),
        out_shape=jax.ShapeDtypeStruct((b, t, e), BF16),
        scratch_shapes=scratch,
        compiler_params=_cparams(("parallel", "arbitrary")),
        name="mlstm_scan_bwd" if backward else "mlstm_scan_fwd",
    )(q, k, kt, v, cols, rows, cmrep, c0, n0, m0)


def _mixer_out(hf, hb, g1, g2, w_out_b, x, mod0, post_g, mod1, pre_g1, w_in1_b):
    b, t, e = hf.shape
    d = x.shape[-1]
    tm = POOL_TILE
    row = lambda bi, i: (bi, i, 0)
    tok = lambda w: pl.BlockSpec((1, tm, w), row)
    modspec = pl.BlockSpec((1, 3, d), lambda bi, i: (bi, 0, 0))
    return pl.pallas_call(
        _mixer_out_kernel,
        grid=(b, t // tm),
        in_specs=[tok(e), tok(e), tok(e), tok(e), _const_spec(w_out_b.shape), tok(d), modspec,
                  _const_spec((1, d)), modspec, _const_spec((1, d)), _const_spec(w_in1_b.shape)],
        out_specs=[tok(d), tok(e), tok(e)],
        out_shape=[jax.ShapeDtypeStruct((b, t, d), F32), jax.ShapeDtypeStruct((b, t, e), BF16),
                   jax.ShapeDtypeStruct((b, t, e), BF16)],
        compiler_params=_cparams(("parallel", "parallel")),
        name="mixer_out_pool_in",
    )(hf, hb, g1, g2, w_out_b, x, mod0, post_g.reshape(1, d), mod1, pre_g1.reshape(1, d), w_in1_b)


def _pool_in_proj(x, mod_ref, g_ref, win_ref, u_ref, sz_ref):
    e = u_ref.shape[2]
    hn = _norm_mod(x, g_ref[...], mod_ref[0, 0:1, :], mod_ref[0, 1:2, :]).astype(BF16)
    for c in range(e // MXU_DIM):
        cs = slice(c * MXU_DIM, (c + 1) * MXU_DIM)
        u_ref[0, :, cs] = _dot(hn, win_ref[:, cs]).astype(BF16)
        sz_ref[0, :, cs] = _silu(_dot(hn, win_ref[:, e + c * MXU_DIM:e + (c + 1) * MXU_DIM])).astype(BF16)


def _pool_out_kernel(up_ref, uc_ref, un_ref, sz_ref, band_ref, icnt_ref, wpool_ref, pscale_ref,
                     wout_ref, x_ref, mod_ref, postg_ref, o_ref, ext_scr):
    i = pl.program_id(1)
    nt = pl.num_programs(1)
    tm = uc_ref.shape[1]
    gd = wpool_ref.shape[1]
    y = None
    for g, w in enumerate(POOL_WINDOWS):
        gs = slice(g * gd, (g + 1) * gd)
        before = (w // 2) * GRID_W
        after = (w - w // 2 - 1) * GRID_W
        ext_scr[tm - before:tm, gs] = jnp.where(i > 0, up_ref[0, tm - before:, gs].astype(F32), 0.0)
        ext_scr[tm:2 * tm, gs] = uc_ref[0, :, gs].astype(F32)
        if after:
            ext_scr[2 * tm:2 * tm + after, gs] = jnp.where(i < nt - 1, un_ref[0, 0:after, gs].astype(F32), 0.0)
        n = tm + (w - 1) * GRID_W
        lo = tm - before
        rsum = ext_scr[lo:lo + n, gs]
        span = 1
        while span < w:
            n -= span * GRID_W
            rsum = rsum[0:n] + rsum[span * GRID_W:span * GRID_W + n]
            span *= 2
        rsum = rsum.astype(BF16)
        parts = [_dot(band_ref[g], rsum[s * MXU_DIM:(s + 1) * MXU_DIM, :]) for s in range(tm // MXU_DIM)]
        mean = jnp.concatenate(parts, axis=0) * icnt_ref[:, g:g + 1]
        diff = mean - ext_scr[tm:2 * tm, gs]
        yg = _dot(diff.astype(BF16), wpool_ref[g]) * pscale_ref[:, gs] * sz_ref[0, :, gs].astype(F32)
        part = _dot(yg.astype(BF16), wout_ref[gs, :])
        y = part if y is None else y + part
    yn = y * lax.rsqrt(jnp.mean(y * y, axis=-1, keepdims=True) + EPS) * postg_ref[...]
    o_ref[0] = x_ref[0] + mod_ref[0, 2:3, :] * yn


def _pool_constants(t):
    rows = t // GRID_W
    bands = np.zeros((len(POOL_WINDOWS), MXU_DIM, MXU_DIM), np.float32)
    icnt = np.zeros((t, LANES), np.float32)
    for g, w in enumerate(POOL_WINDOWS):
        lo_off, hi_off = -(w // 2), w - w // 2
        for p in range(MXU_DIM):
            c = p % GRID_W
            base = p - c
            lo, hi = max(c + lo_off, 0), min(c + hi_off, GRID_W)
            bands[g, p, base + lo:base + hi] = 1.0
        cc = np.arange(GRID_W)
        cnt_c = np.clip(cc + hi_off, 0, GRID_W) - np.clip(cc + lo_off, 0, GRID_W)
        rr = np.arange(rows)
        cnt_r = np.clip(rr + hi_off, 0, rows) - np.clip(rr + lo_off, 0, rows)
        icnt[:, g] = (1.0 / (cnt_r[:, None] * cnt_c[None, :])).reshape(-1)
    return bands, icnt


def _pool_out(u, sz, w_pool_b, pool_scale, w_out_b, x, mod, post_g):
    b, t, e = u.shape
    d = x.shape[-1]
    tm = POOL_TILE
    assert tm >= (max(POOL_WINDOWS) // 2) * GRID_W and tm % MXU_DIM == 0 and t % tm == 0
    nt = t // tm
    gd = e // len(POOL_WINDOWS)
    bands, icnt = _pool_constants(t)
    row = lambda bi, i: (bi, i, 0)
    return pl.pallas_call(
        _pool_out_kernel,
        grid=(b, nt),
        in_specs=[
            pl.BlockSpec((1, tm, e), lambda bi, i: (bi, jnp.maximum(i - 1, 0), 0)),
            pl.BlockSpec((1, tm, e), row),
            pl.BlockSpec((1, tm, e), lambda bi, i: (bi, jnp.minimum(i + 1, nt - 1), 0)),
            pl.BlockSpec((1, tm, e), row),
            _const_spec(bands.shape),
            pl.BlockSpec((tm, LANES), lambda bi, i: (i, 0)),
            _const_spec(w_pool_b.shape),
            _const_spec((1, e)),
            _const_spec(w_out_b.shape),
            pl.BlockSpec((1, tm, d), row),
            pl.BlockSpec((1, 3, d), lambda bi, i: (bi, 0, 0)),
            _const_spec((1, d)),
        ],
        out_specs=pl.BlockSpec((1, tm, d), row),
        out_shape=jax.ShapeDtypeStruct((b, t, d), F32),
        scratch_shapes=[pltpu.VMEM((3 * tm, e), F32)],
        compiler_params=_cparams(("parallel", "parallel")),
        name="pool_out",
    )(u, u, u, sz, jnp.asarray(bands, BF16), jnp.asarray(icnt), w_pool_b, pool_scale.reshape(1, e),
      w_out_b, x, mod, post_g.reshape(1, d))


def _block_diag_tiles(w):
    per = MXU_DIM // QKV_BLOCK
    rows = w.reshape(w.shape[0] // per, MXU_DIM, QKV_BLOCK)
    tiled = jnp.tile(rows, (1, 1, per))
    blk = np.arange(MXU_DIM) // QKV_BLOCK
    return jnp.where(blk[:, None] == blk[None, :], tiled, 0.0)


def _gate_weights(w_f, b_f, w_b, b_b):
    pad = LANES - 4 * N_HEADS
    wg = jnp.pad(jnp.concatenate([w_f, w_b], axis=1), ((0, 0), (0, pad)))
    bg = jnp.pad(jnp.concatenate([b_f, b_b]), (0, pad)).reshape(1, LANES)
    return wg, bg


def kernel(x, c, ctx, c_ctx, w_ada, b_ada, a_norm_pre, a_norm_post, a_w_in, a_conv_w, a_conv_b,
           a_w_q, a_w_k, a_w_v, a_w_gate_f, a_b_gate_f, a_w_gate_b, a_b_gate_b, a_head_norm, a_skip,
           a_w_out, b_norm_pre, b_norm_post, b_w_in, b_w_pool, b_pool_scale, b_w_out):
    bsz, t, d = x.shape
    t_ctx = ctx.shape[1]
    assert t % SCAN_CHUNK == 0 and t_ctx % SCAN_CHUNK == 0 and t % GRID_W == 0

    n_cond = -(-(bsz + 1) // 8) * 8
    cond = jnp.concatenate([c, c_ctx[None], jnp.zeros((n_cond - bsz - 1, d), F32)], axis=0)
    mods = _ada(cond, bsz + 1, w_ada, b_ada).reshape(w_ada.shape[0], n_cond, 3, d)
    mod_lat = [mods[i, :bsz] for i in range(w_ada.shape[0])]
    mod_ctx = jnp.broadcast_to(mods[0, bsz][None], (bsz, 3, d))

    w_in_b = a_w_in[0].astype(BF16)
    bdq, bdk, bdv = (_block_diag_tiles(w[0]) for w in (a_w_q, a_w_k, a_w_v))
    wg, bg = _gate_weights(a_w_gate_f[0], a_b_gate_f[0], a_w_gate_b[0], a_b_gate_b[0])
    k_scale = float((a_w_in.shape[2] // 3 // N_HEADS) ** -0.5)
    bdkq, aqk, av = _fold_weights(bdq, bdk, bdv, wg, k_scale)
    pre_args = (a_norm_pre[0], w_in_b, a_conv_w[0], a_conv_b[0], bdkq, bdv.astype(BF16), aqk, av, bg,
                a_head_norm[0], a_skip[0])
    kc, ktc, vc, cols_c, rows_c, _ = _mlstm_pre(ctx, mod_ctx, *pre_args, states_only=True)
    c0, n0, m0 = _ctx_states(kc, ktc, vc, cols_c, rows_c)

    ql, kl, ktl, vl, g1, g2, cols_l, rows_l, cmrep_l = _mlstm_pre(x, mod_lat[0], *pre_args, states_only=False)
    scan_in = (ql, kl, ktl, vl, cols_l, rows_l, cmrep_l, c0, n0, m0)
    hb = _scan(*scan_in, backward=True)
    hf = _scan(*scan_in, backward=False)

    x, u, sz = _mixer_out(hf, hb, g1, g2, a_w_out[0].astype(BF16), x, mod_lat[0], a_norm_post[0],
                          mod_lat[1], b_norm_pre[0], b_w_in[0].astype(BF16))
    x = _pool_out(u, sz, b_w_pool[0].astype(BF16), b_pool_scale[0], b_w_out[0].astype(BF16),
                  x, mod_lat[1], b_norm_post[0])
    return x
```

```python
import functools

import numpy as np
import jax
import jax.numpy as jnp
from jax import lax
from jax.experimental import pallas as pl
from jax.experimental.pallas import tpu as pltpu

EPS = 1e-6
N_HEADS = 4
QKV_BLOCK = 4
GRID_W = 64
POOL_WINDOWS = (2, 4, 8, 16)

LANES = 128
MXU_DIM = 256
HALO = 16
VMEM_LIMIT_BYTES = 56 * 1024 * 1024

SCAN_CHUNK = 256
PRE_TILE = 512
PROJ_AHEAD = 3
POOL_TILE = 512
SCAN_CHUNKS_PER_STEP = 2
ADA_ROWS_PER_STEP = 256

BF16 = jnp.bfloat16
F32 = jnp.float32


def _dot(a, b):
    return jnp.dot(a, b, preferred_element_type=F32)


def _silu(v):
    return v * jax.nn.sigmoid(v)


def _cparams(sem):
    return pltpu.CompilerParams(dimension_semantics=sem, vmem_limit_bytes=VMEM_LIMIT_BYTES)


def _const_spec(shape):
    nd = len(shape)
    return pl.BlockSpec(shape, lambda *_: (0,) * nd, pipeline_mode=pl.Buffered(1))


def _lane_repeat(col, width):
    return jnp.concatenate([col] * (width // LANES), axis=1) if width > LANES else col


def _ada_kernel(cond_t_ref, w_ref, b_ref, o_ref, *, n_used):
    @pl.when(pl.program_id(1) == 0)
    def _():
        o_ref[0] = jnp.zeros(o_ref.shape[1:], F32)
        o_ref[0, 0:n_used, :] = jnp.broadcast_to(b_ref[0], (n_used, o_ref.shape[2]))

    s = _silu(cond_t_ref[...])
    w = w_ref[0]
    for r in range(n_used):
        o_ref[0, r:r + 1, :] += jnp.sum(w * s[:, r:r + 1], axis=0, keepdims=True)


def _ada(cond, n_used, w_ada, b_ada):
    depth, d, n3 = w_ada.shape
    rows = cond.shape[0]
    tk = ADA_ROWS_PER_STEP
    return pl.pallas_call(
        functools.partial(_ada_kernel, n_used=n_used),
        grid=(depth, d // tk),
        in_specs=[
            pl.BlockSpec((tk, rows), lambda l, k: (k, 0)),
            pl.BlockSpec((1, tk, n3), lambda l, k: (l, k, 0)),
            pl.BlockSpec((1, 1, n3), lambda l, k: (l, 0, 0)),
        ],
        out_specs=pl.BlockSpec((1, rows, n3), lambda l, k: (l, 0, 0)),
        out_shape=jax.ShapeDtypeStruct((depth, rows, n3), F32),
        compiler_params=_cparams(("parallel", "arbitrary")),
        name="ada_mod",
    )(cond.T, w_ada, b_ada.reshape(depth, 1, n3))


def _norm_mod(xv, g, shift, scale):
    y = xv * lax.rsqrt(jnp.mean(xv * xv, axis=-1, keepdims=True) + EPS) * g
    return y * (1.0 + scale) + shift


def _scan_along_lanes(x, fwd_rows, lane, combine, fill):
    l = x.shape[1]
    step = 1
    while step < l:
        from_left = jnp.where(lane >= step, pltpu.roll(x, step, axis=1), fill)
        from_right = jnp.where(lane < l - step, pltpu.roll(x, l - step, axis=1), fill)
        x = combine(x, jnp.where(fwd_rows, from_left, from_right))
        step *= 2
    return x


def _gate_vectors(pre_chunks):
    h = N_HEADS
    l = pre_chunks[0].shape[0]
    pt = jnp.concatenate([p.T[0:4 * h, :] for p in pre_chunks], axis=0)
    row = lax.broadcasted_iota(jnp.int32, pt.shape, 0) % (4 * h)
    lane = lax.broadcasted_iota(jnp.int32, pt.shape, 1)
    fwd_rows = row < 2 * h
    log_f = jnp.minimum(pt, 0.0) - jnp.log1p(jnp.exp(-jnp.abs(pt)))
    b = _scan_along_lanes(log_f, fwd_rows, lane, jnp.add, 0.0)
    a = pt - pltpu.roll(b, pt.shape[0] - h, axis=0)
    cmax = _scan_along_lanes(a, fwd_rows, lane, jnp.maximum, -jnp.inf)
    rows = jnp.where((row % (2 * h)) < h, a, b)
    out = []
    for ch in range(len(pre_chunks)):
        rs = slice(ch * 4 * h, (ch + 1) * 4 * h)
        stacked = jnp.concatenate([rows[rs], cmax[rs], jnp.zeros((LANES - 8 * h, l), F32)], axis=0)
        cols = stacked.T
        reps = []
        for direction in range(2):
            for head in range(h):
                j = 4 * h + direction * 2 * h + head
                reps.append(jnp.broadcast_to(cols[:, j:j + 1], (l, LANES)))
        out.append((cols, rows[rs], jnp.concatenate(reps, axis=1)))
    return out


def _fold_weights_kernel(bdq_ref, bdk_ref, bdv_ref, wgq_ref, wgk_ref, wgv_ref, bdkq_ref, aqk_ref, av_ref,
                         *, k_scale):
    hp = lax.Precision.HIGHEST
    dot = lambda a, b: jnp.dot(a, b, precision=hp, preferred_element_type=F32)
    for c in range(bdq_ref.shape[0]):
        bdq = bdq_ref[c]
        bdk = bdk_ref[c] * k_scale
        bdkq = lax.dot_general(bdk, bdq, (((1,), (1,)), ((), ())), precision=hp, preferred_element_type=F32)
        bdkq_ref[c] = bdkq.astype(BF16)
        aqk_ref[c] = (dot(bdq, wgq_ref[c]) + dot(bdk, wgk_ref[c])).astype(BF16)
        av_ref[c] = dot(bdv_ref[c], wgv_ref[c]).astype(BF16)


def _fold_weights(bdq, bdk, bdv, wg, k_scale):
    nt = bdq.shape[0]
    wg3 = wg.reshape(3, nt, MXU_DIM, LANES)
    tile = pl.BlockSpec((nt, MXU_DIM, MXU_DIM), lambda c: (0, 0, 0))
    gate = pl.BlockSpec((nt, MXU_DIM, LANES), lambda c: (0, 0, 0))
    return pl.pallas_call(
        functools.partial(_fold_weights_kernel, k_scale=k_scale),
        grid=(1,),
        in_specs=[tile, tile, tile, gate, gate, gate],
        out_specs=[tile, gate, gate],
        out_shape=[jax.ShapeDtypeStruct((nt, MXU_DIM, MXU_DIM), BF16),
                   jax.ShapeDtypeStruct((nt, MXU_DIM, LANES), BF16),
                   jax.ShapeDtypeStruct((nt, MXU_DIM, LANES), BF16)],
        compiler_params=_cparams(("parallel",)),
        name="fold_weights",
    )(bdq, bdk, bdv, wg3[0], wg3[1], wg3[2])


def _mlstm_pre_kernel(x_ref, xprev_ref, xnext_ref, mod_ref, g_ref, win_ref, convw_ref, convb_ref,
                      bdkq_ref, bdv_ref, aqk_ref, av_ref, bg_ref, hnorm_ref, skip_ref, *out_and_scratch,
                      states_only, tiles_per_seq):
    if states_only:
        k_ref, kt_ref, v_ref, cols_ref, rows_ref, cmrep_ref, hn_scr, pre_scr = out_and_scratch
    else:
        (q_ref, k_ref, kt_ref, v_ref, g1_ref, g2_ref, cols_ref, rows_ref, cmrep_ref,
         hn_scr, pre_scr) = out_and_scratch
    step = pl.program_id(0)
    n_tiles = pl.num_programs(0) - 1
    tm, e = k_ref.shape

    def gate_stage():
        nch = tm // SCAN_CHUNK
        chunks = [pre_scr[ch * SCAN_CHUNK:(ch + 1) * SCAN_CHUNK, :] for ch in range(nch)]
        for ch, (cols, rows, cmrep) in enumerate(_gate_vectors(chunks)):
            rs = slice(ch * SCAN_CHUNK, (ch + 1) * SCAN_CHUNK)
            cols_ref[rs, :] = cols
            rows_ref[ch] = rows
            half = N_HEADS * LANES
            cmrep_ref[0, 0, rs, :] = cmrep[:, 0:half]
            cmrep_ref[0, 1, rs, :] = cmrep[:, half:]

    @pl.when(step == 0)
    def _():
        pre_scr[...] = jnp.zeros(pre_scr.shape, F32)

    @pl.when(step == n_tiles)
    def _():
        gate_stage()

    @pl.when(step < n_tiles)
    def _():
        gate_stage()
        _mlstm_pre_tile(x_ref, xprev_ref, xnext_ref, mod_ref, g_ref, win_ref, convw_ref, convb_ref, bdkq_ref, bdv_ref,
                        aqk_ref, av_ref, bg_ref, hnorm_ref, skip_ref,
                        None if states_only else (q_ref, g1_ref, g2_ref), k_ref, kt_ref, v_ref,
                        hn_scr, pre_scr, step % tiles_per_seq, tiles_per_seq)


def _mlstm_pre_tile(x_ref, xprev_ref, xnext_ref, mod_ref, g_ref, win_ref, convw_ref, convb_ref, bdkq_ref, bdv_ref,
                    aqk_ref, av_ref, bg_ref, hnorm_ref, skip_ref, gate_outs, k_ref, kt_ref, v_ref,
                    hn_scr, pre_scr, i, nt):
    states_only = gate_outs is None
    if not states_only:
        q_ref, g1_ref, g2_ref = gate_outs
    tm, e = k_ref.shape
    shift = mod_ref[0, 0:1, :]
    scale = mod_ref[0, 1:2, :]
    g = g_ref[...]

    hn_scr[0:HALO, :] = _norm_mod(xprev_ref[...], g, shift, scale).astype(BF16)
    hn_scr[HALO:HALO + tm, :] = _norm_mod(x_ref[...], g, shift, scale).astype(BF16)
    hn_scr[HALO + tm:, :] = _norm_mod(xnext_ref[...], g, shift, scale).astype(BF16)

    def project(c):
        cs = slice(c * MXU_DIM, (c + 1) * MXU_DIM)
        xm_ext = _dot(hn_scr[...], win_ref[:, cs])
        if states_only:
            return xm_ext, None, None
        hc = hn_scr[HALO:HALO + tm, :]
        return (xm_ext, _dot(hc, win_ref[:, e + c * MXU_DIM:e + (c + 1) * MXU_DIM]),
                _dot(hc, win_ref[:, 2 * e + c * MXU_DIM:2 * e + (c + 1) * MXU_DIM]))

    pre = jnp.zeros((tm, LANES), F32) + bg_ref[...]
    n_chunks = e // MXU_DIM
    queue = [project(c) for c in range(min(PROJ_AHEAD, n_chunks))]
    for c in range(n_chunks):
        cs = slice(c * MXU_DIM, (c + 1) * MXU_DIM)
        xm_ext, z, og = queue.pop(0)
        if c + PROJ_AHEAD < n_chunks:
            queue.append(project(c + PROJ_AHEAD))
        xm_all = jnp.concatenate([jnp.where(i > 0, xm_ext[0:HALO], 0.0), xm_ext[HALO:HALO + tm],
                                  jnp.where(i < nt - 1, xm_ext[HALO + tm:], 0.0)], axis=0)
        xm = xm_all[HALO:HALO + tm]
        xm_prev = pltpu.roll(xm_all, 1, axis=0)[HALO:HALO + tm]
        xm_next = pltpu.roll(xm_all, tm + 2 * HALO - 1, axis=0)[HALO:HALO + tm]
        conv = (xm_prev * convw_ref[0:1, cs] + xm * convw_ref[1:2, cs] + xm_next * convw_ref[2:3, cs]
                + convb_ref[:, cs])
        xc = _silu(conv)
        xc_b = xc.astype(BF16)
        xm_b = xm.astype(BF16)
        k32 = _dot(xc_b, bdkq_ref[c])
        pre = pre + _dot(xc_b, aqk_ref[c]) + _dot(xm_b, av_ref[c])
        k_ref[:, cs] = k32.astype(BF16)
        k_t = k32.T.astype(BF16)
        for ch in range(tm // SCAN_CHUNK):
            kt_ref[ch, cs, :] = k_t[:, ch * SCAN_CHUNK:(ch + 1) * SCAN_CHUNK]
        v_ref[:, cs] = _dot(xm_b, bdv_ref[c]).astype(BF16)
        if not states_only:
            sz = _silu(z)
            q_ref[:, cs] = xc_b
            g1_ref[:, cs] = (hnorm_ref[:, cs] * jax.nn.sigmoid(og) * sz).astype(BF16)
            g2_ref[:, cs] = (skip_ref[:, cs] * xc * sz).astype(BF16)

    pre_scr[...] = pre


def _mlstm_pre(x, mod, pre_g, w_in_b, conv_w, conv_b, bdkq, bdv, aqk, av, bg, head_norm, skip, states_only):
    b, t, d = x.shape
    e = w_in_b.shape[1] // 3
    tm = min(PRE_TILE, t)
    nt = t // tm

    n = b * nt
    nch = tm // SCAN_CHUNK
    cur = lambda i: jnp.minimum(i, n - 1)
    prv = lambda i: jnp.maximum(i - 1, 0)
    big = pl.BlockSpec((tm, e), lambda i: (cur(i), 0))
    big_t = pl.BlockSpec((nch, e, SCAN_CHUNK), lambda i: (cur(i), 0, 0))
    tok = jax.ShapeDtypeStruct((b * t, e), BF16)
    tok_t = jax.ShapeDtypeStruct((b * t // SCAN_CHUNK, e, SCAN_CHUNK), BF16)
    gate_specs = [pl.BlockSpec((tm, LANES), lambda i: (prv(i), 0)),
                  pl.BlockSpec((nch, 4 * N_HEADS, SCAN_CHUNK), lambda i: (prv(i), 0, 0)),
                  pl.BlockSpec((1, 2, tm, N_HEADS * LANES), lambda i: (prv(i) // nt, 0, prv(i) % nt, 0))]
    gate_shapes = [jax.ShapeDtypeStruct((b * t, LANES), F32),
                   jax.ShapeDtypeStruct((b * t // SCAN_CHUNK, 4 * N_HEADS, SCAN_CHUNK), F32),
                   jax.ShapeDtypeStruct((b, 2, t, N_HEADS * LANES), F32)]
    if states_only:
        out_specs, out_shape = [big, big_t, big] + gate_specs, [tok, tok_t, tok] + gate_shapes
    else:
        out_specs = [big, big, big_t, big, big, big] + gate_specs
        out_shape = [tok, tok, tok_t, tok, tok, tok] + gate_shapes
    outs = pl.pallas_call(
        functools.partial(_mlstm_pre_kernel, states_only=states_only, tiles_per_seq=nt),
        grid=(n + 1,),
        in_specs=[
            pl.BlockSpec((tm, d), lambda i: (cur(i), 0)),
            pl.BlockSpec((HALO, d), lambda i: (jnp.maximum(cur(i) * (tm // HALO) - 1, 0), 0)),
            pl.BlockSpec((HALO, d), lambda i: (jnp.minimum((cur(i) + 1) * (tm // HALO), b * t // HALO - 1), 0)),
            pl.BlockSpec((1, 3, d), lambda i: (cur(i) // nt, 0, 0)),
            _const_spec((1, d)),
            _const_spec(w_in_b.shape),
            _const_spec(conv_w.shape),
            _const_spec((1, e)),
            _const_spec(bdkq.shape), _const_spec(bdv.shape), _const_spec(aqk.shape), _const_spec(av.shape),
            _const_spec((1, LANES)),
            _const_spec((1, e)), _const_spec((1, e)),
        ],
        out_specs=out_specs,
        out_shape=out_shape,
        scratch_shapes=[pltpu.VMEM((tm + 2 * HALO, d), BF16), pltpu.VMEM((tm, LANES), F32)],
        compiler_params=_cparams(("arbitrary",)),
        name="mlstm_pre",
    )(x.reshape(b * t, d), x.reshape(b * t, d), x.reshape(b * t, d), mod, pre_g.reshape(1, d), w_in_b,
      conv_w, conv_b.reshape(1, e),
      bdkq, bdv, aqk, av, bg, head_norm.reshape(1, e), skip.reshape(1, e))
    by_token = lambda o: o.reshape(b, t, o.shape[-1])
    by_chunk = lambda o: o.reshape((b, t // SCAN_CHUNK) + o.shape[1:])
    *main, cols, rows, cmrep = outs
    kt_pos = 1 if states_only else 2
    main = [by_chunk(o) if j == kt_pos else by_token(o) for j, o in enumerate(main)]
    return main + [by_token(cols), by_chunk(rows), cmrep]


def _scan_chunk(get, cols, rows, cmrep, c_st, n_st, m_st, backward, emit):
    h = N_HEADS
    l = cols.shape[0]
    dh = c_st.shape[-1]
    off = 2 * h if backward else 0
    last = 0 if backward else l - 1
    st = []
    for head in range(h):
        cm_col = cols[:, 4 * h + off + head:4 * h + off + head + 1]
        st.append(dict(
            a_col=cols[:, off + head:off + head + 1],
            b_col=cols[:, off + h + head:off + h + head + 1],
            a_row=rows[off + head:off + head + 1, :],
            total=cols[last:last + 1, off + h + head:off + h + head + 1],
            amax=cm_col[last:last + 1, :],
            m_prev=m_st[head]))

    if emit is not None:
        ti = lax.broadcasted_iota(jnp.int32, (l, l), 0)
        si = lax.broadcasted_iota(jnp.int32, (l, l), 1)
        mask = (si >= ti) if backward else (ti >= si)
        for head in range(h):
            s = st[head]
            q = get("q", head)
            s["u"] = jnp.maximum(s["m_prev"], cmrep[:, head * LANES:(head + 1) * LANES])
            s["p"] = jnp.exp(jnp.where(mask, s["a_row"] - _lane_repeat(s["u"], l), -jnp.inf))
            s["qk"] = _dot(q, get("kt", head))
            s["qc"] = _dot(q, c_st[head].astype(BF16))
            sc = s["qk"] * s["p"]
            w_inter = jnp.exp(s["m_prev"] - s["u"])
            num = _dot(sc.astype(BF16), get("v", head)) + _lane_repeat(w_inter, dh) * s["qc"]
            qn = jnp.sum(get("q", head).astype(F32) * n_st[head], axis=-1, keepdims=True)
            den = jnp.sum(sc, axis=-1, keepdims=True) + w_inter[:, 0:1] * qn
            floor = jnp.exp(-(s["b_col"] + s["u"][:, 0:1]))
            emit(head, num / jnp.maximum(jnp.abs(den), floor))

    for head in range(h):
        s = st[head]
        s["u_last"] = jnp.maximum(s["m_prev"], s["amax"])
        s["decay"] = jnp.exp(s["m_prev"] - s["u_last"])
        wkt = (get("kt", head).astype(F32) * jnp.exp(s["a_row"] - s["u_last"])).astype(BF16)
        s["upd"] = _dot(wkt, get("v", head))
    for head in range(h):
        s = st[head]
        wk = get("k", head).astype(F32) * jnp.exp(s["a_col"] - s["u_last"])
        c_st[head] = s["decay"] * c_st[head] + s["upd"]
        n_st[head] = s["decay"] * n_st[head] + jnp.sum(wk, axis=0, keepdims=True)
        m_st[head] = s["total"] + s["u_last"]


def _tile_getter(refs, dh, chunk):
    rows = slice(chunk * SCAN_CHUNK, (chunk + 1) * SCAN_CHUNK)

    def get(name, head):
        hs = slice(head * dh, (head + 1) * dh)
        if name == "kt":
            return refs["kt"][0, chunk, hs, :]
        return refs[name][0, rows, hs]
    return get


def _ctx_state_kernel(k_ref, kt_ref, v_ref, cols_ref, rows_ref, c_out, n_out, m_out):
    dh = c_out.shape[-1]
    nc = k_ref.shape[1] // SCAN_CHUNK
    c_out[...] = jnp.zeros(c_out.shape, F32)
    n_out[...] = jnp.zeros(n_out.shape, F32)
    m_out[...] = jnp.zeros(m_out.shape, F32)
    refs = dict(k=k_ref, kt=kt_ref, v=v_ref)
    for direction in range(2):
        for step in range(nc):
            c = nc - 1 - step if direction else step
            rs = slice(c * SCAN_CHUNK, (c + 1) * SCAN_CHUNK)
            _scan_chunk(_tile_getter(refs, dh, c), cols_ref[0, rs, :], rows_ref[0, c], None,
                        c_out.at[0, direction], n_out.at[0, direction], m_out.at[0, direction],
                        bool(direction), None)


def _ctx_states(k, kt, v, cols, rows):
    b, t, e = k.shape
    dh = e // N_HEADS
    nc = t // SCAN_CHUNK
    per_b = lambda nd: (lambda bi: (bi,) + (0,) * (nd - 1))
    return pl.pallas_call(
        _ctx_state_kernel,
        grid=(b,),
        in_specs=[pl.BlockSpec((1, t, e), per_b(3)), pl.BlockSpec((1, nc, e, SCAN_CHUNK), per_b(4)),
                  pl.BlockSpec((1, t, e), per_b(3)),
                  pl.BlockSpec((1, t, LANES), per_b(3)),
                  pl.BlockSpec((1, nc, 4 * N_HEADS, SCAN_CHUNK), per_b(4))],
        out_specs=[pl.BlockSpec((1, 2, N_HEADS, dh, dh), per_b(5)),
                   pl.BlockSpec((1, 2, N_HEADS, 1, dh), per_b(5)),
                   pl.BlockSpec((1, 2, N_HEADS, 1, 1), per_b(5))],
        out_shape=[jax.ShapeDtypeStruct((b, 2, N_HEADS, dh, dh), F32),
                   jax.ShapeDtypeStruct((b, 2, N_HEADS, 1, dh), F32),
                   jax.ShapeDtypeStruct((b, 2, N_HEADS, 1, 1), F32)],
        compiler_params=_cparams(("parallel",)),
        name="ctx_states",
    )(k, kt, v, cols, rows)


def _load_state(c0_ref, n0_ref, m0_ref, c_scr, n_scr, m_scr):
    @pl.when(pl.program_id(1) == 0)
    def _():
        c_scr[...] = c0_ref[0, 0]
        n_scr[...] = n0_ref[0, 0]
        m_scr[...] = m0_ref[0, 0]


def _scan_kernel(q_ref, k_ref, kt_ref, v_ref, cols_ref, rows_ref, cmrep_ref, c0_ref, n0_ref, m0_ref,
                 h_ref, c_scr, n_scr, m_scr, *, backward):
    _load_state(c0_ref, n0_ref, m0_ref, c_scr, n_scr, m_scr)
    dh = c_scr.shape[-1]
    refs = dict(q=q_ref, k=k_ref, kt=kt_ref, v=v_ref)
    chunks = range(rows_ref.shape[1])
    for j in (reversed(chunks) if backward else chunks):
        rs = slice(j * SCAN_CHUNK, (j + 1) * SCAN_CHUNK)

        def emit(head, h, rs=rs):
            h_ref[0, rs, head * dh:(head + 1) * dh] = h.astype(h_ref.dtype)

        _scan_chunk(_tile_getter(refs, dh, j), cols_ref[0, rs, :], rows_ref[0, j], cmrep_ref[0, 0, rs, :],
                    c_scr, n_scr, m_scr, backward, emit)


def _mixer_out_kernel(hf_ref, hb_ref, g1_ref, g2_ref, wout_ref, x_ref, mod0_ref, postg_ref,
                      mod1_ref, preg1_ref, win1_ref, x1_ref, u_ref, sz_ref):
    e = hf_ref.shape[2]
    dh = e // N_HEADS
    y = None
    for head in range(N_HEADS):
        hs = slice(head * dh, (head + 1) * dh)
        hsum = hf_ref[0, :, hs].astype(F32) + hb_ref[0, :, hs].astype(F32)
        cen = hsum - jnp.mean(hsum, axis=-1, keepdims=True)
        hn = cen * lax.rsqrt(jnp.mean(cen * cen, axis=-1, keepdims=True) + EPS)
        gated = (hn * g1_ref[0, :, hs].astype(F32) + g2_ref[0, :, hs].astype(F32)).astype(BF16)
        part = _dot(gated, wout_ref[hs, :])
        y = part if y is None else y + part
    yn = y * lax.rsqrt(jnp.mean(y * y, axis=-1, keepdims=True) + EPS) * postg_ref[...]
    x1 = x_ref[0] + mod0_ref[0, 2:3, :] * yn
    x1_ref[0] = x1
    _pool_in_proj(x1, mod1_ref, preg1_ref, win1_ref, u_ref, sz_ref)


def _scan_specs(b, t, e, chunks_per_step, reverse):
    dh = e // N_HEADS
    direction = 1 if reverse else 0
    l = SCAN_CHUNK * chunks_per_step
    ns = t // l
    cidx = (lambda c: ns - 1 - c) if reverse else (lambda c: c)
    tok = lambda w: pl.BlockSpec((1, l, w), lambda bi, c: (bi, cidx(c), 0))
    state = lambda shape: pl.BlockSpec((1, 1) + shape, lambda bi, c: (bi, direction) + (0,) * len(shape),
                                       pipeline_mode=pl.Buffered(1))
    specs = [tok(e), tok(e),
             pl.BlockSpec((1, chunks_per_step, e, SCAN_CHUNK), lambda bi, c: (bi, cidx(c), 0, 0)),
             tok(e), tok(LANES),
             pl.BlockSpec((1, chunks_per_step, 4 * N_HEADS, SCAN_CHUNK), lambda bi, c: (bi, cidx(c), 0, 0)),
             pl.BlockSpec((1, 1, l, N_HEADS * LANES), lambda bi, c: (bi, direction, cidx(c), 0)),
             state((N_HEADS, dh, dh)), state((N_HEADS, 1, dh)), state((N_HEADS, 1, 1))]
    scratch = [pltpu.VMEM((N_HEADS, dh, dh), F32), pltpu.VMEM((N_HEADS, 1, dh), F32),
               pltpu.VMEM((N_HEADS, 1, 1), F32)]
    return specs, scratch, tok, ns


def _scan(q, k, kt, v, cols, rows, cmrep, c0, n0, m0, backward):
    b, t, e = q.shape
    specs, scratch, tok, ns = _scan_specs(b, t, e, SCAN_CHUNKS_PER_STEP, backward)
    return pl.pallas_call(
        functools.partial(_scan_kernel, backward=backward),
        grid=(b, ns),
        in_specs=specs,
        out_specs=tok(e),
        out_shape=jax.ShapeDtypeStruct((b, t, e), BF16),
        scratch_shapes=scratch,
        compiler_params=_cparams(("parallel", "arbitrary")),
        name="mlstm_scan_bwd" if backward else "mlstm_scan_fwd",
    )(q, k, kt, v, cols, rows, cmrep, c0, n0, m0)


def _mixer_out(hf, hb, g1, g2, w_out_b, x, mod0, post_g, mod1, pre_g1, w_in1_b):
    b, t, e = hf.shape
    d = x.shape[-1]
    tm = POOL_TILE
    row = lambda bi, i: (bi, i, 0)
    tok = lambda w: pl.BlockSpec((1, tm, w), row)
    modspec = pl.BlockSpec((1, 3, d), lambda bi, i: (bi, 0, 0))
    return pl.pallas_call(
        _mixer_out_kernel,
        grid=(b, t // tm),
        in_specs=[tok(e), tok(e), tok(e), tok(e), _const_spec(w_out_b.shape), tok(d), modspec,
                  _const_spec((1, d)), modspec, _const_spec((1, d)), _const_spec(w_in1_b.shape)],
        out_specs=[tok(d), tok(e), tok(e)],
        out_shape=[jax.ShapeDtypeStruct((b, t, d), F32), jax.ShapeDtypeStruct((b, t, e), BF16),
                   jax.ShapeDtypeStruct((b, t, e), BF16)],
        compiler_params=_cparams(("parallel", "parallel")),
        name="mixer_out_pool_in",
    )(hf, hb, g1, g2, w_out_b, x, mod0, post_g.reshape(1, d), mod1, pre_g1.reshape(1, d), w_in1_b)


def _pool_in_proj(x, mod_ref, g_ref, win_ref, u_ref, sz_ref):
    e = u_ref.shape[2]
    hn = _norm_mod(x, g_ref[...], mod_ref[0, 0:1, :], mod_ref[0, 1:2, :]).astype(BF16)
    for c in range(e // MXU_DIM):
        cs = slice(c * MXU_DIM, (c + 1) * MXU_DIM)
        u_ref[0, :, cs] = _dot(hn, win_ref[:, cs]).astype(BF16)
        sz_ref[0, :, cs] = _silu(_dot(hn, win_ref[:, e + c * MXU_DIM:e + (c + 1) * MXU_DIM])).astype(BF16)


def _pool_out_kernel(up_ref, uc_ref, un_ref, sz_ref, band_ref, icnt_ref, wpool_ref, pscale_ref,
                     wout_ref, x_ref, mod_ref, postg_ref, o_ref, ext_scr):
    i = pl.program_id(1)
    nt = pl.num_programs(1)
    tm = uc_ref.shape[1]
    gd = wpool_ref.shape[1]
    y = None
    for g, w in enumerate(POOL_WINDOWS):
        gs = slice(g * gd, (g + 1) * gd)
        before = (w // 2) * GRID_W
        after = (w - w // 2 - 1) * GRID_W
        ext_scr[tm - before:tm, gs] = jnp.where(i > 0, up_ref[0, tm - before:, gs].astype(F32), 0.0)
        ext_scr[tm:2 * tm, gs] = uc_ref[0, :, gs].astype(F32)
        if after:
            ext_scr[2 * tm:2 * tm + after, gs] = jnp.where(i < nt - 1, un_ref[0, 0:after, gs].astype(F32), 0.0)
        n = tm + (w - 1) * GRID_W
        lo = tm - before
        rsum = ext_scr[lo:lo + n, gs]
        span = 1
        while span < w:
            n -= span * GRID_W
            rsum = rsum[0:n] + rsum[span * GRID_W:span * GRID_W + n]
            span *= 2
        rsum = rsum.astype(BF16)
        parts = [_dot(band_ref[g], rsum[s * MXU_DIM:(s + 1) * MXU_DIM, :]) for s in range(tm // MXU_DIM)]
        mean = jnp.concatenate(parts, axis=0) * icnt_ref[:, g:g + 1]
        diff = mean - ext_scr[tm:2 * tm, gs]
        yg = _dot(diff.astype(BF16), wpool_ref[g]) * pscale_ref[:, gs] * sz_ref[0, :, gs].astype(F32)
        part = _dot(yg.astype(BF16), wout_ref[gs, :])
        y = part if y is None else y + part
    yn = y * lax.rsqrt(jnp.mean(y * y, axis=-1, keepdims=True) + EPS) * postg_ref[...]
    o_ref[0] = x_ref[0] + mod_ref[0, 2:3, :] * yn


def _pool_constants(t):
    rows = t // GRID_W
    bands = np.zeros((len(POOL_WINDOWS), MXU_DIM, MXU_DIM), np.float32)
    icnt = np.zeros((t, LANES), np.float32)
    for g, w in enumerate(POOL_WINDOWS):
        lo_off, hi_off = -(w // 2), w - w // 2
        for p in range(MXU_DIM):
            c = p % GRID_W
            base = p - c
            lo, hi = max(c + lo_off, 0), min(c + hi_off, GRID_W)
            bands[g, p, base + lo:base + hi] = 1.0
        cc = np.arange(GRID_W)
        cnt_c = np.clip(cc + hi_off, 0, GRID_W) - np.clip(cc + lo_off, 0, GRID_W)
        rr = np.arange(rows)
        cnt_r = np.clip(rr + hi_off, 0, rows) - np.clip(rr + lo_off, 0, rows)
        icnt[:, g] = (1.0 / (cnt_r[:, None] * cnt_c[None, :])).reshape(-1)
    return bands, icnt


def _pool_out(u, sz, w_pool_b, pool_scale, w_out_b, x, mod, post_g):
    b, t, e = u.shape
    d = x.shape[-1]
    tm = POOL_TILE
    assert tm >= (max(POOL_WINDOWS) // 2) * GRID_W and tm % MXU_DIM == 0 and t % tm == 0
    nt = t // tm
    gd = e // len(POOL_WINDOWS)
    bands, icnt = _pool_constants(t)
    row = lambda bi, i: (bi, i, 0)
    return pl.pallas_call(
        _pool_out_kernel,
        grid=(b, nt),
        in_specs=[
            pl.BlockSpec((1, tm, e), lambda bi, i: (bi, jnp.maximum(i - 1, 0), 0)),
            pl.BlockSpec((1, tm, e), row),
            pl.BlockSpec((1, tm, e), lambda bi, i: (bi, jnp.minimum(i + 1, nt - 1), 0)),
            pl.BlockSpec((1, tm, e), row),
            _const_spec(bands.shape),
            pl.BlockSpec((tm, LANES), lambda bi, i: (i, 0)),
            _const_spec(w_pool_b.shape),
            _const_spec((1, e)),
            _const_spec(w_out_b.shape),
            pl.BlockSpec((1, tm, d), row),
            pl.BlockSpec((1, 3, d), lambda bi, i: (bi, 0, 0)),
            _const_spec((1, d)),
        ],
        out_specs=pl.BlockSpec((1, tm, d), row),
        out_shape=jax.ShapeDtypeStruct((b, t, d), F32),
        scratch_shapes=[pltpu.VMEM((3 * tm, e), F32)],
        compiler_params=_cparams(("parallel", "parallel")),
        name="pool_out",
    )(u, u, u, sz, jnp.asarray(bands, BF16), jnp.asarray(icnt), w_pool_b, pool_scale.reshape(1, e),
      w_out_b, x, mod, post_g.reshape(1, d))


def _block_diag_tiles(w):
    per = MXU_DIM // QKV_BLOCK
    rows = w.reshape(w.shape[0] // per, MXU_DIM, QKV_BLOCK)
    tiled = jnp.tile(rows, (1, 1, per))
    blk = np.arange(MXU_DIM) // QKV_BLOCK
    return jnp.where(blk[:, None] == blk[None, :], tiled, 0.0)


def _gate_weights(w_f, b_f, w_b, b_b):
    pad = LANES - 4 * N_HEADS
    wg = jnp.pad(jnp.concatenate([w_f, w_b], axis=1), ((0, 0), (0, pad)))
    bg = jnp.pad(jnp.concatenate([b_f, b_b]), (0, pad)).reshape(1, LANES)
    return wg, bg


def kernel(x, c, ctx, c_ctx, w_ada, b_ada, a_norm_pre, a_norm_post, a_w_in, a_conv_w, a_conv_b,
           a_w_q, a_w_k, a_w_v, a_w_gate_f, a_b_gate_f, a_w_gate_b, a_b_gate_b, a_head_norm, a_skip,
           a_w_out, b_norm_pre, b_norm_post, b_w_in, b_w_pool, b_pool_scale, b_w_out):
    bsz, t, d = x.shape
    t_ctx = ctx.shape[1]
    assert t % SCAN_CHUNK == 0 and t_ctx % SCAN_CHUNK == 0 and t % GRID_W == 0

    n_cond = -(-(bsz + 1) // 8) * 8
    cond = jnp.concatenate([c, c_ctx[None], jnp.zeros((n_cond - bsz - 1, d), F32)], axis=0)
    mods = _ada(cond, bsz + 1, w_ada, b_ada).reshape(w_ada.shape[0], n_cond, 3, d)
    mod_lat = [mods[i, :bsz] for i in range(w_ada.shape[0])]
    mod_ctx = jnp.broadcast_to(mods[0, bsz][None], (bsz, 3, d))

    w_in_b = a_w_in[0].astype(BF16)
    bdq, bdk, bdv = (_block_diag_tiles(w[0]) for w in (a_w_q, a_w_k, a_w_v))
    wg, bg = _gate_weights(a_w_gate_f[0], a_b_gate_f[0], a_w_gate_b[0], a_b_gate_b[0])
    k_scale = float((a_w_in.shape[2] // 3 // N_HEADS) ** -0.5)
    bdkq, aqk, av = _fold_weights(bdq, bdk, bdv, wg, k_scale)
    pre_args = (a_norm_pre[0], w_in_b, a_conv_w[0], a_conv_b[0], bdkq, bdv.astype(BF16), aqk, av, bg,
                a_head_norm[0], a_skip[0])
    kc, ktc, vc, cols_c, rows_c, _ = _mlstm_pre(ctx, mod_ctx, *pre_args, states_only=True)
    c0, n0, m0 = _ctx_states(kc, ktc, vc, cols_c, rows_c)

    ql, kl, ktl, vl, g1, g2, cols_l, rows_l, cmrep_l = _mlstm_pre(x, mod_lat[0], *pre_args, states_only=False)
    scan_in = (ql, kl, ktl, vl, cols_l, rows_l, cmrep_l, c0, n0, m0)
    hb = _scan(*scan_in, backward=True)
    hf = _scan(*scan_in, backward=False)

    x, u, sz = _mixer_out(hf, hb, g1, g2, a_w_out[0].astype(BF16), x, mod_lat[0], a_norm_post[0],
                          mod_lat[1], b_norm_pre[0], b_w_in[0].astype(BF16))
    x = _pool_out(u, sz, b_w_pool[0].astype(BF16), b_pool_scale[0], b_w_out[0].astype(BF16),
                  x, mod_lat[1], b_norm_post[0])
    return x
```

```python
import functools

import numpy as np
import jax
import jax.numpy as jnp
from jax import lax
from jax.experimental import pallas as pl
from jax.experimental.pallas import tpu as pltpu

EPS = 1e-6
N_HEADS = 4
QKV_BLOCK = 4
GRID_W = 64
POOL_WINDOWS = (2, 4, 8, 16)

LANES = 128
MXU_DIM = 256
HALO = 16
VMEM_LIMIT_BYTES = 56 * 1024 * 1024

SCAN_CHUNK = 256
PRE_TILE = 512
PROJ_AHEAD = 3
POOL_TILE = 512
SCAN_CHUNKS_PER_STEP = 2
BF16 = jnp.bfloat16
F32 = jnp.float32


def _dot(a, b):
    return jnp.dot(a, b, preferred_element_type=F32)


def _silu(v):
    return v * jax.nn.sigmoid(v)


def _cparams(sem):
    return pltpu.CompilerParams(dimension_semantics=sem, vmem_limit_bytes=VMEM_LIMIT_BYTES)


def _const_spec(shape):
    nd = len(shape)
    return pl.BlockSpec(shape, lambda *_: (0,) * nd, pipeline_mode=pl.Buffered(1))


def _lane_repeat(col, width):
    return jnp.concatenate([col] * (width // LANES), axis=1) if width > LANES else col


def _ada_kernel(cond_t_ref, w_ref, b_ref, o_ref, *, n_used):
    s = _silu(cond_t_ref[...])
    w = w_ref[0]
    for r in range(n_used):
        o_ref[0, r:r + 1, :] = jnp.sum(w * s[:, r:r + 1], axis=0, keepdims=True) + b_ref[0]
    o_ref[0, n_used:, :] = jnp.zeros((o_ref.shape[1] - n_used, o_ref.shape[2]), F32)


def _ada(cond, n_used, w_ada, b_ada):
    depth, d, n3 = w_ada.shape
    rows = cond.shape[0]
    tn = max(w for w in range(LANES, 1024 + 1, LANES) if n3 % w == 0)
    return pl.pallas_call(
        functools.partial(_ada_kernel, n_used=n_used),
        grid=(depth, n3 // tn),
        in_specs=[
            pl.BlockSpec((d, rows), lambda l, j: (0, 0)),
            pl.BlockSpec((1, d, tn), lambda l, j: (l, 0, j)),
            pl.BlockSpec((1, 1, tn), lambda l, j: (l, 0, j)),
        ],
        out_specs=pl.BlockSpec((1, rows, tn), lambda l, j: (l, 0, j)),
        out_shape=jax.ShapeDtypeStruct((depth, rows, n3), F32),
        compiler_params=_cparams(("parallel", "parallel")),
        name="ada_mod",
    )(cond.T, w_ada, b_ada.reshape(depth, 1, n3))


def _norm_mod(xv, g, shift, scale):
    y = xv * lax.rsqrt(jnp.mean(xv * xv, axis=-1, keepdims=True) + EPS) * g
    return y * (1.0 + scale) + shift


def _scan_along_lanes(x, fwd_rows, lane, combine, fill):
    l = x.shape[1]
    step = 1
    while step < l:
        from_left = jnp.where(lane >= step, pltpu.roll(x, step, axis=1), fill)
        from_right = jnp.where(lane < l - step, pltpu.roll(x, l - step, axis=1), fill)
        x = combine(x, jnp.where(fwd_rows, from_left, from_right))
        step *= 2
    return x


def _gate_vectors(pre_chunks):
    h = N_HEADS
    l = pre_chunks[0].shape[0]
    pt = jnp.concatenate([p.T[0:4 * h, :] for p in pre_chunks], axis=0)
    row = lax.broadcasted_iota(jnp.int32, pt.shape, 0) % (4 * h)
    lane = lax.broadcasted_iota(jnp.int32, pt.shape, 1)
    fwd_rows = row < 2 * h
    log_f = jnp.minimum(pt, 0.0) - jnp.log1p(jnp.exp(-jnp.abs(pt)))
    b = _scan_along_lanes(log_f, fwd_rows, lane, jnp.add, 0.0)
    a = pt - pltpu.roll(b, pt.shape[0] - h, axis=0)
    cmax = _scan_along_lanes(a, fwd_rows, lane, jnp.maximum, -jnp.inf)
    rows = jnp.where((row % (2 * h)) < h, a, b)
    out = []
    for ch in range(len(pre_chunks)):
        rs = slice(ch * 4 * h, (ch + 1) * 4 * h)
        stacked = jnp.concatenate([rows[rs], cmax[rs], jnp.zeros((LANES - 8 * h, l), F32)], axis=0)
        cols = stacked.T
        reps = []
        for direction in range(2):
            for head in range(h):
                j = 4 * h + direction * 2 * h + head
                reps.append(jnp.broadcast_to(cols[:, j:j + 1], (l, LANES)))
        out.append((cols, rows[rs], jnp.concatenate(reps, axis=1)))
    return out


def _fold_weights_kernel(wq_ref, wk_ref, wv_ref, wg_ref, bdkq_ref, bdv_ref, aqk_ref, av_ref, *, k_scale):
    hp = lax.Precision.HIGHEST
    dot = lambda a, b: jnp.dot(a, b, precision=hp, preferred_element_type=F32)
    r = lax.broadcasted_iota(jnp.int32, (MXU_DIM, MXU_DIM), 0)
    c = lax.broadcasted_iota(jnp.int32, (MXU_DIM, MXU_DIM), 1)
    same_block = (r // QKV_BLOCK) == (c // QKV_BLOCK)

    def block_diag(w_ref, t):
        rows = w_ref[t * MXU_DIM:(t + 1) * MXU_DIM, :]
        tile = jnp.zeros((MXU_DIM, MXU_DIM), F32)
        for o in range(QKV_BLOCK):
            tile = jnp.where(same_block & (c % QKV_BLOCK == o), rows[:, o:o + 1], tile)
        return tile

    for t in range(bdkq_ref.shape[0]):
        bdq = block_diag(wq_ref, t)
        bdk = block_diag(wk_ref, t) * k_scale
        bdv = block_diag(wv_ref, t)
        bdkq = lax.dot_general(bdk, bdq, (((1,), (1,)), ((), ())), precision=hp, preferred_element_type=F32)
        bdkq_ref[t] = bdkq.astype(BF16)
        bdv_ref[t] = bdv.astype(BF16)
        aqk_ref[t] = (dot(bdq, wg_ref[0, t]) + dot(bdk, wg_ref[1, t])).astype(BF16)
        av_ref[t] = dot(bdv, wg_ref[2, t]).astype(BF16)


def _fold_weights(w_q, w_k, w_v, w_gate_f, w_gate_b, k_scale):
    e = w_q.shape[0] * QKV_BLOCK
    nt = e // MXU_DIM
    whole = lambda a: pl.BlockSpec(a.shape, lambda i: (0,) * a.ndim)
    tile = pl.BlockSpec((nt, MXU_DIM, MXU_DIM), lambda i: (0, 0, 0))
    gate = pl.BlockSpec((nt, MXU_DIM, LANES), lambda i: (0, 0, 0))
    wg = jnp.pad(jnp.concatenate([w_gate_f, w_gate_b], axis=1), ((0, 0), (0, LANES - 4 * N_HEADS)))
    args = [w.reshape(e, QKV_BLOCK) for w in (w_q, w_k, w_v)] + [wg.reshape(3, nt, MXU_DIM, LANES)]
    return pl.pallas_call(
        functools.partial(_fold_weights_kernel, k_scale=k_scale),
        grid=(1,),
        in_specs=[whole(a) for a in args],
        out_specs=[tile, tile, gate, gate],
        out_shape=[jax.ShapeDtypeStruct((nt, MXU_DIM, MXU_DIM), BF16),
                   jax.ShapeDtypeStruct((nt, MXU_DIM, MXU_DIM), BF16),
                   jax.ShapeDtypeStruct((nt, MXU_DIM, LANES), BF16),
                   jax.ShapeDtypeStruct((nt, MXU_DIM, LANES), BF16)],
        compiler_params=_cparams(("arbitrary",)),
        name="fold_weights",
    )(*args)


def _mlstm_pre_kernel(x_ref, xprev_ref, xnext_ref, mod_ref, g_ref, win_ref, convw_ref, convb_ref,
                      bdkq_ref, bdv_ref, aqk_ref, av_ref, bg_ref, hnorm_ref, skip_ref, *out_and_scratch,
                      states_only, tiles_per_seq):
    if states_only:
        k_ref, kt_ref, v_ref, cols_ref, rows_ref, cmrep_ref, hn_scr, pre_scr = out_and_scratch
    else:
        (q_ref, k_ref, kt_ref, v_ref, g1_ref, g2_ref, cols_ref, rows_ref, cmrep_ref,
         hn_scr, pre_scr) = out_and_scratch
    step = pl.program_id(0)
    n_tiles = pl.num_programs(0) - 1
    tm, e = k_ref.shape

    def gate_stage():
        nch = tm // SCAN_CHUNK
        chunks = [pre_scr[ch * SCAN_CHUNK:(ch + 1) * SCAN_CHUNK, :] for ch in range(nch)]
        for ch, (cols, rows, cmrep) in enumerate(_gate_vectors(chunks)):
            rs = slice(ch * SCAN_CHUNK, (ch + 1) * SCAN_CHUNK)
            cols_ref[rs, :] = cols
            rows_ref[ch] = rows
            half = N_HEADS * LANES
            cmrep_ref[0, 0, rs, :] = cmrep[:, 0:half]
            cmrep_ref[0, 1, rs, :] = cmrep[:, half:]

    @pl.when(step == 0)
    def _():
        pre_scr[...] = jnp.zeros(pre_scr.shape, F32)

    @pl.when(step == n_tiles)
    def _():
        gate_stage()

    @pl.when(step < n_tiles)
    def _():
        gate_stage()
        _mlstm_pre_tile(x_ref, xprev_ref, xnext_ref, mod_ref, g_ref, win_ref, convw_ref, convb_ref, bdkq_ref, bdv_ref,
                        aqk_ref, av_ref, bg_ref, hnorm_ref, skip_ref,
                        None if states_only else (q_ref, g1_ref, g2_ref), k_ref, kt_ref, v_ref,
                        hn_scr, pre_scr, step % tiles_per_seq, tiles_per_seq)


def _mlstm_pre_tile(x_ref, xprev_ref, xnext_ref, mod_ref, g_ref, win_ref, convw_ref, convb_ref, bdkq_ref, bdv_ref,
                    aqk_ref, av_ref, bg_ref, hnorm_ref, skip_ref, gate_outs, k_ref, kt_ref, v_ref,
                    hn_scr, pre_scr, i, nt):
    states_only = gate_outs is None
    if not states_only:
        q_ref, g1_ref, g2_ref = gate_outs
    tm, e = k_ref.shape
    shift = mod_ref[0, 0:1, :]
    scale = mod_ref[0, 1:2, :]
    g = g_ref[...]

    hn_scr[0:HALO, :] = _norm_mod(xprev_ref[...], g, shift, scale).astype(BF16)
    hn_scr[HALO:HALO + tm, :] = _norm_mod(x_ref[...], g, shift, scale).astype(BF16)
    hn_scr[HALO + tm:, :] = _norm_mod(xnext_ref[...], g, shift, scale).astype(BF16)

    def project(c):
        cs = slice(c * MXU_DIM, (c + 1) * MXU_DIM)
        xm_ext = _dot(hn_scr[...], win_ref[:, cs])
        if states_only:
            return xm_ext, None, None
        hc = hn_scr[HALO:HALO + tm, :]
        return (xm_ext, _dot(hc, win_ref[:, e + c * MXU_DIM:e + (c + 1) * MXU_DIM]),
                _dot(hc, win_ref[:, 2 * e + c * MXU_DIM:2 * e + (c + 1) * MXU_DIM]))

    pre = jnp.zeros((tm, LANES), F32) + bg_ref[...]
    n_chunks = e // MXU_DIM
    queue = [project(c) for c in range(min(PROJ_AHEAD, n_chunks))]
    for c in range(n_chunks):
        cs = slice(c * MXU_DIM, (c + 1) * MXU_DIM)
        xm_ext, z, og = queue.pop(0)
        if c + PROJ_AHEAD < n_chunks:
            queue.append(project(c + PROJ_AHEAD))
        xm_all = jnp.concatenate([jnp.where(i > 0, xm_ext[0:HALO], 0.0), xm_ext[HALO:HALO + tm],
                                  jnp.where(i < nt - 1, xm_ext[HALO + tm:], 0.0)], axis=0)
        xm = xm_all[HALO:HALO + tm]
        xm_prev = pltpu.roll(xm_all, 1, axis=0)[HALO:HALO + tm]
        xm_next = pltpu.roll(xm_all, tm + 2 * HALO - 1, axis=0)[HALO:HALO + tm]
        conv = (xm_prev * convw_ref[0:1, cs] + xm * convw_ref[1:2, cs] + xm_next * convw_ref[2:3, cs]
                + convb_ref[:, cs])
        xc = _silu(conv)
        xc_b = xc.astype(BF16)
        xm_b = xm.astype(BF16)
        k32 = _dot(xc_b, bdkq_ref[c])
        pre = pre + _dot(xc_b, aqk_ref[c]) + _dot(xm_b, av_ref[c])
        k_ref[:, cs] = k32.astype(BF16)
        k_t = k32.T.astype(BF16)
        for ch in range(tm // SCAN_CHUNK):
            kt_ref[ch, cs, :] = k_t[:, ch * SCAN_CHUNK:(ch + 1) * SCAN_CHUNK]
        v_ref[:, cs] = _dot(xm_b, bdv_ref[c]).astype(BF16)
        if not states_only:
            sz = _silu(z)
            q_ref[:, cs] = xc_b
            g1_ref[:, cs] = (hnorm_ref[:, cs] * jax.nn.sigmoid(og) * sz).astype(BF16)
            g2_ref[:, cs] = (skip_ref[:, cs] * xc * sz).astype(BF16)

    pre_scr[...] = pre


def _mlstm_pre(x, mod, pre_g, w_in_b, conv_w, conv_b, bdkq, bdv, aqk, av, bg, head_norm, skip, states_only):
    b, t, d = x.shape
    e = w_in_b.shape[1] // 3
    tm = min(PRE_TILE, t)
    nt = t // tm

    n = b * nt
    nch = tm // SCAN_CHUNK
    cur = lambda i: jnp.minimum(i, n - 1)
    prv = lambda i: jnp.maximum(i - 1, 0)
    big = pl.BlockSpec((tm, e), lambda i: (cur(i), 0))
    big_t = pl.BlockSpec((nch, e, SCAN_CHUNK), lambda i: (cur(i), 0, 0))
    tok = jax.ShapeDtypeStruct((b * t, e), BF16)
    tok_t = jax.ShapeDtypeStruct((b * t // SCAN_CHUNK, e, SCAN_CHUNK), BF16)
    gate_specs = [pl.BlockSpec((tm, LANES), lambda i: (prv(i), 0)),
                  pl.BlockSpec((nch, 4 * N_HEADS, SCAN_CHUNK), lambda i: (prv(i), 0, 0)),
                  pl.BlockSpec((1, 2, tm, N_HEADS * LANES), lambda i: (prv(i) // nt, 0, prv(i) % nt, 0))]
    gate_shapes = [jax.ShapeDtypeStruct((b * t, LANES), F32),
                   jax.ShapeDtypeStruct((b * t // SCAN_CHUNK, 4 * N_HEADS, SCAN_CHUNK), F32),
                   jax.ShapeDtypeStruct((b, 2, t, N_HEADS * LANES), F32)]
    if states_only:
        out_specs, out_shape = [big, big_t, big] + gate_specs, [tok, tok_t, tok] + gate_shapes
    else:
        out_specs = [big, big, big_t, big, big, big] + gate_specs
        out_shape = [tok, tok, tok_t, tok, tok, tok] + gate_shapes
    outs = pl.pallas_call(
        functools.partial(_mlstm_pre_kernel, states_only=states_only, tiles_per_seq=nt),
        grid=(n + 1,),
        in_specs=[
            pl.BlockSpec((tm, d), lambda i: (cur(i), 0)),
            pl.BlockSpec((HALO, d), lambda i: (jnp.maximum(cur(i) * (tm // HALO) - 1, 0), 0)),
            pl.BlockSpec((HALO, d), lambda i: (jnp.minimum((cur(i) + 1) * (tm // HALO), b * t // HALO - 1), 0)),
            pl.BlockSpec((1, 3, d), lambda i: (cur(i) // nt, 0, 0)),
            _const_spec((1, d)),
            _const_spec(w_in_b.shape),
            _const_spec(conv_w.shape),
            _const_spec((1, e)),
            _const_spec(bdkq.shape), _const_spec(bdv.shape), _const_spec(aqk.shape), _const_spec(av.shape),
            _const_spec((1, LANES)),
            _const_spec((1, e)), _const_spec((1, e)),
        ],
        out_specs=out_specs,
        out_shape=out_shape,
        scratch_shapes=[pltpu.VMEM((tm + 2 * HALO, d), BF16), pltpu.VMEM((tm, LANES), F32)],
        compiler_params=_cparams(("arbitrary",)),
        name="mlstm_pre",
    )(x.reshape(b * t, d), x.reshape(b * t, d), x.reshape(b * t, d), mod, pre_g.reshape(1, d), w_in_b,
      conv_w, conv_b.reshape(1, e),
      bdkq, bdv, aqk, av, bg, head_norm.reshape(1, e), skip.reshape(1, e))
    by_token = lambda o: o.reshape(b, t, o.shape[-1])
    by_chunk = lambda o: o.reshape((b, t // SCAN_CHUNK) + o.shape[1:])
    *main, cols, rows, cmrep = outs
    kt_pos = 1 if states_only else 2
    main = [by_chunk(o) if j == kt_pos else by_token(o) for j, o in enumerate(main)]
    return main + [by_token(cols), by_chunk(rows), cmrep]


def _scan_chunk(get, cols, rows, cmrep, c_st, n_st, m_st, backward, emit):
    h = N_HEADS
    l = cols.shape[0]
    dh = c_st.shape[-1]
    off = 2 * h if backward else 0
    last = 0 if backward else l - 1
    st = []
    for head in range(h):
        cm_col = cols[:, 4 * h + off + head:4 * h + off + head + 1]
        st.append(dict(
            a_col=cols[:, off + head:off + head + 1],
            b_col=cols[:, off + h + head:off + h + head + 1],
            a_row=rows[off + head:off + head + 1, :],
            total=cols[last:last + 1, off + h + head:off + h + head + 1],
            amax=cm_col[last:last + 1, :],
            m_prev=m_st[head]))

    if emit is not None:
        ti = lax.broadcasted_iota(jnp.int32, (l, l), 0)
        si = lax.broadcasted_iota(jnp.int32, (l, l), 1)
        mask = (si >= ti) if backward else (ti >= si)
        for head in range(h):
            s = st[head]
            q = get("q", head)
            s["u"] = jnp.maximum(s["m_prev"], cmrep[:, head * LANES:(head + 1) * LANES])
            s["p"] = jnp.exp(jnp.where(mask, s["a_row"] - _lane_repeat(s["u"], l), -jnp.inf))
            s["qk"] = _dot(q, get("kt", head))
            s["qc"] = _dot(q, c_st[head].astype(BF16))
            sc = s["qk"] * s["p"]
            w_inter = jnp.exp(s["m_prev"] - s["u"])
            num = _dot(sc.astype(BF16), get("v", head)) + _lane_repeat(w_inter, dh) * s["qc"]
            qn = jnp.sum(get("q", head).astype(F32) * n_st[head], axis=-1, keepdims=True)
            den = jnp.sum(sc, axis=-1, keepdims=True) + w_inter[:, 0:1] * qn
            floor = jnp.exp(-(s["b_col"] + s["u"][:, 0:1]))
            emit(head, num / jnp.maximum(jnp.abs(den), floor))

    for head in range(h):
        s = st[head]
        s["u_last"] = jnp.maximum(s["m_prev"], s["amax"])
        s["decay"] = jnp.exp(s["m_prev"] - s["u_last"])
        wkt = (get("kt", head).astype(F32) * jnp.exp(s["a_row"] - s["u_last"])).astype(BF16)
        s["upd"] = _dot(wkt, get("v", head))
    for head in range(h):
        s = st[head]
        wk = get("k", head).astype(F32) * jnp.exp(s["a_col"] - s["u_last"])
        c_st[head] = s["decay"] * c_st[head] + s["upd"]
        n_st[head] = s["decay"] * n_st[head] + jnp.sum(wk, axis=0, keepdims=True)
        m_st[head] = s["total"] + s["u_last"]


def _tile_getter(refs, dh, chunk):
    rows = slice(chunk * SCAN_CHUNK, (chunk + 1) * SCAN_CHUNK)

    def get(name, head):
        hs = slice(head * dh, (head + 1) * dh)
        if name == "kt":
            return refs["kt"][0, chunk, hs, :]
        return refs[name][0, rows, hs]
    return get


def _ctx_state_kernel(k_ref, kt_ref, v_ref, cols_ref, rows_ref, c_out, n_out, m_out):
    dh = c_out.shape[-1]
    nc = k_ref.shape[1] // SCAN_CHUNK
    c_out[...] = jnp.zeros(c_out.shape, F32)
    n_out[...] = jnp.zeros(n_out.shape, F32)
    m_out[...] = jnp.zeros(m_out.shape, F32)
    refs = dict(k=k_ref, kt=kt_ref, v=v_ref)
    for direction in range(2):
        for step in range(nc):
            c = nc - 1 - step if direction else step
            rs = slice(c * SCAN_CHUNK, (c + 1) * SCAN_CHUNK)
            _scan_chunk(_tile_getter(refs, dh, c), cols_ref[0, rs, :], rows_ref[0, c], None,
                        c_out.at[0, direction], n_out.at[0, direction], m_out.at[0, direction],
                        bool(direction), None)


def _ctx_states(k, kt, v, cols, rows):
    b, t, e = k.shape
    dh = e // N_HEADS
    nc = t // SCAN_CHUNK
    per_b = lambda nd: (lambda bi: (bi,) + (0,) * (nd - 1))
    return pl.pallas_call(
        _ctx_state_kernel,
        grid=(b,),
        in_specs=[pl.BlockSpec((1, t, e), per_b(3)), pl.BlockSpec((1, nc, e, SCAN_CHUNK), per_b(4)),
                  pl.BlockSpec((1, t, e), per_b(3)),
                  pl.BlockSpec((1, t, LANES), per_b(3)),
                  pl.BlockSpec((1, nc, 4 * N_HEADS, SCAN_CHUNK), per_b(4))],
        out_specs=[pl.BlockSpec((1, 2, N_HEADS, dh, dh), per_b(5)),
                   pl.BlockSpec((1, 2, N_HEADS, 1, dh), per_b(5)),
                   pl.BlockSpec((1, 2, N_HEADS, 1, 1), per_b(5))],
        out_shape=[jax.ShapeDtypeStruct((b, 2, N_HEADS, dh, dh), F32),
                   jax.ShapeDtypeStruct((b, 2, N_HEADS, 1, dh), F32),
                   jax.ShapeDtypeStruct((b, 2, N_HEADS, 1, 1), F32)],
        compiler_params=_cparams(("parallel",)),
        name="ctx_states",
    )(k, kt, v, cols, rows)


def _load_state(c0_ref, n0_ref, m0_ref, c_scr, n_scr, m_scr):
    @pl.when(pl.program_id(1) == 0)
    def _():
        c_scr[...] = c0_ref[0, 0]
        n_scr[...] = n0_ref[0, 0]
        m_scr[...] = m0_ref[0, 0]


def _scan_kernel(q_ref, k_ref, kt_ref, v_ref, cols_ref, rows_ref, cmrep_ref, c0_ref, n0_ref, m0_ref,
                 h_ref, c_scr, n_scr, m_scr, *, backward):
    _load_state(c0_ref, n0_ref, m0_ref, c_scr, n_scr, m_scr)
    dh = c_scr.shape[-1]
    refs = dict(q=q_ref, k=k_ref, kt=kt_ref, v=v_ref)
    chunks = range(rows_ref.shape[1])
    for j in (reversed(chunks) if backward else chunks):
        rs = slice(j * SCAN_CHUNK, (j + 1) * SCAN_CHUNK)

        def emit(head, h, rs=rs):
            h_ref[0, rs, head * dh:(head + 1) * dh] = h.astype(h_ref.dtype)

        _scan_chunk(_tile_getter(refs, dh, j), cols_ref[0, rs, :], rows_ref[0, j], cmrep_ref[0, 0, rs, :],
                    c_scr, n_scr, m_scr, backward, emit)


def _mixer_out_kernel(hf_ref, hb_ref, g1_ref, g2_ref, wout_ref, x_ref, mod0_ref, postg_ref,
                      mod1_ref, preg1_ref, win1_ref, x1_ref, u_ref, sz_ref):
    e = hf_ref.shape[2]
    dh = e // N_HEADS
    y = None
    for head in range(N_HEADS):
        hs = slice(head * dh, (head + 1) * dh)
        hsum = hf_ref[0, :, hs].astype(F32) + hb_ref[0, :, hs].astype(F32)
        cen = hsum - jnp.mean(hsum, axis=-1, keepdims=True)
        hn = cen * lax.rsqrt(jnp.mean(cen * cen, axis=-1, keepdims=True) + EPS)
        gated = (hn * g1_ref[0, :, hs].astype(F32) + g2_ref[0, :, hs].astype(F32)).astype(BF16)
        part = _dot(gated, wout_ref[hs, :])
        y = part if y is None else y + part
    yn = y * lax.rsqrt(jnp.mean(y * y, axis=-1, keepdims=True) + EPS) * postg_ref[...]
    x1 = x_ref[0] + mod0_ref[0, 2:3, :] * yn
    x1_ref[0] = x1
    _pool_in_proj(x1, mod1_ref, preg1_ref, win1_ref, u_ref, sz_ref)


def _scan_specs(b, t, e, chunks_per_step, reverse):
    dh = e // N_HEADS
    direction = 1 if reverse else 0
    l = SCAN_CHUNK * chunks_per_step
    ns = t // l
    cidx = (lambda c: ns - 1 - c) if reverse else (lambda c: c)
    tok = lambda w: pl.BlockSpec((1, l, w), lambda bi, c: (bi, cidx(c), 0))
    state = lambda shape: pl.BlockSpec((1, 1) + shape, lambda bi, c: (bi, direction) + (0,) * len(shape),
                                       pipeline_mode=pl.Buffered(1))
    specs = [tok(e), tok(e),
             pl.BlockSpec((1, chunks_per_step, e, SCAN_CHUNK), lambda bi, c: (bi, cidx(c), 0, 0)),
             tok(e), tok(LANES),
             pl.BlockSpec((1, chunks_per_step, 4 * N_HEADS, SCAN_CHUNK), lambda bi, c: (bi, cidx(c), 0, 0)),
             pl.BlockSpec((1, 1, l, N_HEADS * LANES), lambda bi, c: (bi, direction, cidx(c), 0)),
             state((N_HEADS, dh, dh)), state((N_HEADS, 1, dh)), state((N_HEADS, 1, 1))]
    scratch = [pltpu.VMEM((N_HEADS, dh, dh), F32), pltpu.VMEM((N_HEADS, 1, dh), F32),
               pltpu.VMEM((N_HEADS, 1, 1), F32)]
    return specs, scratch, tok, ns


def _scan(q, k, kt, v, cols, rows, cmrep, c0, n0, m0, backward):
    b, t, e = q.shape
    specs, scratch, tok, ns = _scan_specs(b, t, e, SCAN_CHUNKS_PER_STEP, backward)
    return pl.pallas_call(
        functools.partial(_scan_kernel, backward=backward),
        grid=(b, ns),
        in_specs=specs,
        out_specs=tok(e),
        out_shape=jax.ShapeDtypeStruct((b, t, e), BF16),
        scratch_shapes=scratch,
        compiler_params=_cparams(("parallel", "arbitrary")),
        name="mlstm_scan_bwd" if backward else "mlstm_scan_fwd",
    )(q, k, kt, v, cols, rows, cmrep, c0, n0, m0)


def _mixer_out(hf, hb, g1, g2, w_out_b, x, mod0, post_g, mod1, pre_g1, w_in1_b):
    b, t, e = hf.shape
    d = x.shape[-1]
    tm = POOL_TILE
    row = lambda bi, i: (bi, i, 0)
    tok = lambda w: pl.BlockSpec((1, tm, w), row)
    modspec = pl.BlockSpec((1, 3, d), lambda bi, i: (bi, 0, 0))
    return pl.pallas_call(
        _mixer_out_kernel,
        grid=(b, t // tm),
        in_specs=[tok(e), tok(e), tok(e), tok(e), _const_spec(w_out_b.shape), tok(d), modspec,
                  _const_spec((1, d)), modspec, _const_spec((1, d)), _const_spec(w_in1_b.shape)],
        out_specs=[tok(d), tok(e), tok(e)],
        out_shape=[jax.ShapeDtypeStruct((b, t, d), F32), jax.ShapeDtypeStruct((b, t, e), BF16),
                   jax.ShapeDtypeStruct((b, t, e), BF16)],
        compiler_params=_cparams(("parallel", "parallel")),
        name="mixer_out_pool_in",
    )(hf, hb, g1, g2, w_out_b, x, mod0, post_g.reshape(1, d), mod1, pre_g1.reshape(1, d), w_in1_b)


def _pool_in_proj(x, mod_ref, g_ref, win_ref, u_ref, sz_ref):
    e = u_ref.shape[2]
    hn = _norm_mod(x, g_ref[...], mod_ref[0, 0:1, :], mod_ref[0, 1:2, :]).astype(BF16)
    for c in range(e // MXU_DIM):
        cs = slice(c * MXU_DIM, (c + 1) * MXU_DIM)
        u_ref[0, :, cs] = _dot(hn, win_ref[:, cs]).astype(BF16)
        sz_ref[0, :, cs] = _silu(_dot(hn, win_ref[:, e + c * MXU_DIM:e + (c + 1) * MXU_DIM])).astype(BF16)


def _pool_out_kernel(up_ref, uc_ref, un_ref, sz_ref, band_ref, icnt_ref, wpool_ref, pscale_ref,
                     wout_ref, x_ref, mod_ref, postg_ref, o_ref, ext_scr):
    i = pl.program_id(1)
    nt = pl.num_programs(1)
    tm = uc_ref.shape[1]
    gd = wpool_ref.shape[1]
    y = None
    for g, w in enumerate(POOL_WINDOWS):
        gs = slice(g * gd, (g + 1) * gd)
        before = (w // 2) * GRID_W
        after = (w - w // 2 - 1) * GRID_W
        ext_scr[tm - before:tm, gs] = jnp.where(i > 0, up_ref[0, tm - before:, gs].astype(F32), 0.0)
        ext_scr[tm:2 * tm, gs] = uc_ref[0, :, gs].astype(F32)
        if after:
            ext_scr[2 * tm:2 * tm + after, gs] = jnp.where(i < nt - 1, un_ref[0, 0:after, gs].astype(F32), 0.0)
        n = tm + (w - 1) * GRID_W
        lo = tm - before
        rsum = ext_scr[lo:lo + n, gs]
        span = 1
        while span < w:
            n -= span * GRID_W
            rsum = rsum[0:n] + rsum[span * GRID_W:span * GRID_W + n]
            span *= 2
        rsum = rsum.astype(BF16)
        parts = [_dot(band_ref[g], rsum[s * MXU_DIM:(s + 1) * MXU_DIM, :]) for s in range(tm // MXU_DIM)]
        mean = jnp.concatenate(parts, axis=0) * icnt_ref[:, g:g + 1]
        diff = mean - ext_scr[tm:2 * tm, gs]
        yg = _dot(diff.astype(BF16), wpool_ref[g]) * pscale_ref[:, gs] * sz_ref[0, :, gs].astype(F32)
        part = _dot(yg.astype(BF16), wout_ref[gs, :])
        y = part if y is None else y + part
    yn = y * lax.rsqrt(jnp.mean(y * y, axis=-1, keepdims=True) + EPS) * postg_ref[...]
    o_ref[0] = x_ref[0] + mod_ref[0, 2:3, :] * yn


def _pool_constants(t):
    rows = t // GRID_W
    bands = np.zeros((len(POOL_WINDOWS), MXU_DIM, MXU_DIM), np.float32)
    icnt = np.zeros((t, LANES), np.float32)
    for g, w in enumerate(POOL_WINDOWS):
        lo_off, hi_off = -(w // 2), w - w // 2
        for p in range(MXU_DIM):
            c = p % GRID_W
            base = p - c
            lo, hi = max(c + lo_off, 0), min(c + hi_off, GRID_W)
            bands[g, p, base + lo:base + hi] = 1.0
        cc = np.arange(GRID_W)
        cnt_c = np.clip(cc + hi_off, 0, GRID_W) - np.clip(cc + lo_off, 0, GRID_W)
        rr = np.arange(rows)
        cnt_r = np.clip(rr + hi_off, 0, rows) - np.clip(rr + lo_off, 0, rows)
        icnt[:, g] = (1.0 / (cnt_r[:, None] * cnt_c[None, :])).reshape(-1)
    return bands, icnt


def _pool_out(u, sz, w_pool_b, pool_scale, w_out_b, x, mod, post_g):
    b, t, e = u.shape
    d = x.shape[-1]
    tm = POOL_TILE
    assert tm >= (max(POOL_WINDOWS) // 2) * GRID_W and tm % MXU_DIM == 0 and t % tm == 0
    nt = t // tm
    gd = e // len(POOL_WINDOWS)
    bands, icnt = _pool_constants(t)
    row = lambda bi, i: (bi, i, 0)
    return pl.pallas_call(
        _pool_out_kernel,
        grid=(b, nt),
        in_specs=[
            pl.BlockSpec((1, tm, e), lambda bi, i: (bi, jnp.maximum(i - 1, 0), 0)),
            pl.BlockSpec((1, tm, e), row),
            pl.BlockSpec((1, tm, e), lambda bi, i: (bi, jnp.minimum(i + 1, nt - 1), 0)),
            pl.BlockSpec((1, tm, e), row),
            _const_spec(bands.shape),
            pl.BlockSpec((tm, LANES), lambda bi, i: (i, 0)),
            _const_spec(w_pool_b.shape),
            _const_spec((1, e)),
            _const_spec(w_out_b.shape),
            pl.BlockSpec((1, tm, d), row),
            pl.BlockSpec((1, 3, d), lambda bi, i: (bi, 0, 0)),
            _const_spec((1, d)),
        ],
        out_specs=pl.BlockSpec((1, tm, d), row),
        out_shape=jax.ShapeDtypeStruct((b, t, d), F32),
        scratch_shapes=[pltpu.VMEM((3 * tm, e), F32)],
        compiler_params=_cparams(("parallel", "parallel")),
        name="pool_out",
    )(u, u, u, sz, jnp.asarray(bands, BF16), jnp.asarray(icnt), w_pool_b, pool_scale.reshape(1, e),
      w_out_b, x, mod, post_g.reshape(1, d))


def kernel(x, c, ctx, c_ctx, w_ada, b_ada, a_norm_pre, a_norm_post, a_w_in, a_conv_w, a_conv_b,
           a_w_q, a_w_k, a_w_v, a_w_gate_f, a_b_gate_f, a_w_gate_b, a_b_gate_b, a_head_norm, a_skip,
           a_w_out, b_norm_pre, b_norm_post, b_w_in, b_w_pool, b_pool_scale, b_w_out):
    bsz, t, d = x.shape
    t_ctx = ctx.shape[1]
    assert t % SCAN_CHUNK == 0 and t_ctx % SCAN_CHUNK == 0 and t % GRID_W == 0

    n_cond = -(-(bsz + 1) // 8) * 8
    cond = jnp.concatenate([c, c_ctx[None], jnp.zeros((n_cond - bsz - 1, d), F32)], axis=0)
    mods = _ada(cond, bsz + 1, w_ada, b_ada).reshape(w_ada.shape[0], n_cond, 3, d)
    mod_lat = [mods[i, :bsz] for i in range(w_ada.shape[0])]
    mod_ctx = jnp.broadcast_to(mods[0, bsz][None], (bsz, 3, d))

    w_in_b = a_w_in[0].astype(BF16)
    k_scale = float((a_w_in.shape[2] // 3 // N_HEADS) ** -0.5)
    bdkq, bdv, aqk, av = _fold_weights(a_w_q[0], a_w_k[0], a_w_v[0], a_w_gate_f[0], a_w_gate_b[0], k_scale)
    bg = jnp.pad(jnp.concatenate([a_b_gate_f[0], a_b_gate_b[0]]), (0, LANES - 4 * N_HEADS)).reshape(1, LANES)
    pre_args = (a_norm_pre[0], w_in_b, a_conv_w[0], a_conv_b[0], bdkq, bdv, aqk, av, bg,
                a_head_norm[0], a_skip[0])
    kc, ktc, vc, cols_c, rows_c, _ = _mlstm_pre(ctx, mod_ctx, *pre_args, states_only=True)
    c0, n0, m0 = _ctx_states(kc, ktc, vc, cols_c, rows_c)

    ql, kl, ktl, vl, g1, g2, cols_l, rows_l, cmrep_l = _mlstm_pre(x, mod_lat[0], *pre_args, states_only=False)
    scan_in = (ql, kl, ktl, vl, cols_l, rows_l, cmrep_l, c0, n0, m0)
    hb = _scan(*scan_in, backward=True)
    hf = _scan(*scan_in, backward=False)

    x, u, sz = _mixer_out(hf, hb, g1, g2, a_w_out[0].astype(BF16), x, mod_lat[0], a_norm_post[0],
                          mod_lat[1], b_norm_pre[0], b_w_in[0].astype(BF16))
    x = _pool_out(u, sz, b_w_pool[0].astype(BF16), b_pool_scale[0], b_w_out[0].astype(BF16),
                  x, mod_lat[1], b_norm_post[0])
    return x
```

```python
import functools

import numpy as np
import jax
import jax.numpy as jnp
from jax import lax
from jax.experimental import pallas as pl
from jax.experimental.pallas import tpu as pltpu

EPS = 1e-6
N_HEADS = 4
QKV_BLOCK = 4
GRID_W = 64
POOL_WINDOWS = (2, 4, 8, 16)

LANES = 128
MXU_DIM = 256
HALO = 16
VMEM_LIMIT_BYTES = 56 * 1024 * 1024

SCAN_CHUNK = 256
PRE_TILE = 512
PROJ_AHEAD = 3
POOL_TILE = 512
SCAN_CHUNKS_PER_STEP = 2
BF16 = jnp.bfloat16
F32 = jnp.float32


def _dot(a, b):
    return jnp.dot(a, b, preferred_element_type=F32)


def _silu(v):
    return v * jax.nn.sigmoid(v)


def _cparams(sem):
    return pltpu.CompilerParams(dimension_semantics=sem, vmem_limit_bytes=VMEM_LIMIT_BYTES)


def _const_spec(shape):
    nd = len(shape)
    return pl.BlockSpec(shape, lambda *_: (0,) * nd, pipeline_mode=pl.Buffered(1))


def _lane_repeat(col, width):
    return jnp.concatenate([col] * (width // LANES), axis=1) if width > LANES else col


def _ada_kernel(cond_t_ref, w_ref, b_ref, o_ref, *, n_used):
    s = _silu(cond_t_ref[...])
    w = w_ref[0]
    for r in range(n_used):
        o_ref[0, r:r + 1, :] = jnp.sum(w * s[:, r:r + 1], axis=0, keepdims=True) + b_ref[0]
    o_ref[0, n_used:, :] = jnp.zeros((o_ref.shape[1] - n_used, o_ref.shape[2]), F32)


def _ada(cond, n_used, w_ada, b_ada):
    depth, d, n3 = w_ada.shape
    rows = cond.shape[0]
    tn = max(w for w in range(LANES, 1024 + 1, LANES) if n3 % w == 0)
    return pl.pallas_call(
        functools.partial(_ada_kernel, n_used=n_used),
        grid=(depth, n3 // tn),
        in_specs=[
            pl.BlockSpec((d, rows), lambda l, j: (0, 0)),
            pl.BlockSpec((1, d, tn), lambda l, j: (l, 0, j)),
            pl.BlockSpec((1, 1, tn), lambda l, j: (l, 0, j)),
        ],
        out_specs=pl.BlockSpec((1, rows, tn), lambda l, j: (l, 0, j)),
        out_shape=jax.ShapeDtypeStruct((depth, rows, n3), F32),
        compiler_params=_cparams(("parallel", "parallel")),
        name="ada_mod",
    )(cond.T, w_ada, b_ada.reshape(depth, 1, n3))


def _norm_mod(xv, g, shift, scale):
    y = xv * lax.rsqrt(jnp.mean(xv * xv, axis=-1, keepdims=True) + EPS) * g
    return y * (1.0 + scale) + shift


def _scan_along_lanes(x, fwd_rows, lane, combine, fill):
    l = x.shape[1]
    step = 1
    while step < l:
        from_left = jnp.where(lane >= step, pltpu.roll(x, step, axis=1), fill)
        from_right = jnp.where(lane < l - step, pltpu.roll(x, l - step, axis=1), fill)
        x = combine(x, jnp.where(fwd_rows, from_left, from_right))
        step *= 2
    return x


def _gate_vectors(pre_chunks):
    h = N_HEADS
    l = pre_chunks[0].shape[0]
    pt = jnp.concatenate([p.T[0:4 * h, :] for p in pre_chunks], axis=0)
    row = lax.broadcasted_iota(jnp.int32, pt.shape, 0) % (4 * h)
    lane = lax.broadcasted_iota(jnp.int32, pt.shape, 1)
    fwd_rows = row < 2 * h
    log_f = jnp.minimum(pt, 0.0) - jnp.log1p(jnp.exp(-jnp.abs(pt)))
    b = _scan_along_lanes(log_f, fwd_rows, lane, jnp.add, 0.0)
    a = pt - pltpu.roll(b, pt.shape[0] - h, axis=0)
    cmax = _scan_along_lanes(a, fwd_rows, lane, jnp.maximum, -jnp.inf)
    rows = jnp.where((row % (2 * h)) < h, a, b)
    out = []
    for ch in range(len(pre_chunks)):
        rs = slice(ch * 4 * h, (ch + 1) * 4 * h)
        stacked = jnp.concatenate([rows[rs], cmax[rs], jnp.zeros((LANES - 8 * h, l), F32)], axis=0)
        cols = stacked.T
        reps = []
        for direction in range(2):
            for head in range(h):
                j = 4 * h + direction * 2 * h + head
                reps.append(jnp.broadcast_to(cols[:, j:j + 1], (l, LANES)))
        out.append((cols, rows[rs], jnp.concatenate(reps, axis=1)))
    return out


def _fold_weights_kernel(wq_ref, wk_ref, wv_ref, wg_ref, bdkq_ref, bdv_ref, aqk_ref, av_ref, *, k_scale):
    hp = lax.Precision.HIGHEST
    dot = lambda a, b: jnp.dot(a, b, precision=hp, preferred_element_type=F32)
    r = lax.broadcasted_iota(jnp.int32, (MXU_DIM, MXU_DIM), 0)
    c = lax.broadcasted_iota(jnp.int32, (MXU_DIM, MXU_DIM), 1)
    same_block = (r // QKV_BLOCK) == (c // QKV_BLOCK)

    def block_diag(w_ref, t):
        rows = w_ref[t * MXU_DIM:(t + 1) * MXU_DIM, :]
        tile = jnp.zeros((MXU_DIM, MXU_DIM), F32)
        for o in range(QKV_BLOCK):
            tile = jnp.where(same_block & (c % QKV_BLOCK == o), rows[:, o:o + 1], tile)
        return tile

    for t in range(bdkq_ref.shape[0]):
        bdq = block_diag(wq_ref, t)
        bdk = block_diag(wk_ref, t) * k_scale
        bdv = block_diag(wv_ref, t)
        bdkq = lax.dot_general(bdk, bdq, (((1,), (1,)), ((), ())), precision=hp, preferred_element_type=F32)
        bdkq_ref[t] = bdkq.astype(BF16)
        bdv_ref[t] = bdv.astype(BF16)
        aqk_ref[t] = (dot(bdq, wg_ref[0, t]) + dot(bdk, wg_ref[1, t])).astype(BF16)
        av_ref[t] = dot(bdv, wg_ref[2, t]).astype(BF16)


def _fold_weights(w_q, w_k, w_v, w_gate_f, w_gate_b, k_scale):
    e = w_q.shape[0] * QKV_BLOCK
    nt = e // MXU_DIM
    whole = lambda a: pl.BlockSpec(a.shape, lambda i: (0,) * a.ndim)
    tile = pl.BlockSpec((nt, MXU_DIM, MXU_DIM), lambda i: (0, 0, 0))
    gate = pl.BlockSpec((nt, MXU_DIM, LANES), lambda i: (0, 0, 0))
    wg = jnp.pad(jnp.concatenate([w_gate_f, w_gate_b], axis=1), ((0, 0), (0, LANES - 4 * N_HEADS)))
    args = [w.reshape(e, QKV_BLOCK) for w in (w_q, w_k, w_v)] + [wg.reshape(3, nt, MXU_DIM, LANES)]
    return pl.pallas_call(
        functools.partial(_fold_weights_kernel, k_scale=k_scale),
        grid=(1,),
        in_specs=[whole(a) for a in args],
        out_specs=[tile, tile, gate, gate],
        out_shape=[jax.ShapeDtypeStruct((nt, MXU_DIM, MXU_DIM), BF16),
                   jax.ShapeDtypeStruct((nt, MXU_DIM, MXU_DIM), BF16),
                   jax.ShapeDtypeStruct((nt, MXU_DIM, LANES), BF16),
                   jax.ShapeDtypeStruct((nt, MXU_DIM, LANES), BF16)],
        compiler_params=_cparams(("arbitrary",)),
        name="fold_weights",
    )(*args)


def _mlstm_pre_kernel(x_ref, xprev_ref, xnext_ref, mod_ref, g_ref, win_ref, convw_ref, convb_ref,
                      bdkq_ref, bdv_ref, aqk_ref, av_ref, bg_ref, hnorm_ref, skip_ref, *out_and_scratch,
                      states_only, tiles_per_seq):
    if states_only:
        k_ref, kt_ref, v_ref, cols_ref, rows_ref, cmrep_ref, hn_scr, pre_scr = out_and_scratch
    else:
        (q_ref, k_ref, kt_ref, v_ref, g1_ref, g2_ref, cols_ref, rows_ref, cmrep_ref,
         hn_scr, pre_scr) = out_and_scratch
    step = pl.program_id(0)
    n_tiles = pl.num_programs(0) - 1
    tm, e = k_ref.shape

    def gate_stage():
        nch = tm // SCAN_CHUNK
        chunks = [pre_scr[ch * SCAN_CHUNK:(ch + 1) * SCAN_CHUNK, :] for ch in range(nch)]
        for ch, (cols, rows, cmrep) in enumerate(_gate_vectors(chunks)):
            rs = slice(ch * SCAN_CHUNK, (ch + 1) * SCAN_CHUNK)
            cols_ref[rs, :] = cols
            rows_ref[ch] = rows
            half = N_HEADS * LANES
            cmrep_ref[0, 0, rs, :] = cmrep[:, 0:half]
            cmrep_ref[0, 1, rs, :] = cmrep[:, half:]

    @pl.when(step == 0)
    def _():
        pre_scr[...] = jnp.zeros(pre_scr.shape, F32)

    @pl.when(step == n_tiles)
    def _():
        gate_stage()

    @pl.when(step < n_tiles)
    def _():
        gate_stage()
        _mlstm_pre_tile(x_ref, xprev_ref, xnext_ref, mod_ref, g_ref, win_ref, convw_ref, convb_ref, bdkq_ref, bdv_ref,
                        aqk_ref, av_ref, bg_ref, hnorm_ref, skip_ref,
                        None if states_only else (q_ref, g1_ref, g2_ref), k_ref, kt_ref, v_ref,
                        hn_scr, pre_scr, step % tiles_per_seq, tiles_per_seq)


def _mlstm_pre_tile(x_ref, xprev_ref, xnext_ref, mod_ref, g_ref, win_ref, convw_ref, convb_ref, bdkq_ref, bdv_ref,
                    aqk_ref, av_ref, bg_ref, hnorm_ref, skip_ref, gate_outs, k_ref, kt_ref, v_ref,
                    hn_scr, pre_scr, i, nt):
    states_only = gate_outs is None
    if not states_only:
        q_ref, g1_ref, g2_ref = gate_outs
    tm, e = k_ref.shape
    shift = mod_ref[0, 0:1, :]
    scale = mod_ref[0, 1:2, :]
    g = g_ref[...]

    hn_scr[0:HALO, :] = _norm_mod(xprev_ref[...], g, shift, scale).astype(BF16)
    hn_scr[HALO:HALO + tm, :] = _norm_mod(x_ref[...], g, shift, scale).astype(BF16)
    hn_scr[HALO + tm:, :] = _norm_mod(xnext_ref[...], g, shift, scale).astype(BF16)

    def project(c):
        cs = slice(c * MXU_DIM, (c + 1) * MXU_DIM)
        xm_ext = _dot(hn_scr[...], win_ref[:, cs])
        if states_only:
            return xm_ext, None, None
        hc = hn_scr[HALO:HALO + tm, :]
        return (xm_ext, _dot(hc, win_ref[:, e + c * MXU_DIM:e + (c + 1) * MXU_DIM]),
                _dot(hc, win_ref[:, 2 * e + c * MXU_DIM:2 * e + (c + 1) * MXU_DIM]))

    pre = jnp.zeros((tm, LANES), F32) + bg_ref[...]
    n_chunks = e // MXU_DIM
    queue = [project(c) for c in range(min(PROJ_AHEAD, n_chunks))]
    for c in range(n_chunks):
        cs = slice(c * MXU_DIM, (c + 1) * MXU_DIM)
        xm_ext, z, og = queue.pop(0)
        if c + PROJ_AHEAD < n_chunks:
            queue.append(project(c + PROJ_AHEAD))
        xm_all = jnp.concatenate([jnp.where(i > 0, xm_ext[0:HALO], 0.0), xm_ext[HALO:HALO + tm],
                                  jnp.where(i < nt - 1, xm_ext[HALO + tm:], 0.0)], axis=0)
        xm = xm_all[HALO:HALO + tm]
        xm_prev = pltpu.roll(xm_all, 1, axis=0)[HALO:HALO + tm]
        xm_next = pltpu.roll(xm_all, tm + 2 * HALO - 1, axis=0)[HALO:HALO + tm]
        conv = (xm_prev * convw_ref[0:1, cs] + xm * convw_ref[1:2, cs] + xm_next * convw_ref[2:3, cs]
                + convb_ref[:, cs])
        xc = _silu(conv)
        xc_b = xc.astype(BF16)
        xm_b = xm.astype(BF16)
        k32 = _dot(xc_b, bdkq_ref[c])
        pre = pre + _dot(xc_b, aqk_ref[c]) + _dot(xm_b, av_ref[c])
        k_ref[:, cs] = k32.astype(BF16)
        k_t = k32.T.astype(BF16)
        for ch in range(tm // SCAN_CHUNK):
            kt_ref[ch, cs, :] = k_t[:, ch * SCAN_CHUNK:(ch + 1) * SCAN_CHUNK]
        v_ref[:, cs] = _dot(xm_b, bdv_ref[c]).astype(BF16)
        if not states_only:
            sz = _silu(z)
            q_ref[:, cs] = xc_b
            g1_ref[:, cs] = (hnorm_ref[:, cs] * jax.nn.sigmoid(og) * sz).astype(BF16)
            g2_ref[:, cs] = (skip_ref[:, cs] * xc * sz).astype(BF16)

    pre_scr[...] = pre


def _mlstm_pre(x, mod, pre_g, w_in_b, conv_w, conv_b, bdkq, bdv, aqk, av, bg, head_norm, skip, states_only):
    b, t, d = x.shape
    e = w_in_b.shape[1] // 3
    tm = min(PRE_TILE, t)
    nt = t // tm

    n = b * nt
    nch = tm // SCAN_CHUNK
    cur = lambda i: jnp.minimum(i, n - 1)
    prv = lambda i: jnp.maximum(i - 1, 0)
    big = pl.BlockSpec((tm, e), lambda i: (cur(i), 0))
    big_t = pl.BlockSpec((nch, e, SCAN_CHUNK), lambda i: (cur(i), 0, 0))
    tok = jax.ShapeDtypeStruct((b * t, e), BF16)
    tok_t = jax.ShapeDtypeStruct((b * t // SCAN_CHUNK, e, SCAN_CHUNK), BF16)
    gate_specs = [pl.BlockSpec((tm, LANES), lambda i: (prv(i), 0)),
                  pl.BlockSpec((nch, 4 * N_HEADS, SCAN_CHUNK), lambda i: (prv(i), 0, 0)),
                  pl.BlockSpec((1, 2, tm, N_HEADS * LANES), lambda i: (prv(i) // nt, 0, prv(i) % nt, 0))]
    gate_shapes = [jax.ShapeDtypeStruct((b * t, LANES), F32),
                   jax.ShapeDtypeStruct((b * t // SCAN_CHUNK, 4 * N_HEADS, SCAN_CHUNK), F32),
                   jax.ShapeDtypeStruct((b, 2, t, N_HEADS * LANES), F32)]
    if states_only:
        out_specs, out_shape = [big, big_t, big] + gate_specs, [tok, tok_t, tok] + gate_shapes
    else:
        out_specs = [big, big, big_t, big, big, big] + gate_specs
        out_shape = [tok, tok, tok_t, tok, tok, tok] + gate_shapes
    outs = pl.pallas_call(
        functools.partial(_mlstm_pre_kernel, states_only=states_only, tiles_per_seq=nt),
        grid=(n + 1,),
        in_specs=[
            pl.BlockSpec((tm, d), lambda i: (cur(i), 0)),
            pl.BlockSpec((HALO, d), lambda i: (jnp.maximum(cur(i) * (tm // HALO) - 1, 0), 0)),
            pl.BlockSpec((HALO, d), lambda i: (jnp.minimum((cur(i) + 1) * (tm // HALO), b * t // HALO - 1), 0)),
            pl.BlockSpec((1, 3, d), lambda i: (cur(i) // nt, 0, 0)),
            _const_spec((1, d)),
            _const_spec(w_in_b.shape),
            _const_spec(conv_w.shape),
            _const_spec((1, e)),
            _const_spec(bdkq.shape), _const_spec(bdv.shape), _const_spec(aqk.shape), _const_spec(av.shape),
            _const_spec((1, LANES)),
            _const_spec((1, e)), _const_spec((1, e)),
        ],
        out_specs=out_specs,
        out_shape=out_shape,
        scratch_shapes=[pltpu.VMEM((tm + 2 * HALO, d), BF16), pltpu.VMEM((tm, LANES), F32)],
        compiler_params=_cparams(("arbitrary",)),
        name="mlstm_pre",
    )(x.reshape(b * t, d), x.reshape(b * t, d), x.reshape(b * t, d), mod, pre_g.reshape(1, d), w_in_b,
      conv_w, conv_b.reshape(1, e),
      bdkq, bdv, aqk, av, bg, head_norm.reshape(1, e), skip.reshape(1, e))
    by_token = lambda o: o.reshape(b, t, o.shape[-1])
    by_chunk = lambda o: o.reshape((b, t // SCAN_CHUNK) + o.shape[1:])
    *main, cols, rows, cmrep = outs
    kt_pos = 1 if states_only else 2
    main = [by_chunk(o) if j == kt_pos else by_token(o) for j, o in enumerate(main)]
    return main + [by_token(cols), by_chunk(rows), cmrep]


def _scan_chunk(get, cols, rows, cmrep, c_st, n_st, m_st, backward, emit):
    h = N_HEADS
    l = cols.shape[0]
    dh = c_st.shape[-1]
    off = 2 * h if backward else 0
    last = 0 if backward else l - 1
    st = []
    for head in range(h):
        cm_col = cols[:, 4 * h + off + head:4 * h + off + head + 1]
        st.append(dict(
            a_col=cols[:, off + head:off + head + 1],
            b_col=cols[:, off + h + head:off + h + head + 1],
            a_row=rows[off + head:off + head + 1, :],
            total=cols[last:last + 1, off + h + head:off + h + head + 1],
            amax=cm_col[last:last + 1, :],
            m_prev=m_st[head]))

    if emit is not None:
        ti = lax.broadcasted_iota(jnp.int32, (l, l), 0)
        si = lax.broadcasted_iota(jnp.int32, (l, l), 1)
        mask = (si >= ti) if backward else (ti >= si)
        for head in range(h):
            s = st[head]
            q = get("q", head)
            s["u"] = jnp.maximum(s["m_prev"], cmrep[:, head * LANES:(head + 1) * LANES])
            s["p"] = jnp.exp(jnp.where(mask, s["a_row"] - _lane_repeat(s["u"], l), -jnp.inf))
            s["qk"] = _dot(q, get("kt", head))
            s["qc"] = _dot(q, c_st[head].astype(BF16))
            sc = s["qk"] * s["p"]
            w_inter = jnp.exp(s["m_prev"] - s["u"])
            num = _dot(sc.astype(BF16), get("v", head)) + _lane_repeat(w_inter, dh) * s["qc"]
            qn = jnp.sum(get("q", head).astype(F32) * n_st[head], axis=-1, keepdims=True)
            den = jnp.sum(sc, axis=-1, keepdims=True) + w_inter[:, 0:1] * qn
            floor = jnp.exp(-(s["b_col"] + s["u"][:, 0:1]))
            emit(head, num / jnp.maximum(jnp.abs(den), floor))

    for head in range(h):
        s = st[head]
        s["u_last"] = jnp.maximum(s["m_prev"], s["amax"])
        s["decay"] = jnp.exp(s["m_prev"] - s["u_last"])
        wkt = (get("kt", head).astype(F32) * jnp.exp(s["a_row"] - s["u_last"])).astype(BF16)
        s["upd"] = _dot(wkt, get("v", head))
    for head in range(h):
        s = st[head]
        wk = get("k", head).astype(F32) * jnp.exp(s["a_col"] - s["u_last"])
        c_st[head] = s["decay"] * c_st[head] + s["upd"]
        n_st[head] = s["decay"] * n_st[head] + jnp.sum(wk, axis=0, keepdims=True)
        m_st[head] = s["total"] + s["u_last"]


def _tile_getter(refs, dh, chunk):
    rows = slice(chunk * SCAN_CHUNK, (chunk + 1) * SCAN_CHUNK)

    def get(name, head):
        hs = slice(head * dh, (head + 1) * dh)
        if name == "kt":
            return refs["kt"][0, chunk, hs, :]
        return refs[name][0, rows, hs]
    return get


def _ctx_state_kernel(k_ref, kt_ref, v_ref, cols_ref, rows_ref, c_out, n_out, m_out):
    dh = c_out.shape[-1]
    nc = k_ref.shape[1] // SCAN_CHUNK
    c_out[...] = jnp.zeros(c_out.shape, F32)
    n_out[...] = jnp.zeros(n_out.shape, F32)
    m_out[...] = jnp.zeros(m_out.shape, F32)
    refs = dict(k=k_ref, kt=kt_ref, v=v_ref)
    for direction in range(2):
        for step in range(nc):
            c = nc - 1 - step if direction else step
            rs = slice(c * SCAN_CHUNK, (c + 1) * SCAN_CHUNK)
            _scan_chunk(_tile_getter(refs, dh, c), cols_ref[0, rs, :], rows_ref[0, c], None,
                        c_out.at[0, direction], n_out.at[0, direction], m_out.at[0, direction],
                        bool(direction), None)


def _ctx_states(k, kt, v, cols, rows):
    b, t, e = k.shape
    dh = e // N_HEADS
    nc = t // SCAN_CHUNK
    per_b = lambda nd: (lambda bi: (bi,) + (0,) * (nd - 1))
    return pl.pallas_call(
        _ctx_state_kernel,
        grid=(b,),
        in_specs=[pl.BlockSpec((1, t, e), per_b(3)), pl.BlockSpec((1, nc, e, SCAN_CHUNK), per_b(4)),
                  pl.BlockSpec((1, t, e), per_b(3)),
                  pl.BlockSpec((1, t, LANES), per_b(3)),
                  pl.BlockSpec((1, nc, 4 * N_HEADS, SCAN_CHUNK), per_b(4))],
        out_specs=[pl.BlockSpec((1, 2, N_HEADS, dh, dh), per_b(5)),
                   pl.BlockSpec((1, 2, N_HEADS, 1, dh), per_b(5)),
                   pl.BlockSpec((1, 2, N_HEADS, 1, 1), per_b(5))],
        out_shape=[jax.ShapeDtypeStruct((b, 2, N_HEADS, dh, dh), F32),
                   jax.ShapeDtypeStruct((b, 2, N_HEADS, 1, dh), F32),
                   jax.ShapeDtypeStruct((b, 2, N_HEADS, 1, 1), F32)],
        compiler_params=_cparams(("parallel",)),
        name="ctx_states",
    )(k, kt, v, cols, rows)


def _load_state(c0_ref, n0_ref, m0_ref, c_scr, n_scr, m_scr):
    @pl.when(pl.program_id(1) == 0)
    def _():
        c_scr[...] = c0_ref[0, 0]
        n_scr[...] = n0_ref[0, 0]
        m_scr[...] = m0_ref[0, 0]


def _scan_kernel(q_ref, k_ref, kt_ref, v_ref, cols_ref, rows_ref, cmrep_ref, c0_ref, n0_ref, m0_ref,
                 h_ref, c_scr, n_scr, m_scr, *, backward):
    _load_state(c0_ref, n0_ref, m0_ref, c_scr, n_scr, m_scr)
    dh = c_scr.shape[-1]
    refs = dict(q=q_ref, k=k_ref, kt=kt_ref, v=v_ref)
    chunks = range(rows_ref.shape[1])
    for j in (reversed(chunks) if backward else chunks):
        rs = slice(j * SCAN_CHUNK, (j + 1) * SCAN_CHUNK)

        def emit(head, h, rs=rs):
            h_ref[0, rs, head * dh:(head + 1) * dh] = h.astype(h_ref.dtype)

        _scan_chunk(_tile_getter(refs, dh, j), cols_ref[0, rs, :], rows_ref[0, j], cmrep_ref[0, 0, rs, :],
                    c_scr, n_scr, m_scr, backward, emit)


def _mixer_out_kernel(hf_ref, hb_ref, g1_ref, g2_ref, wout_ref, x_ref, mod0_ref, postg_ref,
                      mod1_ref, preg1_ref, win1_ref, x1_ref, u_ref, sz_ref):
    e = hf_ref.shape[2]
    dh = e // N_HEADS
    y = None
    for head in range(N_HEADS):
        hs = slice(head * dh, (head + 1) * dh)
        hsum = hf_ref[0, :, hs].astype(F32) + hb_ref[0, :, hs].astype(F32)
        cen = hsum - jnp.mean(hsum, axis=-1, keepdims=True)
        hn = cen * lax.rsqrt(jnp.mean(cen * cen, axis=-1, keepdims=True) + EPS)
        gated = (hn * g1_ref[0, :, hs].astype(F32) + g2_ref[0, :, hs].astype(F32)).astype(BF16)
        part = _dot(gated, wout_ref[hs, :])
        y = part if y is None else y + part
    yn = y * lax.rsqrt(jnp.mean(y * y, axis=-1, keepdims=True) + EPS) * postg_ref[...]
    x1 = x_ref[0] + mod0_ref[0, 2:3, :] * yn
    x1_ref[0] = x1
    _pool_in_proj(x1, mod1_ref, preg1_ref, win1_ref, u_ref, sz_ref)


def _scan_specs(b, t, e, chunks_per_step, reverse):
    dh = e // N_HEADS
    direction = 1 if reverse else 0
    l = SCAN_CHUNK * chunks_per_step
    ns = t // l
    cidx = (lambda c: ns - 1 - c) if reverse else (lambda c: c)
    tok = lambda w: pl.BlockSpec((1, l, w), lambda bi, c: (bi, cidx(c), 0))
    state = lambda shape: pl.BlockSpec((1, 1) + shape, lambda bi, c: (bi, direction) + (0,) * len(shape),
                                       pipeline_mode=pl.Buffered(1))
    specs = [tok(e), tok(e),
             pl.BlockSpec((1, chunks_per_step, e, SCAN_CHUNK), lambda bi, c: (bi, cidx(c), 0, 0)),
             tok(e), tok(LANES),
             pl.BlockSpec((1, chunks_per_step, 4 * N_HEADS, SCAN_CHUNK), lambda bi, c: (bi, cidx(c), 0, 0)),
             pl.BlockSpec((1, 1, l, N_HEADS * LANES), lambda bi, c: (bi, direction, cidx(c), 0)),
             state((N_HEADS, dh, dh)), state((N_HEADS, 1, dh)), state((N_HEADS, 1, 1))]
    scratch = [pltpu.VMEM((N_HEADS, dh, dh), F32), pltpu.VMEM((N_HEADS, 1, dh), F32),
               pltpu.VMEM((N_HEADS, 1, 1), F32)]
    return specs, scratch, tok, ns


def _scan(q, k, kt, v, cols, rows, cmrep, c0, n0, m0, backward):
    b, t, e = q.shape
    specs, scratch, tok, ns = _scan_specs(b, t, e, SCAN_CHUNKS_PER_STEP, backward)
    return pl.pallas_call(
        functools.partial(_scan_kernel, backward=backward),
        grid=(b, ns),
        in_specs=specs,
        out_specs=tok(e),
        out_shape=jax.ShapeDtypeStruct((b, t, e), BF16),
        scratch_shapes=scratch,
        compiler_params=_cparams(("parallel", "arbitrary")),
        name="mlstm_scan_bwd" if backward else "mlstm_scan_fwd",
    )(q, k, kt, v, cols, rows, cmrep, c0, n0, m0)


def _mixer_out(hf, hb, g1, g2, w_out_b, x, mod0, post_g, mod1, pre_g1, w_in1_b):
    b, t, e = hf.shape
    d = x.shape[-1]
    tm = POOL_TILE
    row = lambda bi, i: (bi, i, 0)
    tok = lambda w: pl.BlockSpec((1, tm, w), row)
    modspec = pl.BlockSpec((1, 3, d), lambda bi, i: (bi, 0, 0))
    return pl.pallas_call(
        _mixer_out_kernel,
        grid=(b, t // tm),
        in_specs=[tok(e), tok(e), tok(e), tok(e), _const_spec(w_out_b.shape), tok(d), modspec,
                  _const_spec((1, d)), modspec, _const_spec((1, d)), _const_spec(w_in1_b.shape)],
        out_specs=[tok(d), tok(e), tok(e)],
        out_shape=[jax.ShapeDtypeStruct((b, t, d), F32), jax.ShapeDtypeStruct((b, t, e), BF16),
                   jax.ShapeDtypeStruct((b, t, e), BF16)],
        compiler_params=_cparams(("parallel", "parallel")),
        name="mixer_out_pool_in",
    )(hf, hb, g1, g2, w_out_b, x, mod0, post_g.reshape(1, d), mod1, pre_g1.reshape(1, d), w_in1_b)


def _pool_in_proj(x, mod_ref, g_ref, win_ref, u_ref, sz_ref):
    e = u_ref.shape[2]
    hn = _norm_mod(x, g_ref[...], mod_ref[0, 0:1, :], mod_ref[0, 1:2, :]).astype(BF16)
    for c in range(e // MXU_DIM):
        cs = slice(c * MXU_DIM, (c + 1) * MXU_DIM)
        u_ref[0, :, cs] = _dot(hn, win_ref[:, cs]).astype(BF16)
        sz_ref[0, :, cs] = _silu(_dot(hn, win_ref[:, e + c * MXU_DIM:e + (c + 1) * MXU_DIM])).astype(BF16)


def _pool_out_kernel(up_ref, uc_ref, un_ref, sz_ref, band_ref, icnt_ref, wpool_ref, pscale_ref,
                     wout_ref, x_ref, mod_ref, postg_ref, o_ref, ext_scr):
    i = pl.program_id(1)
    nt = pl.num_programs(1)
    tm = uc_ref.shape[1]
    gd = wpool_ref.shape[1]
    y = None
    for g, w in enumerate(POOL_WINDOWS):
        gs = slice(g * gd, (g + 1) * gd)
        before = (w // 2) * GRID_W
        after = (w - w // 2 - 1) * GRID_W
        ext_scr[tm - before:tm, gs] = jnp.where(i > 0, up_ref[0, tm - before:, gs].astype(F32), 0.0)
        ext_scr[tm:2 * tm, gs] = uc_ref[0, :, gs].astype(F32)
        if after:
            ext_scr[2 * tm:2 * tm + after, gs] = jnp.where(i < nt - 1, un_ref[0, 0:after, gs].astype(F32), 0.0)
        n = tm + (w - 1) * GRID_W
        lo = tm - before
        rsum = ext_scr[lo:lo + n, gs]
        span = 1
        while span < w:
            n -= span * GRID_W
            rsum = rsum[0:n] + rsum[span * GRID_W:span * GRID_W + n]
            span *= 2
        rsum = rsum.astype(BF16)
        parts = [_dot(band_ref[g], rsum[s * MXU_DIM:(s + 1) * MXU_DIM, :]) for s in range(tm // MXU_DIM)]
        mean = jnp.concatenate(parts, axis=0) * icnt_ref[:, g:g + 1]
        diff = mean - ext_scr[tm:2 * tm, gs]
        yg = _dot(diff.astype(BF16), wpool_ref[g]) * pscale_ref[:, gs] * sz_ref[0, :, gs].astype(F32)
        part = _dot(yg.astype(BF16), wout_ref[gs, :])
        y = part if y is None else y + part
    yn = y * lax.rsqrt(jnp.mean(y * y, axis=-1, keepdims=True) + EPS) * postg_ref[...]
    o_ref[0] = x_ref[0] + mod_ref[0, 2:3, :] * yn


def _pool_constants(t):
    rows = t // GRID_W
    bands = np.zeros((len(POOL_WINDOWS), MXU_DIM, MXU_DIM), np.float32)
    icnt = np.zeros((t, LANES), np.float32)
    for g, w in enumerate(POOL_WINDOWS):
        lo_off, hi_off = -(w // 2), w - w // 2
        for p in range(MXU_DIM):
            c = p % GRID_W
            base = p - c
            lo, hi = max(c + lo_off, 0), min(c + hi_off, GRID_W)
            bands[g, p, base + lo:base + hi] = 1.0
        cc = np.arange(GRID_W)
        cnt_c = np.clip(cc + hi_off, 0, GRID_W) - np.clip(cc + lo_off, 0, GRID_W)
        rr = np.arange(rows)
        cnt_r = np.clip(rr + hi_off, 0, rows) - np.clip(rr + lo_off, 0, rows)
        icnt[:, g] = (1.0 / (cnt_r[:, None] * cnt_c[None, :])).reshape(-1)
    return bands, icnt


def _pool_out(u, sz, w_pool_b, pool_scale, w_out_b, x, mod, post_g):
    b, t, e = u.shape
    d = x.shape[-1]
    tm = POOL_TILE
    assert tm >= (max(POOL_WINDOWS) // 2) * GRID_W and tm % MXU_DIM == 0 and t % tm == 0
    nt = t // tm
    gd = e // len(POOL_WINDOWS)
    bands, icnt = _pool_constants(t)
    row = lambda bi, i: (bi, i, 0)
    return pl.pallas_call(
        _pool_out_kernel,
        grid=(b, nt),
        in_specs=[
            pl.BlockSpec((1, tm, e), lambda bi, i: (bi, jnp.maximum(i - 1, 0), 0)),
            pl.BlockSpec((1, tm, e), row),
            pl.BlockSpec((1, tm, e), lambda bi, i: (bi, jnp.minimum(i + 1, nt - 1), 0)),
            pl.BlockSpec((1, tm, e), row),
            _const_spec(bands.shape),
            pl.BlockSpec((tm, LANES), lambda bi, i: (i, 0)),
            _const_spec(w_pool_b.shape),
            _const_spec((1, e)),
            _const_spec(w_out_b.shape),
            pl.BlockSpec((1, tm, d), row),
            pl.BlockSpec((1, 3, d), lambda bi, i: (bi, 0, 0)),
            _const_spec((1, d)),
        ],
        out_specs=pl.BlockSpec((1, tm, d), row),
        out_shape=jax.ShapeDtypeStruct((b, t, d), F32),
        scratch_shapes=[pltpu.VMEM((3 * tm, e), F32)],
        compiler_params=_cparams(("parallel", "parallel")),
        name="pool_out",
    )(u, u, u, sz, jnp.asarray(bands, BF16), jnp.asarray(icnt), w_pool_b, pool_scale.reshape(1, e),
      w_out_b, x, mod, post_g.reshape(1, d))


def kernel(x, c, ctx, c_ctx, w_ada, b_ada, a_norm_pre, a_norm_post, a_w_in, a_conv_w, a_conv_b,
           a_w_q, a_w_k, a_w_v, a_w_gate_f, a_b_gate_f, a_w_gate_b, a_b_gate_b, a_head_norm, a_skip,
           a_w_out, b_norm_pre, b_norm_post, b_w_in, b_w_pool, b_pool_scale, b_w_out):
    bsz, t, d = x.shape
    t_ctx = ctx.shape[1]
    assert t % SCAN_CHUNK == 0 and t_ctx % SCAN_CHUNK == 0 and t % GRID_W == 0

    n_cond = -(-(bsz + 1) // 8) * 8
    cond = jnp.concatenate([c, c_ctx[None], jnp.zeros((n_cond - bsz - 1, d), F32)], axis=0)
    mods = _ada(cond, bsz + 1, w_ada, b_ada).reshape(w_ada.shape[0], n_cond, 3, d)
    mod_lat = [mods[i, :bsz] for i in range(w_ada.shape[0])]
    mod_ctx = jnp.broadcast_to(mods[0, bsz][None], (bsz, 3, d))

    w_in_b = a_w_in[0].astype(BF16)
    k_scale = float((a_w_in.shape[2] // 3 // N_HEADS) ** -0.5)
    bdkq, bdv, aqk, av = _fold_weights(a_w_q[0], a_w_k[0], a_w_v[0], a_w_gate_f[0], a_w_gate_b[0], k_scale)
    bg = jnp.pad(jnp.concatenate([a_b_gate_f[0], a_b_gate_b[0]]), (0, LANES - 4 * N_HEADS)).reshape(1, LANES)
    pre_args = (a_norm_pre[0], w_in_b, a_conv_w[0], a_conv_b[0], bdkq, bdv, aqk, av, bg,
                a_head_norm[0], a_skip[0])
    ql, kl, ktl, vl, g1, g2, cols_l, rows_l, cmrep_l = _mlstm_pre(x, mod_lat[0], *pre_args, states_only=False)
    kc, ktc, vc, cols_c, rows_c, _ = _mlstm_pre(ctx, mod_ctx, *pre_args, states_only=True)
    c0, n0, m0 = _ctx_states(kc, ktc, vc, cols_c, rows_c)
    scan_in = (ql, kl, ktl, vl, cols_l, rows_l, cmrep_l, c0, n0, m0)
    hb = _scan(*scan_in, backward=True)
    hf = _scan(*scan_in, backward=False)

    x, u, sz = _mixer_out(hf, hb, g1, g2, a_w_out[0].astype(BF16), x, mod_lat[0], a_norm_post[0],
                          mod_lat[1], b_norm_pre[0], b_w_in[0].astype(BF16))
    x = _pool_out(u, sz, b_w_pool[0].astype(BF16), b_pool_scale[0], b_w_out[0].astype(BF16),
                  x, mod_lat[1], b_norm_post[0])
    return x
```

```python
import functools

import numpy as np
import jax
import jax.numpy as jnp
from jax import lax
from jax.experimental import pallas as pl
from jax.experimental.pallas import tpu as pltpu

EPS = 1e-6
N_HEADS = 4
QKV_BLOCK = 4
GRID_W = 64
POOL_WINDOWS = (2, 4, 8, 16)

LANES = 128
MXU_DIM = 256
HALO = 16
VMEM_LIMIT_BYTES = 56 * 1024 * 1024

SCAN_CHUNK = 256
PRE_TILE = 512
PROJ_AHEAD = 3
POOL_TILE = 512
SCAN_CHUNKS_PER_STEP = 2

PACKED_FULL = ("q", "k", "v", "g1", "g2")
PACKED_STATES = ("k", "v")
BF16 = jnp.bfloat16
F32 = jnp.float32


def _dot(a, b):
    return jnp.dot(a, b, preferred_element_type=F32)


def _silu(v):
    return v * jax.nn.sigmoid(v)


def _cparams(sem):
    return pltpu.CompilerParams(dimension_semantics=sem, vmem_limit_bytes=VMEM_LIMIT_BYTES)


def _const_spec(shape):
    nd = len(shape)
    return pl.BlockSpec(shape, lambda *_: (0,) * nd, pipeline_mode=pl.Buffered(1))


def _lane_repeat(col, width):
    return jnp.concatenate([col] * (width // LANES), axis=1) if width > LANES else col


def _ada_kernel(cond_t_ref, w_ref, b_ref, o_ref, *, n_used):
    s = _silu(cond_t_ref[...])
    w = w_ref[0]
    for r in range(n_used):
        o_ref[0, r:r + 1, :] = jnp.sum(w * s[:, r:r + 1], axis=0, keepdims=True) + b_ref[0]
    o_ref[0, n_used:, :] = jnp.zeros((o_ref.shape[1] - n_used, o_ref.shape[2]), F32)


def _ada(cond, n_used, w_ada, b_ada):
    depth, d, n3 = w_ada.shape
    rows = cond.shape[0]
    tn = max(w for w in range(LANES, 1024 + 1, LANES) if n3 % w == 0)
    return pl.pallas_call(
        functools.partial(_ada_kernel, n_used=n_used),
        grid=(depth, n3 // tn),
        in_specs=[
            pl.BlockSpec((d, rows), lambda l, j: (0, 0)),
            pl.BlockSpec((1, d, tn), lambda l, j: (l, 0, j)),
            pl.BlockSpec((1, 1, tn), lambda l, j: (l, 0, j)),
        ],
        out_specs=pl.BlockSpec((1, rows, tn), lambda l, j: (l, 0, j)),
        out_shape=jax.ShapeDtypeStruct((depth, rows, n3), F32),
        compiler_params=_cparams(("parallel", "parallel")),
        name="ada_mod",
    )(cond.T, w_ada, b_ada.reshape(depth, 1, n3))


def _norm_mod(xv, g, shift, scale):
    y = xv * lax.rsqrt(jnp.mean(xv * xv, axis=-1, keepdims=True) + EPS) * g
    return y * (1.0 + scale) + shift


def _scan_along_lanes(x, fwd_rows, lane, combine, fill):
    l = x.shape[1]
    step = 1
    while step < l:
        from_left = jnp.where(lane >= step, pltpu.roll(x, step, axis=1), fill)
        from_right = jnp.where(lane < l - step, pltpu.roll(x, l - step, axis=1), fill)
        x = combine(x, jnp.where(fwd_rows, from_left, from_right))
        step *= 2
    return x


def _gate_vectors(pre_chunks):
    h = N_HEADS
    l = pre_chunks[0].shape[0]
    pt = jnp.concatenate([p.T[0:4 * h, :] for p in pre_chunks], axis=0)
    row = lax.broadcasted_iota(jnp.int32, pt.shape, 0) % (4 * h)
    lane = lax.broadcasted_iota(jnp.int32, pt.shape, 1)
    fwd_rows = row < 2 * h
    log_f = jnp.minimum(pt, 0.0) - jnp.log1p(jnp.exp(-jnp.abs(pt)))
    b = _scan_along_lanes(log_f, fwd_rows, lane, jnp.add, 0.0)
    a = pt - pltpu.roll(b, pt.shape[0] - h, axis=0)
    cmax = _scan_along_lanes(a, fwd_rows, lane, jnp.maximum, -jnp.inf)
    rows = jnp.where((row % (2 * h)) < h, a, b)
    out = []
    for ch in range(len(pre_chunks)):
        rs = slice(ch * 4 * h, (ch + 1) * 4 * h)
        stacked = jnp.concatenate([rows[rs], cmax[rs], jnp.zeros((LANES - 8 * h, l), F32)], axis=0)
        cols = stacked.T
        reps = []
        for direction in range(2):
            for head in range(h):
                j = 4 * h + direction * 2 * h + head
                reps.append(jnp.broadcast_to(cols[:, j:j + 1], (l, LANES)))
        out.append((cols, rows[rs], jnp.concatenate(reps, axis=1)))
    return out


def _fold_weights_kernel(wq_ref, wk_ref, wv_ref, wg_ref, bdkq_ref, bdv_ref, aqk_ref, av_ref, *, k_scale):
    hp = lax.Precision.HIGHEST
    dot = lambda a, b: jnp.dot(a, b, precision=hp, preferred_element_type=F32)
    r = lax.broadcasted_iota(jnp.int32, (MXU_DIM, MXU_DIM), 0)
    c = lax.broadcasted_iota(jnp.int32, (MXU_DIM, MXU_DIM), 1)
    same_block = (r // QKV_BLOCK) == (c // QKV_BLOCK)

    def block_diag(w_ref, t):
        rows = w_ref[t * MXU_DIM:(t + 1) * MXU_DIM, :]
        tile = jnp.zeros((MXU_DIM, MXU_DIM), F32)
        for o in range(QKV_BLOCK):
            tile = jnp.where(same_block & (c % QKV_BLOCK == o), rows[:, o:o + 1], tile)
        return tile

    for t in range(bdkq_ref.shape[0]):
        bdq = block_diag(wq_ref, t)
        bdk = block_diag(wk_ref, t) * k_scale
        bdv = block_diag(wv_ref, t)
        bdkq = lax.dot_general(bdk, bdq, (((1,), (1,)), ((), ())), precision=hp, preferred_element_type=F32)
        bdkq_ref[t] = bdkq.astype(BF16)
        bdv_ref[t] = bdv.astype(BF16)
        aqk_ref[t] = (dot(bdq, wg_ref[0, t]) + dot(bdk, wg_ref[1, t])).astype(BF16)
        av_ref[t] = dot(bdv, wg_ref[2, t]).astype(BF16)


def _fold_weights(w_q, w_k, w_v, w_gate_f, w_gate_b, k_scale):
    e = w_q.shape[0] * QKV_BLOCK
    nt = e // MXU_DIM
    whole = lambda a: pl.BlockSpec(a.shape, lambda i: (0,) * a.ndim)
    tile = pl.BlockSpec((nt, MXU_DIM, MXU_DIM), lambda i: (0, 0, 0))
    gate = pl.BlockSpec((nt, MXU_DIM, LANES), lambda i: (0, 0, 0))
    wg = jnp.pad(jnp.concatenate([w_gate_f, w_gate_b], axis=1), ((0, 0), (0, LANES - 4 * N_HEADS)))
    args = [w.reshape(e, QKV_BLOCK) for w in (w_q, w_k, w_v)] + [wg.reshape(3, nt, MXU_DIM, LANES)]
    return pl.pallas_call(
        functools.partial(_fold_weights_kernel, k_scale=k_scale),
        grid=(1,),
        in_specs=[whole(a) for a in args],
        out_specs=[tile, tile, gate, gate],
        out_shape=[jax.ShapeDtypeStruct((nt, MXU_DIM, MXU_DIM), BF16),
                   jax.ShapeDtypeStruct((nt, MXU_DIM, MXU_DIM), BF16),
                   jax.ShapeDtypeStruct((nt, MXU_DIM, LANES), BF16),
                   jax.ShapeDtypeStruct((nt, MXU_DIM, LANES), BF16)],
        compiler_params=_cparams(("arbitrary",)),
        name="fold_weights",
    )(*args)


def _mlstm_pre_kernel(x_ref, xprev_ref, xnext_ref, mod_ref, g_ref, win_ref, convw_ref, convb_ref,
                      bdkq_ref, bdv_ref, aqk_ref, av_ref, bg_ref, hnorm_ref, skip_ref, *out_and_scratch,
                      states_only, tiles_per_seq):
    packed_ref, kt_ref, cols_ref, rows_ref, cmrep_ref, hn_scr, pre_scr = out_and_scratch
    names = PACKED_STATES if states_only else PACKED_FULL
    tm = packed_ref.shape[0]
    e = packed_ref.shape[1] // len(names)
    part = {name: packed_ref.at[:, j * e:(j + 1) * e] for j, name in enumerate(names)}
    k_ref, v_ref = part["k"], part["v"]
    step = pl.program_id(0)
    n_tiles = pl.num_programs(0) - 1

    def gate_stage():
        nch = tm // SCAN_CHUNK
        chunks = [pre_scr[ch * SCAN_CHUNK:(ch + 1) * SCAN_CHUNK, :] for ch in range(nch)]
        for ch, (cols, rows, cmrep) in enumerate(_gate_vectors(chunks)):
            rs = slice(ch * SCAN_CHUNK, (ch + 1) * SCAN_CHUNK)
            cols_ref[rs, :] = cols
            rows_ref[ch] = rows
            half = N_HEADS * LANES
            cmrep_ref[0, 0, rs, :] = cmrep[:, 0:half]
            cmrep_ref[0, 1, rs, :] = cmrep[:, half:]

    @pl.when(step == 0)
    def _():
        pre_scr[...] = jnp.zeros(pre_scr.shape, F32)

    @pl.when(step == n_tiles)
    def _():
        gate_stage()

    @pl.when(step < n_tiles)
    def _():
        gate_stage()
        _mlstm_pre_tile(x_ref, xprev_ref, xnext_ref, mod_ref, g_ref, win_ref, convw_ref, convb_ref, bdkq_ref, bdv_ref,
                        aqk_ref, av_ref, bg_ref, hnorm_ref, skip_ref,
                        None if states_only else (part["q"], part["g1"], part["g2"]), k_ref, kt_ref, v_ref,
                        hn_scr, pre_scr, step % tiles_per_seq, tiles_per_seq)


def _mlstm_pre_tile(x_ref, xprev_ref, xnext_ref, mod_ref, g_ref, win_ref, convw_ref, convb_ref, bdkq_ref, bdv_ref,
                    aqk_ref, av_ref, bg_ref, hnorm_ref, skip_ref, gate_outs, k_ref, kt_ref, v_ref,
                    hn_scr, pre_scr, i, nt):
    states_only = gate_outs is None
    if not states_only:
        q_ref, g1_ref, g2_ref = gate_outs
    tm, e = k_ref.shape
    shift = mod_ref[0, 0:1, :]
    scale = mod_ref[0, 1:2, :]
    g = g_ref[...]

    hn_scr[0:HALO, :] = _norm_mod(xprev_ref[...], g, shift, scale).astype(BF16)
    hn_scr[HALO:HALO + tm, :] = _norm_mod(x_ref[...], g, shift, scale).astype(BF16)
    hn_scr[HALO + tm:, :] = _norm_mod(xnext_ref[...], g, shift, scale).astype(BF16)

    def project(c):
        cs = slice(c * MXU_DIM, (c + 1) * MXU_DIM)
        xm_ext = _dot(hn_scr[...], win_ref[:, cs])
        if states_only:
            return xm_ext, None, None
        hc = hn_scr[HALO:HALO + tm, :]
        return (xm_ext, _dot(hc, win_ref[:, e + c * MXU_DIM:e + (c + 1) * MXU_DIM]),
                _dot(hc, win_ref[:, 2 * e + c * MXU_DIM:2 * e + (c + 1) * MXU_DIM]))

    pre = jnp.zeros((tm, LANES), F32) + bg_ref[...]
    n_chunks = e // MXU_DIM
    queue = [project(c) for c in range(min(PROJ_AHEAD, n_chunks))]
    for c in range(n_chunks):
        cs = slice(c * MXU_DIM, (c + 1) * MXU_DIM)
        xm_ext, z, og = queue.pop(0)
        if c + PROJ_AHEAD < n_chunks:
            queue.append(project(c + PROJ_AHEAD))
        xm_all = jnp.concatenate([jnp.where(i > 0, xm_ext[0:HALO], 0.0), xm_ext[HALO:HALO + tm],
                                  jnp.where(i < nt - 1, xm_ext[HALO + tm:], 0.0)], axis=0)
        xm = xm_all[HALO:HALO + tm]
        xm_prev = pltpu.roll(xm_all, 1, axis=0)[HALO:HALO + tm]
        xm_next = pltpu.roll(xm_all, tm + 2 * HALO - 1, axis=0)[HALO:HALO + tm]
        conv = (xm_prev * convw_ref[0:1, cs] + xm * convw_ref[1:2, cs] + xm_next * convw_ref[2:3, cs]
                + convb_ref[:, cs])
        xc = _silu(conv)
        xc_b = xc.astype(BF16)
        xm_b = xm.astype(BF16)
        k32 = _dot(xc_b, bdkq_ref[c])
        pre = pre + _dot(xc_b, aqk_ref[c]) + _dot(xm_b, av_ref[c])
        k_ref[:, cs] = k32.astype(BF16)
        k_t = k32.T.astype(BF16)
        for ch in range(tm // SCAN_CHUNK):
            kt_ref[ch, cs, :] = k_t[:, ch * SCAN_CHUNK:(ch + 1) * SCAN_CHUNK]
        v_ref[:, cs] = _dot(xm_b, bdv_ref[c]).astype(BF16)
        if not states_only:
            sz = _silu(z)
            q_ref[:, cs] = xc_b
            g1_ref[:, cs] = (hnorm_ref[:, cs] * jax.nn.sigmoid(og) * sz).astype(BF16)
            g2_ref[:, cs] = (skip_ref[:, cs] * xc * sz).astype(BF16)

    pre_scr[...] = pre


def _mlstm_pre(x, mod, pre_g, w_in_b, conv_w, conv_b, bdkq, bdv, aqk, av, bg, head_norm, skip, states_only):
    b, t, d = x.shape
    e = w_in_b.shape[1] // 3
    tm = min(PRE_TILE, t)
    nt = t // tm

    n = b * nt
    nch = tm // SCAN_CHUNK
    cur = lambda i: jnp.minimum(i, n - 1)
    prv = lambda i: jnp.maximum(i - 1, 0)
    n_packed = len(PACKED_STATES if states_only else PACKED_FULL)
    big = pl.BlockSpec((tm, n_packed * e), lambda i: (cur(i), 0))
    big_t = pl.BlockSpec((nch, e, SCAN_CHUNK), lambda i: (cur(i), 0, 0))
    tok = jax.ShapeDtypeStruct((b * t, n_packed * e), BF16)
    tok_t = jax.ShapeDtypeStruct((b * t // SCAN_CHUNK, e, SCAN_CHUNK), BF16)
    gate_specs = [pl.BlockSpec((tm, LANES), lambda i: (prv(i), 0)),
                  pl.BlockSpec((nch, 4 * N_HEADS, SCAN_CHUNK), lambda i: (prv(i), 0, 0)),
                  pl.BlockSpec((1, 2, tm, N_HEADS * LANES), lambda i: (prv(i) // nt, 0, prv(i) % nt, 0))]
    gate_shapes = [jax.ShapeDtypeStruct((b * t, LANES), F32),
                   jax.ShapeDtypeStruct((b * t // SCAN_CHUNK, 4 * N_HEADS, SCAN_CHUNK), F32),
                   jax.ShapeDtypeStruct((b, 2, t, N_HEADS * LANES), F32)]
    out_specs, out_shape = [big, big_t] + gate_specs, [tok, tok_t] + gate_shapes
    packed, kt, cols, rows, cmrep = pl.pallas_call(
        functools.partial(_mlstm_pre_kernel, states_only=states_only, tiles_per_seq=nt),
        grid=(n + 1,),
        in_specs=[
            pl.BlockSpec((tm, d), lambda i: (cur(i), 0)),
            pl.BlockSpec((HALO, d), lambda i: (jnp.maximum(cur(i) * (tm // HALO) - 1, 0), 0)),
            pl.BlockSpec((HALO, d), lambda i: (jnp.minimum((cur(i) + 1) * (tm // HALO), b * t // HALO - 1), 0)),
            pl.BlockSpec((1, 3, d), lambda i: (cur(i) // nt, 0, 0)),
            _const_spec((1, d)),
            _const_spec(w_in_b.shape),
            _const_spec(conv_w.shape),
            _const_spec((1, e)),
            _const_spec(bdkq.shape), _const_spec(bdv.shape), _const_spec(aqk.shape), _const_spec(av.shape),
            _const_spec((1, LANES)),
            _const_spec((1, e)), _const_spec((1, e)),
        ],
        out_specs=out_specs,
        out_shape=out_shape,
        scratch_shapes=[pltpu.VMEM((tm + 2 * HALO, d), BF16), pltpu.VMEM((tm, LANES), F32)],
        compiler_params=_cparams(("arbitrary",)),
        name="mlstm_pre",
    )(x.reshape(b * t, d), x.reshape(b * t, d), x.reshape(b * t, d), mod, pre_g.reshape(1, d), w_in_b,
      conv_w, conv_b.reshape(1, e),
      bdkq, bdv, aqk, av, bg, head_norm.reshape(1, e), skip.reshape(1, e))
    by_token = lambda o: o.reshape(b, t, o.shape[-1])
    by_chunk = lambda o: o.reshape((b, t // SCAN_CHUNK) + o.shape[1:])
    return by_token(packed), by_chunk(kt), by_token(cols), by_chunk(rows), cmrep


def _scan_chunk(get, cols, rows, cmrep, c_st, n_st, m_st, backward, emit):
    h = N_HEADS
    l = cols.shape[0]
    dh = c_st.shape[-1]
    off = 2 * h if backward else 0
    last = 0 if backward else l - 1
    st = []
    for head in range(h):
        cm_col = cols[:, 4 * h + off + head:4 * h + off + head + 1]
        st.append(dict(
            a_col=cols[:, off + head:off + head + 1],
            b_col=cols[:, off + h + head:off + h + head + 1],
            a_row=rows[off + head:off + head + 1, :],
            total=cols[last:last + 1, off + h + head:off + h + head + 1],
            amax=cm_col[last:last + 1, :],
            m_prev=m_st[head]))

    if emit is not None:
        ti = lax.broadcasted_iota(jnp.int32, (l, l), 0)
        si = lax.broadcasted_iota(jnp.int32, (l, l), 1)
        mask = (si >= ti) if backward else (ti >= si)
        for head in range(h):
            s = st[head]
            q = get("q", head)
            s["u"] = jnp.maximum(s["m_prev"], cmrep[:, head * LANES:(head + 1) * LANES])
            s["p"] = jnp.exp(jnp.where(mask, s["a_row"] - _lane_repeat(s["u"], l), -jnp.inf))
            s["qk"] = _dot(q, get("kt", head))
            s["qc"] = _dot(q, c_st[head].astype(BF16))
            sc = s["qk"] * s["p"]
            w_inter = jnp.exp(s["m_prev"] - s["u"])
            num = _dot(sc.astype(BF16), get("v", head)) + _lane_repeat(w_inter, dh) * s["qc"]
            qn = jnp.sum(get("q", head).astype(F32) * n_st[head], axis=-1, keepdims=True)
            den = jnp.sum(sc, axis=-1, keepdims=True) + w_inter[:, 0:1] * qn
            floor = jnp.exp(-(s["b_col"] + s["u"][:, 0:1]))
            emit(head, num / jnp.maximum(jnp.abs(den), floor))

    for head in range(h):
        s = st[head]
        s["u_last"] = jnp.maximum(s["m_prev"], s["amax"])
        s["decay"] = jnp.exp(s["m_prev"] - s["u_last"])
        wkt = (get("kt", head).astype(F32) * jnp.exp(s["a_row"] - s["u_last"])).astype(BF16)
        s["upd"] = _dot(wkt, get("v", head))
    for head in range(h):
        s = st[head]
        wk = get("k", head).astype(F32) * jnp.exp(s["a_col"] - s["u_last"])
        c_st[head] = s["decay"] * c_st[head] + s["upd"]
        n_st[head] = s["decay"] * n_st[head] + jnp.sum(wk, axis=0, keepdims=True)
        m_st[head] = s["total"] + s["u_last"]


def _tile_getter(refs, dh, chunk):
    rows = slice(chunk * SCAN_CHUNK, (chunk + 1) * SCAN_CHUNK)

    def get(name, head):
        hs = slice(head * dh, (head + 1) * dh)
        if name == "kt":
            return refs["kt"][0, chunk, hs, :]
        return refs[name][0, rows, hs]
    return get


def _ctx_state_kernel(k_ref, kt_ref, v_ref, cols_ref, rows_ref, c_out, n_out, m_out):
    dh = c_out.shape[-1]
    nc = k_ref.shape[1] // SCAN_CHUNK
    c_out[...] = jnp.zeros(c_out.shape, F32)
    n_out[...] = jnp.zeros(n_out.shape, F32)
    m_out[...] = jnp.zeros(m_out.shape, F32)
    refs = dict(k=k_ref, kt=kt_ref, v=v_ref)
    for direction in range(2):
        for step in range(nc):
            c = nc - 1 - step if direction else step
            rs = slice(c * SCAN_CHUNK, (c + 1) * SCAN_CHUNK)
            _scan_chunk(_tile_getter(refs, dh, c), cols_ref[0, rs, :], rows_ref[0, c], None,
                        c_out.at[0, direction], n_out.at[0, direction], m_out.at[0, direction],
                        bool(direction), None)


def _ctx_states(packed, kt, cols, rows):
    b, t, _ = packed.shape
    e = kt.shape[2]
    dh = e // N_HEADS
    nc = t // SCAN_CHUNK
    per_b = lambda nd: (lambda bi: (bi,) + (0,) * (nd - 1))
    col = lambda name: pl.BlockSpec((1, t, e), lambda bi: (bi, 0, PACKED_STATES.index(name)))
    return pl.pallas_call(
        _ctx_state_kernel,
        grid=(b,),
        in_specs=[col("k"), pl.BlockSpec((1, nc, e, SCAN_CHUNK), per_b(4)),
                  col("v"),
                  pl.BlockSpec((1, t, LANES), per_b(3)),
                  pl.BlockSpec((1, nc, 4 * N_HEADS, SCAN_CHUNK), per_b(4))],
        out_specs=[pl.BlockSpec((1, 2, N_HEADS, dh, dh), per_b(5)),
                   pl.BlockSpec((1, 2, N_HEADS, 1, dh), per_b(5)),
                   pl.BlockSpec((1, 2, N_HEADS, 1, 1), per_b(5))],
        out_shape=[jax.ShapeDtypeStruct((b, 2, N_HEADS, dh, dh), F32),
                   jax.ShapeDtypeStruct((b, 2, N_HEADS, 1, dh), F32),
                   jax.ShapeDtypeStruct((b, 2, N_HEADS, 1, 1), F32)],
        compiler_params=_cparams(("parallel",)),
        name="ctx_states",
    )(packed, kt, packed, cols, rows)


def _load_state(c0_ref, n0_ref, m0_ref, c_scr, n_scr, m_scr):
    @pl.when(pl.program_id(1) == 0)
    def _():
        c_scr[...] = c0_ref[0, 0]
        n_scr[...] = n0_ref[0, 0]
        m_scr[...] = m0_ref[0, 0]


def _scan_kernel(q_ref, k_ref, kt_ref, v_ref, cols_ref, rows_ref, cmrep_ref, c0_ref, n0_ref, m0_ref,
                 h_ref, c_scr, n_scr, m_scr, *, backward):
    _load_state(c0_ref, n0_ref, m0_ref, c_scr, n_scr, m_scr)
    dh = c_scr.shape[-1]
    refs = dict(q=q_ref, k=k_ref, kt=kt_ref, v=v_ref)
    chunks = range(rows_ref.shape[1])
    for j in (reversed(chunks) if backward else chunks):
        rs = slice(j * SCAN_CHUNK, (j + 1) * SCAN_CHUNK)

        def emit(head, h, rs=rs):
            h_ref[0, rs, head * dh:(head + 1) * dh] = h.astype(h_ref.dtype)

        _scan_chunk(_tile_getter(refs, dh, j), cols_ref[0, rs, :], rows_ref[0, j], cmrep_ref[0, 0, rs, :],
                    c_scr, n_scr, m_scr, backward, emit)


def _mixer_out_kernel(hf_ref, hb_ref, g1_ref, g2_ref, wout_ref, x_ref, mod0_ref, postg_ref,
                      mod1_ref, preg1_ref, win1_ref, x1_ref, u_ref, sz_ref):
    e = hf_ref.shape[2]
    dh = e // N_HEADS
    y = None
    for head in range(N_HEADS):
        hs = slice(head * dh, (head + 1) * dh)
        hsum = hf_ref[0, :, hs].astype(F32) + hb_ref[0, :, hs].astype(F32)
        cen = hsum - jnp.mean(hsum, axis=-1, keepdims=True)
        hn = cen * lax.rsqrt(jnp.mean(cen * cen, axis=-1, keepdims=True) + EPS)
        gated = (hn * g1_ref[0, :, hs].astype(F32) + g2_ref[0, :, hs].astype(F32)).astype(BF16)
        part = _dot(gated, wout_ref[hs, :])
        y = part if y is None else y + part
    yn = y * lax.rsqrt(jnp.mean(y * y, axis=-1, keepdims=True) + EPS) * postg_ref[...]
    x1 = x_ref[0] + mod0_ref[0, 2:3, :] * yn
    x1_ref[0] = x1
    _pool_in_proj(x1, mod1_ref, preg1_ref, win1_ref, u_ref, sz_ref)


def _scan_specs(b, t, e, chunks_per_step, reverse):
    dh = e // N_HEADS
    direction = 1 if reverse else 0
    l = SCAN_CHUNK * chunks_per_step
    ns = t // l
    cidx = (lambda c: ns - 1 - c) if reverse else (lambda c: c)
    tok = lambda w, col=0: pl.BlockSpec((1, l, w), lambda bi, c: (bi, cidx(c), col))
    state = lambda shape: pl.BlockSpec((1, 1) + shape, lambda bi, c: (bi, direction) + (0,) * len(shape),
                                       pipeline_mode=pl.Buffered(1))
    packed = lambda name: tok(e, PACKED_FULL.index(name))
    specs = [packed("q"), packed("k"),
             pl.BlockSpec((1, chunks_per_step, e, SCAN_CHUNK), lambda bi, c: (bi, cidx(c), 0, 0)),
             packed("v"), tok(LANES),
             pl.BlockSpec((1, chunks_per_step, 4 * N_HEADS, SCAN_CHUNK), lambda bi, c: (bi, cidx(c), 0, 0)),
             pl.BlockSpec((1, 1, l, N_HEADS * LANES), lambda bi, c: (bi, direction, cidx(c), 0)),
             state((N_HEADS, dh, dh)), state((N_HEADS, 1, dh)), state((N_HEADS, 1, 1))]
    scratch = [pltpu.VMEM((N_HEADS, dh, dh), F32), pltpu.VMEM((N_HEADS, 1, dh), F32),
               pltpu.VMEM((N_HEADS, 1, 1), F32)]
    return specs, scratch, tok, ns


def _scan(packed, kt, cols, rows, cmrep, c0, n0, m0, backward):
    b, t, _ = packed.shape
    e = kt.shape[2]
    specs, scratch, tok, ns = _scan_specs(b, t, e, SCAN_CHUNKS_PER_STEP, backward)
    return pl.pallas_call(
        functools.partial(_scan_kernel, backward=backward),
        grid=(b, ns),
        in_specs=specs,
        out_specs=tok(e),
        out_shape=jax.ShapeDtypeStruct((b, t, e), BF16),
        scratch_shapes=scratch,
        compiler_params=_cparams(("parallel", "arbitrary")),
        name="mlstm_scan_bwd" if backward else "mlstm_scan_fwd",
    )(packed, packed, kt, packed, cols, rows, cmrep, c0, n0, m0)


def _mixer_out(hf, hb, packed, w_out_b, x, mod0, post_g, mod1, pre_g1, w_in1_b):
    b, t, e = hf.shape
    d = x.shape[-1]
    tm = POOL_TILE
    row = lambda bi, i: (bi, i, 0)
    tok = lambda w: pl.BlockSpec((1, tm, w), row)
    col = lambda name: pl.BlockSpec((1, tm, e), lambda bi, i: (bi, i, PACKED_FULL.index(name)))
    modspec = pl.BlockSpec((1, 3, d), lambda bi, i: (bi, 0, 0))
    return pl.pallas_call(
        _mixer_out_kernel,
        grid=(b, t // tm),
        in_specs=[tok(e), tok(e), col("g1"), col("g2"), _const_spec(w_out_b.shape), tok(d), modspec,
                  _const_spec((1, d)), modspec, _const_spec((1, d)), _const_spec(w_in1_b.shape)],
        out_specs=[tok(d), tok(e), tok(e)],
        out_shape=[jax.ShapeDtypeStruct((b, t, d), F32), jax.ShapeDtypeStruct((b, t, e), BF16),
                   jax.ShapeDtypeStruct((b, t, e), BF16)],
        compiler_params=_cparams(("parallel", "parallel")),
        name="mixer_out_pool_in",
    )(hf, hb, packed, packed, w_out_b, x, mod0, post_g.reshape(1, d), mod1, pre_g1.reshape(1, d), w_in1_b)


def _pool_in_proj(x, mod_ref, g_ref, win_ref, u_ref, sz_ref):
    e = u_ref.shape[2]
    hn = _norm_mod(x, g_ref[...], mod_ref[0, 0:1, :], mod_ref[0, 1:2, :]).astype(BF16)
    for c in range(e // MXU_DIM):
        cs = slice(c * MXU_DIM, (c + 1) * MXU_DIM)
        u_ref[0, :, cs] = _dot(hn, win_ref[:, cs]).astype(BF16)
        sz_ref[0, :, cs] = _silu(_dot(hn, win_ref[:, e + c * MXU_DIM:e + (c + 1) * MXU_DIM])).astype(BF16)


def _pool_out_kernel(up_ref, uc_ref, un_ref, sz_ref, band_ref, icnt_ref, wpool_ref, pscale_ref,
                     wout_ref, x_ref, mod_ref, postg_ref, o_ref, ext_scr):
    i = pl.program_id(1)
    nt = pl.num_programs(1)
    tm = uc_ref.shape[1]
    gd = wpool_ref.shape[1]
    y = None
    for g, w in enumerate(POOL_WINDOWS):
        gs = slice(g * gd, (g + 1) * gd)
        before = (w // 2) * GRID_W
        after = (w - w // 2 - 1) * GRID_W
        ext_scr[tm - before:tm, gs] = jnp.where(i > 0, up_ref[0, tm - before:, gs].astype(F32), 0.0)
        ext_scr[tm:2 * tm, gs] = uc_ref[0, :, gs].astype(F32)
        if after:
            ext_scr[2 * tm:2 * tm + after, gs] = jnp.where(i < nt - 1, un_ref[0, 0:after, gs].astype(F32), 0.0)
        n = tm + (w - 1) * GRID_W
        lo = tm - before
        rsum = ext_scr[lo:lo + n, gs]
        span = 1
        while span < w:
            n -= span * GRID_W
            rsum = rsum[0:n] + rsum[span * GRID_W:span * GRID_W + n]
            span *= 2
        rsum = rsum.astype(BF16)
        parts = [_dot(band_ref[g], rsum[s * MXU_DIM:(s + 1) * MXU_DIM, :]) for s in range(tm // MXU_DIM)]
        mean = jnp.concatenate(parts, axis=0) * icnt_ref[:, g:g + 1]
        diff = mean - ext_scr[tm:2 * tm, gs]
        yg = _dot(diff.astype(BF16), wpool_ref[g]) * pscale_ref[:, gs] * sz_ref[0, :, gs].astype(F32)
        part = _dot(yg.astype(BF16), wout_ref[gs, :])
        y = part if y is None else y + part
    yn = y * lax.rsqrt(jnp.mean(y * y, axis=-1, keepdims=True) + EPS) * postg_ref[...]
    o_ref[0] = x_ref[0] + mod_ref[0, 2:3, :] * yn


def _pool_constants(t):
    rows = t // GRID_W
    bands = np.zeros((len(POOL_WINDOWS), MXU_DIM, MXU_DIM), np.float32)
    icnt = np.zeros((t, LANES), np.float32)
    for g, w in enumerate(POOL_WINDOWS):
        lo_off, hi_off = -(w // 2), w - w // 2
        for p in range(MXU_DIM):
            c = p % GRID_W
            base = p - c
            lo, hi = max(c + lo_off, 0), min(c + hi_off, GRID_W)
            bands[g, p, base + lo:base + hi] = 1.0
        cc = np.arange(GRID_W)
        cnt_c = np.clip(cc + hi_off, 0, GRID_W) - np.clip(cc + lo_off, 0, GRID_W)
        rr = np.arange(rows)
        cnt_r = np.clip(rr + hi_off, 0, rows) - np.clip(rr + lo_off, 0, rows)
        icnt[:, g] = (1.0 / (cnt_r[:, None] * cnt_c[None, :])).reshape(-1)
    return bands, icnt


def _pool_out(u, sz, w_pool_b, pool_scale, w_out_b, x, mod, post_g):
    b, t, e = u.shape
    d = x.shape[-1]
    tm = POOL_TILE
    assert tm >= (max(POOL_WINDOWS) // 2) * GRID_W and tm % MXU_DIM == 0 and t % tm == 0
    nt = t // tm
    gd = e // len(POOL_WINDOWS)
    bands, icnt = _pool_constants(t)
    row = lambda bi, i: (bi, i, 0)
    return pl.pallas_call(
        _pool_out_kernel,
        grid=(b, nt),
        in_specs=[
            pl.BlockSpec((1, tm, e), lambda bi, i: (bi, jnp.maximum(i - 1, 0), 0)),
            pl.BlockSpec((1, tm, e), row),
            pl.BlockSpec((1, tm, e), lambda bi, i: (bi, jnp.minimum(i + 1, nt - 1), 0)),
            pl.BlockSpec((1, tm, e), row),
            _const_spec(bands.shape),
            pl.BlockSpec((tm, LANES), lambda bi, i: (i, 0)),
            _const_spec(w_pool_b.shape),
            _const_spec((1, e)),
            _const_spec(w_out_b.shape),
            pl.BlockSpec((1, tm, d), row),
            pl.BlockSpec((1, 3, d), lambda bi, i: (bi, 0, 0)),
            _const_spec((1, d)),
        ],
        out_specs=pl.BlockSpec((1, tm, d), row),
        out_shape=jax.ShapeDtypeStruct((b, t, d), F32),
        scratch_shapes=[pltpu.VMEM((3 * tm, e), F32)],
        compiler_params=_cparams(("parallel", "parallel")),
        name="pool_out",
    )(u, u, u, sz, jnp.asarray(bands, BF16), jnp.asarray(icnt), w_pool_b, pool_scale.reshape(1, e),
      w_out_b, x, mod, post_g.reshape(1, d))


def kernel(x, c, ctx, c_ctx, w_ada, b_ada, a_norm_pre, a_norm_post, a_w_in, a_conv_w, a_conv_b,
           a_w_q, a_w_k, a_w_v, a_w_gate_f, a_b_gate_f, a_w_gate_b, a_b_gate_b, a_head_norm, a_skip,
           a_w_out, b_norm_pre, b_norm_post, b_w_in, b_w_pool, b_pool_scale, b_w_out):
    bsz, t, d = x.shape
    t_ctx = ctx.shape[1]
    assert t % SCAN_CHUNK == 0 and t_ctx % SCAN_CHUNK == 0 and t % GRID_W == 0

    n_cond = -(-(bsz + 1) // 8) * 8
    cond = jnp.concatenate([c, c_ctx[None], jnp.zeros((n_cond - bsz - 1, d), F32)], axis=0)
    mods = _ada(cond, bsz + 1, w_ada, b_ada).reshape(w_ada.shape[0], n_cond, 3, d)
    mod_lat = [mods[i, :bsz] for i in range(w_ada.shape[0])]
    mod_ctx = jnp.broadcast_to(mods[0, bsz][None], (bsz, 3, d))

    w_in_b = a_w_in[0].astype(BF16)
    k_scale = float((a_w_in.shape[2] // 3 // N_HEADS) ** -0.5)
    bdkq, bdv, aqk, av = _fold_weights(a_w_q[0], a_w_k[0], a_w_v[0], a_w_gate_f[0], a_w_gate_b[0], k_scale)
    bg = jnp.pad(jnp.concatenate([a_b_gate_f[0], a_b_gate_b[0]]), (0, LANES - 4 * N_HEADS)).reshape(1, LANES)
    pre_args = (a_norm_pre[0], w_in_b, a_conv_w[0], a_conv_b[0], bdkq, bdv, aqk, av, bg,
                a_head_norm[0], a_skip[0])
    packed_l, kt_l, cols_l, rows_l, cmrep_l = _mlstm_pre(x, mod_lat[0], *pre_args, states_only=False)
    packed_c, kt_c, cols_c, rows_c, _ = _mlstm_pre(ctx, mod_ctx, *pre_args, states_only=True)
    c0, n0, m0 = _ctx_states(packed_c, kt_c, cols_c, rows_c)
    scan_in = (packed_l, kt_l, cols_l, rows_l, cmrep_l, c0, n0, m0)
    hb = _scan(*scan_in, backward=True)
    hf = _scan(*scan_in, backward=False)

    x, u, sz = _mixer_out(hf, hb, packed_l, a_w_out[0].astype(BF16), x, mod_lat[0], a_norm_post[0],
                          mod_lat[1], b_norm_pre[0], b_w_in[0].astype(BF16))
    x = _pool_out(u, sz, b_w_pool[0].astype(BF16), b_pool_scale[0], b_w_out[0].astype(BF16),
                  x, mod_lat[1], b_norm_post[0])
    return x
```

```python
import functools

import numpy as np
import jax
import jax.numpy as jnp
from jax import lax
from jax.experimental import pallas as pl
from jax.experimental.pallas import tpu as pltpu

EPS = 1e-6
N_HEADS = 4
QKV_BLOCK = 4
GRID_W = 64
POOL_WINDOWS = (2, 4, 8, 16)

LANES = 128
MXU_DIM = 256
HALO = 16
VMEM_LIMIT_BYTES = 60 * 1024 * 1024

SCAN_CHUNK = 256
PRE_TILE = 512
PROJ_AHEAD = 3
POOL_TILE = 512
SCAN_CHUNKS_PER_STEP = 2
BF16 = jnp.bfloat16
F32 = jnp.float32


def _dot(a, b):
    return jnp.dot(a, b, preferred_element_type=F32)


def _silu(v):
    return v * jax.nn.sigmoid(v)


def _cparams(sem):
    return pltpu.CompilerParams(dimension_semantics=sem, vmem_limit_bytes=VMEM_LIMIT_BYTES)


def _const_spec(shape):
    nd = len(shape)
    return pl.BlockSpec(shape, lambda *_: (0,) * nd, pipeline_mode=pl.Buffered(1))


def _lane_repeat(col, width):
    return jnp.concatenate([col] * (width // LANES), axis=1) if width > LANES else col


def _ada_kernel(cond_t_ref, w_ref, b_ref, o_ref, *, n_used):
    s = _silu(cond_t_ref[...])
    w = w_ref[0]
    for r in range(n_used):
        o_ref[0, r:r + 1, :] = jnp.sum(w * s[:, r:r + 1], axis=0, keepdims=True) + b_ref[0]
    o_ref[0, n_used:, :] = jnp.zeros((o_ref.shape[1] - n_used, o_ref.shape[2]), F32)


def _ada(cond, n_used, w_ada, b_ada):
    depth, d, n3 = w_ada.shape
    rows = cond.shape[0]
    tn = max(w for w in range(LANES, 1024 + 1, LANES) if n3 % w == 0)
    return pl.pallas_call(
        functools.partial(_ada_kernel, n_used=n_used),
        grid=(depth, n3 // tn),
        in_specs=[
            pl.BlockSpec((d, rows), lambda l, j: (0, 0)),
            pl.BlockSpec((1, d, tn), lambda l, j: (l, 0, j)),
            pl.BlockSpec((1, 1, tn), lambda l, j: (l, 0, j)),
        ],
        out_specs=pl.BlockSpec((1, rows, tn), lambda l, j: (l, 0, j)),
        out_shape=jax.ShapeDtypeStruct((depth, rows, n3), F32),
        compiler_params=_cparams(("parallel", "parallel")),
        name="ada_mod",
    )(cond.T, w_ada, b_ada.reshape(depth, 1, n3))


def _norm_mod(xv, g, shift, scale):
    y = xv * lax.rsqrt(jnp.mean(xv * xv, axis=-1, keepdims=True) + EPS) * g
    return y * (1.0 + scale) + shift


def _scan_along_lanes(x, fwd_rows, lane, combine, fill):
    l = x.shape[1]
    step = 1
    while step < l:
        from_left = jnp.where(lane >= step, pltpu.roll(x, step, axis=1), fill)
        from_right = jnp.where(lane < l - step, pltpu.roll(x, l - step, axis=1), fill)
        x = combine(x, jnp.where(fwd_rows, from_left, from_right))
        step *= 2
    return x


def _gate_vectors(pre_chunks):
    h = N_HEADS
    l = pre_chunks[0].shape[0]
    pt = jnp.concatenate([p.T[0:4 * h, :] for p in pre_chunks], axis=0)
    row = lax.broadcasted_iota(jnp.int32, pt.shape, 0) % (4 * h)
    lane = lax.broadcasted_iota(jnp.int32, pt.shape, 1)
    fwd_rows = row < 2 * h
    log_f = jnp.minimum(pt, 0.0) - jnp.log1p(jnp.exp(-jnp.abs(pt)))
    b = _scan_along_lanes(log_f, fwd_rows, lane, jnp.add, 0.0)
    a = pt - pltpu.roll(b, pt.shape[0] - h, axis=0)
    cmax = _scan_along_lanes(a, fwd_rows, lane, jnp.maximum, -jnp.inf)
    rows = jnp.where((row % (2 * h)) < h, a, b)
    out = []
    for ch in range(len(pre_chunks)):
        rs = slice(ch * 4 * h, (ch + 1) * 4 * h)
        stacked = jnp.concatenate([rows[rs], cmax[rs], jnp.zeros((LANES - 8 * h, l), F32)], axis=0)
        cols = stacked.T
        reps = []
        for direction in range(2):
            for head in range(h):
                j = 4 * h + direction * 2 * h + head
                reps.append(jnp.broadcast_to(cols[:, j:j + 1], (l, LANES)))
        out.append((cols, rows[rs], jnp.concatenate(reps, axis=1)))
    return out


def _fold_weights_kernel(wq_ref, wk_ref, wv_ref, wg_ref, bdkq_ref, bdv_ref, aqk_ref, av_ref, *, k_scale):
    hp = lax.Precision.HIGHEST
    dot = lambda a, b: jnp.dot(a, b, precision=hp, preferred_element_type=F32)
    r = lax.broadcasted_iota(jnp.int32, (MXU_DIM, MXU_DIM), 0)
    c = lax.broadcasted_iota(jnp.int32, (MXU_DIM, MXU_DIM), 1)
    same_block = (r // QKV_BLOCK) == (c // QKV_BLOCK)

    def block_diag(w_ref, t):
        rows = w_ref[t * MXU_DIM:(t + 1) * MXU_DIM, :]
        tile = jnp.zeros((MXU_DIM, MXU_DIM), F32)
        for o in range(QKV_BLOCK):
            tile = jnp.where(same_block & (c % QKV_BLOCK == o), rows[:, o:o + 1], tile)
        return tile

    for t in range(bdkq_ref.shape[0]):
        bdq = block_diag(wq_ref, t)
        bdk = block_diag(wk_ref, t) * k_scale
        bdv = block_diag(wv_ref, t)
        bdkq = lax.dot_general(bdk, bdq, (((1,), (1,)), ((), ())), precision=hp, preferred_element_type=F32)
        bdkq_ref[t] = bdkq.astype(BF16)
        bdv_ref[t] = bdv.astype(BF16)
        aqk_ref[t] = (dot(bdq, wg_ref[0, t]) + dot(bdk, wg_ref[1, t])).astype(BF16)
        av_ref[t] = dot(bdv, wg_ref[2, t]).astype(BF16)


def _fold_weights(w_q, w_k, w_v, w_gate_f, w_gate_b, k_scale):
    e = w_q.shape[0] * QKV_BLOCK
    nt = e // MXU_DIM
    whole = lambda a: pl.BlockSpec(a.shape, lambda i: (0,) * a.ndim)
    tile = pl.BlockSpec((nt, MXU_DIM, MXU_DIM), lambda i: (0, 0, 0))
    gate = pl.BlockSpec((nt, MXU_DIM, LANES), lambda i: (0, 0, 0))
    wg = jnp.pad(jnp.concatenate([w_gate_f, w_gate_b], axis=1), ((0, 0), (0, LANES - 4 * N_HEADS)))
    args = [w.reshape(e, QKV_BLOCK) for w in (w_q, w_k, w_v)] + [wg.reshape(3, nt, MXU_DIM, LANES)]
    return pl.pallas_call(
        functools.partial(_fold_weights_kernel, k_scale=k_scale),
        grid=(1,),
        in_specs=[whole(a) for a in args],
        out_specs=[tile, tile, gate, gate],
        out_shape=[jax.ShapeDtypeStruct((nt, MXU_DIM, MXU_DIM), BF16),
                   jax.ShapeDtypeStruct((nt, MXU_DIM, MXU_DIM), BF16),
                   jax.ShapeDtypeStruct((nt, MXU_DIM, LANES), BF16),
                   jax.ShapeDtypeStruct((nt, MXU_DIM, LANES), BF16)],
        compiler_params=_cparams(("arbitrary",)),
        name="fold_weights",
    )(*args)


def _mlstm_pre_kernel(x_ref, xprev_ref, xnext_ref, mod_ref, g_ref, win_ref, convw_ref, convb_ref,
                      bdkq_ref, bdv_ref, aqk_ref, av_ref, bg_ref, hnorm_ref, skip_ref, *rest,
                      states_only, tiles_per_seq, n_later):
    later_f32, rest = rest[:n_later], rest[n_later:]
    later_bf16, (hn_scr, pre_scr) = rest[-2 - n_later:-2], rest[-2:]
    if states_only:
        k_ref, kt_ref, v_ref, cols_ref, rows_ref, cmrep_ref = rest[:-2 - n_later]
    else:
        q_ref, k_ref, kt_ref, v_ref, g1_ref, g2_ref, cols_ref, rows_ref, cmrep_ref = rest[:-2 - n_later]
    step = pl.program_id(0)
    n_tiles = pl.num_programs(0) - 1
    tm, e = k_ref.shape

    def gate_stage():
        nch = tm // SCAN_CHUNK
        chunks = [pre_scr[ch * SCAN_CHUNK:(ch + 1) * SCAN_CHUNK, :] for ch in range(nch)]
        for ch, (cols, rows, cmrep) in enumerate(_gate_vectors(chunks)):
            rs = slice(ch * SCAN_CHUNK, (ch + 1) * SCAN_CHUNK)
            cols_ref[rs, :] = cols
            rows_ref[ch] = rows
            half = N_HEADS * LANES
            cmrep_ref[0, 0, rs, :] = cmrep[:, 0:half]
            cmrep_ref[0, 1, rs, :] = cmrep[:, half:]

    @pl.when(step == 0)
    def _():
        pre_scr[...] = jnp.zeros(pre_scr.shape, F32)

    @pl.when(step == n_tiles)
    def _():
        gate_stage()

    @pl.when(step < n_tiles)
    def _():
        gate_stage()
        _mlstm_pre_tile(x_ref, xprev_ref, xnext_ref, mod_ref, g_ref, win_ref, convw_ref, convb_ref, bdkq_ref, bdv_ref,
                        aqk_ref, av_ref, bg_ref, hnorm_ref, skip_ref,
                        None if states_only else (q_ref, g1_ref, g2_ref), k_ref, kt_ref, v_ref,
                        hn_scr, pre_scr, step % tiles_per_seq, tiles_per_seq)
        for w32_ref, w16_ref in zip(later_f32, later_bf16):
            w16_ref[...] = w32_ref[...].astype(BF16)


def _mlstm_pre_tile(x_ref, xprev_ref, xnext_ref, mod_ref, g_ref, win_ref, convw_ref, convb_ref, bdkq_ref, bdv_ref,
                    aqk_ref, av_ref, bg_ref, hnorm_ref, skip_ref, gate_outs, k_ref, kt_ref, v_ref,
                    hn_scr, pre_scr, i, nt):
    states_only = gate_outs is None
    if not states_only:
        q_ref, g1_ref, g2_ref = gate_outs
    tm, e = k_ref.shape
    shift = mod_ref[0, 0:1, :]
    scale = mod_ref[0, 1:2, :]
    g = g_ref[...]

    hn_scr[0:HALO, :] = _norm_mod(xprev_ref[...], g, shift, scale).astype(BF16)
    hn_scr[HALO:HALO + tm, :] = _norm_mod(x_ref[...], g, shift, scale).astype(BF16)
    hn_scr[HALO + tm:, :] = _norm_mod(xnext_ref[...], g, shift, scale).astype(BF16)

    def project(c):
        cs = slice(c * MXU_DIM, (c + 1) * MXU_DIM)
        xm_ext = _dot(hn_scr[...], win_ref[:, cs])
        if states_only:
            return xm_ext, None, None
        hc = hn_scr[HALO:HALO + tm, :]
        return (xm_ext, _dot(hc, win_ref[:, e + c * MXU_DIM:e + (c + 1) * MXU_DIM]),
                _dot(hc, win_ref[:, 2 * e + c * MXU_DIM:2 * e + (c + 1) * MXU_DIM]))

    pre = jnp.zeros((tm, LANES), F32) + bg_ref[...]
    n_chunks = e // MXU_DIM
    queue = [project(c) for c in range(min(PROJ_AHEAD, n_chunks))]
    for c in range(n_chunks):
        cs = slice(c * MXU_DIM, (c + 1) * MXU_DIM)
        xm_ext, z, og = queue.pop(0)
        if c + PROJ_AHEAD < n_chunks:
            queue.append(project(c + PROJ_AHEAD))
        xm_all = jnp.concatenate([jnp.where(i > 0, xm_ext[0:HALO], 0.0), xm_ext[HALO:HALO + tm],
                                  jnp.where(i < nt - 1, xm_ext[HALO + tm:], 0.0)], axis=0)
        xm = xm_all[HALO:HALO + tm]
        xm_prev = pltpu.roll(xm_all, 1, axis=0)[HALO:HALO + tm]
        xm_next = pltpu.roll(xm_all, tm + 2 * HALO - 1, axis=0)[HALO:HALO + tm]
        conv = (xm_prev * convw_ref[0:1, cs] + xm * convw_ref[1:2, cs] + xm_next * convw_ref[2:3, cs]
                + convb_ref[:, cs])
        xc = _silu(conv)
        xc_b = xc.astype(BF16)
        xm_b = xm.astype(BF16)
        k32 = _dot(xc_b, bdkq_ref[c])
        pre = pre + _dot(xc_b, aqk_ref[c]) + _dot(xm_b, av_ref[c])
        k_ref[:, cs] = k32.astype(BF16)
        k_t = k32.T.astype(BF16)
        for ch in range(tm // SCAN_CHUNK):
            kt_ref[ch, cs, :] = k_t[:, ch * SCAN_CHUNK:(ch + 1) * SCAN_CHUNK]
        v_ref[:, cs] = _dot(xm_b, bdv_ref[c]).astype(BF16)
        if not states_only:
            sz = _silu(z)
            q_ref[:, cs] = xc_b
            g1_ref[:, cs] = (hnorm_ref[:, cs] * jax.nn.sigmoid(og) * sz).astype(BF16)
            g2_ref[:, cs] = (skip_ref[:, cs] * xc * sz).astype(BF16)

    pre_scr[...] = pre


def _mlstm_pre(x, mod, pre_g, w_in_b, conv_w, conv_b, bdkq, bdv, aqk, av, bg, head_norm, skip, states_only,
               later_weights=()):
    b, t, d = x.shape
    e = w_in_b.shape[1] // 3
    tm = min(PRE_TILE, t)
    nt = t // tm

    n = b * nt
    nch = tm // SCAN_CHUNK
    cur = lambda i: jnp.minimum(i, n - 1)
    prv = lambda i: jnp.maximum(i - 1, 0)
    big = pl.BlockSpec((tm, e), lambda i: (cur(i), 0))
    big_t = pl.BlockSpec((nch, e, SCAN_CHUNK), lambda i: (cur(i), 0, 0))
    tok = jax.ShapeDtypeStruct((b * t, e), BF16)
    tok_t = jax.ShapeDtypeStruct((b * t // SCAN_CHUNK, e, SCAN_CHUNK), BF16)
    gate_specs = [pl.BlockSpec((tm, LANES), lambda i: (prv(i), 0)),
                  pl.BlockSpec((nch, 4 * N_HEADS, SCAN_CHUNK), lambda i: (prv(i), 0, 0)),
                  pl.BlockSpec((1, 2, tm, N_HEADS * LANES), lambda i: (prv(i) // nt, 0, prv(i) % nt, 0))]
    gate_shapes = [jax.ShapeDtypeStruct((b * t, LANES), F32),
                   jax.ShapeDtypeStruct((b * t // SCAN_CHUNK, 4 * N_HEADS, SCAN_CHUNK), F32),
                   jax.ShapeDtypeStruct((b, 2, t, N_HEADS * LANES), F32)]
    if states_only:
        out_specs, out_shape = [big, big_t, big] + gate_specs, [tok, tok_t, tok] + gate_shapes
    else:
        out_specs = [big, big, big_t, big, big, big] + gate_specs
        out_shape = [tok, tok, tok_t, tok, tok, tok] + gate_shapes
    later_specs = [pl.BlockSpec((w.shape[0] // n, w.shape[1]), lambda i: (cur(i), 0)) for w in later_weights]
    assert all(w.shape[0] % (HALO * n) == 0 for w in later_weights)
    outs = pl.pallas_call(
        functools.partial(_mlstm_pre_kernel, states_only=states_only, tiles_per_seq=nt,
                          n_later=len(later_weights)),
        grid=(n + 1,),
        in_specs=[
            pl.BlockSpec((tm, d), lambda i: (cur(i), 0)),
            pl.BlockSpec((HALO, d), lambda i: (jnp.maximum(cur(i) * (tm // HALO) - 1, 0), 0)),
            pl.BlockSpec((HALO, d), lambda i: (jnp.minimum((cur(i) + 1) * (tm // HALO), b * t // HALO - 1), 0)),
            pl.BlockSpec((1, 3, d), lambda i: (cur(i) // nt, 0, 0)),
            _const_spec((1, d)),
            _const_spec(w_in_b.shape),
            _const_spec(conv_w.shape),
            _const_spec((1, e)),
            _const_spec(bdkq.shape), _const_spec(bdv.shape), _const_spec(aqk.shape), _const_spec(av.shape),
            _const_spec((1, LANES)),
            _const_spec((1, e)), _const_spec((1, e)),
        ] + later_specs,
        out_specs=out_specs + later_specs,
        out_shape=out_shape + [jax.ShapeDtypeStruct(w.shape, BF16) for w in later_weights],
        scratch_shapes=[pltpu.VMEM((tm + 2 * HALO, d), BF16), pltpu.VMEM((tm, LANES), F32)],
        compiler_params=_cparams(("arbitrary",)),
        name="mlstm_pre",
    )(x.reshape(b * t, d), x.reshape(b * t, d), x.reshape(b * t, d), mod, pre_g.reshape(1, d), w_in_b,
      conv_w, conv_b.reshape(1, e),
      bdkq, bdv, aqk, av, bg, head_norm.reshape(1, e), skip.reshape(1, e), *later_weights)
    by_token = lambda o: o.reshape(b, t, o.shape[-1])
    by_chunk = lambda o: o.reshape((b, t // SCAN_CHUNK) + o.shape[1:])
    outs, later_bf16 = outs[:len(out_specs)], outs[len(out_specs):]
    *main, cols, rows, cmrep = outs
    kt_pos = 1 if states_only else 2
    main = [by_chunk(o) if j == kt_pos else by_token(o) for j, o in enumerate(main)]
    return main + [by_token(cols), by_chunk(rows), cmrep] + list(later_bf16)


def _scan_chunk(get, cols, rows, cmrep, c_st, n_st, m_st, backward, emit):
    h = N_HEADS
    l = cols.shape[0]
    dh = c_st.shape[-1]
    off = 2 * h if backward else 0
    last = 0 if backward else l - 1
    st = []
    for head in range(h):
        cm_col = cols[:, 4 * h + off + head:4 * h + off + head + 1]
        st.append(dict(
            a_col=cols[:, off + head:off + head + 1],
            b_col=cols[:, off + h + head:off + h + head + 1],
            a_row=rows[off + head:off + head + 1, :],
            total=cols[last:last + 1, off + h + head:off + h + head + 1],
            amax=cm_col[last:last + 1, :],
            m_prev=m_st[head]))

    if emit is not None:
        ti = lax.broadcasted_iota(jnp.int32, (l, l), 0)
        si = lax.broadcasted_iota(jnp.int32, (l, l), 1)
        mask = (si >= ti) if backward else (ti >= si)
        for head in range(h):
            s = st[head]
            q = get("q", head)
            s["u"] = jnp.maximum(s["m_prev"], cmrep[:, head * LANES:(head + 1) * LANES])
            s["p"] = jnp.exp(jnp.where(mask, s["a_row"] - _lane_repeat(s["u"], l), -jnp.inf))
            s["qk"] = _dot(q, get("kt", head))
            s["qc"] = _dot(q, c_st[head].astype(BF16))
            sc = s["qk"] * s["p"]
            w_inter = jnp.exp(s["m_prev"] - s["u"])
            num = _dot(sc.astype(BF16), get("v", head)) + _lane_repeat(w_inter, dh) * s["qc"]
            qn = jnp.sum(get("q", head).astype(F32) * n_st[head], axis=-1, keepdims=True)
            den = jnp.sum(sc, axis=-1, keepdims=True) + w_inter[:, 0:1] * qn
            floor = jnp.exp(-(s["b_col"] + s["u"][:, 0:1]))
            emit(head, num / jnp.maximum(jnp.abs(den), floor))

    for head in range(h):
        s = st[head]
        s["u_last"] = jnp.maximum(s["m_prev"], s["amax"])
        s["decay"] = jnp.exp(s["m_prev"] - s["u_last"])
        wkt = (get("kt", head).astype(F32) * jnp.exp(s["a_row"] - s["u_last"])).astype(BF16)
        s["upd"] = _dot(wkt, get("v", head))
    for head in range(h):
        s = st[head]
        wk = get("k", head).astype(F32) * jnp.exp(s["a_col"] - s["u_last"])
        c_st[head] = s["decay"] * c_st[head] + s["upd"]
        n_st[head] = s["decay"] * n_st[head] + jnp.sum(wk, axis=0, keepdims=True)
        m_st[head] = s["total"] + s["u_last"]


def _tile_getter(refs, dh, chunk):
    rows = slice(chunk * SCAN_CHUNK, (chunk + 1) * SCAN_CHUNK)

    def get(name, head):
        hs = slice(head * dh, (head + 1) * dh)
        if name == "kt":
            return refs["kt"][0, chunk, hs, :]
        return refs[name][0, rows, hs]
    return get


def _ctx_state_kernel(k_ref, kt_ref, v_ref, cols_ref, rows_ref, c_out, n_out, m_out):
    dh = c_out.shape[-1]
    nc = k_ref.shape[1] // SCAN_CHUNK
    c_out[...] = jnp.zeros(c_out.shape, F32)
    n_out[...] = jnp.zeros(n_out.shape, F32)
    m_out[...] = jnp.zeros(m_out.shape, F32)
    refs = dict(k=k_ref, kt=kt_ref, v=v_ref)
    for direction in range(2):
        for step in range(nc):
            c = nc - 1 - step if direction else step
            rs = slice(c * SCAN_CHUNK, (c + 1) * SCAN_CHUNK)
            _scan_chunk(_tile_getter(refs, dh, c), cols_ref[0, rs, :], rows_ref[0, c], None,
                        c_out.at[0, direction], n_out.at[0, direction], m_out.at[0, direction],
                        bool(direction), None)


def _ctx_states(k, kt, v, cols, rows):
    b, t, e = k.shape
    dh = e // N_HEADS
    nc = t // SCAN_CHUNK
    per_b = lambda nd: (lambda bi: (bi,) + (0,) * (nd - 1))
    return pl.pallas_call(
        _ctx_state_kernel,
        grid=(b,),
        in_specs=[pl.BlockSpec((1, t, e), per_b(3)), pl.BlockSpec((1, nc, e, SCAN_CHUNK), per_b(4)),
                  pl.BlockSpec((1, t, e), per_b(3)),
                  pl.BlockSpec((1, t, LANES), per_b(3)),
                  pl.BlockSpec((1, nc, 4 * N_HEADS, SCAN_CHUNK), per_b(4))],
        out_specs=[pl.BlockSpec((1, 2, N_HEADS, dh, dh), per_b(5)),
                   pl.BlockSpec((1, 2, N_HEADS, 1, dh), per_b(5)),
                   pl.BlockSpec((1, 2, N_HEADS, 1, 1), per_b(5))],
        out_shape=[jax.ShapeDtypeStruct((b, 2, N_HEADS, dh, dh), F32),
                   jax.ShapeDtypeStruct((b, 2, N_HEADS, 1, dh), F32),
                   jax.ShapeDtypeStruct((b, 2, N_HEADS, 1, 1), F32)],
        compiler_params=_cparams(("parallel",)),
        name="ctx_states",
    )(k, kt, v, cols, rows)


def _load_state(c0_ref, n0_ref, m0_ref, c_scr, n_scr, m_scr):
    @pl.when(pl.program_id(1) == 0)
    def _():
        c_scr[...] = c0_ref[0, 0]
        n_scr[...] = n0_ref[0, 0]
        m_scr[...] = m0_ref[0, 0]


def _scan_kernel(q_ref, k_ref, kt_ref, v_ref, cols_ref, rows_ref, cmrep_ref, c0_ref, n0_ref, m0_ref,
                 h_ref, c_scr, n_scr, m_scr, *, backward):
    _load_state(c0_ref, n0_ref, m0_ref, c_scr, n_scr, m_scr)
    dh = c_scr.shape[-1]
    refs = dict(q=q_ref, k=k_ref, kt=kt_ref, v=v_ref)
    chunks = range(rows_ref.shape[1])
    for j in (reversed(chunks) if backward else chunks):
        rs = slice(j * SCAN_CHUNK, (j + 1) * SCAN_CHUNK)

        def emit(head, h, rs=rs):
            h_ref[0, rs, head * dh:(head + 1) * dh] = h.astype(h_ref.dtype)

        _scan_chunk(_tile_getter(refs, dh, j), cols_ref[0, rs, :], rows_ref[0, j], cmrep_ref[0, 0, rs, :],
                    c_scr, n_scr, m_scr, backward, emit)


def _mixer_out_kernel(hf_ref, hb_ref, g1_ref, g2_ref, wout_ref, x_ref, mod0_ref, postg_ref,
                      mod1_ref, preg1_ref, win1_ref, x1_ref, u_ref, sz_ref):
    e = hf_ref.shape[2]
    dh = e // N_HEADS
    y = None
    for head in range(N_HEADS):
        hs = slice(head * dh, (head + 1) * dh)
        hsum = hf_ref[0, :, hs].astype(F32) + hb_ref[0, :, hs].astype(F32)
        cen = hsum - jnp.mean(hsum, axis=-1, keepdims=True)
        hn = cen * lax.rsqrt(jnp.mean(cen * cen, axis=-1, keepdims=True) + EPS)
        gated = (hn * g1_ref[0, :, hs].astype(F32) + g2_ref[0, :, hs].astype(F32)).astype(BF16)
        part = _dot(gated, wout_ref[hs, :])
        y = part if y is None else y + part
    yn = y * lax.rsqrt(jnp.mean(y * y, axis=-1, keepdims=True) + EPS) * postg_ref[...]
    x1 = x_ref[0] + mod0_ref[0, 2:3, :] * yn
    x1_ref[0] = x1
    _pool_in_proj(x1, mod1_ref, preg1_ref, win1_ref, u_ref, sz_ref)


def _scan_specs(b, t, e, chunks_per_step, reverse):
    dh = e // N_HEADS
    direction = 1 if reverse else 0
    l = SCAN_CHUNK * chunks_per_step
    ns = t // l
    cidx = (lambda c: ns - 1 - c) if reverse else (lambda c: c)
    tok = lambda w: pl.BlockSpec((1, l, w), lambda bi, c: (bi, cidx(c), 0))
    state = lambda shape: pl.BlockSpec((1, 1) + shape, lambda bi, c: (bi, direction) + (0,) * len(shape),
                                       pipeline_mode=pl.Buffered(1))
    specs = [tok(e), tok(e),
             pl.BlockSpec((1, chunks_per_step, e, SCAN_CHUNK), lambda bi, c: (bi, cidx(c), 0, 0)),
             tok(e), tok(LANES),
             pl.BlockSpec((1, chunks_per_step, 4 * N_HEADS, SCAN_CHUNK), lambda bi, c: (bi, cidx(c), 0, 0)),
             pl.BlockSpec((1, 1, l, N_HEADS * LANES), lambda bi, c: (bi, direction, cidx(c), 0)),
             state((N_HEADS, dh, dh)), state((N_HEADS, 1, dh)), state((N_HEADS, 1, 1))]
    scratch = [pltpu.VMEM((N_HEADS, dh, dh), F32), pltpu.VMEM((N_HEADS, 1, dh), F32),
               pltpu.VMEM((N_HEADS, 1, 1), F32)]
    return specs, scratch, tok, ns


def _scan(q, k, kt, v, cols, rows, cmrep, c0, n0, m0, backward):
    b, t, e = q.shape
    specs, scratch, tok, ns = _scan_specs(b, t, e, SCAN_CHUNKS_PER_STEP, backward)
    return pl.pallas_call(
        functools.partial(_scan_kernel, backward=backward),
        grid=(b, ns),
        in_specs=specs,
        out_specs=tok(e),
        out_shape=jax.ShapeDtypeStruct((b, t, e), BF16),
        scratch_shapes=scratch,
        compiler_params=_cparams(("parallel", "arbitrary")),
        name="mlstm_scan_bwd" if backward else "mlstm_scan_fwd",
    )(q, k, kt, v, cols, rows, cmrep, c0, n0, m0)


def _mixer_out(hf, hb, g1, g2, w_out_b, x, mod0, post_g, mod1, pre_g1, w_in1_b):
    b, t, e = hf.shape
    d = x.shape[-1]
    tm = POOL_TILE
    row = lambda bi, i: (bi, i, 0)
    tok = lambda w: pl.BlockSpec((1, tm, w), row)
    modspec = pl.BlockSpec((1, 3, d), lambda bi, i: (bi, 0, 0))
    return pl.pallas_call(
        _mixer_out_kernel,
        grid=(b, t // tm),
        in_specs=[tok(e), tok(e), tok(e), tok(e), _const_spec(w_out_b.shape), tok(d), modspec,
                  _const_spec((1, d)), modspec, _const_spec((1, d)), _const_spec(w_in1_b.shape)],
        out_specs=[tok(d), tok(e), tok(e)],
        out_shape=[jax.ShapeDtypeStruct((b, t, d), F32), jax.ShapeDtypeStruct((b, t, e), BF16),
                   jax.ShapeDtypeStruct((b, t, e), BF16)],
        compiler_params=_cparams(("parallel", "parallel")),
        name="mixer_out_pool_in",
    )(hf, hb, g1, g2, w_out_b, x, mod0, post_g.reshape(1, d), mod1, pre_g1.reshape(1, d), w_in1_b)


def _pool_in_proj(x, mod_ref, g_ref, win_ref, u_ref, sz_ref):
    e = u_ref.shape[2]
    hn = _norm_mod(x, g_ref[...], mod_ref[0, 0:1, :], mod_ref[0, 1:2, :]).astype(BF16)
    for c in range(e // MXU_DIM):
        cs = slice(c * MXU_DIM, (c + 1) * MXU_DIM)
        u_ref[0, :, cs] = _dot(hn, win_ref[:, cs]).astype(BF16)
        sz_ref[0, :, cs] = _silu(_dot(hn, win_ref[:, e + c * MXU_DIM:e + (c + 1) * MXU_DIM])).astype(BF16)


def _pool_out_kernel(up_ref, uc_ref, un_ref, sz_ref, band_ref, icnt_ref, wpool_ref, pscale_ref,
                     wout_ref, x_ref, mod_ref, postg_ref, o_ref, ext_scr):
    i = pl.program_id(1)
    nt = pl.num_programs(1)
    tm = uc_ref.shape[1]
    gd = wpool_ref.shape[1]
    y = None
    for g, w in enumerate(POOL_WINDOWS):
        gs = slice(g * gd, (g + 1) * gd)
        before = (w // 2) * GRID_W
        after = (w - w // 2 - 1) * GRID_W
        ext_scr[tm - before:tm, gs] = jnp.where(i > 0, up_ref[0, tm - before:, gs].astype(F32), 0.0)
        ext_scr[tm:2 * tm, gs] = uc_ref[0, :, gs].astype(F32)
        if after:
            ext_scr[2 * tm:2 * tm + after, gs] = jnp.where(i < nt - 1, un_ref[0, 0:after, gs].astype(F32), 0.0)
        n = tm + (w - 1) * GRID_W
        lo = tm - before
        rsum = ext_scr[lo:lo + n, gs]
        span = 1
        while span < w:
            n -= span * GRID_W
            rsum = rsum[0:n] + rsum[span * GRID_W:span * GRID_W + n]
            span *= 2
        rsum = rsum.astype(BF16)
        parts = [_dot(band_ref[g], rsum[s * MXU_DIM:(s + 1) * MXU_DIM, :]) for s in range(tm // MXU_DIM)]
        mean = jnp.concatenate(parts, axis=0) * icnt_ref[:, g:g + 1]
        diff = mean - ext_scr[tm:2 * tm, gs]
        yg = _dot(diff.astype(BF16), wpool_ref[g]) * pscale_ref[:, gs] * sz_ref[0, :, gs].astype(F32)
        part = _dot(yg.astype(BF16), wout_ref[gs, :])
        y = part if y is None else y + part
    yn = y * lax.rsqrt(jnp.mean(y * y, axis=-1, keepdims=True) + EPS) * postg_ref[...]
    o_ref[0] = x_ref[0] + mod_ref[0, 2:3, :] * yn


def _pool_constants(t):
    rows = t // GRID_W
    bands = np.zeros((len(POOL_WINDOWS), MXU_DIM, MXU_DIM), np.float32)
    icnt = np.zeros((t, LANES), np.float32)
    for g, w in enumerate(POOL_WINDOWS):
        lo_off, hi_off = -(w // 2), w - w // 2
        for p in range(MXU_DIM):
            c = p % GRID_W
            base = p - c
            lo, hi = max(c + lo_off, 0), min(c + hi_off, GRID_W)
            bands[g, p, base + lo:base + hi] = 1.0
        cc = np.arange(GRID_W)
        cnt_c = np.clip(cc + hi_off, 0, GRID_W) - np.clip(cc + lo_off, 0, GRID_W)
        rr = np.arange(rows)
        cnt_r = np.clip(rr + hi_off, 0, rows) - np.clip(rr + lo_off, 0, rows)
        icnt[:, g] = (1.0 / (cnt_r[:, None] * cnt_c[None, :])).reshape(-1)
    return bands, icnt


def _pool_out(u, sz, w_pool_b, pool_scale, w_out_b, x, mod, post_g):
    b, t, e = u.shape
    d = x.shape[-1]
    tm = POOL_TILE
    assert tm >= (max(POOL_WINDOWS) // 2) * GRID_W and tm % MXU_DIM == 0 and t % tm == 0
    nt = t // tm
    gd = e // len(POOL_WINDOWS)
    bands, icnt = _pool_constants(t)
    row = lambda bi, i: (bi, i, 0)
    return pl.pallas_call(
        _pool_out_kernel,
        grid=(b, nt),
        in_specs=[
            pl.BlockSpec((1, tm, e), lambda bi, i: (bi, jnp.maximum(i - 1, 0), 0)),
            pl.BlockSpec((1, tm, e), row),
            pl.BlockSpec((1, tm, e), lambda bi, i: (bi, jnp.minimum(i + 1, nt - 1), 0)),
            pl.BlockSpec((1, tm, e), row),
            _const_spec(bands.shape),
            pl.BlockSpec((tm, LANES), lambda bi, i: (i, 0)),
            _const_spec(w_pool_b.shape),
            _const_spec((1, e)),
            _const_spec(w_out_b.shape),
            pl.BlockSpec((1, tm, d), row),
            pl.BlockSpec((1, 3, d), lambda bi, i: (bi, 0, 0)),
            _const_spec((1, d)),
        ],
        out_specs=pl.BlockSpec((1, tm, d), row),
        out_shape=jax.ShapeDtypeStruct((b, t, d), F32),
        scratch_shapes=[pltpu.VMEM((3 * tm, e), F32)],
        compiler_params=_cparams(("parallel", "parallel")),
        name="pool_out",
    )(u, u, u, sz, jnp.asarray(bands, BF16), jnp.asarray(icnt), w_pool_b, pool_scale.reshape(1, e),
      w_out_b, x, mod, post_g.reshape(1, d))


def kernel(x, c, ctx, c_ctx, w_ada, b_ada, a_norm_pre, a_norm_post, a_w_in, a_conv_w, a_conv_b,
           a_w_q, a_w_k, a_w_v, a_w_gate_f, a_b_gate_f, a_w_gate_b, a_b_gate_b, a_head_norm, a_skip,
           a_w_out, b_norm_pre, b_norm_post, b_w_in, b_w_pool, b_pool_scale, b_w_out):
    bsz, t, d = x.shape
    t_ctx = ctx.shape[1]
    assert t % SCAN_CHUNK == 0 and t_ctx % SCAN_CHUNK == 0 and t % GRID_W == 0

    n_cond = -(-(bsz + 1) // 8) * 8
    cond = jnp.concatenate([c, c_ctx[None], jnp.zeros((n_cond - bsz - 1, d), F32)], axis=0)
    mods = _ada(cond, bsz + 1, w_ada, b_ada).reshape(w_ada.shape[0], n_cond, 3, d)
    mod_lat = [mods[i, :bsz] for i in range(w_ada.shape[0])]
    mod_ctx = jnp.broadcast_to(mods[0, bsz][None], (bsz, 3, d))

    w_in_b = a_w_in[0].astype(BF16)
    k_scale = float((a_w_in.shape[2] // 3 // N_HEADS) ** -0.5)
    bdkq, bdv, aqk, av = _fold_weights(a_w_q[0], a_w_k[0], a_w_v[0], a_w_gate_f[0], a_w_gate_b[0], k_scale)
    bg = jnp.pad(jnp.concatenate([a_b_gate_f[0], a_b_gate_b[0]]), (0, LANES - 4 * N_HEADS)).reshape(1, LANES)
    pre_args = (a_norm_pre[0], w_in_b, a_conv_w[0], a_conv_b[0], bdkq, bdv, aqk, av, bg,
                a_head_norm[0], a_skip[0])
    w_pool = b_w_pool[0]
    later = (a_w_out[0], b_w_in[0], w_pool.reshape(-1, w_pool.shape[-1]), b_w_out[0])
    (ql, kl, ktl, vl, g1, g2, cols_l, rows_l, cmrep_l, a_w_out_b, b_w_in_b, w_pool_b, b_w_out_b) = _mlstm_pre(
        x, mod_lat[0], *pre_args, states_only=False, later_weights=later)
    kc, ktc, vc, cols_c, rows_c, _ = _mlstm_pre(ctx, mod_ctx, *pre_args, states_only=True)
    c0, n0, m0 = _ctx_states(kc, ktc, vc, cols_c, rows_c)
    scan_in = (ql, kl, ktl, vl, cols_l, rows_l, cmrep_l, c0, n0, m0)
    hb = _scan(*scan_in, backward=True)
    hf = _scan(*scan_in, backward=False)

    x, u, sz = _mixer_out(hf, hb, g1, g2, a_w_out_b, x, mod_lat[0], a_norm_post[0],
                          mod_lat[1], b_norm_pre[0], b_w_in_b)
    x = _pool_out(u, sz, w_pool_b.reshape(w_pool.shape), b_pool_scale[0], b_w_out_b,
                  x, mod_lat[1], b_norm_post[0])
    return x
```

```python
import functools

import numpy as np
import jax
import jax.numpy as jnp
from jax import lax
from jax.experimental import pallas as pl
from jax.experimental.pallas import tpu as pltpu

EPS = 1e-6
N_HEADS = 4
QKV_BLOCK = 4
GRID_W = 64
POOL_WINDOWS = (2, 4, 8, 16)

LANES = 128
MXU_DIM = 256
HALO = 16
VMEM_LIMIT_BYTES = 60 * 1024 * 1024

SCAN_CHUNK = 256
PRE_TILE = 512
PROJ_AHEAD = 3
POOL_TILE = 512
SCAN_CHUNKS_PER_STEP = 2
BF16 = jnp.bfloat16
F32 = jnp.float32


def _dot(a, b):
    return jnp.dot(a, b, preferred_element_type=F32)


def _silu(v):
    return v * jax.nn.sigmoid(v)


def _cparams(sem):
    return pltpu.CompilerParams(dimension_semantics=sem, vmem_limit_bytes=VMEM_LIMIT_BYTES)


def _const_spec(shape):
    nd = len(shape)
    return pl.BlockSpec(shape, lambda *_: (0,) * nd, pipeline_mode=pl.Buffered(1))


def _lane_repeat(col, width):
    return jnp.concatenate([col] * (width // LANES), axis=1) if width > LANES else col


def _ada_kernel(cond_t_ref, w_ref, b_ref, o_ref, *, n_used):
    s = _silu(cond_t_ref[...])
    w = w_ref[0]
    for r in range(n_used):
        o_ref[0, r:r + 1, :] = jnp.sum(w * s[:, r:r + 1], axis=0, keepdims=True) + b_ref[0]
    o_ref[0, n_used:, :] = jnp.zeros((o_ref.shape[1] - n_used, o_ref.shape[2]), F32)


def _ada(cond, n_used, w_ada, b_ada):
    depth, d, n3 = w_ada.shape
    rows = cond.shape[0]
    tn = max(w for w in range(LANES, 1024 + 1, LANES) if n3 % w == 0)
    return pl.pallas_call(
        functools.partial(_ada_kernel, n_used=n_used),
        grid=(depth, n3 // tn),
        in_specs=[
            pl.BlockSpec((d, rows), lambda l, j: (0, 0)),
            pl.BlockSpec((1, d, tn), lambda l, j: (l, 0, j)),
            pl.BlockSpec((1, 1, tn), lambda l, j: (l, 0, j)),
        ],
        out_specs=pl.BlockSpec((1, rows, tn), lambda l, j: (l, 0, j)),
        out_shape=jax.ShapeDtypeStruct((depth, rows, n3), F32),
        compiler_params=_cparams(("parallel", "parallel")),
        name="ada_mod",
    )(cond.T, w_ada, b_ada.reshape(depth, 1, n3))


def _norm_mod(xv, g, shift, scale):
    y = xv * lax.rsqrt(jnp.mean(xv * xv, axis=-1, keepdims=True) + EPS) * g
    return y * (1.0 + scale) + shift


def _scan_along_lanes(x, fwd_rows, lane, combine, fill):
    l = x.shape[1]
    step = 1
    while step < l:
        from_left = jnp.where(lane >= step, pltpu.roll(x, step, axis=1), fill)
        from_right = jnp.where(lane < l - step, pltpu.roll(x, l - step, axis=1), fill)
        x = combine(x, jnp.where(fwd_rows, from_left, from_right))
        step *= 2
    return x


def _gate_vectors(pre_chunks):
    h = N_HEADS
    l = pre_chunks[0].shape[0]
    pt = jnp.concatenate([p.T[0:4 * h, :] for p in pre_chunks], axis=0)
    row = lax.broadcasted_iota(jnp.int32, pt.shape, 0) % (4 * h)
    lane = lax.broadcasted_iota(jnp.int32, pt.shape, 1)
    fwd_rows = row < 2 * h
    log_f = jnp.minimum(pt, 0.0) - jnp.log1p(jnp.exp(-jnp.abs(pt)))
    b = _scan_along_lanes(log_f, fwd_rows, lane, jnp.add, 0.0)
    a = pt - pltpu.roll(b, pt.shape[0] - h, axis=0)
    cmax = _scan_along_lanes(a, fwd_rows, lane, jnp.maximum, -jnp.inf)
    rows = jnp.where((row % (2 * h)) < h, a, b)
    out = []
    for ch in range(len(pre_chunks)):
        rs = slice(ch * 4 * h, (ch + 1) * 4 * h)
        stacked = jnp.concatenate([rows[rs], cmax[rs], jnp.zeros((LANES - 8 * h, l), F32)], axis=0)
        cols = stacked.T
        reps = []
        for direction in range(2):
            for head in range(h):
                j = 4 * h + direction * 2 * h + head
                reps.append(jnp.broadcast_to(cols[:, j:j + 1], (l, LANES)))
        out.append((cols, rows[rs], jnp.concatenate(reps, axis=1)))
    return out


def _fold_weights_kernel(wq_ref, wk_ref, wv_ref, wg_ref, win_ref, bdkq_ref, bdv_ref, aqk_ref, av_ref, win16_ref,
                         *, k_scale):
    hp = lax.Precision.HIGHEST
    dot = lambda a, b: jnp.dot(a, b, precision=hp, preferred_element_type=F32)
    r = lax.broadcasted_iota(jnp.int32, (MXU_DIM, MXU_DIM), 0)
    c = lax.broadcasted_iota(jnp.int32, (MXU_DIM, MXU_DIM), 1)
    same_block = (r // QKV_BLOCK) == (c // QKV_BLOCK)

    def block_diag(w_ref):
        rows = w_ref[...]
        tile = jnp.zeros((MXU_DIM, MXU_DIM), F32)
        for o in range(QKV_BLOCK):
            tile = jnp.where(same_block & (c % QKV_BLOCK == o), rows[:, o:o + 1], tile)
        return tile

    bdq = block_diag(wq_ref)
    bdk = block_diag(wk_ref) * k_scale
    bdv = block_diag(wv_ref)
    bdkq = lax.dot_general(bdk, bdq, (((1,), (1,)), ((), ())), precision=hp, preferred_element_type=F32)
    bdkq_ref[0] = bdkq.astype(BF16)
    bdv_ref[0] = bdv.astype(BF16)
    aqk_ref[0] = (dot(bdq, wg_ref[0, 0]) + dot(bdk, wg_ref[1, 0])).astype(BF16)
    av_ref[0] = dot(bdv, wg_ref[2, 0]).astype(BF16)
    win16_ref[...] = win_ref[...].astype(BF16)


def _fold_weights(w_q, w_k, w_v, w_gate_f, w_gate_b, w_in, k_scale):
    e = w_q.shape[0] * QKV_BLOCK
    nt = e // MXU_DIM
    d = w_in.shape[0]
    assert d % (HALO * nt) == 0
    rows = pl.BlockSpec((MXU_DIM, QKV_BLOCK), lambda i: (i, 0))
    tile = pl.BlockSpec((1, MXU_DIM, MXU_DIM), lambda i: (i, 0, 0))
    gate = pl.BlockSpec((1, MXU_DIM, LANES), lambda i: (i, 0, 0))
    win = pl.BlockSpec((d // nt, w_in.shape[1]), lambda i: (i, 0))
    wg = jnp.pad(jnp.concatenate([w_gate_f, w_gate_b], axis=1), ((0, 0), (0, LANES - 4 * N_HEADS)))
    args = [w.reshape(e, QKV_BLOCK) for w in (w_q, w_k, w_v)] + [wg.reshape(3, nt, MXU_DIM, LANES), w_in]
    return pl.pallas_call(
        functools.partial(_fold_weights_kernel, k_scale=k_scale),
        grid=(nt,),
        in_specs=[rows, rows, rows, pl.BlockSpec((3, 1, MXU_DIM, LANES), lambda i: (0, i, 0, 0)), win],
        out_specs=[tile, tile, gate, gate, win],
        out_shape=[jax.ShapeDtypeStruct((nt, MXU_DIM, MXU_DIM), BF16),
                   jax.ShapeDtypeStruct((nt, MXU_DIM, MXU_DIM), BF16),
                   jax.ShapeDtypeStruct((nt, MXU_DIM, LANES), BF16),
                   jax.ShapeDtypeStruct((nt, MXU_DIM, LANES), BF16),
                   jax.ShapeDtypeStruct(w_in.shape, BF16)],
        compiler_params=_cparams(("arbitrary",)),
        name="fold_weights",
    )(*args)


def _mlstm_pre_kernel(x_ref, xprev_ref, xnext_ref, mod_ref, g_ref, win_ref, convw_ref, convb_ref,
                      bdkq_ref, bdv_ref, aqk_ref, av_ref, bg_ref, hnorm_ref, skip_ref, *rest,
                      states_only, tiles_per_seq, n_later):
    later_f32, rest = rest[:n_later], rest[n_later:]
    later_bf16, (hn_scr, pre_scr) = rest[-2 - n_later:-2], rest[-2:]
    if states_only:
        k_ref, kt_ref, v_ref, cols_ref, rows_ref, cmrep_ref = rest[:-2 - n_later]
    else:
        q_ref, k_ref, kt_ref, v_ref, g1_ref, g2_ref, cols_ref, rows_ref, cmrep_ref = rest[:-2 - n_later]
    step = pl.program_id(0)
    n_tiles = pl.num_programs(0) - 1
    tm, e = k_ref.shape

    def gate_stage():
        nch = tm // SCAN_CHUNK
        chunks = [pre_scr[ch * SCAN_CHUNK:(ch + 1) * SCAN_CHUNK, :] for ch in range(nch)]
        for ch, (cols, rows, cmrep) in enumerate(_gate_vectors(chunks)):
            rs = slice(ch * SCAN_CHUNK, (ch + 1) * SCAN_CHUNK)
            cols_ref[rs, :] = cols
            rows_ref[ch] = rows
            half = N_HEADS * LANES
            cmrep_ref[0, 0, rs, :] = cmrep[:, 0:half]
            cmrep_ref[0, 1, rs, :] = cmrep[:, half:]

    @pl.when(step == 0)
    def _():
        pre_scr[...] = jnp.zeros(pre_scr.shape, F32)

    @pl.when(step == n_tiles)
    def _():
        gate_stage()

    @pl.when(step < n_tiles)
    def _():
        gate_stage()
        _mlstm_pre_tile(x_ref, xprev_ref, xnext_ref, mod_ref, g_ref, win_ref, convw_ref, convb_ref, bdkq_ref, bdv_ref,
                        aqk_ref, av_ref, bg_ref, hnorm_ref, skip_ref,
                        None if states_only else (q_ref, g1_ref, g2_ref), k_ref, kt_ref, v_ref,
                        hn_scr, pre_scr, step % tiles_per_seq, tiles_per_seq)
        for w32_ref, w16_ref in zip(later_f32, later_bf16):
            w16_ref[...] = w32_ref[...].astype(BF16)


def _mlstm_pre_tile(x_ref, xprev_ref, xnext_ref, mod_ref, g_ref, win_ref, convw_ref, convb_ref, bdkq_ref, bdv_ref,
                    aqk_ref, av_ref, bg_ref, hnorm_ref, skip_ref, gate_outs, k_ref, kt_ref, v_ref,
                    hn_scr, pre_scr, i, nt):
    states_only = gate_outs is None
    if not states_only:
        q_ref, g1_ref, g2_ref = gate_outs
    tm, e = k_ref.shape
    shift = mod_ref[0, 0:1, :]
    scale = mod_ref[0, 1:2, :]
    g = g_ref[...]

    hn_scr[0:HALO, :] = _norm_mod(xprev_ref[...], g, shift, scale).astype(BF16)
    hn_scr[HALO:HALO + tm, :] = _norm_mod(x_ref[...], g, shift, scale).astype(BF16)
    hn_scr[HALO + tm:, :] = _norm_mod(xnext_ref[...], g, shift, scale).astype(BF16)

    def project(c):
        cs = slice(c * MXU_DIM, (c + 1) * MXU_DIM)
        xm_ext = _dot(hn_scr[...], win_ref[:, cs])
        if states_only:
            return xm_ext, None, None
        hc = hn_scr[HALO:HALO + tm, :]
        return (xm_ext, _dot(hc, win_ref[:, e + c * MXU_DIM:e + (c + 1) * MXU_DIM]),
                _dot(hc, win_ref[:, 2 * e + c * MXU_DIM:2 * e + (c + 1) * MXU_DIM]))

    pre = jnp.zeros((tm, LANES), F32) + bg_ref[...]
    n_chunks = e // MXU_DIM
    queue = [project(c) for c in range(min(PROJ_AHEAD, n_chunks))]
    for c in range(n_chunks):
        cs = slice(c * MXU_DIM, (c + 1) * MXU_DIM)
        xm_ext, z, og = queue.pop(0)
        if c + PROJ_AHEAD < n_chunks:
            queue.append(project(c + PROJ_AHEAD))
        xm_all = jnp.concatenate([jnp.where(i > 0, xm_ext[0:HALO], 0.0), xm_ext[HALO:HALO + tm],
                                  jnp.where(i < nt - 1, xm_ext[HALO + tm:], 0.0)], axis=0)
        xm = xm_all[HALO:HALO + tm]
        xm_prev = pltpu.roll(xm_all, 1, axis=0)[HALO:HALO + tm]
        xm_next = pltpu.roll(xm_all, tm + 2 * HALO - 1, axis=0)[HALO:HALO + tm]
        conv = (xm_prev * convw_ref[0:1, cs] + xm * convw_ref[1:2, cs] + xm_next * convw_ref[2:3, cs]
                + convb_ref[:, cs])
        xc = _silu(conv)
        xc_b = xc.astype(BF16)
        xm_b = xm.astype(BF16)
        k32 = _dot(xc_b, bdkq_ref[c])
        pre = pre + _dot(xc_b, aqk_ref[c]) + _dot(xm_b, av_ref[c])
        k_ref[:, cs] = k32.astype(BF16)
        k_t = k32.T.astype(BF16)
        for ch in range(tm // SCAN_CHUNK):
            kt_ref[ch, cs, :] = k_t[:, ch * SCAN_CHUNK:(ch + 1) * SCAN_CHUNK]
        v_ref[:, cs] = _dot(xm_b, bdv_ref[c]).astype(BF16)
        if not states_only:
            sz = _silu(z)
            q_ref[:, cs] = xc_b
            g1_ref[:, cs] = (hnorm_ref[:, cs] * jax.nn.sigmoid(og) * sz).astype(BF16)
            g2_ref[:, cs] = (skip_ref[:, cs] * xc * sz).astype(BF16)

    pre_scr[...] = pre


def _mlstm_pre(x, mod, pre_g, w_in_b, conv_w, conv_b, bdkq, bdv, aqk, av, bg, head_norm, skip, states_only,
               later_weights=()):
    b, t, d = x.shape
    e = w_in_b.shape[1] // 3
    tm = min(PRE_TILE, t)
    nt = t // tm

    n = b * nt
    nch = tm // SCAN_CHUNK
    cur = lambda i: jnp.minimum(i, n - 1)
    prv = lambda i: jnp.maximum(i - 1, 0)
    big = pl.BlockSpec((tm, e), lambda i: (cur(i), 0))
    big_t = pl.BlockSpec((nch, e, SCAN_CHUNK), lambda i: (cur(i), 0, 0))
    tok = jax.ShapeDtypeStruct((b * t, e), BF16)
    tok_t = jax.ShapeDtypeStruct((b * t // SCAN_CHUNK, e, SCAN_CHUNK), BF16)
    gate_specs = [pl.BlockSpec((tm, LANES), lambda i: (prv(i), 0)),
                  pl.BlockSpec((nch, 4 * N_HEADS, SCAN_CHUNK), lambda i: (prv(i), 0, 0)),
                  pl.BlockSpec((1, 2, tm, N_HEADS * LANES), lambda i: (prv(i) // nt, 0, prv(i) % nt, 0))]
    gate_shapes = [jax.ShapeDtypeStruct((b * t, LANES), F32),
                   jax.ShapeDtypeStruct((b * t // SCAN_CHUNK, 4 * N_HEADS, SCAN_CHUNK), F32),
                   jax.ShapeDtypeStruct((b, 2, t, N_HEADS * LANES), F32)]
    if states_only:
        out_specs, out_shape = [big, big_t, big] + gate_specs, [tok, tok_t, tok] + gate_shapes
    else:
        out_specs = [big, big, big_t, big, big, big] + gate_specs
        out_shape = [tok, tok, tok_t, tok, tok, tok] + gate_shapes
    later_specs = [pl.BlockSpec((w.shape[0] // n, w.shape[1]), lambda i: (cur(i), 0)) for w in later_weights]
    assert all(w.shape[0] % (HALO * n) == 0 for w in later_weights)
    outs = pl.pallas_call(
        functools.partial(_mlstm_pre_kernel, states_only=states_only, tiles_per_seq=nt,
                          n_later=len(later_weights)),
        grid=(n + 1,),
        in_specs=[
            pl.BlockSpec((tm, d), lambda i: (cur(i), 0)),
            pl.BlockSpec((HALO, d), lambda i: (jnp.maximum(cur(i) * (tm // HALO) - 1, 0), 0)),
            pl.BlockSpec((HALO, d), lambda i: (jnp.minimum((cur(i) + 1) * (tm // HALO), b * t // HALO - 1), 0)),
            pl.BlockSpec((1, 3, d), lambda i: (cur(i) // nt, 0, 0)),
            _const_spec((1, d)),
            _const_spec(w_in_b.shape),
            _const_spec(conv_w.shape),
            _const_spec((1, e)),
            _const_spec(bdkq.shape), _const_spec(bdv.shape), _const_spec(aqk.shape), _const_spec(av.shape),
            _const_spec((1, LANES)),
            _const_spec((1, e)), _const_spec((1, e)),
        ] + later_specs,
        out_specs=out_specs + later_specs,
        out_shape=out_shape + [jax.ShapeDtypeStruct(w.shape, BF16) for w in later_weights],
        scratch_shapes=[pltpu.VMEM((tm + 2 * HALO, d), BF16), pltpu.VMEM((tm, LANES), F32)],
        compiler_params=_cparams(("arbitrary",)),
        name="mlstm_pre",
    )(x.reshape(b * t, d), x.reshape(b * t, d), x.reshape(b * t, d), mod, pre_g.reshape(1, d), w_in_b,
      conv_w, conv_b.reshape(1, e),
      bdkq, bdv, aqk, av, bg, head_norm.reshape(1, e), skip.reshape(1, e), *later_weights)
    by_token = lambda o: o.reshape(b, t, o.shape[-1])
    by_chunk = lambda o: o.reshape((b, t // SCAN_CHUNK) + o.shape[1:])
    outs, later_bf16 = outs[:len(out_specs)], outs[len(out_specs):]
    *main, cols, rows, cmrep = outs
    kt_pos = 1 if states_only else 2
    main = [by_chunk(o) if j == kt_pos else by_token(o) for j, o in enumerate(main)]
    return main + [by_token(cols), by_chunk(rows), cmrep] + list(later_bf16)


def _scan_chunk(get, cols, rows, cmrep, c_st, n_st, m_st, backward, emit):
    h = N_HEADS
    l = cols.shape[0]
    dh = c_st.shape[-1]
    off = 2 * h if backward else 0
    last = 0 if backward else l - 1
    st = []
    for head in range(h):
        cm_col = cols[:, 4 * h + off + head:4 * h + off + head + 1]
        st.append(dict(
            a_col=cols[:, off + head:off + head + 1],
            b_col=cols[:, off + h + head:off + h + head + 1],
            a_row=rows[off + head:off + head + 1, :],
            total=cols[last:last + 1, off + h + head:off + h + head + 1],
            amax=cm_col[last:last + 1, :],
            m_prev=m_st[head]))

    if emit is not None:
        ti = lax.broadcasted_iota(jnp.int32, (l, l), 0)
        si = lax.broadcasted_iota(jnp.int32, (l, l), 1)
        mask = (si >= ti) if backward else (ti >= si)
        for head in range(h):
            s = st[head]
            q = get("q", head)
            s["u"] = jnp.maximum(s["m_prev"], cmrep[:, head * LANES:(head + 1) * LANES])
            s["p"] = jnp.exp(jnp.where(mask, s["a_row"] - _lane_repeat(s["u"], l), -jnp.inf))
            s["qk"] = _dot(q, get("kt", head))
            s["qc"] = _dot(q, c_st[head].astype(BF16))
            sc = s["qk"] * s["p"]
            w_inter = jnp.exp(s["m_prev"] - s["u"])
            num = _dot(sc.astype(BF16), get("v", head)) + _lane_repeat(w_inter, dh) * s["qc"]
            qn = jnp.sum(get("q", head).astype(F32) * n_st[head], axis=-1, keepdims=True)
            den = jnp.sum(sc, axis=-1, keepdims=True) + w_inter[:, 0:1] * qn
            floor = jnp.exp(-(s["b_col"] + s["u"][:, 0:1]))
            emit(head, num / jnp.maximum(jnp.abs(den), floor))

    for head in range(h):
        s = st[head]
        s["u_last"] = jnp.maximum(s["m_prev"], s["amax"])
        s["decay"] = jnp.exp(s["m_prev"] - s["u_last"])
        wkt = (get("kt", head).astype(F32) * jnp.exp(s["a_row"] - s["u_last"])).astype(BF16)
        s["upd"] = _dot(wkt, get("v", head))
    for head in range(h):
        s = st[head]
        wk = get("k", head).astype(F32) * jnp.exp(s["a_col"] - s["u_last"])
        c_st[head] = s["decay"] * c_st[head] + s["upd"]
        n_st[head] = s["decay"] * n_st[head] + jnp.sum(wk, axis=0, keepdims=True)
        m_st[head] = s["total"] + s["u_last"]


def _tile_getter(refs, dh, chunk):
    rows = slice(chunk * SCAN_CHUNK, (chunk + 1) * SCAN_CHUNK)

    def get(name, head):
        hs = slice(head * dh, (head + 1) * dh)
        if name == "kt":
            return refs["kt"][0, chunk, hs, :]
        return refs[name][0, rows, hs]
    return get


def _ctx_state_kernel(k_ref, kt_ref, v_ref, cols_ref, rows_ref, c_out, n_out, m_out):
    dh = c_out.shape[-1]
    nc = k_ref.shape[1] // SCAN_CHUNK
    c_out[...] = jnp.zeros(c_out.shape, F32)
    n_out[...] = jnp.zeros(n_out.shape, F32)
    m_out[...] = jnp.zeros(m_out.shape, F32)
    refs = dict(k=k_ref, kt=kt_ref, v=v_ref)
    for direction in range(2):
        for step in range(nc):
            c = nc - 1 - step if direction else step
            rs = slice(c * SCAN_CHUNK, (c + 1) * SCAN_CHUNK)
            _scan_chunk(_tile_getter(refs, dh, c), cols_ref[0, rs, :], rows_ref[0, c], None,
                        c_out.at[0, direction], n_out.at[0, direction], m_out.at[0, direction],
                        bool(direction), None)


def _ctx_states(k, kt, v, cols, rows):
    b, t, e = k.shape
    dh = e // N_HEADS
    nc = t // SCAN_CHUNK
    per_b = lambda nd: (lambda bi: (bi,) + (0,) * (nd - 1))
    return pl.pallas_call(
        _ctx_state_kernel,
        grid=(b,),
        in_specs=[pl.BlockSpec((1, t, e), per_b(3)), pl.BlockSpec((1, nc, e, SCAN_CHUNK), per_b(4)),
                  pl.BlockSpec((1, t, e), per_b(3)),
                  pl.BlockSpec((1, t, LANES), per_b(3)),
                  pl.BlockSpec((1, nc, 4 * N_HEADS, SCAN_CHUNK), per_b(4))],
        out_specs=[pl.BlockSpec((1, 2, N_HEADS, dh, dh), per_b(5)),
                   pl.BlockSpec((1, 2, N_HEADS, 1, dh), per_b(5)),
                   pl.BlockSpec((1, 2, N_HEADS, 1, 1), per_b(5))],
        out_shape=[jax.ShapeDtypeStruct((b, 2, N_HEADS, dh, dh), F32),
                   jax.ShapeDtypeStruct((b, 2, N_HEADS, 1, dh), F32),
                   jax.ShapeDtypeStruct((b, 2, N_HEADS, 1, 1), F32)],
        compiler_params=_cparams(("parallel",)),
        name="ctx_states",
    )(k, kt, v, cols, rows)


def _load_state(c0_ref, n0_ref, m0_ref, c_scr, n_scr, m_scr):
    @pl.when(pl.program_id(1) == 0)
    def _():
        c_scr[...] = c0_ref[0, 0]
        n_scr[...] = n0_ref[0, 0]
        m_scr[...] = m0_ref[0, 0]


def _scan_kernel(q_ref, k_ref, kt_ref, v_ref, cols_ref, rows_ref, cmrep_ref, c0_ref, n0_ref, m0_ref,
                 h_ref, c_scr, n_scr, m_scr, *, backward):
    _load_state(c0_ref, n0_ref, m0_ref, c_scr, n_scr, m_scr)
    dh = c_scr.shape[-1]
    refs = dict(q=q_ref, k=k_ref, kt=kt_ref, v=v_ref)
    chunks = range(rows_ref.shape[1])
    for j in (reversed(chunks) if backward else chunks):
        rs = slice(j * SCAN_CHUNK, (j + 1) * SCAN_CHUNK)

        def emit(head, h, rs=rs):
            h_ref[0, rs, head * dh:(head + 1) * dh] = h.astype(h_ref.dtype)

        _scan_chunk(_tile_getter(refs, dh, j), cols_ref[0, rs, :], rows_ref[0, j], cmrep_ref[0, 0, rs, :],
                    c_scr, n_scr, m_scr, backward, emit)


def _mixer_out_kernel(hf_ref, hb_ref, g1_ref, g2_ref, wout_ref, x_ref, mod0_ref, postg_ref,
                      mod1_ref, preg1_ref, win1_ref, x1_ref, u_ref, sz_ref):
    e = hf_ref.shape[2]
    dh = e // N_HEADS
    y = None
    for head in range(N_HEADS):
        hs = slice(head * dh, (head + 1) * dh)
        hsum = hf_ref[0, :, hs].astype(F32) + hb_ref[0, :, hs].astype(F32)
        cen = hsum - jnp.mean(hsum, axis=-1, keepdims=True)
        hn = cen * lax.rsqrt(jnp.mean(cen * cen, axis=-1, keepdims=True) + EPS)
        gated = (hn * g1_ref[0, :, hs].astype(F32) + g2_ref[0, :, hs].astype(F32)).astype(BF16)
        part = _dot(gated, wout_ref[hs, :])
        y = part if y is None else y + part
    yn = y * lax.rsqrt(jnp.mean(y * y, axis=-1, keepdims=True) + EPS) * postg_ref[...]
    x1 = x_ref[0] + mod0_ref[0, 2:3, :] * yn
    x1_ref[0] = x1
    _pool_in_proj(x1, mod1_ref, preg1_ref, win1_ref, u_ref, sz_ref)


def _scan_specs(b, t, e, chunks_per_step, reverse):
    dh = e // N_HEADS
    direction = 1 if reverse else 0
    l = SCAN_CHUNK * chunks_per_step
    ns = t // l
    cidx = (lambda c: ns - 1 - c) if reverse else (lambda c: c)
    tok = lambda w: pl.BlockSpec((1, l, w), lambda bi, c: (bi, cidx(c), 0))
    state = lambda shape: pl.BlockSpec((1, 1) + shape, lambda bi, c: (bi, direction) + (0,) * len(shape),
                                       pipeline_mode=pl.Buffered(1))
    specs = [tok(e), tok(e),
             pl.BlockSpec((1, chunks_per_step, e, SCAN_CHUNK), lambda bi, c: (bi, cidx(c), 0, 0)),
             tok(e), tok(LANES),
             pl.BlockSpec((1, chunks_per_step, 4 * N_HEADS, SCAN_CHUNK), lambda bi, c: (bi, cidx(c), 0, 0)),
             pl.BlockSpec((1, 1, l, N_HEADS * LANES), lambda bi, c: (bi, direction, cidx(c), 0)),
             state((N_HEADS, dh, dh)), state((N_HEADS, 1, dh)), state((N_HEADS, 1, 1))]
    scratch = [pltpu.VMEM((N_HEADS, dh, dh), F32), pltpu.VMEM((N_HEADS, 1, dh), F32),
               pltpu.VMEM((N_HEADS, 1, 1), F32)]
    return specs, scratch, tok, ns


def _scan(q, k, kt, v, cols, rows, cmrep, c0, n0, m0, backward):
    b, t, e = q.shape
    specs, scratch, tok, ns = _scan_specs(b, t, e, SCAN_CHUNKS_PER_STEP, backward)
    return pl.pallas_call(
        functools.partial(_scan_kernel, backward=backward),
        grid=(b, ns),
        in_specs=specs,
        out_specs=tok(e),
        out_shape=jax.ShapeDtypeStruct((b, t, e), BF16),
        scratch_shapes=scratch,
        compiler_params=_cparams(("parallel", "arbitrary")),
        name="mlstm_scan_bwd" if backward else "mlstm_scan_fwd",
    )(q, k, kt, v, cols, rows, cmrep, c0, n0, m0)


def _mixer_out(hf, hb, g1, g2, w_out_b, x, mod0, post_g, mod1, pre_g1, w_in1_b):
    b, t, e = hf.shape
    d = x.shape[-1]
    tm = POOL_TILE
    row = lambda bi, i: (bi, i, 0)
    tok = lambda w: pl.BlockSpec((1, tm, w), row)
    modspec = pl.BlockSpec((1, 3, d), lambda bi, i: (bi, 0, 0))
    return pl.pallas_call(
        _mixer_out_kernel,
        grid=(b, t // tm),
        in_specs=[tok(e), tok(e), tok(e), tok(e), _const_spec(w_out_b.shape), tok(d), modspec,
                  _const_spec((1, d)), modspec, _const_spec((1, d)), _const_spec(w_in1_b.shape)],
        out_specs=[tok(d), tok(e), tok(e)],
        out_shape=[jax.ShapeDtypeStruct((b, t, d), F32), jax.ShapeDtypeStruct((b, t, e), BF16),
                   jax.ShapeDtypeStruct((b, t, e), BF16)],
        compiler_params=_cparams(("parallel", "parallel")),
        name="mixer_out_pool_in",
    )(hf, hb, g1, g2, w_out_b, x, mod0, post_g.reshape(1, d), mod1, pre_g1.reshape(1, d), w_in1_b)


def _pool_in_proj(x, mod_ref, g_ref, win_ref, u_ref, sz_ref):
    e = u_ref.shape[2]
    hn = _norm_mod(x, g_ref[...], mod_ref[0, 0:1, :], mod_ref[0, 1:2, :]).astype(BF16)
    for c in range(e // MXU_DIM):
        cs = slice(c * MXU_DIM, (c + 1) * MXU_DIM)
        u_ref[0, :, cs] = _dot(hn, win_ref[:, cs]).astype(BF16)
        sz_ref[0, :, cs] = _silu(_dot(hn, win_ref[:, e + c * MXU_DIM:e + (c + 1) * MXU_DIM])).astype(BF16)


def _pool_out_kernel(up_ref, uc_ref, un_ref, sz_ref, band_ref, icnt_ref, wpool_ref, pscale_ref,
                     wout_ref, x_ref, mod_ref, postg_ref, o_ref, ext_scr):
    i = pl.program_id(1)
    nt = pl.num_programs(1)
    tm = uc_ref.shape[1]
    gd = wpool_ref.shape[1]
    y = None
    for g, w in enumerate(POOL_WINDOWS):
        gs = slice(g * gd, (g + 1) * gd)
        before = (w // 2) * GRID_W
        after = (w - w // 2 - 1) * GRID_W
        ext_scr[tm - before:tm, gs] = jnp.where(i > 0, up_ref[0, tm - before:, gs].astype(F32), 0.0)
        ext_scr[tm:2 * tm, gs] = uc_ref[0, :, gs].astype(F32)
        if after:
            ext_scr[2 * tm:2 * tm + after, gs] = jnp.where(i < nt - 1, un_ref[0, 0:after, gs].astype(F32), 0.0)
        n = tm + (w - 1) * GRID_W
        lo = tm - before
        rsum = ext_scr[lo:lo + n, gs]
        span = 1
        while span < w:
            n -= span * GRID_W
            rsum = rsum[0:n] + rsum[span * GRID_W:span * GRID_W + n]
            span *= 2
        rsum = rsum.astype(BF16)
        parts = [_dot(band_ref[g], rsum[s * MXU_DIM:(s + 1) * MXU_DIM, :]) for s in range(tm // MXU_DIM)]
        mean = jnp.concatenate(parts, axis=0) * icnt_ref[:, g:g + 1]
        diff = mean - ext_scr[tm:2 * tm, gs]
        yg = _dot(diff.astype(BF16), wpool_ref[g]) * pscale_ref[:, gs] * sz_ref[0, :, gs].astype(F32)
        part = _dot(yg.astype(BF16), wout_ref[gs, :])
        y = part if y is None else y + part
    yn = y * lax.rsqrt(jnp.mean(y * y, axis=-1, keepdims=True) + EPS) * postg_ref[...]
    o_ref[0] = x_ref[0] + mod_ref[0, 2:3, :] * yn


def _pool_constants(t):
    rows = t // GRID_W
    bands = np.zeros((len(POOL_WINDOWS), MXU_DIM, MXU_DIM), np.float32)
    icnt = np.zeros((t, LANES), np.float32)
    for g, w in enumerate(POOL_WINDOWS):
        lo_off, hi_off = -(w // 2), w - w // 2
        for p in range(MXU_DIM):
            c = p % GRID_W
            base = p - c
            lo, hi = max(c + lo_off, 0), min(c + hi_off, GRID_W)
            bands[g, p, base + lo:base + hi] = 1.0
        cc = np.arange(GRID_W)
        cnt_c = np.clip(cc + hi_off, 0, GRID_W) - np.clip(cc + lo_off, 0, GRID_W)
        rr = np.arange(rows)
        cnt_r = np.clip(rr + hi_off, 0, rows) - np.clip(rr + lo_off, 0, rows)
        icnt[:, g] = (1.0 / (cnt_r[:, None] * cnt_c[None, :])).reshape(-1)
    return bands, icnt


def _pool_out(u, sz, w_pool_b, pool_scale, w_out_b, x, mod, post_g):
    b, t, e = u.shape
    d = x.shape[-1]
    tm = POOL_TILE
    assert tm >= (max(POOL_WINDOWS) // 2) * GRID_W and tm % MXU_DIM == 0 and t % tm == 0
    nt = t // tm
    gd = e // len(POOL_WINDOWS)
    bands, icnt = _pool_constants(t)
    row = lambda bi, i: (bi, i, 0)
    return pl.pallas_call(
        _pool_out_kernel,
        grid=(b, nt),
        in_specs=[
            pl.BlockSpec((1, tm, e), lambda bi, i: (bi, jnp.maximum(i - 1, 0), 0)),
            pl.BlockSpec((1, tm, e), row),
            pl.BlockSpec((1, tm, e), lambda bi, i: (bi, jnp.minimum(i + 1, nt - 1), 0)),
            pl.BlockSpec((1, tm, e), row),
            _const_spec(bands.shape),
            pl.BlockSpec((tm, LANES), lambda bi, i: (i, 0)),
            _const_spec(w_pool_b.shape),
            _const_spec((1, e)),
            _const_spec(w_out_b.shape),
            pl.BlockSpec((1, tm, d), row),
            pl.BlockSpec((1, 3, d), lambda bi, i: (bi, 0, 0)),
            _const_spec((1, d)),
        ],
        out_specs=pl.BlockSpec((1, tm, d), row),
        out_shape=jax.ShapeDtypeStruct((b, t, d), F32),
        scratch_shapes=[pltpu.VMEM((3 * tm, e), F32)],
        compiler_params=_cparams(("parallel", "parallel")),
        name="pool_out",
    )(u, u, u, sz, jnp.asarray(bands, BF16), jnp.asarray(icnt), w_pool_b, pool_scale.reshape(1, e),
      w_out_b, x, mod, post_g.reshape(1, d))


def kernel(x, c, ctx, c_ctx, w_ada, b_ada, a_norm_pre, a_norm_post, a_w_in, a_conv_w, a_conv_b,
           a_w_q, a_w_k, a_w_v, a_w_gate_f, a_b_gate_f, a_w_gate_b, a_b_gate_b, a_head_norm, a_skip,
           a_w_out, b_norm_pre, b_norm_post, b_w_in, b_w_pool, b_pool_scale, b_w_out):
    bsz, t, d = x.shape
    t_ctx = ctx.shape[1]
    assert t % SCAN_CHUNK == 0 and t_ctx % SCAN_CHUNK == 0 and t % GRID_W == 0

    n_cond = -(-(bsz + 1) // 8) * 8
    cond = jnp.concatenate([c, c_ctx[None], jnp.zeros((n_cond - bsz - 1, d), F32)], axis=0)
    mods = _ada(cond, bsz + 1, w_ada, b_ada).reshape(w_ada.shape[0], n_cond, 3, d)
    mod_lat = [mods[i, :bsz] for i in range(w_ada.shape[0])]
    mod_ctx = jnp.broadcast_to(mods[0, bsz][None], (bsz, 3, d))

    k_scale = float((a_w_in.shape[2] // 3 // N_HEADS) ** -0.5)
    bdkq, bdv, aqk, av, w_in_b = _fold_weights(a_w_q[0], a_w_k[0], a_w_v[0], a_w_gate_f[0], a_w_gate_b[0],
                                               a_w_in[0], k_scale)
    bg = jnp.pad(jnp.concatenate([a_b_gate_f[0], a_b_gate_b[0]]), (0, LANES - 4 * N_HEADS)).reshape(1, LANES)
    pre_args = (a_norm_pre[0], w_in_b, a_conv_w[0], a_conv_b[0], bdkq, bdv, aqk, av, bg,
                a_head_norm[0], a_skip[0])
    w_pool = b_w_pool[0]
    later = (a_w_out[0], b_w_in[0], w_pool.reshape(-1, w_pool.shape[-1]), b_w_out[0])
    (ql, kl, ktl, vl, g1, g2, cols_l, rows_l, cmrep_l, a_w_out_b, b_w_in_b, w_pool_b, b_w_out_b) = _mlstm_pre(
        x, mod_lat[0], *pre_args, states_only=False, later_weights=later)
    kc, ktc, vc, cols_c, rows_c, _ = _mlstm_pre(ctx, mod_ctx, *pre_args, states_only=True)
    c0, n0, m0 = _ctx_states(kc, ktc, vc, cols_c, rows_c)
    scan_in = (ql, kl, ktl, vl, cols_l, rows_l, cmrep_l, c0, n0, m0)
    hb = _scan(*scan_in, backward=True)
    hf = _scan(*scan_in, backward=False)

    x, u, sz = _mixer_out(hf, hb, g1, g2, a_w_out_b, x, mod_lat[0], a_norm_post[0],
                          mod_lat[1], b_norm_pre[0], b_w_in_b)
    x = _pool_out(u, sz, w_pool_b.reshape(w_pool.shape), b_pool_scale[0], b_w_out_b,
                  x, mod_lat[1], b_norm_post[0])
    return x
```

```python
import functools

import numpy as np
import jax
import jax.numpy as jnp
from jax import lax
from jax.experimental import pallas as pl
from jax.experimental.pallas import tpu as pltpu

EPS = 1e-6
N_HEADS = 4
QKV_BLOCK = 4
GRID_W = 64
POOL_WINDOWS = (2, 4, 8, 16)

LANES = 128
MXU_DIM = 256
HALO = 16
VMEM_LIMIT_BYTES = 60 * 1024 * 1024

SCAN_CHUNK = 256
PRE_TILE = 512
PROJ_AHEAD = 3
POOL_TILE = 512
SCAN_CHUNKS_PER_STEP = 2
BF16 = jnp.bfloat16
F32 = jnp.float32


def _dot(a, b):
    return jnp.dot(a, b, preferred_element_type=F32)


def _silu(v):
    return v * jax.nn.sigmoid(v)


def _cparams(sem):
    return pltpu.CompilerParams(dimension_semantics=sem, vmem_limit_bytes=VMEM_LIMIT_BYTES)


def _const_spec(shape):
    nd = len(shape)
    return pl.BlockSpec(shape, lambda *_: (0,) * nd, pipeline_mode=pl.Buffered(1))


def _lane_repeat(col, width):
    return jnp.concatenate([col] * (width // LANES), axis=1) if width > LANES else col


def _ada_kernel(cond_t_ref, w_ref, b_ref, o_ref, *, n_used):
    s = _silu(cond_t_ref[...])
    w = w_ref[0]
    for r in range(n_used):
        o_ref[0, r:r + 1, :] = jnp.sum(w * s[:, r:r + 1], axis=0, keepdims=True) + b_ref[0]
    o_ref[0, n_used:, :] = jnp.zeros((o_ref.shape[1] - n_used, o_ref.shape[2]), F32)


def _ada(cond, n_used, w_ada, b_ada):
    depth, d, n3 = w_ada.shape
    rows = cond.shape[0]
    tn = max(w for w in range(LANES, 1024 + 1, LANES) if n3 % w == 0)
    return pl.pallas_call(
        functools.partial(_ada_kernel, n_used=n_used),
        grid=(depth, n3 // tn),
        in_specs=[
            pl.BlockSpec((d, rows), lambda l, j: (0, 0)),
            pl.BlockSpec((1, d, tn), lambda l, j: (l, 0, j)),
            pl.BlockSpec((1, 1, tn), lambda l, j: (l, 0, j)),
        ],
        out_specs=pl.BlockSpec((1, rows, tn), lambda l, j: (l, 0, j)),
        out_shape=jax.ShapeDtypeStruct((depth, rows, n3), F32),
        compiler_params=_cparams(("parallel", "parallel")),
        name="ada_mod",
    )(cond.T, w_ada, b_ada.reshape(depth, 1, n3))


def _norm_mod(xv, g, shift, scale):
    y = xv * lax.rsqrt(jnp.mean(xv * xv, axis=-1, keepdims=True) + EPS) * g
    return y * (1.0 + scale) + shift


def _scan_along_lanes(x, fwd_rows, lane, combine, fill):
    l = x.shape[1]
    step = 1
    while step < l:
        from_left = jnp.where(lane >= step, pltpu.roll(x, step, axis=1), fill)
        from_right = jnp.where(lane < l - step, pltpu.roll(x, l - step, axis=1), fill)
        x = combine(x, jnp.where(fwd_rows, from_left, from_right))
        step *= 2
    return x


def _gate_vectors(pre_chunks):
    h = N_HEADS
    l = pre_chunks[0].shape[0]
    pt = jnp.concatenate([p.T[0:4 * h, :] for p in pre_chunks], axis=0)
    row = lax.broadcasted_iota(jnp.int32, pt.shape, 0) % (4 * h)
    lane = lax.broadcasted_iota(jnp.int32, pt.shape, 1)
    fwd_rows = row < 2 * h
    log_f = jnp.minimum(pt, 0.0) - jnp.log1p(jnp.exp(-jnp.abs(pt)))
    b = _scan_along_lanes(log_f, fwd_rows, lane, jnp.add, 0.0)
    a = pt - pltpu.roll(b, pt.shape[0] - h, axis=0)
    cmax = _scan_along_lanes(a, fwd_rows, lane, jnp.maximum, -jnp.inf)
    rows = jnp.where((row % (2 * h)) < h, a, b)
    out = []
    for ch in range(len(pre_chunks)):
        rs = slice(ch * 4 * h, (ch + 1) * 4 * h)
        stacked = jnp.concatenate([rows[rs], cmax[rs], jnp.zeros((LANES - 8 * h, l), F32)], axis=0)
        cols = stacked.T
        reps = []
        for direction in range(2):
            for head in range(h):
                j = 4 * h + direction * 2 * h + head
                reps.append(jnp.broadcast_to(cols[:, j:j + 1], (l, LANES)))
        out.append((cols, rows[rs], jnp.concatenate(reps, axis=1)))
    return out


def _fold_weights_kernel(wqkv_ref, wg_ref, win_ref, bdkq_ref, bdv_ref, aqk_ref, av_ref, win16_ref, *, k_scale):
    hp = lax.Precision.HIGHEST
    dot = lambda a, b: jnp.dot(a, b, precision=hp, preferred_element_type=F32)
    r = lax.broadcasted_iota(jnp.int32, (MXU_DIM, MXU_DIM), 0)
    c = lax.broadcasted_iota(jnp.int32, (MXU_DIM, MXU_DIM), 1)
    same_block = (r // QKV_BLOCK) == (c // QKV_BLOCK)

    def block_diag(which):
        rows = wqkv_ref[which]
        tile = jnp.zeros((MXU_DIM, MXU_DIM), F32)
        for o in range(QKV_BLOCK):
            tile = jnp.where(same_block & (c % QKV_BLOCK == o), rows[:, o:o + 1], tile)
        return tile

    bdq = block_diag(0)
    bdk = block_diag(1) * k_scale
    bdv = block_diag(2)
    bdkq = lax.dot_general(bdk, bdq, (((1,), (1,)), ((), ())), precision=hp, preferred_element_type=F32)
    bdkq_ref[0] = bdkq.astype(BF16)
    bdv_ref[0] = bdv.astype(BF16)
    aqk_ref[0] = (dot(bdq, wg_ref[0, 0]) + dot(bdk, wg_ref[1, 0])).astype(BF16)
    av_ref[0] = dot(bdv, wg_ref[2, 0]).astype(BF16)
    win16_ref[...] = win_ref[...].astype(BF16)


def _fold_weights(w_q, w_k, w_v, w_gate_f, w_gate_b, w_in, k_scale):
    e = w_q.shape[0] * QKV_BLOCK
    nt = e // MXU_DIM
    d = w_in.shape[0]
    assert d % (HALO * nt) == 0
    rows = pl.BlockSpec((3, MXU_DIM, QKV_BLOCK), lambda i: (0, i, 0))
    tile = pl.BlockSpec((1, MXU_DIM, MXU_DIM), lambda i: (i, 0, 0))
    gate = pl.BlockSpec((1, MXU_DIM, LANES), lambda i: (i, 0, 0))
    win = pl.BlockSpec((d // nt, w_in.shape[1]), lambda i: (i, 0))
    wg = jnp.pad(jnp.concatenate([w_gate_f, w_gate_b], axis=1), ((0, 0), (0, LANES - 4 * N_HEADS)))
    args = [jnp.stack([w_q, w_k, w_v]).reshape(3, e, QKV_BLOCK), wg.reshape(3, nt, MXU_DIM, LANES), w_in]
    return pl.pallas_call(
        functools.partial(_fold_weights_kernel, k_scale=k_scale),
        grid=(nt,),
        in_specs=[rows, pl.BlockSpec((3, 1, MXU_DIM, LANES), lambda i: (0, i, 0, 0)), win],
        out_specs=[tile, tile, gate, gate, win],
        out_shape=[jax.ShapeDtypeStruct((nt, MXU_DIM, MXU_DIM), BF16),
                   jax.ShapeDtypeStruct((nt, MXU_DIM, MXU_DIM), BF16),
                   jax.ShapeDtypeStruct((nt, MXU_DIM, LANES), BF16),
                   jax.ShapeDtypeStruct((nt, MXU_DIM, LANES), BF16),
                   jax.ShapeDtypeStruct(w_in.shape, BF16)],
        compiler_params=_cparams(("arbitrary",)),
        name="fold_weights",
    )(*args)


def _mlstm_pre_kernel(x_ref, xprev_ref, xnext_ref, mod_ref, g_ref, win_ref, convw_ref, convb_ref,
                      bdkq_ref, bdv_ref, aqk_ref, av_ref, bg_ref, hnorm_ref, skip_ref, *rest,
                      states_only, tiles_per_seq, n_later):
    later_f32, rest = rest[:n_later], rest[n_later:]
    later_bf16, (hn_scr, pre_scr) = rest[-2 - n_later:-2], rest[-2:]
    if states_only:
        k_ref, kt_ref, v_ref, cols_ref, rows_ref, cmrep_ref = rest[:-2 - n_later]
    else:
        q_ref, k_ref, kt_ref, v_ref, g1_ref, g2_ref, cols_ref, rows_ref, cmrep_ref = rest[:-2 - n_later]
    step = pl.program_id(0)
    n_tiles = pl.num_programs(0) - 1
    tm, e = k_ref.shape

    def gate_stage():
        nch = tm // SCAN_CHUNK
        chunks = [pre_scr[ch * SCAN_CHUNK:(ch + 1) * SCAN_CHUNK, :] for ch in range(nch)]
        for ch, (cols, rows, cmrep) in enumerate(_gate_vectors(chunks)):
            rs = slice(ch * SCAN_CHUNK, (ch + 1) * SCAN_CHUNK)
            cols_ref[rs, :] = cols
            rows_ref[ch] = rows
            half = N_HEADS * LANES
            cmrep_ref[0, 0, rs, :] = cmrep[:, 0:half]
            cmrep_ref[0, 1, rs, :] = cmrep[:, half:]

    @pl.when(step == 0)
    def _():
        pre_scr[...] = jnp.zeros(pre_scr.shape, F32)

    @pl.when(step == n_tiles)
    def _():
        gate_stage()

    @pl.when(step < n_tiles)
    def _():
        gate_stage()
        _mlstm_pre_tile(x_ref, xprev_ref, xnext_ref, mod_ref, g_ref, win_ref, convw_ref, convb_ref, bdkq_ref, bdv_ref,
                        aqk_ref, av_ref, bg_ref, hnorm_ref, skip_ref,
                        None if states_only else (q_ref, g1_ref, g2_ref), k_ref, kt_ref, v_ref,
                        hn_scr, pre_scr, step % tiles_per_seq, tiles_per_seq)
        for w32_ref, w16_ref in zip(later_f32, later_bf16):
            w16_ref[...] = w32_ref[...].astype(BF16)


def _mlstm_pre_tile(x_ref, xprev_ref, xnext_ref, mod_ref, g_ref, win_ref, convw_ref, convb_ref, bdkq_ref, bdv_ref,
                    aqk_ref, av_ref, bg_ref, hnorm_ref, skip_ref, gate_outs, k_ref, kt_ref, v_ref,
                    hn_scr, pre_scr, i, nt):
    states_only = gate_outs is None
    if not states_only:
        q_ref, g1_ref, g2_ref = gate_outs
    tm, e = k_ref.shape
    shift = mod_ref[0, 0:1, :]
    scale = mod_ref[0, 1:2, :]
    g = g_ref[...]

    hn_scr[0:HALO, :] = _norm_mod(xprev_ref[...], g, shift, scale).astype(BF16)
    hn_scr[HALO:HALO + tm, :] = _norm_mod(x_ref[...], g, shift, scale).astype(BF16)
    hn_scr[HALO + tm:, :] = _norm_mod(xnext_ref[...], g, shift, scale).astype(BF16)

    def project(c):
        cs = slice(c * MXU_DIM, (c + 1) * MXU_DIM)
        xm_ext = _dot(hn_scr[...], win_ref[:, cs])
        if states_only:
            return xm_ext, None, None
        hc = hn_scr[HALO:HALO + tm, :]
        return (xm_ext, _dot(hc, win_ref[:, e + c * MXU_DIM:e + (c + 1) * MXU_DIM]),
                _dot(hc, win_ref[:, 2 * e + c * MXU_DIM:2 * e + (c + 1) * MXU_DIM]))

    pre = jnp.zeros((tm, LANES), F32) + bg_ref[...]
    n_chunks = e // MXU_DIM
    queue = [project(c) for c in range(min(PROJ_AHEAD, n_chunks))]
    for c in range(n_chunks):
        cs = slice(c * MXU_DIM, (c + 1) * MXU_DIM)
        xm_ext, z, og = queue.pop(0)
        if c + PROJ_AHEAD < n_chunks:
            queue.append(project(c + PROJ_AHEAD))
        xm_all = jnp.concatenate([jnp.where(i > 0, xm_ext[0:HALO], 0.0), xm_ext[HALO:HALO + tm],
                                  jnp.where(i < nt - 1, xm_ext[HALO + tm:], 0.0)], axis=0)
        xm = xm_all[HALO:HALO + tm]
        xm_prev = pltpu.roll(xm_all, 1, axis=0)[HALO:HALO + tm]
        xm_next = pltpu.roll(xm_all, tm + 2 * HALO - 1, axis=0)[HALO:HALO + tm]
        conv = (xm_prev * convw_ref[0:1, cs] + xm * convw_ref[1:2, cs] + xm_next * convw_ref[2:3, cs]
                + convb_ref[:, cs])
        xc = _silu(conv)
        xc_b = xc.astype(BF16)
        xm_b = xm.astype(BF16)
        k32 = _dot(xc_b, bdkq_ref[c])
        pre = pre + _dot(xc_b, aqk_ref[c]) + _dot(xm_b, av_ref[c])
        k_ref[:, cs] = k32.astype(BF16)
        k_t = k32.T.astype(BF16)
        for ch in range(tm // SCAN_CHUNK):
            kt_ref[ch, cs, :] = k_t[:, ch * SCAN_CHUNK:(ch + 1) * SCAN_CHUNK]
        v_ref[:, cs] = _dot(xm_b, bdv_ref[c]).astype(BF16)
        if not states_only:
            sz = _silu(z)
            q_ref[:, cs] = xc_b
            g1_ref[:, cs] = (hnorm_ref[:, cs] * jax.nn.sigmoid(og) * sz).astype(BF16)
            g2_ref[:, cs] = (skip_ref[:, cs] * xc * sz).astype(BF16)

    pre_scr[...] = pre


def _mlstm_pre(x, mod, pre_g, w_in_b, conv_w, conv_b, bdkq, bdv, aqk, av, bg, head_norm, skip, states_only,
               later_weights=()):
    b, t, d = x.shape
    e = w_in_b.shape[1] // 3
    tm = min(PRE_TILE, t)
    nt = t // tm

    n = b * nt
    nch = tm // SCAN_CHUNK
    cur = lambda i: jnp.minimum(i, n - 1)
    prv = lambda i: jnp.maximum(i - 1, 0)
    big = pl.BlockSpec((tm, e), lambda i: (cur(i), 0))
    big_t = pl.BlockSpec((nch, e, SCAN_CHUNK), lambda i: (cur(i), 0, 0))
    tok = jax.ShapeDtypeStruct((b * t, e), BF16)
    tok_t = jax.ShapeDtypeStruct((b * t // SCAN_CHUNK, e, SCAN_CHUNK), BF16)
    gate_specs = [pl.BlockSpec((tm, LANES), lambda i: (prv(i), 0)),
                  pl.BlockSpec((nch, 4 * N_HEADS, SCAN_CHUNK), lambda i: (prv(i), 0, 0)),
                  pl.BlockSpec((1, 2, tm, N_HEADS * LANES), lambda i: (prv(i) // nt, 0, prv(i) % nt, 0))]
    gate_shapes = [jax.ShapeDtypeStruct((b * t, LANES), F32),
                   jax.ShapeDtypeStruct((b * t // SCAN_CHUNK, 4 * N_HEADS, SCAN_CHUNK), F32),
                   jax.ShapeDtypeStruct((b, 2, t, N_HEADS * LANES), F32)]
    if states_only:
        out_specs, out_shape = [big, big_t, big] + gate_specs, [tok, tok_t, tok] + gate_shapes
    else:
        out_specs = [big, big, big_t, big, big, big] + gate_specs
        out_shape = [tok, tok, tok_t, tok, tok, tok] + gate_shapes
    later_specs = [pl.BlockSpec((w.shape[0] // n, w.shape[1]), lambda i: (cur(i), 0)) for w in later_weights]
    assert all(w.shape[0] % (HALO * n) == 0 for w in later_weights)
    outs = pl.pallas_call(
        functools.partial(_mlstm_pre_kernel, states_only=states_only, tiles_per_seq=nt,
                          n_later=len(later_weights)),
        grid=(n + 1,),
        in_specs=[
            pl.BlockSpec((tm, d), lambda i: (cur(i), 0)),
            pl.BlockSpec((HALO, d), lambda i: (jnp.maximum(cur(i) * (tm // HALO) - 1, 0), 0)),
            pl.BlockSpec((HALO, d), lambda i: (jnp.minimum((cur(i) + 1) * (tm // HALO), b * t // HALO - 1), 0)),
            pl.BlockSpec((1, 3, d), lambda i: (cur(i) // nt, 0, 0)),
            _const_spec((1, d)),
            _const_spec((d, e) if states_only else w_in_b.shape),
            _const_spec(conv_w.shape),
            _const_spec((1, e)),
            _const_spec(bdkq.shape), _const_spec(bdv.shape), _const_spec(aqk.shape), _const_spec(av.shape),
            _const_spec((1, LANES)),
            _const_spec((1, e)), _const_spec((1, e)),
        ] + later_specs,
        out_specs=out_specs + later_specs,
        out_shape=out_shape + [jax.ShapeDtypeStruct(w.shape, BF16) for w in later_weights],
        scratch_shapes=[pltpu.VMEM((tm + 2 * HALO, d), BF16), pltpu.VMEM((tm, LANES), F32)],
        compiler_params=_cparams(("arbitrary",)),
        name="mlstm_pre",
    )(x.reshape(b * t, d), x.reshape(b * t, d), x.reshape(b * t, d), mod, pre_g.reshape(1, d), w_in_b,
      conv_w, conv_b.reshape(1, e),
      bdkq, bdv, aqk, av, bg, head_norm.reshape(1, e), skip.reshape(1, e), *later_weights)
    by_token = lambda o: o.reshape(b, t, o.shape[-1])
    by_chunk = lambda o: o.reshape((b, t // SCAN_CHUNK) + o.shape[1:])
    outs, later_bf16 = outs[:len(out_specs)], outs[len(out_specs):]
    *main, cols, rows, cmrep = outs
    kt_pos = 1 if states_only else 2
    main = [by_chunk(o) if j == kt_pos else by_token(o) for j, o in enumerate(main)]
    return main + [by_token(cols), by_chunk(rows), cmrep] + list(later_bf16)


def _scan_chunk(get, cols, rows, cmrep, c_st, n_st, m_st, backward, emit):
    h = N_HEADS
    l = cols.shape[0]
    dh = c_st.shape[-1]
    off = 2 * h if backward else 0
    last = 0 if backward else l - 1
    st = []
    for head in range(h):
        cm_col = cols[:, 4 * h + off + head:4 * h + off + head + 1]
        st.append(dict(
            a_col=cols[:, off + head:off + head + 1],
            b_col=cols[:, off + h + head:off + h + head + 1],
            a_row=rows[off + head:off + head + 1, :],
            total=cols[last:last + 1, off + h + head:off + h + head + 1],
            amax=cm_col[last:last + 1, :],
            m_prev=m_st[head]))

    if emit is not None:
        ti = lax.broadcasted_iota(jnp.int32, (l, l), 0)
        si = lax.broadcasted_iota(jnp.int32, (l, l), 1)
        mask = (si >= ti) if backward else (ti >= si)
        for head in range(h):
            s = st[head]
            q = get("q", head)
            s["u"] = jnp.maximum(s["m_prev"], cmrep[:, head * LANES:(head + 1) * LANES])
            s["p"] = jnp.exp(jnp.where(mask, s["a_row"] - _lane_repeat(s["u"], l), -jnp.inf))
            s["qk"] = _dot(q, get("kt", head))
            s["qc"] = _dot(q, c_st[head].astype(BF16))
            sc = s["qk"] * s["p"]
            w_inter = jnp.exp(s["m_prev"] - s["u"])
            num = _dot(sc.astype(BF16), get("v", head)) + _lane_repeat(w_inter, dh) * s["qc"]
            qn = jnp.sum(get("q", head).astype(F32) * n_st[head], axis=-1, keepdims=True)
            den = jnp.sum(sc, axis=-1, keepdims=True) + w_inter[:, 0:1] * qn
            floor = jnp.exp(-(s["b_col"] + s["u"][:, 0:1]))
            emit(head, num / jnp.maximum(jnp.abs(den), floor))

    for head in range(h):
        s = st[head]
        s["u_last"] = jnp.maximum(s["m_prev"], s["amax"])
        s["decay"] = jnp.exp(s["m_prev"] - s["u_last"])
        wkt = (get("kt", head).astype(F32) * jnp.exp(s["a_row"] - s["u_last"])).astype(BF16)
        s["upd"] = _dot(wkt, get("v", head))
    for head in range(h):
        s = st[head]
        wk = get("k", head).astype(F32) * jnp.exp(s["a_col"] - s["u_last"])
        c_st[head] = s["decay"] * c_st[head] + s["upd"]
        n_st[head] = s["decay"] * n_st[head] + jnp.sum(wk, axis=0, keepdims=True)
        m_st[head] = s["total"] + s["u_last"]


def _tile_getter(refs, dh, chunk):
    rows = slice(chunk * SCAN_CHUNK, (chunk + 1) * SCAN_CHUNK)

    def get(name, head):
        hs = slice(head * dh, (head + 1) * dh)
        if name == "kt":
            return refs["kt"][0, chunk, hs, :]
        return refs[name][0, rows, hs]
    return get


def _ctx_state_kernel(k_ref, kt_ref, v_ref, cols_ref, rows_ref, c_out, n_out, m_out):
    dh = c_out.shape[-1]
    nc = k_ref.shape[1] // SCAN_CHUNK
    c_out[...] = jnp.zeros(c_out.shape, F32)
    n_out[...] = jnp.zeros(n_out.shape, F32)
    m_out[...] = jnp.zeros(m_out.shape, F32)
    refs = dict(k=k_ref, kt=kt_ref, v=v_ref)
    for direction in range(2):
        for step in range(nc):
            c = nc - 1 - step if direction else step
            rs = slice(c * SCAN_CHUNK, (c + 1) * SCAN_CHUNK)
            _scan_chunk(_tile_getter(refs, dh, c), cols_ref[0, rs, :], rows_ref[0, c], None,
                        c_out.at[0, direction], n_out.at[0, direction], m_out.at[0, direction],
                        bool(direction), None)


def _ctx_states(k, kt, v, cols, rows):
    b, t, e = k.shape
    dh = e // N_HEADS
    nc = t // SCAN_CHUNK
    per_b = lambda nd: (lambda bi: (bi,) + (0,) * (nd - 1))
    return pl.pallas_call(
        _ctx_state_kernel,
        grid=(b,),
        in_specs=[pl.BlockSpec((1, t, e), per_b(3)), pl.BlockSpec((1, nc, e, SCAN_CHUNK), per_b(4)),
                  pl.BlockSpec((1, t, e), per_b(3)),
                  pl.BlockSpec((1, t, LANES), per_b(3)),
                  pl.BlockSpec((1, nc, 4 * N_HEADS, SCAN_CHUNK), per_b(4))],
        out_specs=[pl.BlockSpec((1, 2, N_HEADS, dh, dh), per_b(5)),
                   pl.BlockSpec((1, 2, N_HEADS, 1, dh), per_b(5)),
                   pl.BlockSpec((1, 2, N_HEADS, 1, 1), per_b(5))],
        out_shape=[jax.ShapeDtypeStruct((b, 2, N_HEADS, dh, dh), F32),
                   jax.ShapeDtypeStruct((b, 2, N_HEADS, 1, dh), F32),
                   jax.ShapeDtypeStruct((b, 2, N_HEADS, 1, 1), F32)],
        compiler_params=_cparams(("parallel",)),
        name="ctx_states",
    )(k, kt, v, cols, rows)


def _load_state(c0_ref, n0_ref, m0_ref, c_scr, n_scr, m_scr):
    @pl.when(pl.program_id(1) == 0)
    def _():
        c_scr[...] = c0_ref[0, 0]
        n_scr[...] = n0_ref[0, 0]
        m_scr[...] = m0_ref[0, 0]


def _scan_kernel(q_ref, k_ref, kt_ref, v_ref, cols_ref, rows_ref, cmrep_ref, c0_ref, n0_ref, m0_ref,
                 h_ref, c_scr, n_scr, m_scr, *, backward):
    _load_state(c0_ref, n0_ref, m0_ref, c_scr, n_scr, m_scr)
    dh = c_scr.shape[-1]
    refs = dict(q=q_ref, k=k_ref, kt=kt_ref, v=v_ref)
    chunks = range(rows_ref.shape[1])
    for j in (reversed(chunks) if backward else chunks):
        rs = slice(j * SCAN_CHUNK, (j + 1) * SCAN_CHUNK)

        def emit(head, h, rs=rs):
            h_ref[0, rs, head * dh:(head + 1) * dh] = h.astype(h_ref.dtype)

        _scan_chunk(_tile_getter(refs, dh, j), cols_ref[0, rs, :], rows_ref[0, j], cmrep_ref[0, 0, rs, :],
                    c_scr, n_scr, m_scr, backward, emit)


def _mixer_out_kernel(hf_ref, hb_ref, g1_ref, g2_ref, wout_ref, x_ref, mod0_ref, postg_ref,
                      mod1_ref, preg1_ref, win1_ref, x1_ref, u_ref, sz_ref):
    e = hf_ref.shape[2]
    dh = e // N_HEADS
    y = None
    for head in range(N_HEADS):
        hs = slice(head * dh, (head + 1) * dh)
        hsum = hf_ref[0, :, hs].astype(F32) + hb_ref[0, :, hs].astype(F32)
        cen = hsum - jnp.mean(hsum, axis=-1, keepdims=True)
        hn = cen * lax.rsqrt(jnp.mean(cen * cen, axis=-1, keepdims=True) + EPS)
        gated = (hn * g1_ref[0, :, hs].astype(F32) + g2_ref[0, :, hs].astype(F32)).astype(BF16)
        part = _dot(gated, wout_ref[hs, :])
        y = part if y is None else y + part
    yn = y * lax.rsqrt(jnp.mean(y * y, axis=-1, keepdims=True) + EPS) * postg_ref[...]
    x1 = x_ref[0] + mod0_ref[0, 2:3, :] * yn
    x1_ref[0] = x1
    _pool_in_proj(x1, mod1_ref, preg1_ref, win1_ref, u_ref, sz_ref)


def _scan_specs(b, t, e, chunks_per_step, reverse):
    dh = e // N_HEADS
    direction = 1 if reverse else 0
    l = SCAN_CHUNK * chunks_per_step
    ns = t // l
    cidx = (lambda c: ns - 1 - c) if reverse else (lambda c: c)
    tok = lambda w: pl.BlockSpec((1, l, w), lambda bi, c: (bi, cidx(c), 0))
    state = lambda shape: pl.BlockSpec((1, 1) + shape, lambda bi, c: (bi, direction) + (0,) * len(shape),
                                       pipeline_mode=pl.Buffered(1))
    specs = [tok(e), tok(e),
             pl.BlockSpec((1, chunks_per_step, e, SCAN_CHUNK), lambda bi, c: (bi, cidx(c), 0, 0)),
             tok(e), tok(LANES),
             pl.BlockSpec((1, chunks_per_step, 4 * N_HEADS, SCAN_CHUNK), lambda bi, c: (bi, cidx(c), 0, 0)),
             pl.BlockSpec((1, 1, l, N_HEADS * LANES), lambda bi, c: (bi, direction, cidx(c), 0)),
             state((N_HEADS, dh, dh)), state((N_HEADS, 1, dh)), state((N_HEADS, 1, 1))]
    scratch = [pltpu.VMEM((N_HEADS, dh, dh), F32), pltpu.VMEM((N_HEADS, 1, dh), F32),
               pltpu.VMEM((N_HEADS, 1, 1), F32)]
    return specs, scratch, tok, ns


def _scan(q, k, kt, v, cols, rows, cmrep, c0, n0, m0, backward):
    b, t, e = q.shape
    specs, scratch, tok, ns = _scan_specs(b, t, e, SCAN_CHUNKS_PER_STEP, backward)
    return pl.pallas_call(
        functools.partial(_scan_kernel, backward=backward),
        grid=(b, ns),
        in_specs=specs,
        out_specs=tok(e),
        out_shape=jax.ShapeDtypeStruct((b, t, e), BF16),
        scratch_shapes=scratch,
        compiler_params=_cparams(("parallel", "arbitrary")),
        name="mlstm_scan_bwd" if backward else "mlstm_scan_fwd",
    )(q, k, kt, v, cols, rows, cmrep, c0, n0, m0)


def _mixer_out(hf, hb, g1, g2, w_out_b, x, mod0, post_g, mod1, pre_g1, w_in1_b):
    b, t, e = hf.shape
    d = x.shape[-1]
    tm = POOL_TILE
    row = lambda bi, i: (bi, i, 0)
    tok = lambda w: pl.BlockSpec((1, tm, w), row)
    modspec = pl.BlockSpec((1, 3, d), lambda bi, i: (bi, 0, 0))
    return pl.pallas_call(
        _mixer_out_kernel,
        grid=(b, t // tm),
        in_specs=[tok(e), tok(e), tok(e), tok(e), _const_spec(w_out_b.shape), tok(d), modspec,
                  _const_spec((1, d)), modspec, _const_spec((1, d)), _const_spec(w_in1_b.shape)],
        out_specs=[tok(d), tok(e), tok(e)],
        out_shape=[jax.ShapeDtypeStruct((b, t, d), F32), jax.ShapeDtypeStruct((b, t, e), BF16),
                   jax.ShapeDtypeStruct((b, t, e), BF16)],
        compiler_params=_cparams(("parallel", "parallel")),
        name="mixer_out_pool_in",
    )(hf, hb, g1, g2, w_out_b, x, mod0, post_g.reshape(1, d), mod1, pre_g1.reshape(1, d), w_in1_b)


def _pool_in_proj(x, mod_ref, g_ref, win_ref, u_ref, sz_ref):
    e = u_ref.shape[2]
    hn = _norm_mod(x, g_ref[...], mod_ref[0, 0:1, :], mod_ref[0, 1:2, :]).astype(BF16)
    for c in range(e // MXU_DIM):
        cs = slice(c * MXU_DIM, (c + 1) * MXU_DIM)
        u_ref[0, :, cs] = _dot(hn, win_ref[:, cs]).astype(BF16)
        sz_ref[0, :, cs] = _silu(_dot(hn, win_ref[:, e + c * MXU_DIM:e + (c + 1) * MXU_DIM])).astype(BF16)


def _pool_out_kernel(up_ref, uc_ref, un_ref, sz_ref, band_ref, icnt_ref, wpool_ref, pscale_ref,
                     wout_ref, x_ref, mod_ref, postg_ref, o_ref, ext_scr):
    i = pl.program_id(1)
    nt = pl.num_programs(1)
    tm = uc_ref.shape[1]
    gd = wpool_ref.shape[1]
    y = None
    for g, w in enumerate(POOL_WINDOWS):
        gs = slice(g * gd, (g + 1) * gd)
        before = (w // 2) * GRID_W
        after = (w - w // 2 - 1) * GRID_W
        ext_scr[tm - before:tm, gs] = jnp.where(i > 0, up_ref[0, tm - before:, gs].astype(F32), 0.0)
        ext_scr[tm:2 * tm, gs] = uc_ref[0, :, gs].astype(F32)
        if after:
            ext_scr[2 * tm:2 * tm + after, gs] = jnp.where(i < nt - 1, un_ref[0, 0:after, gs].astype(F32), 0.0)
        n = tm + (w - 1) * GRID_W
        lo = tm - before
        rsum = ext_scr[lo:lo + n, gs]
        span = 1
        while span < w:
            n -= span * GRID_W
            rsum = rsum[0:n] + rsum[span * GRID_W:span * GRID_W + n]
            span *= 2
        rsum = rsum.astype(BF16)
        parts = [_dot(band_ref[g], rsum[s * MXU_DIM:(s + 1) * MXU_DIM, :]) for s in range(tm // MXU_DIM)]
        mean = jnp.concatenate(parts, axis=0) * icnt_ref[:, g:g + 1]
        diff = mean - ext_scr[tm:2 * tm, gs]
        yg = _dot(diff.astype(BF16), wpool_ref[g]) * pscale_ref[:, gs] * sz_ref[0, :, gs].astype(F32)
        part = _dot(yg.astype(BF16), wout_ref[gs, :])
        y = part if y is None else y + part
    yn = y * lax.rsqrt(jnp.mean(y * y, axis=-1, keepdims=True) + EPS) * postg_ref[...]
    o_ref[0] = x_ref[0] + mod_ref[0, 2:3, :] * yn


def _pool_constants(t):
    rows = t // GRID_W
    bands = np.zeros((len(POOL_WINDOWS), MXU_DIM, MXU_DIM), np.float32)
    icnt = np.zeros((t, LANES), np.float32)
    for g, w in enumerate(POOL_WINDOWS):
        lo_off, hi_off = -(w // 2), w - w // 2
        for p in range(MXU_DIM):
            c = p % GRID_W
            base = p - c
            lo, hi = max(c + lo_off, 0), min(c + hi_off, GRID_W)
            bands[g, p, base + lo:base + hi] = 1.0
        cc = np.arange(GRID_W)
        cnt_c = np.clip(cc + hi_off, 0, GRID_W) - np.clip(cc + lo_off, 0, GRID_W)
        rr = np.arange(rows)
        cnt_r = np.clip(rr + hi_off, 0, rows) - np.clip(rr + lo_off, 0, rows)
        icnt[:, g] = (1.0 / (cnt_r[:, None] * cnt_c[None, :])).reshape(-1)
    return bands, icnt


def _pool_out(u, sz, w_pool_b, pool_scale, w_out_b, x, mod, post_g):
    b, t, e = u.shape
    d = x.shape[-1]
    tm = POOL_TILE
    assert tm >= (max(POOL_WINDOWS) // 2) * GRID_W and tm % MXU_DIM == 0 and t % tm == 0
    nt = t // tm
    gd = e // len(POOL_WINDOWS)
    bands, icnt = _pool_constants(t)
    row = lambda bi, i: (bi, i, 0)
    return pl.pallas_call(
        _pool_out_kernel,
        grid=(b, nt),
        in_specs=[
            pl.BlockSpec((1, tm, e), lambda bi, i: (bi, jnp.maximum(i - 1, 0), 0)),
            pl.BlockSpec((1, tm, e), row),
            pl.BlockSpec((1, tm, e), lambda bi, i: (bi, jnp.minimum(i + 1, nt - 1), 0)),
            pl.BlockSpec((1, tm, e), row),
            _const_spec(bands.shape),
            pl.BlockSpec((tm, LANES), lambda bi, i: (i, 0)),
            _const_spec(w_pool_b.shape),
            _const_spec((1, e)),
            _const_spec(w_out_b.shape),
            pl.BlockSpec((1, tm, d), row),
            pl.BlockSpec((1, 3, d), lambda bi, i: (bi, 0, 0)),
            _const_spec((1, d)),
        ],
        out_specs=pl.BlockSpec((1, tm, d), row),
        out_shape=jax.ShapeDtypeStruct((b, t, d), F32),
        scratch_shapes=[pltpu.VMEM((3 * tm, e), F32)],
        compiler_params=_cparams(("parallel", "parallel")),
        name="pool_out",
    )(u, u, u, sz, jnp.asarray(bands, BF16), jnp.asarray(icnt), w_pool_b, pool_scale.reshape(1, e),
      w_out_b, x, mod, post_g.reshape(1, d))


def kernel(x, c, ctx, c_ctx, w_ada, b_ada, a_norm_pre, a_norm_post, a_w_in, a_conv_w, a_conv_b,
           a_w_q, a_w_k, a_w_v, a_w_gate_f, a_b_gate_f, a_w_gate_b, a_b_gate_b, a_head_norm, a_skip,
           a_w_out, b_norm_pre, b_norm_post, b_w_in, b_w_pool, b_pool_scale, b_w_out):
    bsz, t, d = x.shape
    t_ctx = ctx.shape[1]
    assert t % SCAN_CHUNK == 0 and t_ctx % SCAN_CHUNK == 0 and t % GRID_W == 0

    n_cond = -(-(bsz + 1) // 8) * 8
    cond = jnp.concatenate([c, c_ctx[None], jnp.zeros((n_cond - bsz - 1, d), F32)], axis=0)
    mods = _ada(cond, bsz + 1, w_ada, b_ada).reshape(w_ada.shape[0], n_cond, 3, d)
    mod_lat = [mods[i, :bsz] for i in range(w_ada.shape[0])]
    mod_ctx = jnp.broadcast_to(mods[0, bsz][None], (bsz, 3, d))

    k_scale = float((a_w_in.shape[2] // 3 // N_HEADS) ** -0.5)
    bdkq, bdv, aqk, av, w_in_b = _fold_weights(a_w_q[0], a_w_k[0], a_w_v[0], a_w_gate_f[0], a_w_gate_b[0],
                                               a_w_in[0], k_scale)
    bg = jnp.pad(jnp.concatenate([a_b_gate_f[0], a_b_gate_b[0]]), (0, LANES - 4 * N_HEADS)).reshape(1, LANES)
    pre_args = (a_norm_pre[0], w_in_b, a_conv_w[0], a_conv_b[0], bdkq, bdv, aqk, av, bg,
                a_head_norm[0], a_skip[0])
    w_pool = b_w_pool[0]
    later = (a_w_out[0], b_w_in[0], w_pool.reshape(-1, w_pool.shape[-1]), b_w_out[0])
    (ql, kl, ktl, vl, g1, g2, cols_l, rows_l, cmrep_l, a_w_out_b, b_w_in_b, w_pool_b, b_w_out_b) = _mlstm_pre(
        x, mod_lat[0], *pre_args, states_only=False, later_weights=later)
    kc, ktc, vc, cols_c, rows_c, _ = _mlstm_pre(ctx, mod_ctx, *pre_args, states_only=True)
    c0, n0, m0 = _ctx_states(kc, ktc, vc, cols_c, rows_c)
    scan_in = (ql, kl, ktl, vl, cols_l, rows_l, cmrep_l, c0, n0, m0)
    hb = _scan(*scan_in, backward=True)
    hf = _scan(*scan_in, backward=False)

    x, u, sz = _mixer_out(hf, hb, g1, g2, a_w_out_b, x, mod_lat[0], a_norm_post[0],
                          mod_lat[1], b_norm_pre[0], b_w_in_b)
    x = _pool_out(u, sz, w_pool_b.reshape(w_pool.shape), b_pool_scale[0], b_w_out_b,
                  x, mod_lat[1], b_norm_post[0])
    return x
```

```python
import functools

import numpy as np
import jax
import jax.numpy as jnp
from jax import lax
from jax.experimental import pallas as pl
from jax.experimental.pallas import tpu as pltpu

EPS = 1e-6
N_HEADS = 4
QKV_BLOCK = 4
GRID_W = 64
POOL_WINDOWS = (2, 4, 8, 16)

LANES = 128
MXU_DIM = 256
HALO = 16
VMEM_LIMIT_BYTES = 60 * 1024 * 1024

SCAN_CHUNK = 256
PRE_TILE = 512
PROJ_AHEAD = 3
POOL_TILE = 512
SCAN_CHUNKS_PER_STEP = 2
BF16 = jnp.bfloat16
F32 = jnp.float32


def _dot(a, b):
    return jnp.dot(a, b, preferred_element_type=F32)


def _silu(v):
    return v * jax.nn.sigmoid(v)


def _cparams(sem):
    return pltpu.CompilerParams(dimension_semantics=sem, vmem_limit_bytes=VMEM_LIMIT_BYTES)


def _const_spec(shape):
    nd = len(shape)
    return pl.BlockSpec(shape, lambda *_: (0,) * nd, pipeline_mode=pl.Buffered(1))


def _lane_repeat(col, width):
    return jnp.concatenate([col] * (width // LANES), axis=1) if width > LANES else col


def _ada_kernel(cond_t_ref, w_ref, b_ref, o_ref, *, n_used):
    s = _silu(cond_t_ref[...])
    w = w_ref[0]
    for r in range(n_used):
        o_ref[0, r:r + 1, :] = jnp.sum(w * s[:, r:r + 1], axis=0, keepdims=True) + b_ref[0]
    o_ref[0, n_used:, :] = jnp.zeros((o_ref.shape[1] - n_used, o_ref.shape[2]), F32)


def _ada(cond, n_used, w_ada, b_ada):
    depth, d, n3 = w_ada.shape
    rows = cond.shape[0]
    tn = max(w for w in range(LANES, 1024 + 1, LANES) if n3 % w == 0)
    return pl.pallas_call(
        functools.partial(_ada_kernel, n_used=n_used),
        grid=(depth, n3 // tn),
        in_specs=[
            pl.BlockSpec((d, rows), lambda l, j: (0, 0)),
            pl.BlockSpec((1, d, tn), lambda l, j: (l, 0, j)),
            pl.BlockSpec((1, 1, tn), lambda l, j: (l, 0, j)),
        ],
        out_specs=pl.BlockSpec((1, rows, tn), lambda l, j: (l, 0, j)),
        out_shape=jax.ShapeDtypeStruct((depth, rows, n3), F32),
        compiler_params=_cparams(("parallel", "parallel")),
        name="ada_mod",
    )(cond.T, w_ada, b_ada.reshape(depth, 1, n3))


def _norm_mod(xv, g, shift, scale):
    y = xv * lax.rsqrt(jnp.mean(xv * xv, axis=-1, keepdims=True) + EPS) * g
    return y * (1.0 + scale) + shift


def _scan_along_lanes(x, fwd_rows, lane, combine, fill):
    l = x.shape[1]
    step = 1
    while step < l:
        from_left = jnp.where(lane >= step, pltpu.roll(x, step, axis=1), fill)
        from_right = jnp.where(lane < l - step, pltpu.roll(x, l - step, axis=1), fill)
        x = combine(x, jnp.where(fwd_rows, from_left, from_right))
        step *= 2
    return x


def _gate_vectors(pre_chunks):
    h = N_HEADS
    l = pre_chunks[0].shape[0]
    pt = jnp.concatenate([p.T[0:4 * h, :] for p in pre_chunks], axis=0)
    row = lax.broadcasted_iota(jnp.int32, pt.shape, 0) % (4 * h)
    lane = lax.broadcasted_iota(jnp.int32, pt.shape, 1)
    fwd_rows = row < 2 * h
    log_f = jnp.minimum(pt, 0.0) - jnp.log1p(jnp.exp(-jnp.abs(pt)))
    b = _scan_along_lanes(log_f, fwd_rows, lane, jnp.add, 0.0)
    a = pt - pltpu.roll(b, pt.shape[0] - h, axis=0)
    cmax = _scan_along_lanes(a, fwd_rows, lane, jnp.maximum, -jnp.inf)
    rows = jnp.where((row % (2 * h)) < h, a, b)
    out = []
    for ch in range(len(pre_chunks)):
        rs = slice(ch * 4 * h, (ch + 1) * 4 * h)
        stacked = jnp.concatenate([rows[rs], cmax[rs], jnp.zeros((LANES - 8 * h, l), F32)], axis=0)
        cols = stacked.T
        reps = []
        for direction in range(2):
            for head in range(h):
                j = 4 * h + direction * 2 * h + head
                reps.append(jnp.broadcast_to(cols[:, j:j + 1], (l, LANES)))
        out.append((cols, rows[rs], jnp.concatenate(reps, axis=1)))
    return out


def _fold_weights_kernel(wq_ref, wk_ref, wv_ref, wg_ref, win_ref, bdkq_ref, bdv_ref, aqk_ref, av_ref, win16_ref,
                         *, k_scale):
    hp = lax.Precision.HIGHEST
    dot = lambda a, b: jnp.dot(a, b, precision=hp, preferred_element_type=F32)
    r = lax.broadcasted_iota(jnp.int32, (MXU_DIM, MXU_DIM), 0)
    c = lax.broadcasted_iota(jnp.int32, (MXU_DIM, MXU_DIM), 1)
    same_block = (r // QKV_BLOCK) == (c // QKV_BLOCK)

    def block_diag(w_ref):
        rows = w_ref[...]
        tile = jnp.zeros((MXU_DIM, MXU_DIM), F32)
        for o in range(QKV_BLOCK):
            tile = jnp.where(same_block & (c % QKV_BLOCK == o), rows[:, o:o + 1], tile)
        return tile

    bdq = block_diag(wq_ref)
    bdk = block_diag(wk_ref) * k_scale
    bdv = block_diag(wv_ref)
    bdkq = lax.dot_general(bdk, bdq, (((1,), (1,)), ((), ())), precision=hp, preferred_element_type=F32)
    bdkq_ref[0] = bdkq.astype(BF16)
    bdv_ref[0] = bdv.astype(BF16)
    aqk_ref[0] = (dot(bdq, wg_ref[0, 0]) + dot(bdk, wg_ref[1, 0])).astype(BF16)
    av_ref[0] = dot(bdv, wg_ref[2, 0]).astype(BF16)
    win16_ref[...] = win_ref[...].astype(BF16)


def _fold_weights(w_q, w_k, w_v, w_gate_f, w_gate_b, w_in, k_scale):
    e = w_q.shape[0] * QKV_BLOCK
    nt = e // MXU_DIM
    d = w_in.shape[0]
    assert d % (HALO * nt) == 0
    rows = pl.BlockSpec((MXU_DIM, QKV_BLOCK), lambda i: (i, 0))
    tile = pl.BlockSpec((1, MXU_DIM, MXU_DIM), lambda i: (i, 0, 0))
    gate = pl.BlockSpec((1, MXU_DIM, LANES), lambda i: (i, 0, 0))
    win = pl.BlockSpec((d // nt, w_in.shape[1]), lambda i: (i, 0))
    wg = jnp.pad(jnp.concatenate([w_gate_f, w_gate_b], axis=1), ((0, 0), (0, LANES - 4 * N_HEADS)))
    args = [w.reshape(e, QKV_BLOCK) for w in (w_q, w_k, w_v)] + [wg.reshape(3, nt, MXU_DIM, LANES), w_in]
    return pl.pallas_call(
        functools.partial(_fold_weights_kernel, k_scale=k_scale),
        grid=(nt,),
        in_specs=[rows, rows, rows, pl.BlockSpec((3, 1, MXU_DIM, LANES), lambda i: (0, i, 0, 0)), win],
        out_specs=[tile, tile, gate, gate, win],
        out_shape=[jax.ShapeDtypeStruct((nt, MXU_DIM, MXU_DIM), BF16),
                   jax.ShapeDtypeStruct((nt, MXU_DIM, MXU_DIM), BF16),
                   jax.ShapeDtypeStruct((nt, MXU_DIM, LANES), BF16),
                   jax.ShapeDtypeStruct((nt, MXU_DIM, LANES), BF16),
                   jax.ShapeDtypeStruct(w_in.shape, BF16)],
        compiler_params=_cparams(("arbitrary",)),
        name="fold_weights",
    )(*args)


def _mlstm_pre_kernel(x_ref, xprev_ref, xnext_ref, mod_ref, g_ref, win_ref, convw_ref, convb_ref,
                      bdkq_ref, bdv_ref, aqk_ref, av_ref, bg_ref, hnorm_ref, skip_ref, *rest,
                      states_only, tiles_per_seq, n_later):
    later_f32, rest = rest[:n_later], rest[n_later:]
    later_bf16, (hn_scr, pre_scr) = rest[-2 - n_later:-2], rest[-2:]
    if states_only:
        k_ref, kt_ref, v_ref, cols_ref, rows_ref, cmrep_ref = rest[:-2 - n_later]
    else:
        q_ref, k_ref, kt_ref, v_ref, g1_ref, g2_ref, cols_ref, rows_ref, cmrep_ref = rest[:-2 - n_later]
    step = pl.program_id(0)
    n_tiles = pl.num_programs(0) - 1
    tm, e = k_ref.shape

    def gate_stage():
        nch = tm // SCAN_CHUNK
        chunks = [pre_scr[ch * SCAN_CHUNK:(ch + 1) * SCAN_CHUNK, :] for ch in range(nch)]
        for ch, (cols, rows, cmrep) in enumerate(_gate_vectors(chunks)):
            rs = slice(ch * SCAN_CHUNK, (ch + 1) * SCAN_CHUNK)
            cols_ref[rs, :] = cols
            rows_ref[ch] = rows
            half = N_HEADS * LANES
            cmrep_ref[0, 0, rs, :] = cmrep[:, 0:half]
            cmrep_ref[0, 1, rs, :] = cmrep[:, half:]

    @pl.when(step == 0)
    def _():
        pre_scr[...] = jnp.zeros(pre_scr.shape, F32)

    @pl.when(step == n_tiles)
    def _():
        gate_stage()

    @pl.when(step < n_tiles)
    def _():
        gate_stage()
        _mlstm_pre_tile(x_ref, xprev_ref, xnext_ref, mod_ref, g_ref, win_ref, convw_ref, convb_ref, bdkq_ref, bdv_ref,
                        aqk_ref, av_ref, bg_ref, hnorm_ref, skip_ref,
                        None if states_only else (q_ref, g1_ref, g2_ref), k_ref, kt_ref, v_ref,
                        hn_scr, pre_scr, step % tiles_per_seq, tiles_per_seq)
        for w32_ref, w16_ref in zip(later_f32, later_bf16):
            w16_ref[...] = w32_ref[...].astype(BF16)


def _mlstm_pre_tile(x_ref, xprev_ref, xnext_ref, mod_ref, g_ref, win_ref, convw_ref, convb_ref, bdkq_ref, bdv_ref,
                    aqk_ref, av_ref, bg_ref, hnorm_ref, skip_ref, gate_outs, k_ref, kt_ref, v_ref,
                    hn_scr, pre_scr, i, nt):
    states_only = gate_outs is None
    if not states_only:
        q_ref, g1_ref, g2_ref = gate_outs
    tm, e = k_ref.shape
    shift = mod_ref[0, 0:1, :]
    scale = mod_ref[0, 1:2, :]
    g = g_ref[...]

    hn_scr[0:HALO, :] = _norm_mod(xprev_ref[...], g, shift, scale).astype(BF16)
    hn_scr[HALO:HALO + tm, :] = _norm_mod(x_ref[...], g, shift, scale).astype(BF16)
    hn_scr[HALO + tm:, :] = _norm_mod(xnext_ref[...], g, shift, scale).astype(BF16)

    def project(c):
        cs = slice(c * MXU_DIM, (c + 1) * MXU_DIM)
        xm_ext = _dot(hn_scr[...], win_ref[:, cs])
        if states_only:
            return xm_ext, None, None
        hc = hn_scr[HALO:HALO + tm, :]
        return (xm_ext, _dot(hc, win_ref[:, e + c * MXU_DIM:e + (c + 1) * MXU_DIM]),
                _dot(hc, win_ref[:, 2 * e + c * MXU_DIM:2 * e + (c + 1) * MXU_DIM]))

    pre = jnp.zeros((tm, LANES), F32) + bg_ref[...]
    n_chunks = e // MXU_DIM
    queue = [project(c) for c in range(min(PROJ_AHEAD, n_chunks))]
    for c in range(n_chunks):
        cs = slice(c * MXU_DIM, (c + 1) * MXU_DIM)
        xm_ext, z, og = queue.pop(0)
        if c + PROJ_AHEAD < n_chunks:
            queue.append(project(c + PROJ_AHEAD))
        xm_all = jnp.concatenate([jnp.where(i > 0, xm_ext[0:HALO], 0.0), xm_ext[HALO:HALO + tm],
                                  jnp.where(i < nt - 1, xm_ext[HALO + tm:], 0.0)], axis=0)
        xm = xm_all[HALO:HALO + tm]
        xm_prev = pltpu.roll(xm_all, 1, axis=0)[HALO:HALO + tm]
        xm_next = pltpu.roll(xm_all, tm + 2 * HALO - 1, axis=0)[HALO:HALO + tm]
        conv = (xm_prev * convw_ref[0:1, cs] + xm * convw_ref[1:2, cs] + xm_next * convw_ref[2:3, cs]
                + convb_ref[:, cs])
        xc = _silu(conv)
        xc_b = xc.astype(BF16)
        xm_b = xm.astype(BF16)
        k32 = _dot(xc_b, bdkq_ref[c])
        pre = pre + _dot(xc_b, aqk_ref[c]) + _dot(xm_b, av_ref[c])
        k_ref[:, cs] = k32.astype(BF16)
        k_t = k32.T.astype(BF16)
        for ch in range(tm // SCAN_CHUNK):
            kt_ref[ch, cs, :] = k_t[:, ch * SCAN_CHUNK:(ch + 1) * SCAN_CHUNK]
        v_ref[:, cs] = _dot(xm_b, bdv_ref[c]).astype(BF16)
        if not states_only:
            sz = _silu(z)
            q_ref[:, cs] = xc_b
            g1_ref[:, cs] = (hnorm_ref[:, cs] * jax.nn.sigmoid(og) * sz).astype(BF16)
            g2_ref[:, cs] = (skip_ref[:, cs] * xc * sz).astype(BF16)

    pre_scr[...] = pre


def _mlstm_pre(x, mod, pre_g, w_in_b, conv_w, conv_b, bdkq, bdv, aqk, av, bg, head_norm, skip, states_only,
               later_weights=()):
    b, t, d = x.shape
    e = w_in_b.shape[1] // 3
    tm = min(PRE_TILE, t)
    nt = t // tm

    n = b * nt
    nch = tm // SCAN_CHUNK
    cur = lambda i: jnp.minimum(i, n - 1)
    prv = lambda i: jnp.maximum(i - 1, 0)
    big = pl.BlockSpec((tm, e), lambda i: (cur(i), 0))
    big_t = pl.BlockSpec((nch, e, SCAN_CHUNK), lambda i: (cur(i), 0, 0))
    tok = jax.ShapeDtypeStruct((b * t, e), BF16)
    tok_t = jax.ShapeDtypeStruct((b * t // SCAN_CHUNK, e, SCAN_CHUNK), BF16)
    gate_specs = [pl.BlockSpec((tm, LANES), lambda i: (prv(i), 0)),
                  pl.BlockSpec((nch, 4 * N_HEADS, SCAN_CHUNK), lambda i: (prv(i), 0, 0)),
                  pl.BlockSpec((1, 2, tm, N_HEADS * LANES), lambda i: (prv(i) // nt, 0, prv(i) % nt, 0))]
    gate_shapes = [jax.ShapeDtypeStruct((b * t, LANES), F32),
                   jax.ShapeDtypeStruct((b * t // SCAN_CHUNK, 4 * N_HEADS, SCAN_CHUNK), F32),
                   jax.ShapeDtypeStruct((b, 2, t, N_HEADS * LANES), F32)]
    if states_only:
        out_specs, out_shape = [big, big_t, big] + gate_specs, [tok, tok_t, tok] + gate_shapes
    else:
        out_specs = [big, big, big_t, big, big, big] + gate_specs
        out_shape = [tok, tok, tok_t, tok, tok, tok] + gate_shapes
    later_specs = [pl.BlockSpec((w.shape[0] // n, w.shape[1]), lambda i: (cur(i), 0)) for w in later_weights]
    assert all(w.shape[0] % (HALO * n) == 0 for w in later_weights)
    outs = pl.pallas_call(
        functools.partial(_mlstm_pre_kernel, states_only=states_only, tiles_per_seq=nt,
                          n_later=len(later_weights)),
        grid=(n + 1,),
        in_specs=[
            pl.BlockSpec((tm, d), lambda i: (cur(i), 0)),
            pl.BlockSpec((HALO, d), lambda i: (jnp.maximum(cur(i) * (tm // HALO) - 1, 0), 0)),
            pl.BlockSpec((HALO, d), lambda i: (jnp.minimum((cur(i) + 1) * (tm // HALO), b * t // HALO - 1), 0)),
            pl.BlockSpec((1, 3, d), lambda i: (cur(i) // nt, 0, 0)),
            _const_spec((1, d)),
            _const_spec(w_in_b.shape),
            _const_spec(conv_w.shape),
            _const_spec((1, e)),
            _const_spec(bdkq.shape), _const_spec(bdv.shape), _const_spec(aqk.shape), _const_spec(av.shape),
            _const_spec((1, LANES)),
            _const_spec((1, e)), _const_spec((1, e)),
        ] + later_specs,
        out_specs=out_specs + later_specs,
        out_shape=out_shape + [jax.ShapeDtypeStruct(w.shape, BF16) for w in later_weights],
        scratch_shapes=[pltpu.VMEM((tm + 2 * HALO, d), BF16), pltpu.VMEM((tm, LANES), F32)],
        compiler_params=_cparams(("arbitrary",)),
        name="mlstm_pre",
    )(x.reshape(b * t, d), x.reshape(b * t, d), x.reshape(b * t, d), mod, pre_g.reshape(1, d), w_in_b,
      conv_w, conv_b.reshape(1, e),
      bdkq, bdv, aqk, av, bg, head_norm.reshape(1, e), skip.reshape(1, e), *later_weights)
    by_token = lambda o: o.reshape(b, t, o.shape[-1])
    by_chunk = lambda o: o.reshape((b, t // SCAN_CHUNK) + o.shape[1:])
    outs, later_bf16 = outs[:len(out_specs)], outs[len(out_specs):]
    *main, cols, rows, cmrep = outs
    kt_pos = 1 if states_only else 2
    main = [by_chunk(o) if j == kt_pos else by_token(o) for j, o in enumerate(main)]
    return main + [by_token(cols), by_chunk(rows), cmrep] + list(later_bf16)


def _scan_chunk(get, cols, rows, cmrep, c_st, n_st, m_st, backward, emit):
    h = N_HEADS
    l = cols.shape[0]
    dh = c_st.shape[-1]
    off = 2 * h if backward else 0
    last = 0 if backward else l - 1
    st = []
    for head in range(h):
        cm_col = cols[:, 4 * h + off + head:4 * h + off + head + 1]
        st.append(dict(
            a_col=cols[:, off + head:off + head + 1],
            b_col=cols[:, off + h + head:off + h + head + 1],
            a_row=rows[off + head:off + head + 1, :],
            total=cols[last:last + 1, off + h + head:off + h + head + 1],
            amax=cm_col[last:last + 1, :],
            m_prev=m_st[head]))

    if emit is not None:
        ti = lax.broadcasted_iota(jnp.int32, (l, l), 0)
        si = lax.broadcasted_iota(jnp.int32, (l, l), 1)
        mask = (si >= ti) if backward else (ti >= si)
        for head in range(h):
            s = st[head]
            q = get("q", head)
            s["u"] = jnp.maximum(s["m_prev"], cmrep[:, head * LANES:(head + 1) * LANES])
            s["p"] = jnp.exp(jnp.where(mask, s["a_row"] - _lane_repeat(s["u"], l), -jnp.inf))
            s["qk"] = _dot(q, get("kt", head))
            s["qc"] = _dot(q, c_st[head].astype(BF16))
            sc = s["qk"] * s["p"]
            w_inter = jnp.exp(s["m_prev"] - s["u"])
            num = _dot(sc.astype(BF16), get("v", head)) + _lane_repeat(w_inter, dh) * s["qc"]
            qn = jnp.sum(get("q", head).astype(F32) * n_st[head], axis=-1, keepdims=True)
            den = jnp.sum(sc, axis=-1, keepdims=True) + w_inter[:, 0:1] * qn
            floor = jnp.exp(-(s["b_col"] + s["u"][:, 0:1]))
            emit(head, num / jnp.maximum(jnp.abs(den), floor))

    for head in range(h):
        s = st[head]
        s["u_last"] = jnp.maximum(s["m_prev"], s["amax"])
        s["decay"] = jnp.exp(s["m_prev"] - s["u_last"])
        wkt = (get("kt", head).astype(F32) * jnp.exp(s["a_row"] - s["u_last"])).astype(BF16)
        s["upd"] = _dot(wkt, get("v", head))
    for head in range(h):
        s = st[head]
        wk = get("k", head).astype(F32) * jnp.exp(s["a_col"] - s["u_last"])
        c_st[head] = s["decay"] * c_st[head] + s["upd"]
        n_st[head] = s["decay"] * n_st[head] + jnp.sum(wk, axis=0, keepdims=True)
        m_st[head] = s["total"] + s["u_last"]


def _tile_getter(refs, dh, chunk):
    rows = slice(chunk * SCAN_CHUNK, (chunk + 1) * SCAN_CHUNK)

    def get(name, head):
        hs = slice(head * dh, (head + 1) * dh)
        if name == "kt":
            return refs["kt"][0, chunk, hs, :]
        return refs[name][0, rows, hs]
    return get


def _ctx_state_kernel(k_ref, kt_ref, v_ref, cols_ref, rows_ref, c_out, n_out, m_out):
    dh = c_out.shape[-1]
    nc = k_ref.shape[1] // SCAN_CHUNK
    c_out[...] = jnp.zeros(c_out.shape, F32)
    n_out[...] = jnp.zeros(n_out.shape, F32)
    m_out[...] = jnp.zeros(m_out.shape, F32)
    refs = dict(k=k_ref, kt=kt_ref, v=v_ref)
    for direction in range(2):
        for step in range(nc):
            c = nc - 1 - step if direction else step
            rs = slice(c * SCAN_CHUNK, (c + 1) * SCAN_CHUNK)
            _scan_chunk(_tile_getter(refs, dh, c), cols_ref[0, rs, :], rows_ref[0, c], None,
                        c_out.at[0, direction], n_out.at[0, direction], m_out.at[0, direction],
                        bool(direction), None)


def _ctx_states(k, kt, v, cols, rows):
    b, t, e = k.shape
    dh = e // N_HEADS
    nc = t // SCAN_CHUNK
    per_b = lambda nd: (lambda bi: (bi,) + (0,) * (nd - 1))
    return pl.pallas_call(
        _ctx_state_kernel,
        grid=(b,),
        in_specs=[pl.BlockSpec((1, t, e), per_b(3)), pl.BlockSpec((1, nc, e, SCAN_CHUNK), per_b(4)),
                  pl.BlockSpec((1, t, e), per_b(3)),
                  pl.BlockSpec((1, t, LANES), per_b(3)),
                  pl.BlockSpec((1, nc, 4 * N_HEADS, SCAN_CHUNK), per_b(4))],
        out_specs=[pl.BlockSpec((1, 2, N_HEADS, dh, dh), per_b(5)),
                   pl.BlockSpec((1, 2, N_HEADS, 1, dh), per_b(5)),
                   pl.BlockSpec((1, 2, N_HEADS, 1, 1), per_b(5))],
        out_shape=[jax.ShapeDtypeStruct((b, 2, N_HEADS, dh, dh), F32),
                   jax.ShapeDtypeStruct((b, 2, N_HEADS, 1, dh), F32),
                   jax.ShapeDtypeStruct((b, 2, N_HEADS, 1, 1), F32)],
        compiler_params=_cparams(("parallel",)),
        name="ctx_states",
    )(k, kt, v, cols, rows)


def _load_state(c0_ref, n0_ref, m0_ref, c_scr, n_scr, m_scr):
    @pl.when(pl.program_id(1) == 0)
    def _():
        c_scr[...] = c0_ref[0, 0]
        n_scr[...] = n0_ref[0, 0]
        m_scr[...] = m0_ref[0, 0]


N_SCAN_INPUTS = 10


def _scan_both_kernel(*refs):
    n_in = N_SCAN_INPUTS
    ins = (refs[:n_in], refs[n_in:2 * n_in])
    outs = refs[2 * n_in:2 * n_in + 2]
    scr = (refs[2 * n_in + 2:2 * n_in + 5], refs[2 * n_in + 5:])
    for d in range(2):
        _load_state(*ins[d][7:], *scr[d])
    for d in range(2):
        _scan_step(*ins[d][:7], outs[d], *scr[d], backward=bool(d))


def _scan_step(q_ref, k_ref, kt_ref, v_ref, cols_ref, rows_ref, cmrep_ref, h_ref, c_scr, n_scr, m_scr, *,
               backward):
    dh = c_scr.shape[-1]
    refs = dict(q=q_ref, k=k_ref, kt=kt_ref, v=v_ref)
    chunks = range(rows_ref.shape[1])
    for j in (reversed(chunks) if backward else chunks):
        rs = slice(j * SCAN_CHUNK, (j + 1) * SCAN_CHUNK)

        def emit(head, h, rs=rs):
            h_ref[0, rs, head * dh:(head + 1) * dh] = h.astype(h_ref.dtype)

        _scan_chunk(_tile_getter(refs, dh, j), cols_ref[0, rs, :], rows_ref[0, j], cmrep_ref[0, 0, rs, :],
                    c_scr, n_scr, m_scr, backward, emit)


def _mixer_out_kernel(hf_ref, hb_ref, g1_ref, g2_ref, wout_ref, x_ref, mod0_ref, postg_ref,
                      mod1_ref, preg1_ref, win1_ref, x1_ref, u_ref, sz_ref):
    e = hf_ref.shape[2]
    dh = e // N_HEADS
    y = None
    for head in range(N_HEADS):
        hs = slice(head * dh, (head + 1) * dh)
        hsum = hf_ref[0, :, hs].astype(F32) + hb_ref[0, :, hs].astype(F32)
        cen = hsum - jnp.mean(hsum, axis=-1, keepdims=True)
        hn = cen * lax.rsqrt(jnp.mean(cen * cen, axis=-1, keepdims=True) + EPS)
        gated = (hn * g1_ref[0, :, hs].astype(F32) + g2_ref[0, :, hs].astype(F32)).astype(BF16)
        part = _dot(gated, wout_ref[hs, :])
        y = part if y is None else y + part
    yn = y * lax.rsqrt(jnp.mean(y * y, axis=-1, keepdims=True) + EPS) * postg_ref[...]
    x1 = x_ref[0] + mod0_ref[0, 2:3, :] * yn
    x1_ref[0] = x1
    _pool_in_proj(x1, mod1_ref, preg1_ref, win1_ref, u_ref, sz_ref)


def _scan_specs(b, t, e, chunks_per_step, reverse):
    dh = e // N_HEADS
    direction = 1 if reverse else 0
    l = SCAN_CHUNK * chunks_per_step
    ns = t // l
    cidx = (lambda c: ns - 1 - c) if reverse else (lambda c: c)
    tok = lambda w: pl.BlockSpec((1, l, w), lambda bi, c: (bi, cidx(c), 0))
    state = lambda shape: pl.BlockSpec((1, 1) + shape, lambda bi, c: (bi, direction) + (0,) * len(shape),
                                       pipeline_mode=pl.Buffered(1))
    specs = [tok(e), tok(e),
             pl.BlockSpec((1, chunks_per_step, e, SCAN_CHUNK), lambda bi, c: (bi, cidx(c), 0, 0)),
             tok(e), tok(LANES),
             pl.BlockSpec((1, chunks_per_step, 4 * N_HEADS, SCAN_CHUNK), lambda bi, c: (bi, cidx(c), 0, 0)),
             pl.BlockSpec((1, 1, l, N_HEADS * LANES), lambda bi, c: (bi, direction, cidx(c), 0)),
             state((N_HEADS, dh, dh)), state((N_HEADS, 1, dh)), state((N_HEADS, 1, 1))]
    scratch = [pltpu.VMEM((N_HEADS, dh, dh), F32), pltpu.VMEM((N_HEADS, 1, dh), F32),
               pltpu.VMEM((N_HEADS, 1, 1), F32)]
    return specs, scratch, tok, ns


def _scan_both(q, k, kt, v, cols, rows, cmrep, c0, n0, m0):
    b, t, e = q.shape
    specs_f, scratch_f, tok_f, ns = _scan_specs(b, t, e, 1, False)
    specs_b, scratch_b, tok_b, _ = _scan_specs(b, t, e, 1, True)
    assert len(specs_f) == N_SCAN_INPUTS
    args = (q, k, kt, v, cols, rows, cmrep, c0, n0, m0)
    out = jax.ShapeDtypeStruct((b, t, e), BF16)
    return pl.pallas_call(
        _scan_both_kernel,
        grid=(b, ns),
        in_specs=specs_f + specs_b,
        out_specs=[tok_f(e), tok_b(e)],
        out_shape=[out, out],
        scratch_shapes=scratch_f + scratch_b,
        compiler_params=_cparams(("parallel", "arbitrary")),
        name="mlstm_scan_both",
    )(*args, *args)


def _mixer_out(hf, hb, g1, g2, w_out_b, x, mod0, post_g, mod1, pre_g1, w_in1_b):
    b, t, e = hf.shape
    d = x.shape[-1]
    tm = POOL_TILE
    row = lambda bi, i: (bi, i, 0)
    tok = lambda w: pl.BlockSpec((1, tm, w), row)
    modspec = pl.BlockSpec((1, 3, d), lambda bi, i: (bi, 0, 0))
    return pl.pallas_call(
        _mixer_out_kernel,
        grid=(b, t // tm),
        in_specs=[tok(e), tok(e), tok(e), tok(e), _const_spec(w_out_b.shape), tok(d), modspec,
                  _const_spec((1, d)), modspec, _const_spec((1, d)), _const_spec(w_in1_b.shape)],
        out_specs=[tok(d), tok(e), tok(e)],
        out_shape=[jax.ShapeDtypeStruct((b, t, d), F32), jax.ShapeDtypeStruct((b, t, e), BF16),
                   jax.ShapeDtypeStruct((b, t, e), BF16)],
        compiler_params=_cparams(("parallel", "parallel")),
        name="mixer_out_pool_in",
    )(hf, hb, g1, g2, w_out_b, x, mod0, post_g.reshape(1, d), mod1, pre_g1.reshape(1, d), w_in1_b)


def _pool_in_proj(x, mod_ref, g_ref, win_ref, u_ref, sz_ref):
    e = u_ref.shape[2]
    hn = _norm_mod(x, g_ref[...], mod_ref[0, 0:1, :], mod_ref[0, 1:2, :]).astype(BF16)
    for c in range(e // MXU_DIM):
        cs = slice(c * MXU_DIM, (c + 1) * MXU_DIM)
        u_ref[0, :, cs] = _dot(hn, win_ref[:, cs]).astype(BF16)
        sz_ref[0, :, cs] = _silu(_dot(hn, win_ref[:, e + c * MXU_DIM:e + (c + 1) * MXU_DIM])).astype(BF16)


def _pool_out_kernel(up_ref, uc_ref, un_ref, sz_ref, band_ref, icnt_ref, wpool_ref, pscale_ref,
                     wout_ref, x_ref, mod_ref, postg_ref, o_ref, ext_scr):
    i = pl.program_id(1)
    nt = pl.num_programs(1)
    tm = uc_ref.shape[1]
    gd = wpool_ref.shape[1]
    y = None
    for g, w in enumerate(POOL_WINDOWS):
        gs = slice(g * gd, (g + 1) * gd)
        before = (w // 2) * GRID_W
        after = (w - w // 2 - 1) * GRID_W
        ext_scr[tm - before:tm, gs] = jnp.where(i > 0, up_ref[0, tm - before:, gs].astype(F32), 0.0)
        ext_scr[tm:2 * tm, gs] = uc_ref[0, :, gs].astype(F32)
        if after:
            ext_scr[2 * tm:2 * tm + after, gs] = jnp.where(i < nt - 1, un_ref[0, 0:after, gs].astype(F32), 0.0)
        n = tm + (w - 1) * GRID_W
        lo = tm - before
        rsum = ext_scr[lo:lo + n, gs]
        span = 1
        while span < w:
            n -= span * GRID_W
            rsum = rsum[0:n] + rsum[span * GRID_W:span * GRID_W + n]
            span *= 2
        rsum = rsum.astype(BF16)
        parts = [_dot(band_ref[g], rsum[s * MXU_DIM:(s + 1) * MXU_DIM, :]) for s in range(tm // MXU_DIM)]
        mean = jnp.concatenate(parts, axis=0) * icnt_ref[:, g:g + 1]
        diff = mean - ext_scr[tm:2 * tm, gs]
        yg = _dot(diff.astype(BF16), wpool_ref[g]) * pscale_ref[:, gs] * sz_ref[0, :, gs].astype(F32)
        part = _dot(yg.astype(BF16), wout_ref[gs, :])
        y = part if y is None else y + part
    yn = y * lax.rsqrt(jnp.mean(y * y, axis=-1, keepdims=True) + EPS) * postg_ref[...]
    o_ref[0] = x_ref[0] + mod_ref[0, 2:3, :] * yn


def _pool_constants(t):
    rows = t // GRID_W
    bands = np.zeros((len(POOL_WINDOWS), MXU_DIM, MXU_DIM), np.float32)
    icnt = np.zeros((t, LANES), np.float32)
    for g, w in enumerate(POOL_WINDOWS):
        lo_off, hi_off = -(w // 2), w - w // 2
        for p in range(MXU_DIM):
            c = p % GRID_W
            base = p - c
            lo, hi = max(c + lo_off, 0), min(c + hi_off, GRID_W)
            bands[g, p, base + lo:base + hi] = 1.0
        cc = np.arange(GRID_W)
        cnt_c = np.clip(cc + hi_off, 0, GRID_W) - np.clip(cc + lo_off, 0, GRID_W)
        rr = np.arange(rows)
        cnt_r = np.clip(rr + hi_off, 0, rows) - np.clip(rr + lo_off, 0, rows)
        icnt[:, g] = (1.0 / (cnt_r[:, None] * cnt_c[None, :])).reshape(-1)
    return bands, icnt


def _pool_out(u, sz, w_pool_b, pool_scale, w_out_b, x, mod, post_g):
    b, t, e = u.shape
    d = x.shape[-1]
    tm = POOL_TILE
    assert tm >= (max(POOL_WINDOWS) // 2) * GRID_W and tm % MXU_DIM == 0 and t % tm == 0
    nt = t // tm
    gd = e // len(POOL_WINDOWS)
    bands, icnt = _pool_constants(t)
    row = lambda bi, i: (bi, i, 0)
    return pl.pallas_call(
        _pool_out_kernel,
        grid=(b, nt),
        in_specs=[
            pl.BlockSpec((1, tm, e), lambda bi, i: (bi, jnp.maximum(i - 1, 0), 0)),
            pl.BlockSpec((1, tm, e), row),
            pl.BlockSpec((1, tm, e), lambda bi, i: (bi, jnp.minimum(i + 1, nt - 1), 0)),
            pl.BlockSpec((1, tm, e), row),
            _const_spec(bands.shape),
            pl.BlockSpec((tm, LANES), lambda bi, i: (i, 0)),
            _const_spec(w_pool_b.shape),
            _const_spec((1, e)),
            _const_spec(w_out_b.shape),
            pl.BlockSpec((1, tm, d), row),
            pl.BlockSpec((1, 3, d), lambda bi, i: (bi, 0, 0)),
            _const_spec((1, d)),
        ],
        out_specs=pl.BlockSpec((1, tm, d), row),
        out_shape=jax.ShapeDtypeStruct((b, t, d), F32),
        scratch_shapes=[pltpu.VMEM((3 * tm, e), F32)],
        compiler_params=_cparams(("parallel", "parallel")),
        name="pool_out",
    )(u, u, u, sz, jnp.asarray(bands, BF16), jnp.asarray(icnt), w_pool_b, pool_scale.reshape(1, e),
      w_out_b, x, mod, post_g.reshape(1, d))


def kernel(x, c, ctx, c_ctx, w_ada, b_ada, a_norm_pre, a_norm_post, a_w_in, a_conv_w, a_conv_b,
           a_w_q, a_w_k, a_w_v, a_w_gate_f, a_b_gate_f, a_w_gate_b, a_b_gate_b, a_head_norm, a_skip,
           a_w_out, b_norm_pre, b_norm_post, b_w_in, b_w_pool, b_pool_scale, b_w_out):
    bsz, t, d = x.shape
    t_ctx = ctx.shape[1]
    assert t % SCAN_CHUNK == 0 and t_ctx % SCAN_CHUNK == 0 and t % GRID_W == 0

    n_cond = -(-(bsz + 1) // 8) * 8
    cond = jnp.concatenate([c, c_ctx[None], jnp.zeros((n_cond - bsz - 1, d), F32)], axis=0)
    mods = _ada(cond, bsz + 1, w_ada, b_ada).reshape(w_ada.shape[0], n_cond, 3, d)
    mod_lat = [mods[i, :bsz] for i in range(w_ada.shape[0])]
    mod_ctx = jnp.broadcast_to(mods[0, bsz][None], (bsz, 3, d))

    k_scale = float((a_w_in.shape[2] // 3 // N_HEADS) ** -0.5)
    bdkq, bdv, aqk, av, w_in_b = _fold_weights(a_w_q[0], a_w_k[0], a_w_v[0], a_w_gate_f[0], a_w_gate_b[0],
                                               a_w_in[0], k_scale)
    bg = jnp.pad(jnp.concatenate([a_b_gate_f[0], a_b_gate_b[0]]), (0, LANES - 4 * N_HEADS)).reshape(1, LANES)
    pre_args = (a_norm_pre[0], w_in_b, a_conv_w[0], a_conv_b[0], bdkq, bdv, aqk, av, bg,
                a_head_norm[0], a_skip[0])
    w_pool = b_w_pool[0]
    later = (a_w_out[0], b_w_in[0], w_pool.reshape(-1, w_pool.shape[-1]), b_w_out[0])
    (ql, kl, ktl, vl, g1, g2, cols_l, rows_l, cmrep_l, a_w_out_b, b_w_in_b, w_pool_b, b_w_out_b) = _mlstm_pre(
        x, mod_lat[0], *pre_args, states_only=False, later_weights=later)
    kc, ktc, vc, cols_c, rows_c, _ = _mlstm_pre(ctx, mod_ctx, *pre_args, states_only=True)
    c0, n0, m0 = _ctx_states(kc, ktc, vc, cols_c, rows_c)
    scan_in = (ql, kl, ktl, vl, cols_l, rows_l, cmrep_l, c0, n0, m0)
    hf, hb = _scan_both(*scan_in)

    x, u, sz = _mixer_out(hf, hb, g1, g2, a_w_out_b, x, mod_lat[0], a_norm_post[0],
                          mod_lat[1], b_norm_pre[0], b_w_in_b)
    x = _pool_out(u, sz, w_pool_b.reshape(w_pool.shape), b_pool_scale[0], b_w_out_b,
                  x, mod_lat[1], b_norm_post[0])
    return x
```

```python
import functools

import numpy as np
import jax
import jax.numpy as jnp
from jax import lax
from jax.experimental import pallas as pl
from jax.experimental.pallas import tpu as pltpu

EPS = 1e-6
N_HEADS = 4
QKV_BLOCK = 4
GRID_W = 64
POOL_WINDOWS = (2, 4, 8, 16)

LANES = 128
MXU_DIM = 256
HALO = 16
VMEM_LIMIT_BYTES = 60 * 1024 * 1024

SCAN_CHUNK = 256
PRE_TILE = 512
PROJ_AHEAD = 3
POOL_TILE = 512
SCAN_CHUNKS_PER_STEP = 2
BF16 = jnp.bfloat16
F32 = jnp.float32


def _dot(a, b):
    return jnp.dot(a, b, preferred_element_type=F32)


def _silu(v):
    return v * jax.nn.sigmoid(v)


def _cparams(sem):
    return pltpu.CompilerParams(dimension_semantics=sem, vmem_limit_bytes=VMEM_LIMIT_BYTES)


def _const_spec(shape):
    nd = len(shape)
    return pl.BlockSpec(shape, lambda *_: (0,) * nd, pipeline_mode=pl.Buffered(1))


def _lane_repeat(col, width):
    return jnp.concatenate([col] * (width // LANES), axis=1) if width > LANES else col


def _ada_kernel(cond_t_ref, w_ref, b_ref, o_ref, *, n_used):
    s = _silu(cond_t_ref[...])
    w = w_ref[0]
    for r in range(n_used):
        o_ref[0, r:r + 1, :] = jnp.sum(w * s[:, r:r + 1], axis=0, keepdims=True) + b_ref[0]
    o_ref[0, n_used:, :] = jnp.zeros((o_ref.shape[1] - n_used, o_ref.shape[2]), F32)


def _ada(cond, n_used, w_ada, b_ada):
    depth, d, n3 = w_ada.shape
    rows = cond.shape[0]
    tn = max(w for w in range(LANES, 1024 + 1, LANES) if n3 % w == 0)
    return pl.pallas_call(
        functools.partial(_ada_kernel, n_used=n_used),
        grid=(depth, n3 // tn),
        in_specs=[
            pl.BlockSpec((d, rows), lambda l, j: (0, 0)),
            pl.BlockSpec((1, d, tn), lambda l, j: (l, 0, j)),
            pl.BlockSpec((1, 1, tn), lambda l, j: (l, 0, j)),
        ],
        out_specs=pl.BlockSpec((1, rows, tn), lambda l, j: (l, 0, j)),
        out_shape=jax.ShapeDtypeStruct((depth, rows, n3), F32),
        compiler_params=_cparams(("parallel", "parallel")),
        name="ada_mod",
    )(cond.T, w_ada, b_ada.reshape(depth, 1, n3))


def _norm_mod(xv, g, shift, scale):
    y = xv * lax.rsqrt(jnp.mean(xv * xv, axis=-1, keepdims=True) + EPS) * g
    return y * (1.0 + scale) + shift


def _scan_along_lanes(x, fwd_rows, lane, combine, fill):
    l = x.shape[1]
    step = 1
    while step < l:
        from_left = jnp.where(lane >= step, pltpu.roll(x, step, axis=1), fill)
        from_right = jnp.where(lane < l - step, pltpu.roll(x, l - step, axis=1), fill)
        x = combine(x, jnp.where(fwd_rows, from_left, from_right))
        step *= 2
    return x


def _gate_vectors(pre_chunks):
    h = N_HEADS
    l = pre_chunks[0].shape[0]
    pt = jnp.concatenate([p.T[0:4 * h, :] for p in pre_chunks], axis=0)
    row = lax.broadcasted_iota(jnp.int32, pt.shape, 0) % (4 * h)
    lane = lax.broadcasted_iota(jnp.int32, pt.shape, 1)
    fwd_rows = row < 2 * h
    log_f = jnp.minimum(pt, 0.0) - jnp.log1p(jnp.exp(-jnp.abs(pt)))
    b = _scan_along_lanes(log_f, fwd_rows, lane, jnp.add, 0.0)
    a = pt - pltpu.roll(b, pt.shape[0] - h, axis=0)
    cmax = _scan_along_lanes(a, fwd_rows, lane, jnp.maximum, -jnp.inf)
    rows = jnp.where((row % (2 * h)) < h, a, b)
    out = []
    for ch in range(len(pre_chunks)):
        rs = slice(ch * 4 * h, (ch + 1) * 4 * h)
        stacked = jnp.concatenate([rows[rs], cmax[rs], jnp.zeros((LANES - 8 * h, l), F32)], axis=0)
        cols = stacked.T
        reps = []
        for direction in range(2):
            for head in range(h):
                j = 4 * h + direction * 2 * h + head
                reps.append(jnp.broadcast_to(cols[:, j:j + 1], (l, LANES)))
        out.append((cols, rows[rs], jnp.concatenate(reps, axis=1)))
    return out


def _fold_weights_kernel(wq_ref, wk_ref, wv_ref, wg_ref, win_ref, bdkq_ref, bdv_ref, aqk_ref, av_ref, win16_ref,
                         *, k_scale):
    hp = lax.Precision.HIGHEST
    dot = lambda a, b: jnp.dot(a, b, precision=hp, preferred_element_type=F32)
    r = lax.broadcasted_iota(jnp.int32, (MXU_DIM, MXU_DIM), 0)
    c = lax.broadcasted_iota(jnp.int32, (MXU_DIM, MXU_DIM), 1)
    same_block = (r // QKV_BLOCK) == (c // QKV_BLOCK)

    def block_diag(w_ref):
        rows = w_ref[...]
        tile = jnp.zeros((MXU_DIM, MXU_DIM), F32)
        for o in range(QKV_BLOCK):
            tile = jnp.where(same_block & (c % QKV_BLOCK == o), rows[:, o:o + 1], tile)
        return tile

    bdq = block_diag(wq_ref)
    bdk = block_diag(wk_ref) * k_scale
    bdv = block_diag(wv_ref)
    bdkq = lax.dot_general(bdk, bdq, (((1,), (1,)), ((), ())), precision=hp, preferred_element_type=F32)
    bdkq_ref[0] = bdkq.astype(BF16)
    bdv_ref[0] = bdv.astype(BF16)
    aqk_ref[0] = (dot(bdq, wg_ref[0, 0]) + dot(bdk, wg_ref[1, 0])).astype(BF16)
    av_ref[0] = dot(bdv, wg_ref[2, 0]).astype(BF16)
    win16_ref[...] = win_ref[...].astype(BF16)


def _fold_weights(w_q, w_k, w_v, w_gate_f, w_gate_b, w_in, k_scale):
    e = w_q.shape[0] * QKV_BLOCK
    nt = e // MXU_DIM
    d = w_in.shape[0]
    assert d % (HALO * nt) == 0
    rows = pl.BlockSpec((MXU_DIM, QKV_BLOCK), lambda i: (i, 0))
    tile = pl.BlockSpec((1, MXU_DIM, MXU_DIM), lambda i: (i, 0, 0))
    gate = pl.BlockSpec((1, MXU_DIM, LANES), lambda i: (i, 0, 0))
    win = pl.BlockSpec((d // nt, w_in.shape[1]), lambda i: (i, 0))
    wg = jnp.pad(jnp.concatenate([w_gate_f, w_gate_b], axis=1), ((0, 0), (0, LANES - 4 * N_HEADS)))
    args = [w.reshape(e, QKV_BLOCK) for w in (w_q, w_k, w_v)] + [wg.reshape(3, nt, MXU_DIM, LANES), w_in]
    return pl.pallas_call(
        functools.partial(_fold_weights_kernel, k_scale=k_scale),
        grid=(nt,),
        in_specs=[rows, rows, rows, pl.BlockSpec((3, 1, MXU_DIM, LANES), lambda i: (0, i, 0, 0)), win],
        out_specs=[tile, tile, gate, gate, win],
        out_shape=[jax.ShapeDtypeStruct((nt, MXU_DIM, MXU_DIM), BF16),
                   jax.ShapeDtypeStruct((nt, MXU_DIM, MXU_DIM), BF16),
                   jax.ShapeDtypeStruct((nt, MXU_DIM, LANES), BF16),
                   jax.ShapeDtypeStruct((nt, MXU_DIM, LANES), BF16),
                   jax.ShapeDtypeStruct(w_in.shape, BF16)],
        compiler_params=_cparams(("arbitrary",)),
        name="fold_weights",
    )(*args)


def _mlstm_pre_kernel(x_ref, xprev_ref, xnext_ref, mod_ref, g_ref, win_ref, convw_ref, convb_ref,
                      bdkq_ref, bdv_ref, aqk_ref, av_ref, bg_ref, hnorm_ref, skip_ref, *rest,
                      states_only, tiles_per_seq, n_later):
    later_f32, rest = rest[:n_later], rest[n_later:]
    later_bf16, (hn_scr, pre_scr) = rest[-2 - n_later:-2], rest[-2:]
    if states_only:
        k_ref, kt_ref, v_ref, cols_ref, rows_ref, cmrep_ref = rest[:-2 - n_later]
    else:
        q_ref, k_ref, kt_ref, v_ref, g1_ref, g2_ref, cols_ref, rows_ref, cmrep_ref = rest[:-2 - n_later]
    step = pl.program_id(0)
    n_tiles = pl.num_programs(0) - 1
    tm, e = k_ref.shape

    def gate_stage():
        nch = tm // SCAN_CHUNK
        chunks = [pre_scr[ch * SCAN_CHUNK:(ch + 1) * SCAN_CHUNK, :] for ch in range(nch)]
        for ch, (cols, rows, cmrep) in enumerate(_gate_vectors(chunks)):
            rs = slice(ch * SCAN_CHUNK, (ch + 1) * SCAN_CHUNK)
            cols_ref[rs, :] = cols
            rows_ref[ch] = rows
            half = N_HEADS * LANES
            cmrep_ref[0, 0, rs, :] = cmrep[:, 0:half]
            cmrep_ref[0, 1, rs, :] = cmrep[:, half:]

    @pl.when(step == 0)
    def _():
        pre_scr[...] = jnp.zeros(pre_scr.shape, F32)

    @pl.when(step == n_tiles)
    def _():
        gate_stage()

    @pl.when(step < n_tiles)
    def _():
        gate_stage()
        _mlstm_pre_tile(x_ref, xprev_ref, xnext_ref, mod_ref, g_ref, win_ref, convw_ref, convb_ref, bdkq_ref, bdv_ref,
                        aqk_ref, av_ref, bg_ref, hnorm_ref, skip_ref,
                        None if states_only else (q_ref, g1_ref, g2_ref), k_ref, kt_ref, v_ref,
                        hn_scr, pre_scr, step % tiles_per_seq, tiles_per_seq)
        for w32_ref, w16_ref in zip(later_f32, later_bf16):
            w16_ref[...] = w32_ref[...].astype(BF16)


def _mlstm_pre_tile(x_ref, xprev_ref, xnext_ref, mod_ref, g_ref, win_ref, convw_ref, convb_ref, bdkq_ref, bdv_ref,
                    aqk_ref, av_ref, bg_ref, hnorm_ref, skip_ref, gate_outs, k_ref, kt_ref, v_ref,
                    hn_scr, pre_scr, i, nt):
    states_only = gate_outs is None
    if not states_only:
        q_ref, g1_ref, g2_ref = gate_outs
    tm, e = k_ref.shape
    shift = mod_ref[0, 0:1, :]
    scale = mod_ref[0, 1:2, :]
    g = g_ref[...]

    hn_scr[0:HALO, :] = _norm_mod(xprev_ref[...], g, shift, scale).astype(BF16)
    hn_scr[HALO:HALO + tm, :] = _norm_mod(x_ref[...], g, shift, scale).astype(BF16)
    hn_scr[HALO + tm:, :] = _norm_mod(xnext_ref[...], g, shift, scale).astype(BF16)

    def project(c):
        cs = slice(c * MXU_DIM, (c + 1) * MXU_DIM)
        xm_ext = _dot(hn_scr[...], win_ref[:, cs])
        if states_only:
            return xm_ext, None, None
        hc = hn_scr[HALO:HALO + tm, :]
        return (xm_ext, _dot(hc, win_ref[:, e + c * MXU_DIM:e + (c + 1) * MXU_DIM]),
                _dot(hc, win_ref[:, 2 * e + c * MXU_DIM:2 * e + (c + 1) * MXU_DIM]))

    pre = jnp.zeros((tm, LANES), F32) + bg_ref[...]
    n_chunks = e // MXU_DIM
    queue = [project(c) for c in range(min(PROJ_AHEAD, n_chunks))]
    for c in range(n_chunks):
        cs = slice(c * MXU_DIM, (c + 1) * MXU_DIM)
        xm_ext, z, og = queue.pop(0)
        if c + PROJ_AHEAD < n_chunks:
            queue.append(project(c + PROJ_AHEAD))
        xm_all = jnp.concatenate([jnp.where(i > 0, xm_ext[0:HALO], 0.0), xm_ext[HALO:HALO + tm],
                                  jnp.where(i < nt - 1, xm_ext[HALO + tm:], 0.0)], axis=0)
        xm = xm_all[HALO:HALO + tm]
        xm_prev = pltpu.roll(xm_all, 1, axis=0)[HALO:HALO + tm]
        xm_next = pltpu.roll(xm_all, tm + 2 * HALO - 1, axis=0)[HALO:HALO + tm]
        conv = (xm_prev * convw_ref[0:1, cs] + xm * convw_ref[1:2, cs] + xm_next * convw_ref[2:3, cs]
                + convb_ref[:, cs])
        xc = _silu(conv)
        xc_b = xc.astype(BF16)
        xm_b = xm.astype(BF16)
        k32 = _dot(xc_b, bdkq_ref[c])
        pre = pre + _dot(xc_b, aqk_ref[c]) + _dot(xm_b, av_ref[c])
        k_ref[:, cs] = k32.astype(BF16)
        k_t = k32.T.astype(BF16)
        for ch in range(tm // SCAN_CHUNK):
            kt_ref[ch, cs, :] = k_t[:, ch * SCAN_CHUNK:(ch + 1) * SCAN_CHUNK]
        v_ref[:, cs] = _dot(xm_b, bdv_ref[c]).astype(BF16)
        if not states_only:
            sz = _silu(z)
            q_ref[:, cs] = xc_b
            g1_ref[:, cs] = (hnorm_ref[:, cs] * jax.nn.sigmoid(og) * sz).astype(BF16)
            g2_ref[:, cs] = (skip_ref[:, cs] * xc * sz).astype(BF16)

    pre_scr[...] = pre


def _mlstm_pre(x, mod, pre_g, w_in_b, conv_w, conv_b, bdkq, bdv, aqk, av, bg, head_norm, skip, states_only,
               later_weights=()):
    b, t, d = x.shape
    e = w_in_b.shape[1] // 3
    tm = min(PRE_TILE, t)
    nt = t // tm

    n = b * nt
    nch = tm // SCAN_CHUNK
    cur = lambda i: jnp.minimum(i, n - 1)
    prv = lambda i: jnp.maximum(i - 1, 0)
    big = pl.BlockSpec((tm, e), lambda i: (cur(i), 0))
    big_t = pl.BlockSpec((nch, e, SCAN_CHUNK), lambda i: (cur(i), 0, 0))
    tok = jax.ShapeDtypeStruct((b * t, e), BF16)
    tok_t = jax.ShapeDtypeStruct((b * t // SCAN_CHUNK, e, SCAN_CHUNK), BF16)
    gate_specs = [pl.BlockSpec((tm, LANES), lambda i: (prv(i), 0)),
                  pl.BlockSpec((nch, 4 * N_HEADS, SCAN_CHUNK), lambda i: (prv(i), 0, 0)),
                  pl.BlockSpec((1, 2, tm, N_HEADS * LANES), lambda i: (prv(i) // nt, 0, prv(i) % nt, 0))]
    gate_shapes = [jax.ShapeDtypeStruct((b * t, LANES), F32),
                   jax.ShapeDtypeStruct((b * t // SCAN_CHUNK, 4 * N_HEADS, SCAN_CHUNK), F32),
                   jax.ShapeDtypeStruct((b, 2, t, N_HEADS * LANES), F32)]
    if states_only:
        out_specs, out_shape = [big, big_t, big] + gate_specs, [tok, tok_t, tok] + gate_shapes
    else:
        out_specs = [big, big, big_t, big, big, big] + gate_specs
        out_shape = [tok, tok, tok_t, tok, tok, tok] + gate_shapes
    later_specs = [pl.BlockSpec((w.shape[0] // n, w.shape[1]), lambda i: (cur(i), 0)) for w in later_weights]
    assert all(w.shape[0] % (HALO * n) == 0 for w in later_weights)
    outs = pl.pallas_call(
        functools.partial(_mlstm_pre_kernel, states_only=states_only, tiles_per_seq=nt,
                          n_later=len(later_weights)),
        grid=(n + 1,),
        in_specs=[
            pl.BlockSpec((tm, d), lambda i: (cur(i), 0)),
            pl.BlockSpec((HALO, d), lambda i: (jnp.maximum(cur(i) * (tm // HALO) - 1, 0), 0)),
            pl.BlockSpec((HALO, d), lambda i: (jnp.minimum((cur(i) + 1) * (tm // HALO), b * t // HALO - 1), 0)),
            pl.BlockSpec((1, 3, d), lambda i: (cur(i) // nt, 0, 0)),
            _const_spec((1, d)),
            _const_spec(w_in_b.shape),
            _const_spec(conv_w.shape),
            _const_spec((1, e)),
            _const_spec(bdkq.shape), _const_spec(bdv.shape), _const_spec(aqk.shape), _const_spec(av.shape),
            _const_spec((1, LANES)),
            _const_spec((1, e)), _const_spec((1, e)),
        ] + later_specs,
        out_specs=out_specs + later_specs,
        out_shape=out_shape + [jax.ShapeDtypeStruct(w.shape, BF16) for w in later_weights],
        scratch_shapes=[pltpu.VMEM((tm + 2 * HALO, d), BF16), pltpu.VMEM((tm, LANES), F32)],
        compiler_params=_cparams(("arbitrary",)),
        name="mlstm_pre",
    )(x.reshape(b * t, d), x.reshape(b * t, d), x.reshape(b * t, d), mod, pre_g.reshape(1, d), w_in_b,
      conv_w, conv_b.reshape(1, e),
      bdkq, bdv, aqk, av, bg, head_norm.reshape(1, e), skip.reshape(1, e), *later_weights)
    by_token = lambda o: o.reshape(b, t, o.shape[-1])
    by_chunk = lambda o: o.reshape((b, t // SCAN_CHUNK) + o.shape[1:])
    outs, later_bf16 = outs[:len(out_specs)], outs[len(out_specs):]
    *main, cols, rows, cmrep = outs
    kt_pos = 1 if states_only else 2
    main = [by_chunk(o) if j == kt_pos else by_token(o) for j, o in enumerate(main)]
    return main + [by_token(cols), by_chunk(rows), cmrep] + list(later_bf16)


def _scan_chunk(get, cols, rows, cmrep, c_st, n_st, m_st, backward, emit):
    h = N_HEADS
    l = cols.shape[0]
    dh = c_st.shape[-1]
    off = 2 * h if backward else 0
    last = 0 if backward else l - 1
    st = []
    for head in range(h):
        cm_col = cols[:, 4 * h + off + head:4 * h + off + head + 1]
        st.append(dict(
            a_col=cols[:, off + head:off + head + 1],
            b_col=cols[:, off + h + head:off + h + head + 1],
            a_row=rows[off + head:off + head + 1, :],
            total=cols[last:last + 1, off + h + head:off + h + head + 1],
            amax=cm_col[last:last + 1, :],
            m_prev=m_st[head]))

    if emit is not None:
        ti = lax.broadcasted_iota(jnp.int32, (l, l), 0)
        si = lax.broadcasted_iota(jnp.int32, (l, l), 1)
        mask = (si >= ti) if backward else (ti >= si)
        for head in range(h):
            s = st[head]
            q = get("q", head)
            s["u"] = jnp.maximum(s["m_prev"], cmrep[:, head * LANES:(head + 1) * LANES])
            s["p"] = jnp.exp(jnp.where(mask, s["a_row"] - _lane_repeat(s["u"], l), -jnp.inf))
            s["qk"] = _dot(q, get("kt", head))
            s["qc"] = _dot(q, c_st[head].astype(BF16))
            sc = s["qk"] * s["p"]
            w_inter = jnp.exp(s["m_prev"] - s["u"])
            num = _dot(sc.astype(BF16), get("v", head)) + _lane_repeat(w_inter, dh) * s["qc"]
            qn = jnp.sum(get("q", head).astype(F32) * n_st[head], axis=-1, keepdims=True)
            den = jnp.sum(sc, axis=-1, keepdims=True) + w_inter[:, 0:1] * qn
            floor = jnp.exp(-(s["b_col"] + s["u"][:, 0:1]))
            emit(head, num / jnp.maximum(jnp.abs(den), floor))

    for head in range(h):
        s = st[head]
        s["u_last"] = jnp.maximum(s["m_prev"], s["amax"])
        s["decay"] = jnp.exp(s["m_prev"] - s["u_last"])
        wkt = (get("kt", head).astype(F32) * jnp.exp(s["a_row"] - s["u_last"])).astype(BF16)
        s["upd"] = _dot(wkt, get("v", head))
    for head in range(h):
        s = st[head]
        wk = get("k", head).astype(F32) * jnp.exp(s["a_col"] - s["u_last"])
        c_st[head] = s["decay"] * c_st[head] + s["upd"]
        n_st[head] = s["decay"] * n_st[head] + jnp.sum(wk, axis=0, keepdims=True)
        m_st[head] = s["total"] + s["u_last"]


def _tile_getter(refs, dh, chunk):
    rows = slice(chunk * SCAN_CHUNK, (chunk + 1) * SCAN_CHUNK)

    def get(name, head):
        hs = slice(head * dh, (head + 1) * dh)
        if name == "kt":
            return refs["kt"][0, chunk, hs, :]
        return refs[name][0, rows, hs]
    return get


def _ctx_state_kernel(k_ref, kt_ref, v_ref, cols_ref, rows_ref, c_out, n_out, m_out):
    dh = c_out.shape[-1]
    nc = k_ref.shape[1] // SCAN_CHUNK
    c_out[...] = jnp.zeros(c_out.shape, F32)
    n_out[...] = jnp.zeros(n_out.shape, F32)
    m_out[...] = jnp.zeros(m_out.shape, F32)
    refs = dict(k=k_ref, kt=kt_ref, v=v_ref)
    for direction in range(2):
        for step in range(nc):
            c = nc - 1 - step if direction else step
            rs = slice(c * SCAN_CHUNK, (c + 1) * SCAN_CHUNK)
            _scan_chunk(_tile_getter(refs, dh, c), cols_ref[0, rs, :], rows_ref[0, c], None,
                        c_out.at[0, direction], n_out.at[0, direction], m_out.at[0, direction],
                        bool(direction), None)


def _ctx_states(k, kt, v, cols, rows):
    b, t, e = k.shape
    dh = e // N_HEADS
    nc = t // SCAN_CHUNK
    per_b = lambda nd: (lambda bi: (bi,) + (0,) * (nd - 1))
    return pl.pallas_call(
        _ctx_state_kernel,
        grid=(b,),
        in_specs=[pl.BlockSpec((1, t, e), per_b(3)), pl.BlockSpec((1, nc, e, SCAN_CHUNK), per_b(4)),
                  pl.BlockSpec((1, t, e), per_b(3)),
                  pl.BlockSpec((1, t, LANES), per_b(3)),
                  pl.BlockSpec((1, nc, 4 * N_HEADS, SCAN_CHUNK), per_b(4))],
        out_specs=[pl.BlockSpec((1, 2, N_HEADS, dh, dh), per_b(5)),
                   pl.BlockSpec((1, 2, N_HEADS, 1, dh), per_b(5)),
                   pl.BlockSpec((1, 2, N_HEADS, 1, 1), per_b(5))],
        out_shape=[jax.ShapeDtypeStruct((b, 2, N_HEADS, dh, dh), F32),
                   jax.ShapeDtypeStruct((b, 2, N_HEADS, 1, dh), F32),
                   jax.ShapeDtypeStruct((b, 2, N_HEADS, 1, 1), F32)],
        compiler_params=_cparams(("parallel",)),
        name="ctx_states",
    )(k, kt, v, cols, rows)


def _load_state(c0_ref, n0_ref, m0_ref, c_scr, n_scr, m_scr):
    @pl.when(pl.program_id(1) == 0)
    def _():
        c_scr[...] = c0_ref[0, 0]
        n_scr[...] = n0_ref[0, 0]
        m_scr[...] = m0_ref[0, 0]


def _scan_kernel(q_ref, k_ref, kt_ref, v_ref, cols_ref, rows_ref, cmrep_ref, c0_ref, n0_ref, m0_ref,
                 h_ref, c_scr, n_scr, m_scr, *, backward):
    _load_state(c0_ref, n0_ref, m0_ref, c_scr, n_scr, m_scr)
    dh = c_scr.shape[-1]
    refs = dict(q=q_ref, k=k_ref, kt=kt_ref, v=v_ref)
    chunks = range(rows_ref.shape[1])
    for j in (reversed(chunks) if backward else chunks):
        rs = slice(j * SCAN_CHUNK, (j + 1) * SCAN_CHUNK)

        def emit(head, h, rs=rs):
            h_ref[0, rs, head * dh:(head + 1) * dh] = h.astype(h_ref.dtype)

        _scan_chunk(_tile_getter(refs, dh, j), cols_ref[0, rs, :], rows_ref[0, j], cmrep_ref[0, 0, rs, :],
                    c_scr, n_scr, m_scr, backward, emit)


def _mixer_out_kernel(hf_ref, hb_ref, g1_ref, g2_ref, wout_ref, x_ref, mod0_ref, postg_ref,
                      mod1_ref, preg1_ref, win1_ref, x1_ref, u_ref, sz_ref):
    e = hf_ref.shape[2]
    dh = e // N_HEADS
    y = None
    for head in range(N_HEADS):
        hs = slice(head * dh, (head + 1) * dh)
        hsum = hf_ref[0, :, hs].astype(F32) + hb_ref[0, :, hs].astype(F32)
        cen = hsum - jnp.mean(hsum, axis=-1, keepdims=True)
        hn = cen * lax.rsqrt(jnp.mean(cen * cen, axis=-1, keepdims=True) + EPS)
        gated = (hn * g1_ref[0, :, hs].astype(F32) + g2_ref[0, :, hs].astype(F32)).astype(BF16)
        part = _dot(gated, wout_ref[hs, :])
        y = part if y is None else y + part
    yn = y * lax.rsqrt(jnp.mean(y * y, axis=-1, keepdims=True) + EPS) * postg_ref[...]
    x1 = x_ref[0] + mod0_ref[0, 2:3, :] * yn
    x1_ref[0] = x1
    _pool_in_proj(x1, mod1_ref, preg1_ref, win1_ref, u_ref, sz_ref)


def _scan_specs(b, t, e, chunks_per_step, reverse):
    dh = e // N_HEADS
    direction = 1 if reverse else 0
    l = SCAN_CHUNK * chunks_per_step
    ns = t // l
    cidx = (lambda c: ns - 1 - c) if reverse else (lambda c: c)
    tok = lambda w: pl.BlockSpec((1, l, w), lambda bi, c: (bi, cidx(c), 0))
    state = lambda shape: pl.BlockSpec((1, 1) + shape, lambda bi, c: (bi, direction) + (0,) * len(shape))
    specs = [tok(e), tok(e),
             pl.BlockSpec((1, chunks_per_step, e, SCAN_CHUNK), lambda bi, c: (bi, cidx(c), 0, 0)),
             tok(e), tok(LANES),
             pl.BlockSpec((1, chunks_per_step, 4 * N_HEADS, SCAN_CHUNK), lambda bi, c: (bi, cidx(c), 0, 0)),
             pl.BlockSpec((1, 1, l, N_HEADS * LANES), lambda bi, c: (bi, direction, cidx(c), 0)),
             state((N_HEADS, dh, dh)), state((N_HEADS, 1, dh)), state((N_HEADS, 1, 1))]
    scratch = [pltpu.VMEM((N_HEADS, dh, dh), F32), pltpu.VMEM((N_HEADS, 1, dh), F32),
               pltpu.VMEM((N_HEADS, 1, 1), F32)]
    return specs, scratch, tok, ns


def _scan(q, k, kt, v, cols, rows, cmrep, c0, n0, m0, backward):
    b, t, e = q.shape
    specs, scratch, tok, ns = _scan_specs(b, t, e, SCAN_CHUNKS_PER_STEP, backward)
    return pl.pallas_call(
        functools.partial(_scan_kernel, backward=backward),
        grid=(b, ns),
        in_specs=specs,
        out_specs=tok(e),
        out_shape=jax.ShapeDtypeStruct((b, t, e), BF16),
        scratch_shapes=scratch,
        compiler_params=_cparams(("parallel", "arbitrary")),
        name="mlstm_scan_bwd" if backward else "mlstm_scan_fwd",
    )(q, k, kt, v, cols, rows, cmrep, c0, n0, m0)


def _mixer_out(hf, hb, g1, g2, w_out_b, x, mod0, post_g, mod1, pre_g1, w_in1_b):
    b, t, e = hf.shape
    d = x.shape[-1]
    tm = POOL_TILE
    row = lambda bi, i: (bi, i, 0)
    tok = lambda w: pl.BlockSpec((1, tm, w), row)
    modspec = pl.BlockSpec((1, 3, d), lambda bi, i: (bi, 0, 0))
    return pl.pallas_call(
        _mixer_out_kernel,
        grid=(b, t // tm),
        in_specs=[tok(e), tok(e), tok(e), tok(e), _const_spec(w_out_b.shape), tok(d), modspec,
                  _const_spec((1, d)), modspec, _const_spec((1, d)), _const_spec(w_in1_b.shape)],
        out_specs=[tok(d), tok(e), tok(e)],
        out_shape=[jax.ShapeDtypeStruct((b, t, d), F32), jax.ShapeDtypeStruct((b, t, e), BF16),
                   jax.ShapeDtypeStruct((b, t, e), BF16)],
        compiler_params=_cparams(("parallel", "parallel")),
        name="mixer_out_pool_in",
    )(hf, hb, g1, g2, w_out_b, x, mod0, post_g.reshape(1, d), mod1, pre_g1.reshape(1, d), w_in1_b)


def _pool_in_proj(x, mod_ref, g_ref, win_ref, u_ref, sz_ref):
    e = u_ref.shape[2]
    hn = _norm_mod(x, g_ref[...], mod_ref[0, 0:1, :], mod_ref[0, 1:2, :]).astype(BF16)
    for c in range(e // MXU_DIM):
        cs = slice(c * MXU_DIM, (c + 1) * MXU_DIM)
        u_ref[0, :, cs] = _dot(hn, win_ref[:, cs]).astype(BF16)
        sz_ref[0, :, cs] = _silu(_dot(hn, win_ref[:, e + c * MXU_DIM:e + (c + 1) * MXU_DIM])).astype(BF16)


def _pool_out_kernel(up_ref, uc_ref, un_ref, sz_ref, band_ref, icnt_ref, wpool_ref, pscale_ref,
                     wout_ref, x_ref, mod_ref, postg_ref, o_ref, ext_scr):
    i = pl.program_id(1)
    nt = pl.num_programs(1)
    tm = uc_ref.shape[1]
    gd = wpool_ref.shape[1]
    y = None
    for g, w in enumerate(POOL_WINDOWS):
        gs = slice(g * gd, (g + 1) * gd)
        before = (w // 2) * GRID_W
        after = (w - w // 2 - 1) * GRID_W
        ext_scr[tm - before:tm, gs] = jnp.where(i > 0, up_ref[0, tm - before:, gs].astype(F32), 0.0)
        ext_scr[tm:2 * tm, gs] = uc_ref[0, :, gs].astype(F32)
        if after:
            ext_scr[2 * tm:2 * tm + after, gs] = jnp.where(i < nt - 1, un_ref[0, 0:after, gs].astype(F32), 0.0)
        n = tm + (w - 1) * GRID_W
        lo = tm - before
        rsum = ext_scr[lo:lo + n, gs]
        span = 1
        while span < w:
            n -= span * GRID_W
            rsum = rsum[0:n] + rsum[span * GRID_W:span * GRID_W + n]
            span *= 2
        rsum = rsum.astype(BF16)
        parts = [_dot(band_ref[g], rsum[s * MXU_DIM:(s + 1) * MXU_DIM, :]) for s in range(tm // MXU_DIM)]
        mean = jnp.concatenate(parts, axis=0) * icnt_ref[:, g:g + 1]
        diff = mean - ext_scr[tm:2 * tm, gs]
        yg = _dot(diff.astype(BF16), wpool_ref[g]) * pscale_ref[:, gs] * sz_ref[0, :, gs].astype(F32)
        part = _dot(yg.astype(BF16), wout_ref[gs, :])
        y = part if y is None else y + part
    yn = y * lax.rsqrt(jnp.mean(y * y, axis=-1, keepdims=True) + EPS) * postg_ref[...]
    o_ref[0] = x_ref[0] + mod_ref[0, 2:3, :] * yn


def _pool_constants(t):
    rows = t // GRID_W
    bands = np.zeros((len(POOL_WINDOWS), MXU_DIM, MXU_DIM), np.float32)
    icnt = np.zeros((t, LANES), np.float32)
    for g, w in enumerate(POOL_WINDOWS):
        lo_off, hi_off = -(w // 2), w - w // 2
        for p in range(MXU_DIM):
            c = p % GRID_W
            base = p - c
            lo, hi = max(c + lo_off, 0), min(c + hi_off, GRID_W)
            bands[g, p, base + lo:base + hi] = 1.0
        cc = np.arange(GRID_W)
        cnt_c = np.clip(cc + hi_off, 0, GRID_W) - np.clip(cc + lo_off, 0, GRID_W)
        rr = np.arange(rows)
        cnt_r = np.clip(rr + hi_off, 0, rows) - np.clip(rr + lo_off, 0, rows)
        icnt[:, g] = (1.0 / (cnt_r[:, None] * cnt_c[None, :])).reshape(-1)
    return bands, icnt


def _pool_out(u, sz, w_pool_b, pool_scale, w_out_b, x, mod, post_g):
    b, t, e = u.shape
    d = x.shape[-1]
    tm = POOL_TILE
    assert tm >= (max(POOL_WINDOWS) // 2) * GRID_W and tm % MXU_DIM == 0 and t % tm == 0
    nt = t // tm
    gd = e // len(POOL_WINDOWS)
    bands, icnt = _pool_constants(t)
    row = lambda bi, i: (bi, i, 0)
    return pl.pallas_call(
        _pool_out_kernel,
        grid=(b, nt),
        in_specs=[
            pl.BlockSpec((1, tm, e), lambda bi, i: (bi, jnp.maximum(i - 1, 0), 0)),
            pl.BlockSpec((1, tm, e), row),
            pl.BlockSpec((1, tm, e), lambda bi, i: (bi, jnp.minimum(i + 1, nt - 1), 0)),
            pl.BlockSpec((1, tm, e), row),
            _const_spec(bands.shape),
            pl.BlockSpec((tm, LANES), lambda bi, i: (i, 0)),
            _const_spec(w_pool_b.shape),
            _const_spec((1, e)),
            _const_spec(w_out_b.shape),
            pl.BlockSpec((1, tm, d), row),
            pl.BlockSpec((1, 3, d), lambda bi, i: (bi, 0, 0)),
            _const_spec((1, d)),
        ],
        out_specs=pl.BlockSpec((1, tm, d), row),
        out_shape=jax.ShapeDtypeStruct((b, t, d), F32),
        scratch_shapes=[pltpu.VMEM((3 * tm, e), F32)],
        compiler_params=_cparams(("parallel", "parallel")),
        name="pool_out",
    )(u, u, u, sz, jnp.asarray(bands, BF16), jnp.asarray(icnt), w_pool_b, pool_scale.reshape(1, e),
      w_out_b, x, mod, post_g.reshape(1, d))


def kernel(x, c, ctx, c_ctx, w_ada, b_ada, a_norm_pre, a_norm_post, a_w_in, a_conv_w, a_conv_b,
           a_w_q, a_w_k, a_w_v, a_w_gate_f, a_b_gate_f, a_w_gate_b, a_b_gate_b, a_head_norm, a_skip,
           a_w_out, b_norm_pre, b_norm_post, b_w_in, b_w_pool, b_pool_scale, b_w_out):
    bsz, t, d = x.shape
    t_ctx = ctx.shape[1]
    assert t % SCAN_CHUNK == 0 and t_ctx % SCAN_CHUNK == 0 and t % GRID_W == 0

    n_cond = -(-(bsz + 1) // 8) * 8
    cond = jnp.concatenate([c, c_ctx[None], jnp.zeros((n_cond - bsz - 1, d), F32)], axis=0)
    mods = _ada(cond, bsz + 1, w_ada, b_ada).reshape(w_ada.shape[0], n_cond, 3, d)
    mod_lat = [mods[i, :bsz] for i in range(w_ada.shape[0])]
    mod_ctx = jnp.broadcast_to(mods[0, bsz][None], (bsz, 3, d))

    k_scale = float((a_w_in.shape[2] // 3 // N_HEADS) ** -0.5)
    bdkq, bdv, aqk, av, w_in_b = _fold_weights(a_w_q[0], a_w_k[0], a_w_v[0], a_w_gate_f[0], a_w_gate_b[0],
                                               a_w_in[0], k_scale)
    bg = jnp.pad(jnp.concatenate([a_b_gate_f[0], a_b_gate_b[0]]), (0, LANES - 4 * N_HEADS)).reshape(1, LANES)
    pre_args = (a_norm_pre[0], w_in_b, a_conv_w[0], a_conv_b[0], bdkq, bdv, aqk, av, bg,
                a_head_norm[0], a_skip[0])
    w_pool = b_w_pool[0]
    later = (a_w_out[0], b_w_in[0], w_pool.reshape(-1, w_pool.shape[-1]), b_w_out[0])
    (ql, kl, ktl, vl, g1, g2, cols_l, rows_l, cmrep_l, a_w_out_b, b_w_in_b, w_pool_b, b_w_out_b) = _mlstm_pre(
        x, mod_lat[0], *pre_args, states_only=False, later_weights=later)
    kc, ktc, vc, cols_c, rows_c, _ = _mlstm_pre(ctx, mod_ctx, *pre_args, states_only=True)
    c0, n0, m0 = _ctx_states(kc, ktc, vc, cols_c, rows_c)
    scan_in = (ql, kl, ktl, vl, cols_l, rows_l, cmrep_l, c0, n0, m0)
    hb = _scan(*scan_in, backward=True)
    hf = _scan(*scan_in, backward=False)

    x, u, sz = _mixer_out(hf, hb, g1, g2, a_w_out_b, x, mod_lat[0], a_norm_post[0],
                          mod_lat[1], b_norm_pre[0], b_w_in_b)
    x = _pool_out(u, sz, w_pool_b.reshape(w_pool.shape), b_pool_scale[0], b_w_out_b,
                  x, mod_lat[1], b_norm_post[0])
    return x
```
